```python
import math
import jax, jax.numpy as jnp
from jax import lax
import numpy as np

D_MODEL = 2048
BATCH = 4
SEQ = 2048
DEPTH = 2
DEC_BATCH = 8
DEC_SEQ = 4
PAST_LEN = 16384
PAGE_SIZE = 128

HEAD_DIM = 128
N_HEADS = D_MODEL // HEAD_DIM
N_MIXERS = 2
N_LAYERS_A = (DEPTH + N_MIXERS - 1) // N_MIXERS
N_LAYERS_B = DEPTH // N_MIXERS
DIL_PAIRS = ((128, 1), (512, 4), (2048, 16))
N_DIL = len(DIL_PAIRS)
BAND_BLOCK = 128
NSA_KV_HEADS = 4
NSA_GROUP = N_HEADS // NSA_KV_HEADS
CMP_BLOCK = 32
CMP_STRIDE = 16
CMP_SPAN = CMP_BLOCK // CMP_STRIDE
CMP_HIDDEN = 128
SEL_BLOCK = 64
SEL_TOPN = 16
NSA_WINDOW = 512
SEL_Q_CHUNK = 32
N_NSA_KV = 6
D_FF = 5632
CONV_W = 3
PLE_DIM = 256
REL_BUCKETS = 32
REL_MAX_DIST = 2048
EPS = 1e-6
NEG = -1e30
FORCED_SCORE = 1e4
SCALE = HEAD_DIM ** -0.5
QKV_A = N_DIL * 3 * N_HEADS * HEAD_DIM
NSA_Q = N_HEADS * HEAD_DIM
NSA_KV = NSA_KV_HEADS * HEAD_DIM
PROJ_B = NSA_Q + N_NSA_KV * NSA_KV + 3 * N_HEADS

kernel_name = 'dilated_nsa_convffn_hybrid_step'


def rmsnorm(x, g):
    xf = x.astype(jnp.float32)
    y = xf * lax.rsqrt(jnp.mean(xf * xf, axis=-1, keepdims=True) + EPS)
    return (y * g.astype(jnp.float32)).astype(x.dtype)


def rel_bucket(dist):
    dist = jnp.maximum(dist, 0)
    max_exact = REL_BUCKETS // 2
    large = max_exact + (jnp.log(jnp.maximum(dist, 1).astype(jnp.float32) / max_exact)
                         / math.log(REL_MAX_DIST / max_exact) * (REL_BUCKETS - max_exact)).astype(jnp.int32)
    large = jnp.minimum(large, REL_BUCKETS - 1)
    return jnp.where(dist < max_exact, dist, large)


def masked_softmax_lse(s, mask):
    s = jnp.where(mask, s, NEG)
    m = jnp.max(s, axis=-1, keepdims=True)
    p = jnp.exp(s - m) * mask
    l = jnp.sum(p, axis=-1, keepdims=True)
    lc = jnp.maximum(l, 1e-30)
    return p / lc, (m + jnp.log(lc))[..., 0]


def band_bias(rel_bias, dist_scale, n_prev, kh, g):
    nk = (n_prev + 1) * BAND_BLOCK
    dist = (jnp.arange(BAND_BLOCK)[:, None] - jnp.arange(nk)[None, :] + n_prev * BAND_BLOCK) * dist_scale
    b = rel_bias[rel_bucket(dist)]
    return b.reshape(BAND_BLOCK, nk, kh, g).transpose(2, 3, 0, 1).astype(jnp.float32)


def banded_attn(q, k, v, bias, n_prev, max_dist):
    N, L = q.shape[:2]
    BB = BAND_BLOCK
    nb = -(-L // BB)
    Lp = nb * BB
    pad = Lp - L
    qb = jnp.pad(q, ((0, 0), (0, pad), (0, 0), (0, 0), (0, 0))).reshape(N, nb, BB, *q.shape[2:])
    kp = jnp.pad(k, ((0, 0), (n_prev * BB, pad), (0, 0), (0, 0)))
    vp = jnp.pad(v, ((0, 0), (n_prev * BB, pad), (0, 0), (0, 0)))

    def blocks(a):
        return jnp.concatenate([a[:, s * BB: s * BB + Lp].reshape(N, nb, BB, *a.shape[2:])
                                for s in range(n_prev + 1)], axis=2)

    kb, vb = blocks(kp), blocks(vp)
    nk = (n_prev + 1) * BB
    s = jnp.einsum('nbqhgd,nbjhd->nbhgqj', qb, kb).astype(jnp.float32) * SCALE + bias
    qi = jnp.arange(BB)[:, None]
    kj = jnp.arange(nk)[None, :]
    dist = qi - kj + n_prev * BB
    kpos = jnp.arange(nb)[:, None, None] * BB + kj[None] - n_prev * BB
    mask = ((dist >= 0) & (dist <= max_dist))[None] & (kpos >= 0)
    p, lse = masked_softmax_lse(s, mask[None, :, None, None])
    o = jnp.einsum('nbhgqj,nbjhd->nbqhgd', p.astype(vb.dtype), vb)
    o = o.reshape(N, Lp, *q.shape[2:])[:, :L]
    lse = lse.transpose(0, 1, 4, 2, 3).reshape(N, Lp, *q.shape[2:4])[:, :L]
    return o, lse


def to_residue(t, dil):
    B, S = t.shape[:2]
    return t.reshape(B, S // dil, dil, *t.shape[2:]).swapaxes(1, 2).reshape(B * dil, S // dil, *t.shape[2:])


def from_residue(t, B, dil):
    Ls = t.shape[1]
    return t.reshape(B, dil, Ls, *t.shape[2:]).swapaxes(1, 2).reshape(B, dil * Ls, *t.shape[2:])


def combine_by_denominator(outs, lses):
    w = jax.nn.softmax(jnp.stack(lses, 0), axis=0)
    o = jnp.einsum('gnth,gnthd->nthd', w, jnp.stack(outs, 0).astype(jnp.float32))
    return o.astype(outs[0].dtype)


def dilated_attn_prompt(a, w_in, w_out, rel_bias):
    B, S, _ = a.shape
    qkv = (a @ w_in).reshape(B, S, N_DIL, 3, N_HEADS, HEAD_DIM)
    outs, lses, new_bufs = [], [], []
    for g, (win, dil) in enumerate(DIL_PAIRS):
        q, k, v = qkv[:, :, g, 0], qkv[:, :, g, 1], qkv[:, :, g, 2]
        band = win // dil
        n_prev = -(-band // BAND_BLOCK)
        bias = band_bias(rel_bias, dil, n_prev, N_HEADS, 1)
        o, lse = banded_attn(to_residue(q, dil)[:, :, :, None], to_residue(k, dil), to_residue(v, dil),
                             bias, n_prev, band)
        outs.append(from_residue(o[:, :, :, 0], B, dil))
        lses.append(from_residue(lse[:, :, :, 0], B, dil))
        keep = min(win, S)
        new_bufs.append(jnp.stack([k, v], axis=2)[:, S - keep:])
    o = combine_by_denominator(outs, lses)
    return o.reshape(B, S, NSA_Q) @ w_out, new_bufs


def dilated_attn_sample(a, bufs, w_in, w_out, rel_bias):
    N, T, _ = a.shape
    qkv = (a @ w_in).reshape(N, T, N_DIL, 3, N_HEADS, HEAD_DIM)
    outs, lses, new_bufs = [], [], []
    for g, (win, dil) in enumerate(DIL_PAIRS):
        buf = bufs[g]
        lb = buf.shape[1]
        q, k, v = qkv[:, :, g, 0], qkv[:, :, g, 1], qkv[:, :, g, 2]
        kk = jnp.concatenate([buf[:, :, 0].astype(k.dtype), k], axis=1)
        vv = jnp.concatenate([buf[:, :, 1].astype(v.dtype), v], axis=1)
        j = jnp.arange(win // dil + 1)
        idx = lb + jnp.arange(T)[:, None] - dil * j[None, :]
        valid = idx >= 0
        idx = jnp.maximum(idx, 0)
        kg, vg = kk[:, idx], vv[:, idx]
        bias = rel_bias[rel_bucket(dil * j)].T.astype(jnp.float32)
        s = jnp.einsum('nthd,ntjhd->nthj', q, kg).astype(jnp.float32) * SCALE + bias
        p, lse = masked_softmax_lse(s, valid[None, :, None, :])
        outs.append(jnp.einsum('nthj,ntjhd->nthd', p.astype(vg.dtype), vg))
        lses.append(lse)
        new_bufs.append(jnp.concatenate([buf, jnp.stack([k, v], axis=2).astype(buf.dtype)], axis=1)[:, T:])
    o = combine_by_denominator(outs, lses)
    return o.reshape(N, T, NSA_Q) @ w_out, new_bufs


def nsa_project(a, w_in):
    N, T, _ = a.shape
    pr = a @ w_in
    q = pr[..., :NSA_Q].reshape(N, T, NSA_KV_HEADS, NSA_GROUP, HEAD_DIM)
    kv = pr[..., NSA_Q:NSA_Q + N_NSA_KV * NSA_KV].reshape(N, T, N_NSA_KV, NSA_KV_HEADS, HEAD_DIM)
    gates = jax.nn.sigmoid(pr[..., NSA_Q + N_NSA_KV * NSA_KV:].astype(jnp.float32))
    return q, kv, gates.reshape(N, T, 3, NSA_KV_HEADS, NSA_GROUP)


def compress(x, pe, w1, b1, w2, b2):
    N, L = x.shape[:2]
    nch = L // CMP_STRIDE
    nc = nch - CMP_SPAN + 1
    ch = x[:, :nch * CMP_STRIDE].reshape(N, nch, CMP_STRIDE, *x.shape[2:])
    pe_r = pe.reshape(CMP_SPAN, CMP_STRIDE, 1, HEAD_DIM).astype(x.dtype)
    w1_r = w1.reshape(CMP_SPAN, CMP_STRIDE, HEAD_DIM, CMP_HIDDEN)
    h = b1
    for s in range(CMP_SPAN):
        h = h + jnp.einsum('ncshd,sdf->nchf', ch[:, s:s + nc] + pe_r[s], w1_r[s])
    return jax.nn.gelu(h) @ w2 + b2


def sel_blocks(x):
    N, L = x.shape[:2]
    nsb = -(-L // SEL_BLOCK)
    xp = jnp.pad(x, ((0, 0), (0, nsb * SEL_BLOCK - L), (0, 0), (0, 0)))
    return xp.reshape(N, nsb, SEL_BLOCK, *x.shape[2:]).transpose(0, 3, 1, 2, 4)


def nsa_cmp_sel(q, qpos, kc, vc, ksb, vsb, table):
    N, T = q.shape[:2]
    nc = kc.shape[1]
    nsb = ksb.shape[2]
    cend = jnp.arange(nc) * CMP_STRIDE + (CMP_BLOCK - 1)
    dist_c = qpos[:, None] - cend[None, :]
    bias_c = table[rel_bucket(dist_c)].transpose(0, 2, 3, 1)
    s = jnp.einsum('nthgd,nchd->nthgc', q, kc).astype(jnp.float32) * SCALE + bias_c[None]
    p_cmp, _ = masked_softmax_lse(s, (dist_c >= 0)[None, :, None, None, :])
    o_cmp = jnp.einsum('nthgc,nchd->nthgd', p_cmp.astype(vc.dtype), vc)
    pc = jnp.sum(p_cmp, axis=3)
    ratio = SEL_BLOCK // CMP_STRIDE
    right = max(0, ratio * nsb - nc)
    pcp = jnp.pad(pc, ((0, 0), (0, 0), (0, 0), (CMP_SPAN - 1, right)))
    p_slc = 0.0
    for m in range(ratio):
        for n in range(CMP_SPAN):
            o = m + n
            p_slc = p_slc + pcp[..., o:o + ratio * (nsb - 1) + 1:ratio]
    jb = jnp.arange(nsb)[None, :]
    cur = (qpos // SEL_BLOCK)[:, None]
    valid = jb <= cur
    forced = (jb == 0) | (jb == cur) | (jb == cur - 1)
    score = jnp.where(forced[None, :, None, :], FORCED_SCORE,
                      jnp.where(valid[None, :, None, :], p_slc, -1.0))
    n_sel = min(SEL_TOPN, nsb)
    _, idx = lax.top_k(score, n_sel)
    ni = jnp.arange(N)[:, None, None, None]
    hi = jnp.arange(NSA_KV_HEADS)[None, None, :, None]
    kg, vg = ksb[ni, hi, idx], vsb[ni, hi, idx]
    kpos = idx[..., None] * SEL_BLOCK + jnp.arange(SEL_BLOCK)
    dist_s = qpos[None, :, None, None, None] - kpos
    bias_s = table[rel_bucket(dist_s), hi[..., None]].transpose(0, 1, 2, 5, 3, 4)
    s = jnp.einsum('nthgd,nthjsd->nthgjs', q, kg).astype(jnp.float32) * SCALE + bias_s
    nk = n_sel * SEL_BLOCK
    p, _ = masked_softmax_lse(s.reshape(N, T, NSA_KV_HEADS, NSA_GROUP, nk),
                              (dist_s >= 0).reshape(N, T, NSA_KV_HEADS, 1, nk))
    o_sel = jnp.einsum('nthgj,nthjd->nthgd', p.astype(vg.dtype), vg.reshape(N, T, NSA_KV_HEADS, nk, HEAD_DIM))
    return o_cmp, o_sel


def nsa_merge(a, gates, o_cmp, o_sel, o_win, w_out):
    N, T = a.shape[:2]
    o = (gates[:, :, 0, :, :, None] * o_cmp.astype(jnp.float32)
         + gates[:, :, 1, :, :, None] * o_sel.astype(jnp.float32)
         + gates[:, :, 2, :, :, None] * o_win.astype(jnp.float32)).astype(a.dtype)
    return o.reshape(N, T, NSA_Q) @ w_out


def nsa_prompt(a, w_in, w_out, cmp, rel_bias):
    B, S, _ = a.shape
    pe, w1, b1, w2, b2 = cmp
    q, kv, gates = nsa_project(a, w_in)
    table = rel_bias.reshape(REL_BUCKETS, NSA_KV_HEADS, NSA_GROUP).astype(jnp.float32)
    kc = compress(kv[:, :, 0], pe[0], w1[0], b1[0], w2[0], b2[0])
    vc = compress(kv[:, :, 1], pe[1], w1[1], b1[1], w2[1], b2[1])
    ksb, vsb = sel_blocks(kv[:, :, 2]), sel_blocks(kv[:, :, 3])
    nq = S // SEL_Q_CHUNK
    qs = q.reshape(B, nq, SEL_Q_CHUNK, NSA_KV_HEADS, NSA_GROUP, HEAD_DIM).swapaxes(0, 1)
    ps = jnp.arange(S, dtype=jnp.int32).reshape(nq, SEL_Q_CHUNK)

    def body(xs):
        return nsa_cmp_sel(xs[0], xs[1], kc, vc, ksb, vsb, table)

    o_cmp, o_sel = lax.map(body, (qs, ps))
    o_cmp = o_cmp.swapaxes(0, 1).reshape(B, S, NSA_KV_HEADS, NSA_GROUP, HEAD_DIM)
    o_sel = o_sel.swapaxes(0, 1).reshape(B, S, NSA_KV_HEADS, NSA_GROUP, HEAD_DIM)
    n_prev = -(-(NSA_WINDOW - 1) // BAND_BLOCK)
    bias = band_bias(rel_bias, 1, n_prev, NSA_KV_HEADS, NSA_GROUP)
    o_win, _ = banded_attn(q, kv[:, :, 4], kv[:, :, 5], bias, n_prev, NSA_WINDOW - 1)
    y = nsa_merge(a, gates, o_cmp, o_sel, o_win, w_out)
    keep = min(NSA_WINDOW, S)
    return y, kv[:, S - keep:, 4:], kv[:, :, :4]


def nsa_sample(a, pool, page_table, win_buf, w_in, w_out, cmp, rel_bias):
    N, T, _ = a.shape
    pe, w1, b1, w2, b2 = cmp
    q, kv, gates = nsa_project(a, w_in)
    table = rel_bias.reshape(REL_BUCKETS, NSA_KV_HEADS, NSA_GROUP).astype(jnp.float32)
    past = pool[page_table]
    past = past.reshape(N, -1, *past.shape[3:]).astype(kv.dtype)
    P = past.shape[1]
    full = jnp.concatenate([past, kv[:, :, :4]], axis=1)
    kc = compress(full[:, :, 0], pe[0], w1[0], b1[0], w2[0], b2[0])
    vc = compress(full[:, :, 1], pe[1], w1[1], b1[1], w2[1], b2[1])
    ksb, vsb = sel_blocks(full[:, :, 2]), sel_blocks(full[:, :, 3])
    qpos = P + jnp.arange(T, dtype=jnp.int32)
    o_cmp, o_sel = nsa_cmp_sel(q, qpos, kc, vc, ksb, vsb, table)
    lw = win_buf.shape[1]
    kw = jnp.concatenate([win_buf[:, :, 0].astype(kv.dtype), kv[:, :, 4]], axis=1)
    vw = jnp.concatenate([win_buf[:, :, 1].astype(kv.dtype), kv[:, :, 5]], axis=1)
    kpos = P - lw + jnp.arange(lw + T)
    dist = qpos[:, None] - kpos[None, :]
    mask = (dist >= 0) & (dist <= NSA_WINDOW - 1)
    bias = table[rel_bucket(dist)].transpose(2, 3, 0, 1)
    s = jnp.einsum('nthgd,njhd->nhgtj', q, kw).astype(jnp.float32) * SCALE + bias
    p, _ = masked_softmax_lse(s, mask)
    o_win = jnp.einsum('nhgtj,njhd->nthgd', p.astype(vw.dtype), vw)
    y = nsa_merge(a, gates, o_cmp, o_sel, o_win, w_out)
    new_win = jnp.concatenate([win_buf, kv[:, :, 4:].astype(win_buf.dtype)], axis=1)[:, T:]
    return y, new_win, kv[:, :, :4]


def conv_ffn(f, conv_prev, w_in, conv_w, conv_b, w_out):
    T = f.shape[1]
    hu = f @ w_in
    gate, val = hu[..., :D_FF], hu[..., D_FF:]
    ext = jnp.concatenate([conv_prev.astype(gate.dtype), gate], axis=1)
    c = conv_b
    for j in range(CONV_W):
        c = c + ext[:, j:j + T] * conv_w[j]
    return (jax.nn.gelu(c) * val) @ w_out, ext[:, T:]


def ple_add(h, p, g_norm, w_gate, w_proj):
    return h + jax.nn.sigmoid(rmsnorm(h, g_norm) @ w_gate) * (p.astype(h.dtype) @ w_proj)


def setup_inputs(seed: int = 0) -> dict:
    key = jax.random.key(seed)
    ks = iter(jax.random.split(key, 48))

    def nrm(shape, scale):
        return scale * jax.random.normal(next(ks), shape, jnp.float32)

    n_pages = PAST_LEN // PAGE_SIZE
    n_phys = (5 * DEC_BATCH * n_pages) // 4
    page_table = jax.random.permutation(next(ks), n_phys)[:DEC_BATCH * n_pages]
    page_table = page_table.reshape(DEC_BATCH, n_pages).astype(jnp.int32)
    dil_len = [min(w, PAST_LEN) for (w, _) in DIL_PAIRS]
    return {
        'x_prompt': nrm((BATCH, SEQ, D_MODEL), 1.0),
        'x_sample': nrm((DEC_BATCH, DEC_SEQ, D_MODEL), 1.0),
        'state_dil_w128': nrm((N_LAYERS_A, DEC_BATCH, dil_len[0], 2, N_HEADS, HEAD_DIM), 1.0),
        'state_dil_w512': nrm((N_LAYERS_A, DEC_BATCH, dil_len[1], 2, N_HEADS, HEAD_DIM), 1.0),
        'state_dil_w2048': nrm((N_LAYERS_A, DEC_BATCH, dil_len[2], 2, N_HEADS, HEAD_DIM), 1.0),
        'state_nsa_win': nrm((N_LAYERS_B, DEC_BATCH, min(NSA_WINDOW, PAST_LEN), 2, NSA_KV_HEADS, HEAD_DIM), 1.0),
        'state_conv': nrm((DEPTH, DEC_BATCH, CONV_W - 1, D_FF), 1.0),
        'cache_nsa_kv': nrm((N_LAYERS_B, n_phys, PAGE_SIZE, 4, NSA_KV_HEADS, HEAD_DIM), 1.0),
        'page_table': page_table,
        'p_prompt': nrm((DEPTH, BATCH, SEQ, PLE_DIM), 1.0),
        'p_sample': nrm((DEPTH, DEC_BATCH, DEC_SEQ, PLE_DIM), 1.0),
        'rel_bias': nrm((REL_BUCKETS, N_HEADS), 0.5),
        'norm_mix': 1.0 + nrm((DEPTH, D_MODEL), 0.1),
        'norm_ffn': 1.0 + nrm((DEPTH, D_MODEL), 0.1),
        'norm_ple': 1.0 + nrm((DEPTH, D_MODEL), 0.1),
        'norm_final': 1.0 + nrm((D_MODEL,), 0.1),
        'w_in_a': nrm((N_LAYERS_A, D_MODEL, QKV_A), D_MODEL ** -0.5),
        'w_out_a': nrm((N_LAYERS_A, NSA_Q, D_MODEL), NSA_Q ** -0.5),
        'w_in_b': nrm((N_LAYERS_B, D_MODEL, PROJ_B), D_MODEL ** -0.5),
        'w_out_b': nrm((N_LAYERS_B, NSA_Q, D_MODEL), NSA_Q ** -0.5),
        'cmp_pe': nrm((N_LAYERS_B, 2, CMP_BLOCK, HEAD_DIM), 0.2),
        'cmp_w1': nrm((N_LAYERS_B, 2, CMP_BLOCK, HEAD_DIM, CMP_HIDDEN), (CMP_BLOCK * HEAD_DIM) ** -0.5),
        'cmp_b1': nrm((N_LAYERS_B, 2, CMP_HIDDEN), 0.02),
        'cmp_w2': nrm((N_LAYERS_B, 2, CMP_HIDDEN, HEAD_DIM), CMP_HIDDEN ** -0.5),
        'cmp_b2': nrm((N_LAYERS_B, 2, HEAD_DIM), 0.02),
        'w_ffn_in': nrm((DEPTH, D_MODEL, 2 * D_FF), D_MODEL ** -0.5),
        'conv_w': nrm((DEPTH, CONV_W, D_FF), CONV_W ** -0.5),
        'conv_b': nrm((DEPTH, D_FF), 0.02),
        'w_ffn_out': nrm((DEPTH, D_FF, D_MODEL), D_FF ** -0.5),
        'w_ple_gate': nrm((DEPTH, D_MODEL, D_MODEL), D_MODEL ** -0.5),
        'w_ple_proj': nrm((DEPTH, PLE_DIM, D_MODEL), PLE_DIM ** -0.5),
    }


def reference(x_prompt, x_sample, state_dil_w128, state_dil_w512, state_dil_w2048, state_nsa_win,
              state_conv, cache_nsa_kv, page_table, p_prompt, p_sample, rel_bias, norm_mix, norm_ffn,
              norm_ple, norm_final, w_in_a, w_out_a, w_in_b, w_out_b, cmp_pe, cmp_w1, cmp_b1, cmp_w2,
              cmp_b2, w_ffn_in, conv_w, conv_b, w_ffn_out, w_ple_gate, w_ple_proj):
    hp, hs = x_prompt, x_sample
    B = x_prompt.shape[0]
    dil_p = [[] for _ in range(N_DIL)]
    dil_s = [[] for _ in range(N_DIL)]
    win_p, win_s, kv_p, kv_s, conv_p, conv_s = [], [], [], [], [], []
    for i in range(DEPTH):
        li = i // N_MIXERS
        ap, asm = rmsnorm(hp, norm_mix[i]), rmsnorm(hs, norm_mix[i])
        if i % N_MIXERS == 0:
            yp, bp = dilated_attn_prompt(ap, w_in_a[li], w_out_a[li], rel_bias)
            ys, bs = dilated_attn_sample(asm, (state_dil_w128[li], state_dil_w512[li], state_dil_w2048[li]),
                                         w_in_a[li], w_out_a[li], rel_bias)
            for g in range(N_DIL):
                dil_p[g].append(bp[g])
                dil_s[g].append(bs[g])
        else:
            cmp = (cmp_pe[li], cmp_w1[li], cmp_b1[li], cmp_w2[li], cmp_b2[li])
            yp, wp, rp = nsa_prompt(ap, w_in_b[li], w_out_b[li], cmp, rel_bias)
            ys, ws, rs = nsa_sample(asm, cache_nsa_kv[li], page_table, state_nsa_win[li],
                                    w_in_b[li], w_out_b[li], cmp, rel_bias)
            win_p.append(wp)
            win_s.append(ws)
            kv_p.append(rp)
            kv_s.append(rs)
        hp, hs = hp + yp, hs + ys
        zeros_prev = jnp.zeros((B, CONV_W - 1, D_FF), hp.dtype)
        fp, cp = conv_ffn(rmsnorm(hp, norm_ffn[i]), zeros_prev, w_ffn_in[i], conv_w[i], conv_b[i], w_ffn_out[i])
        fs, cs = conv_ffn(rmsnorm(hs, norm_ffn[i]), state_conv[i], w_ffn_in[i], conv_w[i], conv_b[i], w_ffn_out[i])
        conv_p.append(cp)
        conv_s.append(cs)
        hp, hs = hp + fp, hs + fs
        hp = ple_add(hp, p_prompt[i], norm_ple[i], w_ple_gate[i], w_ple_proj[i])
        hs = ple_add(hs, p_sample[i], norm_ple[i], w_ple_gate[i], w_ple_proj[i])
    y_prompt = rmsnorm(hp, norm_final)
    y_sample = rmsnorm(hs, norm_final)
    return (y_prompt, y_sample,
            jnp.stack(dil_p[0]), jnp.stack(dil_s[0]),
            jnp.stack(dil_p[1]), jnp.stack(dil_s[1]),
            jnp.stack(dil_p[2]), jnp.stack(dil_s[2]),
            jnp.stack(win_p), jnp.stack(win_s),
            jnp.stack(conv_p), jnp.stack(conv_s),
            jnp.stack(kv_p), jnp.stack(kv_s))
```

```python
import functools

import jax
import jax.numpy as jnp
from jax import lax
from jax.experimental import pallas as pl
from jax.experimental.pallas import tpu as pltpu

F32 = jnp.float32
BF16 = jnp.bfloat16

D_MODEL = 2048
HEAD_DIM = 128
N_HEADS = 16
DIL_PAIRS = ((128, 1), (512, 4), (2048, 16))
N_DIL = 3
BLK = 128
KV_HEADS = 4
GROUP = 4
CMP_BLOCK = 32
CMP_STRIDE = 16
SEL_BLOCK = 64
SEL_TOPN = 16
NSA_WINDOW = 512
D_FF = 5632
CONV_W = 3
REL_BUCKETS = 32
EPS = 1e-6
NEG = -1e30
FORCED_SCORE = 1e4
SCALE = HEAD_DIM ** -0.5
QKV_A = N_DIL * 3 * N_HEADS * HEAD_DIM
NSA_Q = N_HEADS * HEAD_DIM
NSA_KV = KV_HEADS * HEAD_DIM

BUCKET_START = (1, 2, 3, 4, 5, 6, 7, 8, 9, 10, 11, 12, 13, 14, 15, 16, 22, 30, 40, 54, 73, 99,
                134, 182, 246, 332, 450, 609, 825, 1117, 1513)

VMEM_LIMIT_V7X = 56 * 1024 * 1024


def _params(sem, vmem=VMEM_LIMIT_V7X):
    return pltpu.CompilerParams(dimension_semantics=sem, vmem_limit_bytes=vmem)


def _pick(n, cands):
    for c in cands:
        if n % c == 0:
            return c
    return n


def _rmsnorm_kernel(x_ref, g_ref, o_ref):
    x = x_ref[...]
    ms = jnp.mean(x * x, axis=-1, keepdims=True)
    o_ref[...] = ((x * lax.rsqrt(ms + EPS)) * g_ref[...]).astype(o_ref.dtype)


def _rmsnorm(x, g, out_dtype):
    m, d = x.shape
    tm = _pick(m, (512, 256, 128, 32))
    return pl.pallas_call(
        _rmsnorm_kernel,
        out_shape=jax.ShapeDtypeStruct((m, d), out_dtype),
        grid=(m // tm,),
        in_specs=[pl.BlockSpec((tm, d), lambda i: (i, 0)), pl.BlockSpec((1, d), lambda i: (0, 0))],
        out_specs=pl.BlockSpec((tm, d), lambda i: (i, 0)),
        compiler_params=_params(("parallel",)),
        name="rmsnorm",
    )(x, g.reshape(1, d))


def _mm_kernel(a_ref, w_ref, o_ref, *, sigmoid):
    acc = jnp.dot(a_ref[...], w_ref[...], preferred_element_type=F32)
    if sigmoid:
        acc = jax.nn.sigmoid(acc)
    o_ref[...] = acc.astype(o_ref.dtype)


def _matmul(a, w, out_dtype=F32, sigmoid=False):
    m, k = a.shape
    n = w.shape[1]
    tm = _pick(m, (1024, 512, 256, 128))
    tn = _pick(n, (1024, 512, 256, 128))
    return pl.pallas_call(
        functools.partial(_mm_kernel, sigmoid=sigmoid),
        out_shape=jax.ShapeDtypeStruct((m, n), out_dtype),
        grid=(m // tm, n // tn),
        in_specs=[pl.BlockSpec((tm, k), lambda i, j: (i, 0)), pl.BlockSpec((k, tn), lambda i, j: (0, j))],
        out_specs=pl.BlockSpec((tm, tn), lambda i, j: (i, j)),
        compiler_params=_params(("parallel", "arbitrary")),
        name="matmul",
    )(a, w)


def _mm_res_kernel(a_ref, w_ref, r_ref, o_ref):
    o_ref[...] = r_ref[...] + jnp.dot(a_ref[...], w_ref[...], preferred_element_type=F32)


def _matmul_residual(a, w, res):
    m, k = a.shape
    n = w.shape[1]
    tm = _pick(m, (1024, 512, 256, 128))
    tn = _pick(n, (1024, 512, 256, 128))
    return pl.pallas_call(
        _mm_res_kernel,
        out_shape=jax.ShapeDtypeStruct((m, n), F32),
        grid=(m // tm, n // tn),
        in_specs=[pl.BlockSpec((tm, k), lambda i, j: (i, 0)), pl.BlockSpec((k, tn), lambda i, j: (0, j)),
                  pl.BlockSpec((tm, tn), lambda i, j: (i, j))],
        out_specs=pl.BlockSpec((tm, tn), lambda i, j: (i, j)),
        compiler_params=_params(("parallel", "arbitrary")),
        name="matmul_residual",
    )(a, w, res)


def _conv_gelu_val(g, g1, g2, val, cw_ref, cb_ref):
    c = cb_ref[...] + g2 * cw_ref[0:1, :]
    c = c + g1 * cw_ref[1:2, :]
    c = c + g * cw_ref[2:3, :]
    return (jax.nn.gelu(c) * val).astype(BF16)


def _ffn_out_prompt_kernel(g_ref, halo_ref, v_ref, cw_ref, cb_ref, w_ref, r_ref, o_ref, acc_ref, *,
                           tiles_per_seq):
    i, k = pl.program_id(0), pl.program_id(1)

    @pl.when(k == 0)
    def _():
        acc_ref[...] = jnp.zeros_like(acc_ref)

    g = g_ref[...]
    row = lax.broadcasted_iota(jnp.int32, g.shape, 0)
    halo = jnp.where(i % tiles_per_seq == 0, 0.0, halo_ref[...])
    g1 = jnp.where(row == 0, halo[7:8, :], pltpu.roll(g, 1, 0))
    g2 = jnp.where(row == 0, halo[6:7, :], jnp.where(row == 1, halo[7:8, :], pltpu.roll(g, 2, 0)))
    u = _conv_gelu_val(g, g1, g2, v_ref[...], cw_ref, cb_ref)
    acc_ref[...] += jnp.dot(u, w_ref[...], preferred_element_type=F32)

    @pl.when(k == pl.num_programs(1) - 1)
    def _():
        o_ref[...] = r_ref[...] + acc_ref[...]


def _ffn_out_prompt(hu, conv_w, conv_b, w_out, res, seq):
    m = hu.shape[0]
    tm, tk = 1024, 512
    nk = D_FF // tk
    return pl.pallas_call(
        functools.partial(_ffn_out_prompt_kernel, tiles_per_seq=seq // tm),
        out_shape=jax.ShapeDtypeStruct((m, D_MODEL), F32),
        grid=(m // tm, nk),
        in_specs=[
            pl.BlockSpec((tm, tk), lambda i, k: (i, k)),
            pl.BlockSpec((8, tk), lambda i, k: (jnp.maximum(i * (tm // 8) - 1, 0), k)),
            pl.BlockSpec((tm, tk), lambda i, k: (i, k + nk)),
            pl.BlockSpec((CONV_W, tk), lambda i, k: (0, k)),
            pl.BlockSpec((1, tk), lambda i, k: (0, k)),
            pl.BlockSpec((tk, D_MODEL), lambda i, k: (k, 0)),
            pl.BlockSpec((tm, D_MODEL), lambda i, k: (i, 0)),
        ],
        out_specs=pl.BlockSpec((tm, D_MODEL), lambda i, k: (i, 0)),
        scratch_shapes=[pltpu.VMEM((tm, D_MODEL), F32)],
        compiler_params=_params(("parallel", "arbitrary")),
        name="ffn_out_prompt",
    )(hu, hu, hu, conv_w, conv_b.reshape(1, D_FF), w_out, res)


def _ffn_out_sample_kernel(g_ref, e1_ref, e2_ref, v_ref, cw_ref, cb_ref, w_ref, r_ref, o_ref, acc_ref, *,
                           t_len):
    k = pl.program_id(0)

    @pl.when(k == 0)
    def _():
        acc_ref[...] = jnp.zeros_like(acc_ref)

    g = g_ref[...]
    t = lax.broadcasted_iota(jnp.int32, g.shape, 0) % t_len
    g1 = jnp.where(t == 0, e1_ref[...], pltpu.roll(g, 1, 0))
    g2 = jnp.where(t < 2, e2_ref[...], pltpu.roll(g, 2, 0))
    u = _conv_gelu_val(g, g1, g2, v_ref[...], cw_ref, cb_ref)
    acc_ref[...] += jnp.dot(u, w_ref[...], preferred_element_type=F32)

    @pl.when(k == pl.num_programs(0) - 1)
    def _():
        o_ref[...] = r_ref[...] + acc_ref[...]


def _ffn_out_sample(hu, conv_prev, conv_w, conv_b, w_out, res, t_len):
    m = hu.shape[0]
    n = m // t_len
    tk = 512
    nk = D_FF // tk
    zeros = jnp.zeros((n, t_len - 1, D_FF), F32)
    e1 = jnp.concatenate([conv_prev[:, 1:2], zeros], axis=1).reshape(m, D_FF)
    e2 = jnp.concatenate([conv_prev, zeros[:, 1:]], axis=1).reshape(m, D_FF)
    return pl.pallas_call(
        functools.partial(_ffn_out_sample_kernel, t_len=t_len),
        out_shape=jax.ShapeDtypeStruct((m, D_MODEL), F32),
        grid=(nk,),
        in_specs=[
            pl.BlockSpec((m, tk), lambda k: (0, k)),
            pl.BlockSpec((m, tk), lambda k: (0, k)),
            pl.BlockSpec((m, tk), lambda k: (0, k)),
            pl.BlockSpec((m, tk), lambda k: (0, k + nk)),
            pl.BlockSpec((CONV_W, tk), lambda k: (0, k)),
            pl.BlockSpec((1, tk), lambda k: (0, k)),
            pl.BlockSpec((tk, D_MODEL), lambda k: (k, 0)),
            pl.BlockSpec((m, D_MODEL), lambda k: (0, 0)),
        ],
        out_specs=pl.BlockSpec((m, D_MODEL), lambda k: (0, 0)),
        scratch_shapes=[pltpu.VMEM((m, D_MODEL), F32)],
        compiler_params=_params(("arbitrary",)),
        name="ffn_out_sample",
    )(hu, e1, e2, hu, conv_w, conv_b.reshape(1, D_FF), w_out, res)


def _ple_kernel(a_ref, wg_ref, p_ref, wp_ref, h_ref, o_ref):
    gate = jax.nn.sigmoid(jnp.dot(a_ref[...], wg_ref[...], preferred_element_type=F32))
    proj = jnp.dot(p_ref[...], wp_ref[...], preferred_element_type=F32)
    o_ref[...] = h_ref[...] + gate * proj


def _ple_add(a, w_gate, p, w_proj, h):
    m, d = h.shape
    kp = p.shape[1]
    tm = _pick(m, (1024, 512, 256, 128))
    tn = 1024
    return pl.pallas_call(
        _ple_kernel,
        out_shape=jax.ShapeDtypeStruct((m, d), F32),
        grid=(m // tm, d // tn),
        in_specs=[
            pl.BlockSpec((tm, d), lambda i, j: (i, 0)),
            pl.BlockSpec((d, tn), lambda i, j: (0, j)),
            pl.BlockSpec((tm, kp), lambda i, j: (i, 0)),
            pl.BlockSpec((kp, tn), lambda i, j: (0, j)),
            pl.BlockSpec((tm, tn), lambda i, j: (i, j)),
        ],
        out_specs=pl.BlockSpec((tm, tn), lambda i, j: (i, j)),
        compiler_params=_params(("parallel", "arbitrary")),
        name="ple_add",
    )(a, w_gate, p, w_proj, h)


def _bias_table_kernel(rb_ref, o_ref, *, a0, ag, ar, ac, lo, hi, mod):
    h, g = pl.program_id(0), pl.program_id(1)
    shape = o_ref.shape[2:]
    dist = (a0 + ag * g + ar * lax.broadcasted_iota(jnp.int32, shape, 0)
            + ac * lax.broadcasted_iota(jnp.int32, shape, 1))
    d = jnp.maximum(dist, 0)
    out = jnp.full(shape, rb_ref[0, h], F32)
    for k in range(1, REL_BUCKETS):
        out = jnp.where(d >= BUCKET_START[k - 1], rb_ref[k, h], out)
    ok = (dist >= lo) & (dist <= hi)
    if mod > 1:
        ok = ok & ((d & (mod - 1)) == 0)
    o_ref[0, 0] = jnp.where(ok, out, NEG)


def _bias_table(rel_bias, n_g, n_r, n_c, a0, ag, ar, ac, lo, hi, mod=1):
    assert mod & (mod - 1) == 0
    return pl.pallas_call(
        functools.partial(_bias_table_kernel, a0=a0, ag=ag, ar=ar, ac=ac, lo=lo, hi=hi, mod=mod),
        out_shape=jax.ShapeDtypeStruct((N_HEADS, n_g, n_r, n_c), F32),
        grid=(N_HEADS, n_g),
        in_specs=[pl.BlockSpec(memory_space=pltpu.SMEM)],
        out_specs=pl.BlockSpec((1, 1, n_r, n_c), lambda h, g: (h, g, 0, 0)),
        compiler_params=_params(("parallel", "parallel")),
        name="bias_table",
    )(rel_bias)


def _dil_prompt_kernel(q_ref, kc_ref, kp_ref, vc_ref, vp_ref, bias_ref, o_ref, lse_ref):
    first = pl.program_id(2) == 0
    col = lax.broadcasted_iota(jnp.int32, (BLK, 2 * BLK), 1)
    edge = jnp.where(first & (col < BLK), NEG, 0.0)
    lane = lax.broadcasted_iota(jnp.int32, (BLK, BLK), 1)

    def head(h, lse_all):
        sl = pl.ds(pl.multiple_of(h * HEAD_DIM, HEAD_DIM), HEAD_DIM)
        q = q_ref[0, :, sl].astype(BF16)
        k = jnp.concatenate([kp_ref[0, :, sl], kc_ref[0, :, sl]], axis=0).astype(BF16)
        v = jnp.concatenate([vp_ref[0, :, sl], vc_ref[0, :, sl]], axis=0).astype(BF16)
        s = lax.dot_general(q, k, (((1,), (1,)), ((), ())), preferred_element_type=F32)
        s = s * SCALE + bias_ref[h] + edge
        m = jnp.max(s, axis=-1, keepdims=True)
        p = jnp.exp(s - m)
        l = jnp.sum(p, axis=-1, keepdims=True)
        o_ref[0, :, sl] = jnp.dot(p.astype(BF16), v, preferred_element_type=F32) / l
        return jnp.where(lane == h, m + jnp.log(l), lse_all)

    lse_ref[0] = lax.fori_loop(0, N_HEADS, head, jnp.zeros((BLK, BLK), F32))


def _dil_prompt_group(qkv, bias, grp, dil):
    b, s, _ = qkv.shape
    ln = s // dil
    nb = ln // BLK
    x = qkv.reshape(b, ln, dil * QKV_A)
    wide = N_HEADS * HEAD_DIM
    per_r = QKV_A // wide

    def col(part):
        return lambda bi, r, mi: (bi, mi, r * per_r + grp * 3 + part)

    def col_prev(part):
        return lambda bi, r, mi: (bi, jnp.maximum(mi - 1, 0), r * per_r + grp * 3 + part)

    blk = (1, BLK, wide)
    o, lse = pl.pallas_call(
        _dil_prompt_kernel,
        out_shape=(jax.ShapeDtypeStruct((b, ln, dil * wide), F32),
                   jax.ShapeDtypeStruct((b, ln, dil * BLK), F32)),
        grid=(b, dil, nb),
        in_specs=[pl.BlockSpec(blk, col(0)), pl.BlockSpec(blk, col(1)), pl.BlockSpec(blk, col_prev(1)),
                  pl.BlockSpec(blk, col(2)), pl.BlockSpec(blk, col_prev(2)),
                  pl.BlockSpec((N_HEADS, BLK, 2 * BLK), lambda bi, r, mi: (0, 0, 0))],
        out_specs=(pl.BlockSpec(blk, lambda bi, r, mi: (bi, mi, r)),
                   pl.BlockSpec((1, BLK, BLK), lambda bi, r, mi: (bi, mi, r))),
        compiler_params=_params(("parallel", "parallel", "arbitrary")),
        name=f"dil_attn_prompt_g{grp}",
    )(x, x, x, x, x, bias)
    return o.reshape(b, s, wide), lse.reshape(b, s, BLK)


def _dil_combine_kernel(o0_ref, o1_ref, o2_ref, l0_ref, l1_ref, l2_ref, o_ref):
    l0, l1, l2 = l0_ref[...], l1_ref[...], l2_ref[...]
    mx = jnp.maximum(jnp.maximum(l0, l1), l2)
    e0, e1, e2 = jnp.exp(l0 - mx), jnp.exp(l1 - mx), jnp.exp(l2 - mx)
    den = e0 + e1 + e2
    w0, w1, w2 = e0 / den, e1 / den, e2 / den
    for h in range(N_HEADS):
        sl = slice(h * HEAD_DIM, (h + 1) * HEAD_DIM)
        o = (w0[:, h:h + 1] * o0_ref[:, sl] + w1[:, h:h + 1] * o1_ref[:, sl]) + w2[:, h:h + 1] * o2_ref[:, sl]
        o_ref[:, sl] = o.astype(o_ref.dtype)


def _dil_combine(outs, lses):
    m, wide = outs[0].shape
    tm = _pick(m, (256, 128, 32))
    ob = pl.BlockSpec((tm, wide), lambda i: (i, 0))
    lb = pl.BlockSpec((tm, BLK), lambda i: (i, 0))
    return pl.pallas_call(
        _dil_combine_kernel,
        out_shape=jax.ShapeDtypeStruct((m, wide), BF16),
        grid=(m // tm,),
        in_specs=[ob, ob, ob, lb, lb, lb],
        out_specs=ob,
        compiler_params=_params(("parallel",)),
        name="dil_combine",
    )(*outs, *lses)


def _dil_sample_kernel(q_ref, k_ref, v_ref, kn_ref, vn_ref, bias_ref, biasn_ref, o_ref, lse_ref,
                       m_scr, l_scr, acc_scr):
    c = pl.program_id(1)
    last = pl.num_programs(1) - 1

    @pl.when(c == 0)
    def _():
        m_scr[...] = jnp.full_like(m_scr, NEG)
        l_scr[...] = jnp.zeros_like(l_scr)
        acc_scr[...] = jnp.zeros_like(acc_scr)

    def step(h, k, v, bias):
        sl = pl.ds(pl.multiple_of(h * HEAD_DIM, HEAD_DIM), HEAD_DIM)
        q = q_ref[0, :, sl].astype(BF16)
        s = lax.dot_general(q, k.astype(BF16), (((1,), (1,)), ((), ())), preferred_element_type=F32)
        s = s * SCALE + bias
        m_old = m_scr[h]
        m_new = jnp.maximum(m_old, jnp.max(s, axis=-1, keepdims=True))
        alpha = jnp.exp(m_old - m_new)
        p = jnp.where(bias > 0.5 * NEG, jnp.exp(s - m_new), 0.0)
        l_scr[h] = alpha * l_scr[h] + jnp.sum(p, axis=-1, keepdims=True)
        acc_scr[h] = alpha * acc_scr[h] + jnp.dot(p.astype(BF16), v.astype(BF16), preferred_element_type=F32)
        m_scr[h] = m_new

    def head(h, carry):
        sl = pl.ds(pl.multiple_of(h * HEAD_DIM, HEAD_DIM), HEAD_DIM)
        step(h, k_ref[0, :, sl], v_ref[0, :, sl], bias_ref[h])
        return carry

    lax.fori_loop(0, N_HEADS, head, 0)

    @pl.when(c == last)
    def _():
        lane = lax.broadcasted_iota(jnp.int32, (8, BLK), 1)

        def fin(h, lse_all):
            sl = pl.ds(pl.multiple_of(h * HEAD_DIM, HEAD_DIM), HEAD_DIM)
            step(h, kn_ref[0, :, sl], vn_ref[0, :, sl], biasn_ref[h])
            l = jnp.maximum(l_scr[h], 1e-30)
            o_ref[0, :, sl] = acc_scr[h] / l
            return jnp.where(lane == h, m_scr[h] + jnp.log(l), lse_all)

        lse_ref[0] = lax.fori_loop(0, N_HEADS, fin, jnp.zeros((8, BLK), F32))


def _dil_sample_group(q, kn, vn, buf, bias, bias_new):
    n, lb, _ = buf.shape
    wide = N_HEADS * HEAD_DIM
    ck = min(lb, 512)
    o, lse = pl.pallas_call(
        _dil_sample_kernel,
        out_shape=(jax.ShapeDtypeStruct((n, 8, wide), F32), jax.ShapeDtypeStruct((n, 8, BLK), F32)),
        grid=(n, lb // ck),
        in_specs=[
            pl.BlockSpec((1, 8, wide), lambda i, c: (i, 0, 0)),
            pl.BlockSpec((1, ck, wide), lambda i, c: (i, c, 0)),
            pl.BlockSpec((1, ck, wide), lambda i, c: (i, c, 1)),
            pl.BlockSpec((1, BLK, wide), lambda i, c: (i, 0, 0)),
            pl.BlockSpec((1, BLK, wide), lambda i, c: (i, 0, 0)),
            pl.BlockSpec((N_HEADS, 8, ck), lambda i, c: (0, 0, c)),
            pl.BlockSpec((N_HEADS, 8, BLK), lambda i, c: (0, 0, 0)),
        ],
        out_specs=(pl.BlockSpec((1, 8, wide), lambda i, c: (i, 0, 0)),
                   pl.BlockSpec((1, 8, BLK), lambda i, c: (i, 0, 0))),
        scratch_shapes=[pltpu.VMEM((N_HEADS, 8, 1), F32), pltpu.VMEM((N_HEADS, 8, 1), F32),
                        pltpu.VMEM((N_HEADS, 8, HEAD_DIM), F32)],
        compiler_params=_params(("parallel", "arbitrary")),
        name="dil_attn_sample",
    )(q, buf, buf, kn, vn, bias, bias_new)
    return o, lse


def _pad_rows(x, rows):
    return jnp.pad(x, ((0, 0), (0, rows - x.shape[1]), (0, 0)))


def _layer_a(hp, hs, b, s, n, t, norm_g, w_in, w_out, rel_bias, bufs):
    wide = N_HEADS * HEAD_DIM
    qkv_p = _matmul(_rmsnorm(hp, norm_g, BF16), w_in).reshape(b, s, QKV_A)
    qkv_s = _matmul(_rmsnorm(hs, norm_g, BF16), w_in).reshape(n, t, QKV_A)
    outs_p, lses_p, outs_s, lses_s, new_p, new_s = [], [], [], [], [], []
    for grp, (win, dil) in enumerate(DIL_PAIRS):
        base = grp * 3 * wide
        bias = _bias_table(rel_bias, 1, BLK, 2 * BLK, BLK * dil, 0, dil, -dil, 0, win).reshape(N_HEADS, BLK, 2 * BLK)
        o, lse = _dil_prompt_group(qkv_p, bias, grp, dil)
        outs_p.append(o.reshape(b * s, wide))
        lses_p.append(lse.reshape(b * s, BLK))
        keep = min(win, s)
        new_p.append(qkv_p[:, s - keep:, base + wide:base + 3 * wide].reshape(b, keep, 2, N_HEADS, HEAD_DIM))
        buf = bufs[grp]
        lb = buf.shape[1]
        bias_buf = _bias_table(rel_bias, 1, 8, lb, lb, 0, 1, -1, 0, win, dil).reshape(N_HEADS, 8, lb)
        bias_new = _bias_table(rel_bias, 1, 8, BLK, 0, 0, 1, -1, 0, win, dil).reshape(N_HEADS, 8, BLK)
        q = _pad_rows(qkv_s[:, :, base:base + wide], 8)
        kn = _pad_rows(qkv_s[:, :, base + wide:base + 2 * wide], BLK)
        vn = _pad_rows(qkv_s[:, :, base + 2 * wide:base + 3 * wide], BLK)
        o, lse = _dil_sample_group(q, kn, vn, buf.reshape(n, lb, 2 * wide), bias_buf, bias_new)
        outs_s.append(o[:, :t].reshape(n * t, wide))
        lses_s.append(lse[:, :t].reshape(n * t, BLK))
        kv_new = qkv_s[:, :, base + wide:base + 3 * wide].reshape(n, t, 2, N_HEADS, HEAD_DIM)
        new_s.append(jnp.concatenate([buf, kv_new], axis=1)[:, t:])
    hp = _matmul_residual(_dil_combine(outs_p, lses_p), w_out, hp)
    hs = _matmul_residual(_dil_combine(outs_s, lses_s), w_out, hs)
    return hp, hs, new_p, new_s


PAGES_PER_STEP = 16
CHUNKS_PER_PAGE = BLK // CMP_STRIDE
NT_DIMS = (((1,), (1,)), ((), ()))
TN_DIMS = (((0,), (0,)), ((), ()))


def _page_spec(p, col_block):
    return pl.BlockSpec((1, BLK, 2 * NSA_KV),
                        lambda i, j, *rest: (rest[-1][i, j * PAGES_PER_STEP + p], 0, col_block))


def _cmp_proj_kernel(pt_ref, *refs):
    pages = refs[:PAGES_PER_STEP]
    perm_ref, w_ref, o_ref = refs[PAGES_PER_STEP:PAGES_PER_STEP + 3]
    xs = [jnp.dot(perm_ref[...], pg[0].astype(BF16), preferred_element_type=F32) for pg in pages]
    for c in range(2):
        acc = jnp.zeros((KV_HEADS * BLK, 2 * HEAD_DIM), F32)
        for t in range(CMP_STRIDE):
            rows = [x[t * CHUNKS_PER_PAGE:(t + 1) * CHUNKS_PER_PAGE,
                      c * NSA_KV + kh * HEAD_DIM:c * NSA_KV + (kh + 1) * HEAD_DIM]
                    for kh in range(KV_HEADS) for x in xs]
            lhs = jnp.concatenate(rows, axis=0).astype(BF16)
            acc = acc + jnp.dot(lhs, w_ref[c, t], preferred_element_type=F32)
        for kh in range(KV_HEADS):
            o_ref[0, c, kh] = acc[kh * BLK:(kh + 1) * BLK]


def _cmp_proj(pages, page_table, w1r):
    n, n_pages = page_table.shape
    chunks = n_pages * CHUNKS_PER_PAGE
    r = jnp.arange(BLK)
    perm = (r[None, :] == (r[:, None] % CHUNKS_PER_PAGE) * CMP_STRIDE + r[:, None] // CHUNKS_PER_PAGE).astype(BF16)
    grid_spec = pltpu.PrefetchScalarGridSpec(
        num_scalar_prefetch=1,
        grid=(n, n_pages // PAGES_PER_STEP),
        in_specs=[_page_spec(p, 0) for p in range(PAGES_PER_STEP)]
        + [pl.BlockSpec((BLK, BLK), lambda i, j, pt: (0, 0)),
           pl.BlockSpec((2, CMP_STRIDE, HEAD_DIM, 2 * HEAD_DIM), lambda i, j, pt: (0, 0, 0, 0))],
        out_specs=pl.BlockSpec((1, 2, KV_HEADS, BLK, 2 * HEAD_DIM), lambda i, j, pt: (i, 0, 0, j, 0)),
    )
    return pl.pallas_call(
        _cmp_proj_kernel,
        out_shape=jax.ShapeDtypeStruct((n, 2, KV_HEADS, chunks, 2 * HEAD_DIM), F32),
        grid_spec=grid_spec,
        compiler_params=_params(("parallel", "arbitrary")),
        name="nsa_cmp_proj",
    )(page_table, *([pages] * PAGES_PER_STEP), perm, w1r)


def _finish_compress(a, pe_row, w1f, b1, w2, b2, n_blocks):
    rows = a.shape[0]
    cst = jnp.dot(pe_row, w1f, preferred_element_type=F32)[0:1]
    h = (b1 + cst) + a[:, :HEAD_DIM] + pltpu.roll(a[:, HEAD_DIM:], rows - 1, 0)
    x = jnp.dot(jax.nn.gelu(h).astype(BF16), w2, preferred_element_type=F32) + b2
    return jnp.where(lax.broadcasted_iota(jnp.int32, x.shape, 0) < n_blocks, x, 0.0)


def _split3(x):
    hi = x.astype(BF16)
    r = x - hi.astype(F32)
    mid = r.astype(BF16)
    return hi, mid, (r - mid.astype(F32)).astype(BF16)


def _top_n(score, n, axis):
    idx = lax.broadcasted_iota(jnp.int32, score.shape, axis).astype(F32)
    big = float(score.shape[axis])

    def body(_, carry):
        sc, sel = carry
        mx = jnp.max(sc, axis=axis, keepdims=True)
        first = jnp.min(jnp.where(sc == mx, idx, big), axis=axis, keepdims=True)
        hit = idx == first
        return jnp.where(hit, -jnp.inf, sc), jnp.where(hit, 1.0, sel)

    return lax.fori_loop(0, n, body, (score, jnp.zeros(score.shape, F32)))[1]


def _sel_scores(p_slc, blk, cur, n_blocks):
    forced = (blk == 0) | (blk == cur) | (blk == cur - 1)
    score = jnp.where(forced, FORCED_SCORE, jnp.where(blk <= cur, p_slc, -1.0))
    return jnp.where(blk < n_blocks, score, -2.0)


def _nsa_prompt_kernel(q_ref, a_ref, pe_ref, w1f_ref, b1_ref, w2_ref, b2_ref, ksel_ref, vselt_ref, kwin_ref,
                       vwint_ref, tbl_ref, biasc_ref, gates_ref, msel_ref, o_ref, kc_scr, vc_scr, sel_scr, *,
                       n_cmp, n_sel):
    i = pl.program_id(2)

    @pl.when(i == 0)
    def _():
        for c, scr in ((0, kc_scr), (1, vc_scr)):
            scr[...] = _finish_compress(a_ref[0, c, 0], pe_ref[c], w1f_ref[c], b1_ref[c], w2_ref[c], b2_ref[c],
                                        n_cmp).astype(BF16)

    q = q_ref[...]
    qs = jnp.concatenate([q[:, g * HEAD_DIM:(g + 1) * HEAD_DIM] for g in range(GROUP)], axis=0).astype(BF16)
    key_i = lax.broadcasted_iota(jnp.int32, (BLK, BLK), 0)
    tok_i = lax.broadcasted_iota(jnp.int32, (BLK, BLK), 1)

    def lanes4(x):
        return jnp.concatenate([x] * GROUP, axis=1)

    def table(delta):
        return jnp.concatenate([tbl_ref[g, delta] for g in range(GROUP)], axis=1)

    mask_c = lanes4(i * BLK + tok_i - (key_i * CMP_STRIDE + (CMP_BLOCK - 1)) >= 0)
    s = lax.dot_general(kc_scr[...], qs, NT_DIMS, preferred_element_type=F32) * SCALE
    s = jnp.where(mask_c, s + jnp.concatenate([biasc_ref[g, 0] for g in range(GROUP)], axis=1), NEG)
    m = jnp.max(s, axis=0, keepdims=True)
    p = jnp.where(mask_c, jnp.exp(s - m), 0.0)
    pn = p / jnp.maximum(jnp.sum(p, axis=0, keepdims=True), 1e-30)
    o_cmp = lax.dot_general(vc_scr[...], pn.astype(BF16), TN_DIMS, preferred_element_type=F32)
    pc = ((pn[:, 0:BLK] + pn[:, BLK:2 * BLK]) + pn[:, 2 * BLK:3 * BLK]) + pn[:, 3 * BLK:4 * BLK]

    msel = msel_ref[...]
    p_slc = sum(jnp.dot(msel, part, preferred_element_type=F32) for part in _split3(pc))
    blk = lax.broadcasted_iota(jnp.int32, p_slc.shape, 0)
    cur = (i * BLK + lax.broadcasted_iota(jnp.int32, p_slc.shape, 1)) // SEL_BLOCK
    sel_scr[...] = _top_n(_sel_scores(p_slc, blk, cur, n_sel), SEL_TOPN, 0)

    def attend(k_ref, vt_ref, c, delta, mask, carry):
        m_run, l_run, acc = carry
        off = pl.multiple_of(c * BLK, BLK)
        k = k_ref[pl.ds(off, BLK), :].astype(BF16)
        mask = lanes4(mask)
        s = lax.dot_general(k, qs, NT_DIMS, preferred_element_type=F32) * SCALE + table(delta)
        s = jnp.where(mask, s, NEG)
        m_new = jnp.maximum(m_run, jnp.max(s, axis=0, keepdims=True))
        alpha = jnp.exp(m_run - m_new)
        p = jnp.where(mask, jnp.exp(s - m_new), 0.0)
        l_new = alpha * l_run + jnp.sum(p, axis=0, keepdims=True)
        vt = vt_ref[0, 0, :, pl.ds(off, BLK)]
        return m_new, l_new, alpha * acc + jnp.dot(vt, p.astype(BF16), preferred_element_type=F32)

    init = (jnp.full((1, GROUP * BLK), NEG, F32), jnp.zeros((1, GROUP * BLK), F32),
            jnp.zeros((HEAD_DIM, GROUP * BLK), F32))

    def sel_step(c, carry):
        dist = (i - c) * BLK + tok_i - key_i
        picked = jnp.where(key_i < SEL_BLOCK, sel_scr[pl.ds(2 * c, 1), :], sel_scr[pl.ds(2 * c + 1, 1), :])
        return attend(ksel_ref, vselt_ref, c, i - c, (dist >= 0) & (picked > 0.5), carry)

    _, l_sel, acc_sel = lax.fori_loop(0, i + 1, sel_step, init)
    o_sel = acc_sel / jnp.maximum(l_sel, 1e-30)

    def win_step(delta, carry):
        dist = delta * BLK + tok_i - key_i
        return attend(kwin_ref, vwint_ref, i - delta, delta, (dist >= 0) & (dist <= NSA_WINDOW - 1), carry)

    n_win = jnp.minimum(i, (NSA_WINDOW - 1 + BLK - 1) // BLK) + 1
    _, l_win, acc_win = lax.fori_loop(0, n_win, win_step, init)
    o_win = acc_win / jnp.maximum(l_win, 1e-30)

    gt = gates_ref[0]

    def gate(branch):
        return jnp.concatenate([gt[branch * GROUP + g:branch * GROUP + g + 1, :] for g in range(GROUP)], axis=1)

    o = (gate(0) * o_cmp + gate(1) * o_sel) + gate(2) * o_win
    for g in range(GROUP):
        o_ref[0, g] = o[:, g * BLK:(g + 1) * BLK]


def _nsa_prompt(proj, a_cmp, gates_t, cmp_w, tbl, bias_c, msel, vsel_t, vwin_t, b, s):
    pe, w1f, b1, w2, b2 = cmp_w
    nq = s // BLK
    kcol = NSA_Q // HEAD_DIM
    const = lambda shape: pl.BlockSpec(shape, lambda bi, kh, i: (0,) * len(shape))
    return pl.pallas_call(
        functools.partial(_nsa_prompt_kernel, n_cmp=s // CMP_STRIDE - 1, n_sel=s // SEL_BLOCK),
        out_shape=jax.ShapeDtypeStruct((b, N_HEADS, HEAD_DIM, s), F32),
        grid=(b, KV_HEADS, nq),
        in_specs=[
            pl.BlockSpec((BLK, GROUP * HEAD_DIM), lambda bi, kh, i: (bi * nq + i, kh)),
            pl.BlockSpec((1, 2, 1, s // CMP_STRIDE, 2 * HEAD_DIM), lambda bi, kh, i: (bi, 0, kh, 0, 0)),
            const(pe.shape), const(w1f.shape), const(b1.shape), const(w2.shape), const(b2.shape),
            pl.BlockSpec((s, HEAD_DIM), lambda bi, kh, i: (bi, kcol + 2 * KV_HEADS + kh)),
            pl.BlockSpec((1, 1, HEAD_DIM, s), lambda bi, kh, i: (bi, kh, 0, 0)),
            pl.BlockSpec((s, HEAD_DIM), lambda bi, kh, i: (bi, kcol + 4 * KV_HEADS + kh)),
            pl.BlockSpec((1, 1, HEAD_DIM, s), lambda bi, kh, i: (bi, kh, 0, 0)),
            pl.BlockSpec((GROUP, nq, BLK, BLK), lambda bi, kh, i: (kh, 0, 0, 0)),
            pl.BlockSpec((GROUP, 1, s // CMP_STRIDE, BLK), lambda bi, kh, i: (kh, 0, 0, i)),
            pl.BlockSpec((1, 16, BLK), lambda bi, kh, i: (kh, 0, bi * nq + i)),
            const(msel.shape),
        ],
        out_specs=pl.BlockSpec((1, GROUP, HEAD_DIM, BLK), lambda bi, kh, i: (bi, kh, 0, i)),
        scratch_shapes=[pltpu.VMEM((s // CMP_STRIDE, HEAD_DIM), BF16), pltpu.VMEM((s // CMP_STRIDE, HEAD_DIM), BF16),
                        pltpu.VMEM((s // SEL_BLOCK, BLK), F32)],
        compiler_params=_params(("parallel", "parallel", "arbitrary")),
        name="nsa_prompt",
    )(proj, a_cmp, pe, w1f, b1, w2, b2, proj, vsel_t, proj, vwin_t, tbl, bias_c, gates_t, msel)


def _masked_softmax_rows(s, mask):
    s = jnp.where(mask, s, NEG)
    m = jnp.max(s, axis=-1, keepdims=True)
    p = jnp.where(mask, jnp.exp(s - m), 0.0)
    return p / jnp.maximum(jnp.sum(p, axis=-1, keepdims=True), 1e-30)


def _nsa_sample_select_kernel(q_ref, a_ref, pe_ref, w1f_ref, b1_ref, w2_ref, b2_ref, biasc_ref, msel_ref,
                              regroup_ref, ocmp_ref, sel_ref, *, n_cmp, n_sel, past):
    pcs = []
    for kh in range(KV_HEADS):
        kc, vc = (_finish_compress(a_ref[0, c, kh], pe_ref[c], w1f_ref[c], b1_ref[c], w2_ref[c], b2_ref[c],
                                   n_cmp).astype(BF16) for c in range(2))
        bias = biasc_ref[kh]
        s = lax.dot_general(q_ref[0, kh].astype(BF16), kc, NT_DIMS, preferred_element_type=F32) * SCALE + bias
        pn = _masked_softmax_rows(s, bias > 0.5 * NEG)
        ocmp_ref[0, kh] = jnp.dot(pn.astype(BF16), vc, preferred_element_type=F32)
        pcs.append(((pn[0:8] + pn[8:16]) + pn[16:24]) + pn[24:32])
    pc = jnp.concatenate(pcs, axis=0)
    msel = msel_ref[...]
    p_slc = sum(jnp.dot(part, msel, preferred_element_type=F32) for part in _split3(pc))
    blk = lax.broadcasted_iota(jnp.int32, p_slc.shape, 1)
    cur = (past + lax.broadcasted_iota(jnp.int32, p_slc.shape, 0) % 8) // SEL_BLOCK
    sel = _top_n(_sel_scores(p_slc, blk, cur, n_sel), SEL_TOPN, 1).astype(BF16)
    for j in range(regroup_ref.shape[0]):
        part = jnp.dot(sel, regroup_ref[j], preferred_element_type=F32)
        for kh in range(KV_HEADS):
            sel_ref[0, kh, j] = part[kh * 8:(kh + 1) * 8]


def _nsa_sample_select(q, a_cmp, cmp_w, bias_c, msel, regroup, past, t_len):
    pe, w1f, b1, w2, b2 = cmp_w
    n = q.shape[0]
    chunks = a_cmp.shape[3]
    n_steps = regroup.shape[0]
    const = lambda shape: pl.BlockSpec(shape, lambda i: (0,) * len(shape))
    return pl.pallas_call(
        functools.partial(_nsa_sample_select_kernel, n_cmp=chunks - 1, n_sel=(past + t_len + SEL_BLOCK - 1) // SEL_BLOCK,
                          past=past),
        out_shape=(jax.ShapeDtypeStruct((n, KV_HEADS, GROUP * 8, HEAD_DIM), F32),
                   jax.ShapeDtypeStruct((n, KV_HEADS, n_steps, 8, BLK), F32)),
        grid=(n,),
        in_specs=[
            pl.BlockSpec((1, KV_HEADS, GROUP * 8, HEAD_DIM), lambda i: (i, 0, 0, 0)),
            pl.BlockSpec((1, 2, KV_HEADS, chunks, 2 * HEAD_DIM), lambda i: (i, 0, 0, 0, 0)),
            const(pe.shape), const(w1f.shape), const(b1.shape), const(w2.shape), const(b2.shape),
            const(bias_c.shape), const(msel.shape), const(regroup.shape),
        ],
        out_specs=(pl.BlockSpec((1, KV_HEADS, GROUP * 8, HEAD_DIM), lambda i: (i, 0, 0, 0)),
                   pl.BlockSpec((1, KV_HEADS, n_steps, 8, BLK), lambda i: (i, 0, 0, 0, 0))),
        compiler_params=_params(("parallel",)),
        name="nsa_sample_select",
    )(q, a_cmp, pe, w1f, b1, w2, b2, bias_c, msel, regroup)


def _nsa_sample_attend_kernel(pt_ref, *refs):
    pages = refs[:PAGES_PER_STEP]
    (q_ref, sel_ref, expand_ref, bias_ref, kn_ref, vn_ref, biasn_ref, win_ref, kwn_ref, vwn_ref, biasw_ref,
     biaswn_ref, gates_ref, ocmp_ref, o_ref, m_scr, l_scr, acc_scr) = refs[PAGES_PER_STEP:]
    j, kh = pl.program_id(1), pl.program_id(2)
    n_steps = pl.num_programs(1)
    qs = q_ref[0, 0].astype(BF16)

    @pl.when(j == 0)
    def _():
        m_scr[kh] = jnp.full((GROUP * 8, 1), NEG, F32)
        l_scr[kh] = jnp.zeros((GROUP * 8, 1), F32)
        acc_scr[kh] = jnp.zeros((GROUP * 8, HEAD_DIM), F32)

    def picked(step):
        sel = sel_ref[0, 0, step].astype(BF16)
        return jnp.concatenate([sel] * GROUP, axis=0)

    def update(k, v, bias, mask):
        s = lax.dot_general(qs, k, NT_DIMS, preferred_element_type=F32) * SCALE + bias
        s = jnp.where(mask, s, NEG)
        m_old = m_scr[kh]
        m_new = jnp.maximum(m_old, jnp.max(s, axis=-1, keepdims=True))
        alpha = jnp.exp(m_old - m_new)
        p = jnp.where(mask, jnp.exp(s - m_new), 0.0)
        l_scr[kh] = alpha * l_scr[kh] + jnp.sum(p, axis=-1, keepdims=True)
        acc_scr[kh] = alpha * acc_scr[kh] + jnp.dot(p.astype(BF16), v, preferred_element_type=F32)
        m_scr[kh] = m_new

    koff = pl.multiple_of(kh * HEAD_DIM, HEAD_DIM)
    voff = pl.multiple_of(NSA_KV + kh * HEAD_DIM, HEAD_DIM)
    k = jnp.concatenate([pg[0, :, pl.ds(koff, HEAD_DIM)] for pg in pages], axis=0).astype(BF16)
    v = jnp.concatenate([pg[0, :, pl.ds(voff, HEAD_DIM)] for pg in pages], axis=0).astype(BF16)
    in_sel = jnp.dot(picked(j), expand_ref[...], preferred_element_type=F32) > 0.5
    update(k, v, bias_ref[0], in_sel)

    @pl.when(j == n_steps - 1)
    def _():
        biasn = biasn_ref[0]
        new_sel = picked(n_steps)[:, 0:1] > 0.5
        update(kn_ref[0, 0].astype(BF16), vn_ref[0, 0].astype(BF16), biasn, (biasn > 0.5 * NEG) & new_sel)
        o_sel = acc_scr[kh] / jnp.maximum(l_scr[kh], 1e-30)
        biasw = jnp.concatenate([biasw_ref[0], biaswn_ref[0]], axis=1)
        kw = jnp.concatenate([win_ref[0, :, pl.ds(koff, HEAD_DIM)], kwn_ref[0, 0]], axis=0).astype(BF16)
        vw = jnp.concatenate([win_ref[0, :, pl.ds(voff, HEAD_DIM)], vwn_ref[0, 0]], axis=0).astype(BF16)
        sw = lax.dot_general(qs, kw, NT_DIMS, preferred_element_type=F32) * SCALE + biasw
        pw = _masked_softmax_rows(sw, biasw > 0.5 * NEG)
        o_win = jnp.dot(pw.astype(BF16), vw, preferred_element_type=F32)
        o_ref[0, kh] = (gates_ref[0, 0, 0] * ocmp_ref[0, 0] + gates_ref[1, 0, 0] * o_sel) + gates_ref[2, 0, 0] * o_win


def _nsa_sample_attend(pool, page_table, q, sel, expand, bias_sel, k_new, v_new, bias_new, win_buf, kw_new, vw_new,
                       bias_win, bias_win_new, gates, o_cmp):
    n, n_pages = page_table.shape
    n_steps = n_pages // PAGES_PER_STEP
    keys = PAGES_PER_STEP * BLK
    rows = GROUP * 8
    lw = win_buf.shape[1]
    per = lambda shape: pl.BlockSpec((1, 1) + shape, lambda i, j, kh, pt: (i, kh) + (0,) * len(shape))
    by_head = lambda shape: pl.BlockSpec((1,) + shape, lambda i, j, kh, pt: (kh,) + (0,) * len(shape))
    grid_spec = pltpu.PrefetchScalarGridSpec(
        num_scalar_prefetch=1,
        grid=(n, n_steps, KV_HEADS),
        in_specs=[_page_spec(p, 1) for p in range(PAGES_PER_STEP)] + [
            per((rows, HEAD_DIM)),
            pl.BlockSpec((1, 1, n_steps + 1, 8, BLK), lambda i, j, kh, pt: (i, kh, 0, 0, 0)),
            pl.BlockSpec(expand.shape, lambda i, j, kh, pt: (0, 0)),
            pl.BlockSpec((1, rows, keys), lambda i, j, kh, pt: (kh, 0, j)),
            per((BLK, HEAD_DIM)), per((BLK, HEAD_DIM)), by_head((rows, BLK)),
            pl.BlockSpec((1, lw, 2 * NSA_KV), lambda i, j, kh, pt: (i, 0, 0)),
            per((BLK, HEAD_DIM)), per((BLK, HEAD_DIM)), by_head((rows, lw)), by_head((rows, BLK)),
            pl.BlockSpec((3, 1, 1, rows, HEAD_DIM), lambda i, j, kh, pt: (0, i, kh, 0, 0)),
            per((rows, HEAD_DIM)),
        ],
        out_specs=pl.BlockSpec((1, KV_HEADS, rows, HEAD_DIM), lambda i, j, kh, pt: (i, 0, 0, 0)),
        scratch_shapes=[pltpu.VMEM((KV_HEADS, rows, 1), F32), pltpu.VMEM((KV_HEADS, rows, 1), F32),
                        pltpu.VMEM((KV_HEADS, rows, HEAD_DIM), F32)],
    )
    return pl.pallas_call(
        _nsa_sample_attend_kernel,
        out_shape=jax.ShapeDtypeStruct((n, KV_HEADS, rows, HEAD_DIM), F32),
        grid_spec=grid_spec,
        compiler_params=_params(("parallel", "arbitrary", "arbitrary")),
        name="nsa_sample_attend",
    )(page_table, *([pool] * PAGES_PER_STEP), q, sel, expand, bias_sel, k_new, v_new, bias_new, win_buf, kw_new,
      vw_new, bias_win, bias_win_new, gates, o_cmp)


def _sel_weights(n_cmp_rows, n_sel_cols):
    ratio = SEL_BLOCK // CMP_STRIDE
    span = CMP_BLOCK // CMP_STRIDE
    c = jnp.arange(n_cmp_rows)[:, None]
    j = jnp.arange(n_sel_cols)[None, :]
    o = c - ratio * j + (span - 1)
    cnt = jnp.minimum(o, span - 1) - jnp.maximum(o - (ratio - 1), 0) + 1
    return jnp.where((o >= 0) & (o <= ratio + span - 2), cnt, 0).astype(BF16)


def _layer_b(hp, hs, b, s, n, t, norm_g, w_in, w_out, rel_bias, cmp, pool, page_table, win_buf):
    cmp_pe, cmp_w1, cmp_b1, cmp_w2, cmp_b2 = cmp
    n_kvcol = 6 * NSA_KV
    w_main = w_in[:, :NSA_Q + n_kvcol].astype(BF16)
    w_gate = jnp.pad(w_in[:, NSA_Q + n_kvcol:], ((0, 0), (0, BLK - 3 * N_HEADS))).astype(BF16)
    w1 = cmp_w1.reshape(2, 2, CMP_STRIDE, HEAD_DIM, HEAD_DIM)
    w1r = jnp.concatenate([w1[:, 0], w1[:, 1]], axis=-1).astype(BF16)
    pe_row = jnp.pad(cmp_pe.reshape(2, 1, CMP_BLOCK * HEAD_DIM), ((0, 0), (0, 7), (0, 0))).astype(BF16)
    cmp_w = (pe_row, cmp_w1.reshape(2, CMP_BLOCK * HEAD_DIM, HEAD_DIM).astype(BF16), cmp_b1.reshape(2, 1, HEAD_DIM),
             cmp_w2.astype(BF16), cmp_b2.reshape(2, 1, HEAD_DIM))

    a_p = _rmsnorm(hp, norm_g, BF16)
    proj_p = _matmul(a_p, w_main)
    gates_p = _matmul(a_p, w_gate, sigmoid=True)
    kv_p = proj_p[:, NSA_Q:].reshape(b, s, 6, KV_HEADS, HEAD_DIM)
    pages_p = proj_p[:, NSA_Q:NSA_Q + 2 * NSA_KV].reshape(b * s // BLK, BLK, 2 * NSA_KV)
    table_p = jnp.arange(b * s // BLK, dtype=jnp.int32).reshape(b, s // BLK)
    a_cmp_p = _cmp_proj(pages_p, table_p, w1r)
    nq = s // BLK
    tbl = _bias_table(rel_bias, nq, BLK, BLK, 0, BLK, -1, 1, -(1 << 30), 1 << 30)
    bias_c = _bias_table(rel_bias, 1, s // CMP_STRIDE, s, -(CMP_BLOCK - 1), 0, -CMP_STRIDE, 1, 0, 1 << 30)
    gates_t = gates_p[:, :3 * N_HEADS].reshape(b * s, 3, KV_HEADS, GROUP).transpose(2, 1, 3, 0)
    gates_t = jnp.pad(gates_t.reshape(KV_HEADS, 3 * GROUP, b * s), ((0, 0), (0, 16 - 3 * GROUP), (0, 0)))
    msel_p = _sel_weights(s // CMP_STRIDE, s // SEL_BLOCK).T
    vsel_t = kv_p[:, :, 3].transpose(0, 2, 3, 1).astype(BF16)
    vwin_t = kv_p[:, :, 5].transpose(0, 2, 3, 1).astype(BF16)
    o_t = _nsa_prompt(proj_p, a_cmp_p, gates_t, cmp_w, tbl, bias_c, msel_p, vsel_t, vwin_t, b, s)
    o_p = o_t.transpose(0, 3, 1, 2).reshape(b * s, NSA_Q).astype(BF16)
    hp = _matmul_residual(o_p, w_out, hp)
    keep = min(NSA_WINDOW, s)
    new_win_p = kv_p[:, s - keep:, 4:]
    new_kv_p = kv_p[:, :, :4]

    past = page_table.shape[1] * BLK
    a_s = _rmsnorm(hs, norm_g, BF16)
    proj_s = _matmul(a_s, w_main)
    gates_s = _matmul(a_s, w_gate, sigmoid=True)
    kv_s = proj_s[:, NSA_Q:].reshape(n, t, 6, KV_HEADS, HEAD_DIM)
    rows = GROUP * 8

    def head_rows(x):
        x = jnp.pad(x.transpose(0, 2, 3, 1, 4), ((0, 0), (0, 0), (0, 0), (0, 8 - t), (0, 0)))
        return x.reshape(n, KV_HEADS, rows, x.shape[-1])

    def new_rows(c):
        return jnp.pad(kv_s[:, :, c].transpose(0, 2, 1, 3), ((0, 0), (0, 0), (0, BLK - t), (0, 0)))

    def head_table(x, cols):
        return x.reshape(KV_HEADS, rows, cols)

    q_s = head_rows(proj_s[:, :NSA_Q].reshape(n, t, KV_HEADS, GROUP, HEAD_DIM))
    pool2 = pool.reshape(pool.shape[0], BLK, 4 * NSA_KV)
    a_cmp_s = _cmp_proj(pool2, page_table, w1r)
    chunks = past // CMP_STRIDE
    big = 1 << 30
    bias_cs = head_table(_bias_table(rel_bias, 1, 8, chunks, past - (CMP_BLOCK - 1), 0, 1, -CMP_STRIDE, 0, big), chunks)
    n_steps = page_table.shape[1] // PAGES_PER_STEP
    n_sel_pad = (n_steps + 1) * BLK
    msel_s = _sel_weights(chunks, n_sel_pad)
    per_step = PAGES_PER_STEP * BLK // SEL_BLOCK
    jj = jnp.arange(n_sel_pad)[None, :, None]
    ll = jnp.arange(BLK)[None, None, :]
    st = jnp.arange(n_steps + 1)[:, None, None]
    regroup = ((jj == st * per_step + ll) & (ll < per_step)).astype(BF16)
    o_cmp_s, sel_s = _nsa_sample_select(q_s, a_cmp_s, cmp_w, bias_cs, msel_s, regroup, past, t)
    expand = (jnp.arange(BLK)[:, None] == jnp.arange(PAGES_PER_STEP * BLK)[None, :] // SEL_BLOCK).astype(BF16)
    bias_sel = head_table(_bias_table(rel_bias, 1, 8, past, past, 0, 1, -1, 0, big), past)
    bias_new = head_table(_bias_table(rel_bias, 1, 8, BLK, 0, 0, 1, -1, 0, big), BLK)
    lw = win_buf.shape[1]
    bias_win = head_table(_bias_table(rel_bias, 1, 8, lw, lw, 0, 1, -1, 0, NSA_WINDOW - 1), lw)
    bias_win_new = head_table(_bias_table(rel_bias, 1, 8, BLK, 0, 0, 1, -1, 0, NSA_WINDOW - 1), BLK)
    g_s = gates_s[:, :3 * N_HEADS].reshape(n, t, 3, KV_HEADS, GROUP, 1)
    g_s = jnp.stack([head_rows(g_s[:, :, c]) for c in range(3)])
    g_s = jnp.broadcast_to(g_s, (3, n, KV_HEADS, rows, HEAD_DIM))
    o_s = _nsa_sample_attend(pool2, page_table, q_s, sel_s, expand, bias_sel, new_rows(2), new_rows(3), bias_new,
                             win_buf.reshape(n, lw, 2 * NSA_KV), new_rows(4), new_rows(5), bias_win, bias_win_new,
                             g_s, o_cmp_s)
    o_s = o_s.reshape(n, KV_HEADS, GROUP, 8, HEAD_DIM)[:, :, :, :t].transpose(0, 3, 1, 2, 4)
    hs = _matmul_residual(o_s.reshape(n * t, NSA_Q).astype(BF16), w_out, hs)
    new_win_s = jnp.concatenate([win_buf, kv_s[:, :, 4:]], axis=1)[:, t:]
    new_kv_s = kv_s[:, :, :4]
    return hp, hs, new_win_p, new_win_s, new_kv_p, new_kv_s


def _ffn_and_ple(hp, hs, b, s, n, t, i, norm_ffn, norm_ple, w_ffn_in, conv_w, conv_b, w_ffn_out, state_conv,
                 p_prompt, p_sample, w_ple_gate, w_ple_proj):
    hu_p = _matmul(_rmsnorm(hp, norm_ffn, BF16), w_ffn_in)
    hu_s = _matmul(_rmsnorm(hs, norm_ffn, BF16), w_ffn_in)
    conv_p = hu_p.reshape(b, s, 2 * D_FF)[:, s - (CONV_W - 1):, :D_FF]
    conv_s = jnp.concatenate([state_conv, hu_s.reshape(n, t, 2 * D_FF)[:, :, :D_FF]], axis=1)[:, t:]
    hp = _ffn_out_prompt(hu_p, conv_w, conv_b, w_ffn_out, hp, s)
    hs = _ffn_out_sample(hu_s, state_conv, conv_w, conv_b, w_ffn_out, hs, t)
    hp = _ple_add(_rmsnorm(hp, norm_ple, BF16), w_ple_gate, p_prompt.astype(BF16), w_ple_proj, hp)
    hs = _ple_add(_rmsnorm(hs, norm_ple, BF16), w_ple_gate, p_sample.astype(BF16), w_ple_proj, hs)
    return hp, hs, conv_p, conv_s


def kernel(x_prompt, x_sample, state_dil_w128, state_dil_w512, state_dil_w2048, state_nsa_win, state_conv,
           cache_nsa_kv, page_table, p_prompt, p_sample, rel_bias, norm_mix, norm_ffn, norm_ple, norm_final,
           w_in_a, w_out_a, w_in_b, w_out_b, cmp_pe, cmp_w1, cmp_b1, cmp_w2, cmp_b2, w_ffn_in, conv_w, conv_b,
           w_ffn_out, w_ple_gate, w_ple_proj):
    b, s, d = x_prompt.shape
    n, t, _ = x_sample.shape
    depth = norm_mix.shape[0]
    hp, hs = x_prompt.reshape(b * s, d), x_sample.reshape(n * t, d)
    dil_p, dil_s = [[] for _ in range(N_DIL)], [[] for _ in range(N_DIL)]
    win_p, win_s, kv_p, kv_s, conv_p, conv_s = [], [], [], [], [], []
    for i in range(depth):
        li = i // 2
        if i % 2 == 0:
            hp, hs, new_p, new_s = _layer_a(
                hp, hs, b, s, n, t, norm_mix[i], w_in_a[li].astype(BF16), w_out_a[li].astype(BF16), rel_bias,
                (state_dil_w128[li], state_dil_w512[li], state_dil_w2048[li]))
            for g in range(N_DIL):
                dil_p[g].append(new_p[g])
                dil_s[g].append(new_s[g])
        else:
            hp, hs, wp, ws, rp, rs = _layer_b(
                hp, hs, b, s, n, t, norm_mix[i], w_in_b[li], w_out_b[li].astype(BF16), rel_bias,
                (cmp_pe[li], cmp_w1[li], cmp_b1[li], cmp_w2[li], cmp_b2[li]), cache_nsa_kv[li], page_table,
                state_nsa_win[li])
            win_p.append(wp)
            win_s.append(ws)
            kv_p.append(rp)
            kv_s.append(rs)
        hp, hs, cp, cs = _ffn_and_ple(
            hp, hs, b, s, n, t, i, norm_ffn[i], norm_ple[i], w_ffn_in[i].astype(BF16), conv_w[i], conv_b[i],
            w_ffn_out[i].astype(BF16), state_conv[i], p_prompt[i].reshape(b * s, -1), p_sample[i].reshape(n * t, -1),
            w_ple_gate[i].astype(BF16), w_ple_proj[i].astype(BF16))
        conv_p.append(cp)
        conv_s.append(cs)
    y_prompt = _rmsnorm(hp, norm_final, F32).reshape(b, s, d)
    y_sample = _rmsnorm(hs, norm_final, F32).reshape(n, t, d)
    return (y_prompt, y_sample,
            jnp.stack(dil_p[0]), jnp.stack(dil_s[0]), jnp.stack(dil_p[1]), jnp.stack(dil_s[1]),
            jnp.stack(dil_p[2]), jnp.stack(dil_s[2]),
            jnp.stack(win_p), jnp.stack(win_s), jnp.stack(conv_p), jnp.stack(conv_s),
            jnp.stack(kv_p), jnp.stack(kv_s))
```

```python
import functools

import jax
import jax.numpy as jnp
from jax import lax
from jax.experimental import pallas as pl
from jax.experimental.pallas import tpu as pltpu

F32 = jnp.float32
BF16 = jnp.bfloat16

D_MODEL = 2048
HEAD_DIM = 128
N_HEADS = 16
DIL_PAIRS = ((128, 1), (512, 4), (2048, 16))
N_DIL = 3
BLK = 128
KV_HEADS = 4
GROUP = 4
CMP_BLOCK = 32
CMP_STRIDE = 16
SEL_BLOCK = 64
SEL_TOPN = 16
NSA_WINDOW = 512
D_FF = 5632
CONV_W = 3
REL_BUCKETS = 32
EPS = 1e-6
NEG = -1e30
FORCED_SCORE = 1e4
SCALE = HEAD_DIM ** -0.5
QKV_A = N_DIL * 3 * N_HEADS * HEAD_DIM
NSA_Q = N_HEADS * HEAD_DIM
NSA_KV = KV_HEADS * HEAD_DIM

BUCKET_START = (1, 2, 3, 4, 5, 6, 7, 8, 9, 10, 11, 12, 13, 14, 15, 16, 22, 30, 40, 54, 73, 99,
                134, 182, 246, 332, 450, 609, 825, 1117, 1513)

VMEM_LIMIT_V7X = 56 * 1024 * 1024


def _params(sem, vmem=VMEM_LIMIT_V7X):
    return pltpu.CompilerParams(dimension_semantics=sem, vmem_limit_bytes=vmem)


def _pick(n, cands):
    for c in cands:
        if n % c == 0:
            return c
    return n


def _rmsnorm_kernel(x_ref, g_ref, o_ref):
    x = x_ref[...]
    ms = jnp.mean(x * x, axis=-1, keepdims=True)
    o_ref[...] = ((x * lax.rsqrt(ms + EPS)) * g_ref[...]).astype(o_ref.dtype)


def _rmsnorm(x, g, out_dtype):
    m, d = x.shape
    tm = _pick(m, (512, 256, 128, 32))
    return pl.pallas_call(
        _rmsnorm_kernel,
        out_shape=jax.ShapeDtypeStruct((m, d), out_dtype),
        grid=(m // tm,),
        in_specs=[pl.BlockSpec((tm, d), lambda i: (i, 0)), pl.BlockSpec((1, d), lambda i: (0, 0))],
        out_specs=pl.BlockSpec((tm, d), lambda i: (i, 0)),
        compiler_params=_params(("parallel",)),
        name="rmsnorm",
    )(x, g.reshape(1, d))


def _mm_kernel(a_ref, w_ref, o_ref, *, sigmoid):
    acc = jnp.dot(a_ref[...], w_ref[...], preferred_element_type=F32)
    if sigmoid:
        acc = jax.nn.sigmoid(acc)
    o_ref[...] = acc.astype(o_ref.dtype)


def _matmul(a, w, out_dtype=F32, sigmoid=False):
    m, k = a.shape
    n = w.shape[1]
    tm = _pick(m, (1024, 512, 256, 128))
    tn = _pick(n, (1024, 512, 256, 128))
    return pl.pallas_call(
        functools.partial(_mm_kernel, sigmoid=sigmoid),
        out_shape=jax.ShapeDtypeStruct((m, n), out_dtype),
        grid=(m // tm, n // tn),
        in_specs=[pl.BlockSpec((tm, k), lambda i, j: (i, 0)), pl.BlockSpec((k, tn), lambda i, j: (0, j))],
        out_specs=pl.BlockSpec((tm, tn), lambda i, j: (i, j)),
        compiler_params=_params(("parallel", "arbitrary")),
        name="matmul",
    )(a, w)


def _mm_res_kernel(a_ref, w_ref, r_ref, o_ref):
    o_ref[...] = r_ref[...] + jnp.dot(a_ref[...], w_ref[...], preferred_element_type=F32)


def _matmul_residual(a, w, res):
    m, k = a.shape
    n = w.shape[1]
    tm = _pick(m, (1024, 512, 256, 128))
    tn = _pick(n, (1024, 512, 256, 128))
    return pl.pallas_call(
        _mm_res_kernel,
        out_shape=jax.ShapeDtypeStruct((m, n), F32),
        grid=(m // tm, n // tn),
        in_specs=[pl.BlockSpec((tm, k), lambda i, j: (i, 0)), pl.BlockSpec((k, tn), lambda i, j: (0, j)),
                  pl.BlockSpec((tm, tn), lambda i, j: (i, j))],
        out_specs=pl.BlockSpec((tm, tn), lambda i, j: (i, j)),
        compiler_params=_params(("parallel", "arbitrary")),
        name="matmul_residual",
    )(a, w, res)


def _conv_gelu_val(g, g1, g2, val, cw_ref, cb_ref):
    c = cb_ref[...] + g2 * cw_ref[0:1, :]
    c = c + g1 * cw_ref[1:2, :]
    c = c + g * cw_ref[2:3, :]
    return (jax.nn.gelu(c) * val).astype(BF16)


def _ffn_out_prompt_kernel(g_ref, halo_ref, v_ref, cw_ref, cb_ref, w_ref, r_ref, o_ref, acc_ref, *,
                           tiles_per_seq):
    i, k = pl.program_id(0), pl.program_id(1)

    @pl.when(k == 0)
    def _():
        acc_ref[...] = jnp.zeros_like(acc_ref)

    g = g_ref[...]
    row = lax.broadcasted_iota(jnp.int32, g.shape, 0)
    halo = jnp.where(i % tiles_per_seq == 0, 0.0, halo_ref[...])
    g1 = jnp.where(row == 0, halo[7:8, :], pltpu.roll(g, 1, 0))
    g2 = jnp.where(row == 0, halo[6:7, :], jnp.where(row == 1, halo[7:8, :], pltpu.roll(g, 2, 0)))
    u = _conv_gelu_val(g, g1, g2, v_ref[...], cw_ref, cb_ref)
    acc_ref[...] += jnp.dot(u, w_ref[...], preferred_element_type=F32)

    @pl.when(k == pl.num_programs(1) - 1)
    def _():
        o_ref[...] = r_ref[...] + acc_ref[...]


def _ffn_out_prompt(hu, conv_w, conv_b, w_out, res, seq):
    m = hu.shape[0]
    tm, tk = 1024, 512
    nk = D_FF // tk
    return pl.pallas_call(
        functools.partial(_ffn_out_prompt_kernel, tiles_per_seq=seq // tm),
        out_shape=jax.ShapeDtypeStruct((m, D_MODEL), F32),
        grid=(m // tm, nk),
        in_specs=[
            pl.BlockSpec((tm, tk), lambda i, k: (i, k)),
            pl.BlockSpec((8, tk), lambda i, k: (jnp.maximum(i * (tm // 8) - 1, 0), k)),
            pl.BlockSpec((tm, tk), lambda i, k: (i, k + nk)),
            pl.BlockSpec((CONV_W, tk), lambda i, k: (0, k)),
            pl.BlockSpec((1, tk), lambda i, k: (0, k)),
            pl.BlockSpec((tk, D_MODEL), lambda i, k: (k, 0)),
            pl.BlockSpec((tm, D_MODEL), lambda i, k: (i, 0)),
        ],
        out_specs=pl.BlockSpec((tm, D_MODEL), lambda i, k: (i, 0)),
        scratch_shapes=[pltpu.VMEM((tm, D_MODEL), F32)],
        compiler_params=_params(("parallel", "arbitrary")),
        name="ffn_out_prompt",
    )(hu, hu, hu, conv_w, conv_b.reshape(1, D_FF), w_out, res)


def _ffn_out_sample_kernel(g_ref, e1_ref, e2_ref, v_ref, cw_ref, cb_ref, w_ref, r_ref, o_ref, acc_ref, *,
                           t_len):
    k = pl.program_id(0)

    @pl.when(k == 0)
    def _():
        acc_ref[...] = jnp.zeros_like(acc_ref)

    g = g_ref[...]
    t = lax.broadcasted_iota(jnp.int32, g.shape, 0) % t_len
    g1 = jnp.where(t == 0, e1_ref[...], pltpu.roll(g, 1, 0))
    g2 = jnp.where(t < 2, e2_ref[...], pltpu.roll(g, 2, 0))
    u = _conv_gelu_val(g, g1, g2, v_ref[...], cw_ref, cb_ref)
    acc_ref[...] += jnp.dot(u, w_ref[...], preferred_element_type=F32)

    @pl.when(k == pl.num_programs(0) - 1)
    def _():
        o_ref[...] = r_ref[...] + acc_ref[...]


def _ffn_out_sample(hu, conv_prev, conv_w, conv_b, w_out, res, t_len):
    m = hu.shape[0]
    n = m // t_len
    tk = 512
    nk = D_FF // tk
    zeros = jnp.zeros((n, t_len - 1, D_FF), F32)
    e1 = jnp.concatenate([conv_prev[:, 1:2], zeros], axis=1).reshape(m, D_FF)
    e2 = jnp.concatenate([conv_prev, zeros[:, 1:]], axis=1).reshape(m, D_FF)
    return pl.pallas_call(
        functools.partial(_ffn_out_sample_kernel, t_len=t_len),
        out_shape=jax.ShapeDtypeStruct((m, D_MODEL), F32),
        grid=(nk,),
        in_specs=[
            pl.BlockSpec((m, tk), lambda k: (0, k)),
            pl.BlockSpec((m, tk), lambda k: (0, k)),
            pl.BlockSpec((m, tk), lambda k: (0, k)),
            pl.BlockSpec((m, tk), lambda k: (0, k + nk)),
            pl.BlockSpec((CONV_W, tk), lambda k: (0, k)),
            pl.BlockSpec((1, tk), lambda k: (0, k)),
            pl.BlockSpec((tk, D_MODEL), lambda k: (k, 0)),
            pl.BlockSpec((m, D_MODEL), lambda k: (0, 0)),
        ],
        out_specs=pl.BlockSpec((m, D_MODEL), lambda k: (0, 0)),
        scratch_shapes=[pltpu.VMEM((m, D_MODEL), F32)],
        compiler_params=_params(("arbitrary",)),
        name="ffn_out_sample",
    )(hu, e1, e2, hu, conv_w, conv_b.reshape(1, D_FF), w_out, res)


def _ple_kernel(a_ref, wg_ref, p_ref, wp_ref, h_ref, o_ref):
    gate = jax.nn.sigmoid(jnp.dot(a_ref[...], wg_ref[...], preferred_element_type=F32))
    proj = jnp.dot(p_ref[...], wp_ref[...], preferred_element_type=F32)
    o_ref[...] = h_ref[...] + gate * proj


def _ple_add(a, w_gate, p, w_proj, h):
    m, d = h.shape
    kp = p.shape[1]
    tm = _pick(m, (1024, 512, 256, 128))
    tn = 1024
    return pl.pallas_call(
        _ple_kernel,
        out_shape=jax.ShapeDtypeStruct((m, d), F32),
        grid=(m // tm, d // tn),
        in_specs=[
            pl.BlockSpec((tm, d), lambda i, j: (i, 0)),
            pl.BlockSpec((d, tn), lambda i, j: (0, j)),
            pl.BlockSpec((tm, kp), lambda i, j: (i, 0)),
            pl.BlockSpec((kp, tn), lambda i, j: (0, j)),
            pl.BlockSpec((tm, tn), lambda i, j: (i, j)),
        ],
        out_specs=pl.BlockSpec((tm, tn), lambda i, j: (i, j)),
        compiler_params=_params(("parallel", "arbitrary")),
        name="ple_add",
    )(a, w_gate, p, w_proj, h)


def _bias_table_kernel(rb_ref, o_ref, *, a0, ag, ar, ac, lo, hi, mod):
    h, g = pl.program_id(0), pl.program_id(1)
    shape = o_ref.shape[2:]
    dist = (a0 + ag * g + ar * lax.broadcasted_iota(jnp.int32, shape, 0)
            + ac * lax.broadcasted_iota(jnp.int32, shape, 1))
    d = jnp.maximum(dist, 0)
    out = jnp.full(shape, rb_ref[0, h], F32)
    for k in range(1, REL_BUCKETS):
        out = jnp.where(d >= BUCKET_START[k - 1], rb_ref[k, h], out)
    ok = (dist >= lo) & (dist <= hi)
    if mod > 1:
        ok = ok & ((d & (mod - 1)) == 0)
    o_ref[0, 0] = jnp.where(ok, out, NEG)


def _bias_table(rel_bias, n_g, n_r, n_c, a0, ag, ar, ac, lo, hi, mod=1):
    assert mod & (mod - 1) == 0
    return pl.pallas_call(
        functools.partial(_bias_table_kernel, a0=a0, ag=ag, ar=ar, ac=ac, lo=lo, hi=hi, mod=mod),
        out_shape=jax.ShapeDtypeStruct((N_HEADS, n_g, n_r, n_c), F32),
        grid=(N_HEADS, n_g),
        in_specs=[pl.BlockSpec(memory_space=pltpu.SMEM)],
        out_specs=pl.BlockSpec((1, 1, n_r, n_c), lambda h, g: (h, g, 0, 0)),
        compiler_params=_params(("parallel", "parallel")),
        name="bias_table",
    )(rel_bias)


def _mm_heads_kernel(a_ref, w_ref, o_ref):
    acc = jnp.dot(a_ref[...], w_ref[...], preferred_element_type=F32)
    for j in range(o_ref.shape[0]):
        o_ref[j] = acc[:, j * HEAD_DIM:(j + 1) * HEAD_DIM]


def _matmul_heads(a, w):
    m, k = a.shape
    n = w.shape[1]
    tm, tn = _pick(m, (1024, 512, 256, 128)), 1024
    return pl.pallas_call(
        _mm_heads_kernel,
        out_shape=jax.ShapeDtypeStruct((n // HEAD_DIM, m, HEAD_DIM), F32),
        grid=(m // tm, n // tn),
        in_specs=[pl.BlockSpec((tm, k), lambda i, j: (i, 0)), pl.BlockSpec((k, tn), lambda i, j: (0, j))],
        out_specs=pl.BlockSpec((tn // HEAD_DIM, tm, HEAD_DIM), lambda i, j: (j, i, 0)),
        compiler_params=_params(("parallel", "arbitrary")),
        name="matmul_heads",
    )(a, w)


def _dil_prompt_kernel(q_ref, kc_ref, kp_ref, vc_ref, vp_ref, bias_ref, o_ref, lse_ref, *, dil, hps):
    first = pl.program_id(1) == 0
    hb = pl.program_id(2)
    col = lax.broadcasted_iota(jnp.int32, (BLK, 2 * BLK), 1)
    edge = jnp.where(first & (col < BLK), NEG, 0.0)
    lane = lax.broadcasted_iota(jnp.int32, (BLK, BLK), 1)

    @pl.when(hb == 0)
    def _():
        lse_ref[...] = jnp.zeros_like(lse_ref)

    def unit(u, carry):
        hh, r = u // dil, u % dil
        rows = pl.ds(r, BLK, stride=dil)
        q = q_ref[hh, rows, :].astype(BF16)
        k = jnp.concatenate([kp_ref[hh, rows, :], kc_ref[hh, rows, :]], axis=0).astype(BF16)
        v = jnp.concatenate([vp_ref[hh, rows, :], vc_ref[hh, rows, :]], axis=0).astype(BF16)
        h = hb * hps + hh
        s = lax.dot_general(q, k, (((1,), (1,)), ((), ())), preferred_element_type=F32)
        s = s * SCALE + bias_ref[h] + edge
        m = jnp.max(s, axis=-1, keepdims=True)
        p = jnp.exp(s - m)
        l = jnp.sum(p, axis=-1, keepdims=True)
        o_ref[hh, rows, :] = jnp.dot(p.astype(BF16), v, preferred_element_type=F32) / l
        lse_ref[rows, :] = jnp.where(lane == h, m + jnp.log(l), lse_ref[rows, :])
        return carry

    lax.fori_loop(0, hps * dil, unit, 0)


def _dil_prompt_group(qkv_hm, bias, grp, dil, b, s):
    span = BLK * dil
    nsp = s // span
    hps = N_HEADS // dil
    nhb = N_HEADS // hps

    def slab(part, prev):
        base = (grp * 3 + part) * N_HEADS // hps
        if prev:
            return lambda bi, sp, hb: (base + hb, bi * nsp + jnp.maximum(sp - 1, 0), 0)
        return lambda bi, sp, hb: (base + hb, bi * nsp + sp, 0)

    blk = (hps, span, HEAD_DIM)
    return pl.pallas_call(
        functools.partial(_dil_prompt_kernel, dil=dil, hps=hps),
        out_shape=(jax.ShapeDtypeStruct((N_HEADS, b * s, HEAD_DIM), F32),
                   jax.ShapeDtypeStruct((b * s, BLK), F32)),
        grid=(b, nsp, nhb),
        in_specs=[pl.BlockSpec(blk, slab(0, False)), pl.BlockSpec(blk, slab(1, False)),
                  pl.BlockSpec(blk, slab(1, True)), pl.BlockSpec(blk, slab(2, False)),
                  pl.BlockSpec(blk, slab(2, True)),
                  pl.BlockSpec((N_HEADS, BLK, 2 * BLK), lambda bi, sp, hb: (0, 0, 0))],
        out_specs=(pl.BlockSpec(blk, lambda bi, sp, hb: (hb, bi * nsp + sp, 0)),
                   pl.BlockSpec((span, BLK), lambda bi, sp, hb: (bi * nsp + sp, 0))),
        compiler_params=_params(("parallel", "parallel", "arbitrary")),
        name=f"dil_attn_prompt_g{grp}",
    )(qkv_hm, qkv_hm, qkv_hm, qkv_hm, qkv_hm, bias)


def _dil_combine_kernel(o0_ref, o1_ref, o2_ref, l0_ref, l1_ref, l2_ref, o_ref, *, head_major):
    l0, l1, l2 = l0_ref[...], l1_ref[...], l2_ref[...]
    mx = jnp.maximum(jnp.maximum(l0, l1), l2)
    e0, e1, e2 = jnp.exp(l0 - mx), jnp.exp(l1 - mx), jnp.exp(l2 - mx)
    den = e0 + e1 + e2
    w0, w1, w2 = e0 / den, e1 / den, e2 / den
    for h in range(N_HEADS):
        sl = slice(h * HEAD_DIM, (h + 1) * HEAD_DIM)
        g0, g1, g2 = ((r[h] for r in (o0_ref, o1_ref, o2_ref)) if head_major
                      else (r[:, sl] for r in (o0_ref, o1_ref, o2_ref)))
        o = (w0[:, h:h + 1] * g0 + w1[:, h:h + 1] * g1) + w2[:, h:h + 1] * g2
        o_ref[:, sl] = o.astype(o_ref.dtype)


def _dil_combine(outs, lses, head_major):
    m = lses[0].shape[0]
    wide = N_HEADS * HEAD_DIM
    tm = _pick(m, (256, 128, 32))
    ob = pl.BlockSpec((tm, wide), lambda i: (i, 0))
    ib = pl.BlockSpec((N_HEADS, tm, HEAD_DIM), lambda i: (0, i, 0)) if head_major else ob
    lb = pl.BlockSpec((tm, BLK), lambda i: (i, 0))
    return pl.pallas_call(
        functools.partial(_dil_combine_kernel, head_major=head_major),
        out_shape=jax.ShapeDtypeStruct((m, wide), BF16),
        grid=(m // tm,),
        in_specs=[ib, ib, ib, lb, lb, lb],
        out_specs=ob,
        compiler_params=_params(("parallel",)),
        name="dil_combine",
    )(*outs, *lses)


HEAD_TILE = 8


def _heads_first(x):
    return pltpu.einshape("mhd->hmd", x)


def _dil_sample_kernel(q_ref, k_ref, v_ref, kn_ref, vn_ref, bias_ref, biasn_ref, o_ref, lse_ref,
                       m_scr, l_scr, acc_scr, k_scr, v_scr):
    ht, c = pl.program_id(1), pl.program_id(2)
    k_scr[...] = _heads_first(k_ref[...])
    v_scr[...] = _heads_first(v_ref[...])

    @pl.when(c == 0)
    def _():
        m_scr[...] = jnp.full_like(m_scr, NEG)
        l_scr[...] = jnp.zeros_like(l_scr)
        acc_scr[...] = jnp.zeros_like(acc_scr)

    @pl.when((c == 0) & (ht == 0))
    def _():
        lse_ref[...] = jnp.zeros_like(lse_ref)

    def step(hh, k, v, bias):
        sl = pl.ds(pl.multiple_of(hh * HEAD_DIM, HEAD_DIM), HEAD_DIM)
        q = q_ref[0, :, sl].astype(BF16)
        s = lax.dot_general(q, k.astype(BF16), (((1,), (1,)), ((), ())), preferred_element_type=F32)
        s = s * SCALE + bias
        m_old = m_scr[hh]
        m_new = jnp.maximum(m_old, jnp.max(s, axis=-1, keepdims=True))
        alpha = jnp.exp(m_old - m_new)
        p = jnp.where(bias > 0.5 * NEG, jnp.exp(s - m_new), 0.0)
        l_scr[hh] = alpha * l_scr[hh] + jnp.sum(p, axis=-1, keepdims=True)
        acc_scr[hh] = alpha * acc_scr[hh] + jnp.dot(p.astype(BF16), v.astype(BF16), preferred_element_type=F32)
        m_scr[hh] = m_new

    def head(hh, carry):
        step(hh, k_scr[hh], v_scr[hh], bias_ref[hh, 0])
        return carry

    lax.fori_loop(0, HEAD_TILE, head, 0)

    @pl.when(c == pl.num_programs(2) - 1)
    def _():
        lane = lax.broadcasted_iota(jnp.int32, (8, BLK), 1)

        def fin(hh, carry):
            sl = pl.ds(pl.multiple_of(hh * HEAD_DIM, HEAD_DIM), HEAD_DIM)
            step(hh, kn_ref[0, :, sl], vn_ref[0, :, sl], biasn_ref[hh])
            l = jnp.maximum(l_scr[hh], 1e-30)
            o_ref[0, :, sl] = acc_scr[hh] / l
            lse_ref[0] = jnp.where(lane == ht * HEAD_TILE + hh, m_scr[hh] + jnp.log(l), lse_ref[0])
            return carry

        lax.fori_loop(0, HEAD_TILE, fin, 0)


def _dil_sample_group(q, kn, vn, buf, bias, bias_new, dil):
    n, lb = buf.shape[:2]
    n_cls = bias.shape[1]
    wide = N_HEADS * HEAD_DIM
    half = wide // 2
    rows = lb // dil
    tiles = 2 * N_HEADS // HEAD_TILE
    view = buf.reshape(n, rows, dil * tiles, HEAD_TILE, HEAD_DIM)
    nht = N_HEADS // HEAD_TILE
    return pl.pallas_call(
        _dil_sample_kernel,
        out_shape=(jax.ShapeDtypeStruct((n, 8, wide), F32), jax.ShapeDtypeStruct((n, 8, BLK), F32)),
        grid=(n, nht, n_cls),
        in_specs=[
            pl.BlockSpec((1, 8, half), lambda i, ht, c: (i, 0, ht)),
            pl.BlockSpec((None, rows, None, HEAD_TILE, HEAD_DIM), lambda i, ht, c: (i, 0, c * tiles + ht, 0, 0)),
            pl.BlockSpec((None, rows, None, HEAD_TILE, HEAD_DIM),
                         lambda i, ht, c: (i, 0, c * tiles + nht + ht, 0, 0)),
            pl.BlockSpec((1, BLK, half), lambda i, ht, c: (i, 0, ht)),
            pl.BlockSpec((1, BLK, half), lambda i, ht, c: (i, 0, ht)),
            pl.BlockSpec((HEAD_TILE, 1, 8, rows), lambda i, ht, c: (ht, c, 0, 0)),
            pl.BlockSpec((HEAD_TILE, 8, BLK), lambda i, ht, c: (ht, 0, 0)),
        ],
        out_specs=(pl.BlockSpec((1, 8, half), lambda i, ht, c: (i, 0, ht)),
                   pl.BlockSpec((1, 8, BLK), lambda i, ht, c: (i, 0, 0))),
        scratch_shapes=[pltpu.VMEM((HEAD_TILE, 8, 1), F32), pltpu.VMEM((HEAD_TILE, 8, 1), F32),
                        pltpu.VMEM((HEAD_TILE, 8, HEAD_DIM), F32),
                        pltpu.VMEM((HEAD_TILE, rows, HEAD_DIM), F32), pltpu.VMEM((HEAD_TILE, rows, HEAD_DIM), F32)],
        compiler_params=_params(("parallel", "arbitrary", "arbitrary")),
        name="dil_attn_sample",
    )(q, view, view, kn, vn, bias, bias_new)


def _pad_rows(x, rows):
    return jnp.pad(x, ((0, 0), (0, rows - x.shape[1]), (0, 0)))


def _layer_a(hp, hs, b, s, n, t, norm_g, w_in, w_out, rel_bias, bufs):
    wide = N_HEADS * HEAD_DIM
    qkv_p = _matmul_heads(_rmsnorm(hp, norm_g, BF16), w_in)
    qkv_s = _matmul(_rmsnorm(hs, norm_g, BF16), w_in).reshape(n, t, QKV_A)
    outs_p, lses_p, outs_s, lses_s, new_p, new_s = [], [], [], [], [], []
    for grp, (win, dil) in enumerate(DIL_PAIRS):
        base = grp * 3 * wide
        bias = _bias_table(rel_bias, 1, BLK, 2 * BLK, BLK * dil, 0, dil, -dil, 0, win).reshape(N_HEADS, BLK, 2 * BLK)
        o, lse = _dil_prompt_group(qkv_p, bias, grp, dil, b, s)
        outs_p.append(o)
        lses_p.append(lse)
        keep = min(win, s)
        kv = qkv_p[(grp * 3 + 1) * N_HEADS:(grp * 3 + 3) * N_HEADS].reshape(2, N_HEADS, b, s, HEAD_DIM)
        new_p.append(kv[:, :, :, s - keep:].transpose(2, 3, 0, 1, 4))
        buf = bufs[grp]
        lb = buf.shape[1]
        n_cls = min(dil, t)
        bias_buf = _bias_table(rel_bias, n_cls, 8, lb // dil, lb, -1, 1, -dil, 0, win, dil)
        bias_new = _bias_table(rel_bias, 1, 8, BLK, 0, 0, 1, -1, 0, win, dil).reshape(N_HEADS, 8, BLK)
        q = _pad_rows(qkv_s[:, :, base:base + wide], 8)
        kn = _pad_rows(qkv_s[:, :, base + wide:base + 2 * wide], BLK)
        vn = _pad_rows(qkv_s[:, :, base + 2 * wide:base + 3 * wide], BLK)
        o, lse = _dil_sample_group(q, kn, vn, buf, bias_buf, bias_new, dil)
        outs_s.append(o[:, :t].reshape(n * t, wide))
        lses_s.append(lse[:, :t].reshape(n * t, BLK))
        kv_new = qkv_s[:, :, base + wide:base + 3 * wide].reshape(n, t, 2, N_HEADS, HEAD_DIM)
        new_s.append(jnp.concatenate([buf, kv_new], axis=1)[:, t:])
    hp = _matmul_residual(_dil_combine(outs_p, lses_p, True), w_out, hp)
    hs = _matmul_residual(_dil_combine(outs_s, lses_s, False), w_out, hs)
    return hp, hs, new_p, new_s


PAGES_PER_STEP = 16
CHUNKS_PER_PAGE = BLK // CMP_STRIDE
NT_DIMS = (((1,), (1,)), ((), ()))
TN_DIMS = (((0,), (0,)), ((), ()))


def _page_spec(p, half):
    return pl.BlockSpec((BLK, HEAD_TILE, HEAD_DIM),
                        lambda i, j, *rest: (rest[-1][i, j * PAGES_PER_STEP + p], half, 0))


def _cmp_proj_kernel(pt_ref, *refs):
    pages = refs[:PAGES_PER_STEP]
    w_ref, o_ref = refs[PAGES_PER_STEP:PAGES_PER_STEP + 2]
    ys = [pltpu.einshape("ctgd->tgcd", pg[...].reshape(CHUNKS_PER_PAGE, CMP_STRIDE, HEAD_TILE, HEAD_DIM))
          for pg in pages]
    for c in range(2):
        acc = jnp.zeros((KV_HEADS * BLK, 2 * HEAD_DIM), F32)
        for t in range(CMP_STRIDE):
            rows = [y[t, c * KV_HEADS + kh] for kh in range(KV_HEADS) for y in ys]
            lhs = jnp.concatenate(rows, axis=0).astype(BF16)
            acc = acc + jnp.dot(lhs, w_ref[c, t], preferred_element_type=F32)
        for kh in range(KV_HEADS):
            o_ref[0, c, kh] = acc[kh * BLK:(kh + 1) * BLK]


def _cmp_proj(pages, page_table, w1r):
    n, n_pages = page_table.shape
    chunks = n_pages * CHUNKS_PER_PAGE
    grid_spec = pltpu.PrefetchScalarGridSpec(
        num_scalar_prefetch=1,
        grid=(n, n_pages // PAGES_PER_STEP),
        in_specs=[_page_spec(p, 0) for p in range(PAGES_PER_STEP)]
        + [pl.BlockSpec((2, CMP_STRIDE, HEAD_DIM, 2 * HEAD_DIM), lambda i, j, pt: (0, 0, 0, 0))],
        out_specs=pl.BlockSpec((1, 2, KV_HEADS, BLK, 2 * HEAD_DIM), lambda i, j, pt: (i, 0, 0, j, 0)),
    )
    return pl.pallas_call(
        _cmp_proj_kernel,
        out_shape=jax.ShapeDtypeStruct((n, 2, KV_HEADS, chunks, 2 * HEAD_DIM), F32),
        grid_spec=grid_spec,
        compiler_params=_params(("parallel", "arbitrary")),
        name="nsa_cmp_proj",
    )(page_table, *([pages] * PAGES_PER_STEP), w1r)


def _finish_compress(a, pe_row, w1f, b1, w2, b2, n_blocks):
    rows = a.shape[0]
    cst = jnp.dot(pe_row, w1f, preferred_element_type=F32)[0:1]
    h = (b1 + cst) + a[:, :HEAD_DIM] + pltpu.roll(a[:, HEAD_DIM:], rows - 1, 0)
    x = jnp.dot(jax.nn.gelu(h).astype(BF16), w2, preferred_element_type=F32) + b2
    return jnp.where(lax.broadcasted_iota(jnp.int32, x.shape, 0) < n_blocks, x, 0.0)


def _split3(x):
    hi = x.astype(BF16)
    r = x - hi.astype(F32)
    mid = r.astype(BF16)
    return hi, mid, (r - mid.astype(F32)).astype(BF16)


def _top_n(score, n, axis):
    idx = lax.broadcasted_iota(jnp.int32, score.shape, axis).astype(F32)
    big = float(score.shape[axis])

    def body(_, carry):
        sc, sel = carry
        mx = jnp.max(sc, axis=axis, keepdims=True)
        first = jnp.min(jnp.where(sc == mx, idx, big), axis=axis, keepdims=True)
        hit = idx == first
        return jnp.where(hit, -jnp.inf, sc), jnp.where(hit, 1.0, sel)

    return lax.fori_loop(0, n, body, (score, jnp.zeros(score.shape, F32)))[1]


def _sel_scores(p_slc, blk, cur, n_blocks):
    forced = (blk == 0) | (blk == cur) | (blk == cur - 1)
    score = jnp.where(forced, FORCED_SCORE, jnp.where(blk <= cur, p_slc, -1.0))
    return jnp.where(blk < n_blocks, score, -2.0)


def _nsa_prompt_kernel(q_ref, a_ref, pe_ref, w1f_ref, b1_ref, w2_ref, b2_ref, ksel_ref, vsel_ref, kwin_ref,
                       vwin_ref, tbl_ref, biasc_ref, gates_ref, msel_ref, o_ref, kc_scr, vc_scr, sel_scr,
                       vselt_ref, vwint_ref, *, n_cmp, n_sel):
    i = pl.program_id(2)

    @pl.when(i == 0)
    def _():
        for c, scr in ((0, kc_scr), (1, vc_scr)):
            scr[...] = _finish_compress(a_ref[0, c, 0], pe_ref[c], w1f_ref[c], b1_ref[c], w2_ref[c], b2_ref[c],
                                        n_cmp).astype(BF16)
        for src, dst in ((vsel_ref, vselt_ref), (vwin_ref, vwint_ref)):
            for c in range(src.shape[0] // BLK):
                dst[:, c * BLK:(c + 1) * BLK] = src[c * BLK:(c + 1) * BLK, :].T.astype(BF16)

    q = q_ref[...]
    qs = jnp.concatenate([q[:, g * HEAD_DIM:(g + 1) * HEAD_DIM] for g in range(GROUP)], axis=0).astype(BF16)
    key_i = lax.broadcasted_iota(jnp.int32, (BLK, BLK), 0)
    tok_i = lax.broadcasted_iota(jnp.int32, (BLK, BLK), 1)

    def lanes4(x):
        return jnp.concatenate([x] * GROUP, axis=1)

    def table(delta):
        return jnp.concatenate([tbl_ref[g, delta] for g in range(GROUP)], axis=1)

    mask_c = lanes4(i * BLK + tok_i - (key_i * CMP_STRIDE + (CMP_BLOCK - 1)) >= 0)
    s = lax.dot_general(kc_scr[...], qs, NT_DIMS, preferred_element_type=F32) * SCALE
    s = jnp.where(mask_c, s + jnp.concatenate([biasc_ref[g, 0] for g in range(GROUP)], axis=1), NEG)
    m = jnp.max(s, axis=0, keepdims=True)
    p = jnp.where(mask_c, jnp.exp(s - m), 0.0)
    pn = p / jnp.maximum(jnp.sum(p, axis=0, keepdims=True), 1e-30)
    o_cmp = lax.dot_general(vc_scr[...], pn.astype(BF16), TN_DIMS, preferred_element_type=F32)
    pc = ((pn[:, 0:BLK] + pn[:, BLK:2 * BLK]) + pn[:, 2 * BLK:3 * BLK]) + pn[:, 3 * BLK:4 * BLK]

    msel = msel_ref[...]
    p_slc = sum(jnp.dot(msel, part, preferred_element_type=F32) for part in _split3(pc))
    blk = lax.broadcasted_iota(jnp.int32, p_slc.shape, 0)
    cur = (i * BLK + lax.broadcasted_iota(jnp.int32, p_slc.shape, 1)) // SEL_BLOCK
    sel_scr[...] = _top_n(_sel_scores(p_slc, blk, cur, n_sel), SEL_TOPN, 0)

    def attend(k_ref, vt_ref, c, delta, mask, carry):
        m_run, l_run, acc = carry
        off = pl.multiple_of(c * BLK, BLK)
        k = k_ref[pl.ds(off, BLK), :].astype(BF16)
        mask = lanes4(mask)
        s = lax.dot_general(k, qs, NT_DIMS, preferred_element_type=F32) * SCALE + table(delta)
        s = jnp.where(mask, s, NEG)
        m_new = jnp.maximum(m_run, jnp.max(s, axis=0, keepdims=True))
        alpha = jnp.exp(m_run - m_new)
        p = jnp.where(mask, jnp.exp(s - m_new), 0.0)
        l_new = alpha * l_run + jnp.sum(p, axis=0, keepdims=True)
        vt = vt_ref[:, pl.ds(off, BLK)]
        return m_new, l_new, alpha * acc + jnp.dot(vt, p.astype(BF16), preferred_element_type=F32)

    init = (jnp.full((1, GROUP * BLK), NEG, F32), jnp.zeros((1, GROUP * BLK), F32),
            jnp.zeros((HEAD_DIM, GROUP * BLK), F32))

    def sel_step(c, carry):
        dist = (i - c) * BLK + tok_i - key_i
        picked = jnp.where(key_i < SEL_BLOCK, sel_scr[pl.ds(2 * c, 1), :], sel_scr[pl.ds(2 * c + 1, 1), :])
        return attend(ksel_ref, vselt_ref, c, i - c, (dist >= 0) & (picked > 0.5), carry)

    _, l_sel, acc_sel = lax.fori_loop(0, i + 1, sel_step, init)
    o_sel = acc_sel / jnp.maximum(l_sel, 1e-30)

    def win_step(delta, carry):
        dist = delta * BLK + tok_i - key_i
        return attend(kwin_ref, vwint_ref, i - delta, delta, (dist >= 0) & (dist <= NSA_WINDOW - 1), carry)

    n_win = jnp.minimum(i, (NSA_WINDOW - 1 + BLK - 1) // BLK) + 1
    _, l_win, acc_win = lax.fori_loop(0, n_win, win_step, init)
    o_win = acc_win / jnp.maximum(l_win, 1e-30)

    gt = gates_ref[0]

    def gate(branch):
        return jnp.concatenate([gt[branch * GROUP + g:branch * GROUP + g + 1, :] for g in range(GROUP)], axis=1)

    o = (gate(0) * o_cmp + gate(1) * o_sel) + gate(2) * o_win
    for g in range(GROUP):
        o_ref[0, g] = o[:, g * BLK:(g + 1) * BLK]


def _nsa_prompt(proj, a_cmp, gates_t, cmp_w, tbl, bias_c, msel, b, s):
    pe, w1f, b1, w2, b2 = cmp_w
    nq = s // BLK
    kcol = NSA_Q // HEAD_DIM
    const = lambda shape: pl.BlockSpec(shape, lambda bi, kh, i: (0,) * len(shape))
    return pl.pallas_call(
        functools.partial(_nsa_prompt_kernel, n_cmp=s // CMP_STRIDE - 1, n_sel=s // SEL_BLOCK),
        out_shape=jax.ShapeDtypeStruct((b, N_HEADS, HEAD_DIM, s), F32),
        grid=(b, KV_HEADS, nq),
        in_specs=[
            pl.BlockSpec((BLK, GROUP * HEAD_DIM), lambda bi, kh, i: (bi * nq + i, kh)),
            pl.BlockSpec((1, 2, 1, s // CMP_STRIDE, 2 * HEAD_DIM), lambda bi, kh, i: (bi, 0, kh, 0, 0)),
            const(pe.shape), const(w1f.shape), const(b1.shape), const(w2.shape), const(b2.shape),
            pl.BlockSpec((s, HEAD_DIM), lambda bi, kh, i: (bi, kcol + 2 * KV_HEADS + kh)),
            pl.BlockSpec((s, HEAD_DIM), lambda bi, kh, i: (bi, kcol + 3 * KV_HEADS + kh)),
            pl.BlockSpec((s, HEAD_DIM), lambda bi, kh, i: (bi, kcol + 4 * KV_HEADS + kh)),
            pl.BlockSpec((s, HEAD_DIM), lambda bi, kh, i: (bi, kcol + 5 * KV_HEADS + kh)),
            pl.BlockSpec((GROUP, nq, BLK, BLK), lambda bi, kh, i: (kh, 0, 0, 0)),
            pl.BlockSpec((GROUP, 1, s // CMP_STRIDE, BLK), lambda bi, kh, i: (kh, 0, 0, i)),
            pl.BlockSpec((1, 16, BLK), lambda bi, kh, i: (kh, 0, bi * nq + i)),
            const(msel.shape),
        ],
        out_specs=pl.BlockSpec((1, GROUP, HEAD_DIM, BLK), lambda bi, kh, i: (bi, kh, 0, i)),
        scratch_shapes=[pltpu.VMEM((s // CMP_STRIDE, HEAD_DIM), BF16), pltpu.VMEM((s // CMP_STRIDE, HEAD_DIM), BF16),
                        pltpu.VMEM((s // SEL_BLOCK, BLK), F32),
                        pltpu.VMEM((HEAD_DIM, s), BF16), pltpu.VMEM((HEAD_DIM, s), BF16)],
        compiler_params=_params(("parallel", "parallel", "arbitrary")),
        name="nsa_prompt",
    )(proj, a_cmp, pe, w1f, b1, w2, b2, proj, proj, proj, proj, tbl, bias_c, gates_t, msel)


def _masked_softmax_rows(s, mask):
    s = jnp.where(mask, s, NEG)
    m = jnp.max(s, axis=-1, keepdims=True)
    p = jnp.where(mask, jnp.exp(s - m), 0.0)
    return p / jnp.maximum(jnp.sum(p, axis=-1, keepdims=True), 1e-30)


def _nsa_sample_select_kernel(q_ref, a_ref, pe_ref, w1f_ref, b1_ref, w2_ref, b2_ref, biasc_ref, msel_ref,
                              regroup_ref, ocmp_ref, sel_ref, *, n_cmp, n_sel, past):
    pcs = []
    for kh in range(KV_HEADS):
        kc, vc = (_finish_compress(a_ref[0, c, kh], pe_ref[c], w1f_ref[c], b1_ref[c], w2_ref[c], b2_ref[c],
                                   n_cmp).astype(BF16) for c in range(2))
        bias = biasc_ref[kh]
        s = lax.dot_general(q_ref[0, kh].astype(BF16), kc, NT_DIMS, preferred_element_type=F32) * SCALE + bias
        pn = _masked_softmax_rows(s, bias > 0.5 * NEG)
        ocmp_ref[0, kh] = jnp.dot(pn.astype(BF16), vc, preferred_element_type=F32)
        pcs.append(((pn[0:8] + pn[8:16]) + pn[16:24]) + pn[24:32])
    pc = jnp.concatenate(pcs, axis=0)
    msel = msel_ref[...]
    p_slc = sum(jnp.dot(part, msel, preferred_element_type=F32) for part in _split3(pc))
    blk = lax.broadcasted_iota(jnp.int32, p_slc.shape, 1)
    cur = (past + lax.broadcasted_iota(jnp.int32, p_slc.shape, 0) % 8) // SEL_BLOCK
    sel = _top_n(_sel_scores(p_slc, blk, cur, n_sel), SEL_TOPN, 1).astype(BF16)
    for j in range(regroup_ref.shape[0]):
        part = jnp.dot(sel, regroup_ref[j], preferred_element_type=F32)
        for kh in range(KV_HEADS):
            sel_ref[0, kh, j] = part[kh * 8:(kh + 1) * 8]


def _nsa_sample_select(q, a_cmp, cmp_w, bias_c, msel, regroup, past, t_len):
    pe, w1f, b1, w2, b2 = cmp_w
    n = q.shape[0]
    chunks = a_cmp.shape[3]
    n_steps = regroup.shape[0]
    const = lambda shape: pl.BlockSpec(shape, lambda i: (0,) * len(shape))
    return pl.pallas_call(
        functools.partial(_nsa_sample_select_kernel, n_cmp=chunks - 1, n_sel=(past + t_len + SEL_BLOCK - 1) // SEL_BLOCK,
                          past=past),
        out_shape=(jax.ShapeDtypeStruct((n, KV_HEADS, GROUP * 8, HEAD_DIM), F32),
                   jax.ShapeDtypeStruct((n, KV_HEADS, n_steps, 8, BLK), F32)),
        grid=(n,),
        in_specs=[
            pl.BlockSpec((1, KV_HEADS, GROUP * 8, HEAD_DIM), lambda i: (i, 0, 0, 0)),
            pl.BlockSpec((1, 2, KV_HEADS, chunks, 2 * HEAD_DIM), lambda i: (i, 0, 0, 0, 0)),
            const(pe.shape), const(w1f.shape), const(b1.shape), const(w2.shape), const(b2.shape),
            const(bias_c.shape), const(msel.shape), const(regroup.shape),
        ],
        out_specs=(pl.BlockSpec((1, KV_HEADS, GROUP * 8, HEAD_DIM), lambda i: (i, 0, 0, 0)),
                   pl.BlockSpec((1, KV_HEADS, n_steps, 8, BLK), lambda i: (i, 0, 0, 0, 0))),
        compiler_params=_params(("parallel",)),
        name="nsa_sample_select",
    )(q, a_cmp, pe, w1f, b1, w2, b2, bias_c, msel, regroup)


def _nsa_sample_attend_kernel(pt_ref, *refs):
    pages = refs[:PAGES_PER_STEP]
    (q_ref, sel_ref, expand_ref, bias_ref, kn_ref, vn_ref, biasn_ref, win_ref, kwn_ref, vwn_ref, biasw_ref,
     biaswn_ref, gates_ref, ocmp_ref, o_ref, m_scr, l_scr, acc_scr, kv_scr, win_scr) = refs[PAGES_PER_STEP:]
    j, kh = pl.program_id(1), pl.program_id(2)
    n_steps = pl.num_programs(1)
    qs = q_ref[0, 0].astype(BF16)

    @pl.when(j == 0)
    def _():
        m_scr[kh] = jnp.full((GROUP * 8, 1), NEG, F32)
        l_scr[kh] = jnp.zeros((GROUP * 8, 1), F32)
        acc_scr[kh] = jnp.zeros((GROUP * 8, HEAD_DIM), F32)

    def picked(step):
        sel = sel_ref[0, 0, step].astype(BF16)
        return jnp.concatenate([sel] * GROUP, axis=0)

    def update(k, v, bias, mask):
        s = lax.dot_general(qs, k, NT_DIMS, preferred_element_type=F32) * SCALE + bias
        s = jnp.where(mask, s, NEG)
        m_old = m_scr[kh]
        m_new = jnp.maximum(m_old, jnp.max(s, axis=-1, keepdims=True))
        alpha = jnp.exp(m_old - m_new)
        p = jnp.where(mask, jnp.exp(s - m_new), 0.0)
        l_scr[kh] = alpha * l_scr[kh] + jnp.sum(p, axis=-1, keepdims=True)
        acc_scr[kh] = alpha * acc_scr[kh] + jnp.dot(p.astype(BF16), v, preferred_element_type=F32)
        m_scr[kh] = m_new

    @pl.when(kh == 0)
    def _():
        for p, pg in enumerate(pages):
            kv_scr[:, p * BLK:(p + 1) * BLK, :] = _heads_first(pg[...])

    in_sel = jnp.dot(picked(j), expand_ref[...], preferred_element_type=F32) > 0.5
    update(kv_scr[kh].astype(BF16), kv_scr[KV_HEADS + kh].astype(BF16), bias_ref[0], in_sel)

    @pl.when(j == n_steps - 1)
    def _():
        biasn = biasn_ref[0]
        new_sel = picked(n_steps)[:, 0:1] > 0.5
        update(kn_ref[0, 0].astype(BF16), vn_ref[0, 0].astype(BF16), biasn, (biasn > 0.5 * NEG) & new_sel)
        o_sel = acc_scr[kh] / jnp.maximum(l_scr[kh], 1e-30)
        @pl.when(kh == 0)
        def _():
            win_scr[...] = _heads_first(win_ref[...])

        biasw = jnp.concatenate([biasw_ref[0], biaswn_ref[0]], axis=1)
        kw = jnp.concatenate([win_scr[kh], kwn_ref[0, 0]], axis=0).astype(BF16)
        vw = jnp.concatenate([win_scr[KV_HEADS + kh], vwn_ref[0, 0]], axis=0).astype(BF16)
        sw = lax.dot_general(qs, kw, NT_DIMS, preferred_element_type=F32) * SCALE + biasw
        pw = _masked_softmax_rows(sw, biasw > 0.5 * NEG)
        o_win = jnp.dot(pw.astype(BF16), vw, preferred_element_type=F32)
        o_ref[0, kh] = (gates_ref[0, 0, 0] * ocmp_ref[0, 0] + gates_ref[1, 0, 0] * o_sel) + gates_ref[2, 0, 0] * o_win


def _nsa_sample_attend(pool, page_table, q, sel, expand, bias_sel, k_new, v_new, bias_new, win_buf, kw_new, vw_new,
                       bias_win, bias_win_new, gates, o_cmp):
    n, n_pages = page_table.shape
    n_steps = n_pages // PAGES_PER_STEP
    keys = PAGES_PER_STEP * BLK
    rows = GROUP * 8
    lw = win_buf.shape[1]
    per = lambda shape: pl.BlockSpec((1, 1) + shape, lambda i, j, kh, pt: (i, kh) + (0,) * len(shape))
    by_head = lambda shape: pl.BlockSpec((1,) + shape, lambda i, j, kh, pt: (kh,) + (0,) * len(shape))
    grid_spec = pltpu.PrefetchScalarGridSpec(
        num_scalar_prefetch=1,
        grid=(n, n_steps, KV_HEADS),
        in_specs=[_page_spec(p, 1) for p in range(PAGES_PER_STEP)] + [
            per((rows, HEAD_DIM)),
            pl.BlockSpec((1, 1, n_steps + 1, 8, BLK), lambda i, j, kh, pt: (i, kh, 0, 0, 0)),
            pl.BlockSpec(expand.shape, lambda i, j, kh, pt: (0, 0)),
            pl.BlockSpec((1, rows, keys), lambda i, j, kh, pt: (kh, 0, j)),
            per((BLK, HEAD_DIM)), per((BLK, HEAD_DIM)), by_head((rows, BLK)),
            pl.BlockSpec((None, lw, HEAD_TILE, HEAD_DIM), lambda i, j, kh, pt: (i, 0, 0, 0)),
            per((BLK, HEAD_DIM)), per((BLK, HEAD_DIM)), by_head((rows, lw)), by_head((rows, BLK)),
            pl.BlockSpec((3, 1, 1, rows, HEAD_DIM), lambda i, j, kh, pt: (0, i, kh, 0, 0)),
            per((rows, HEAD_DIM)),
        ],
        out_specs=pl.BlockSpec((1, KV_HEADS, rows, HEAD_DIM), lambda i, j, kh, pt: (i, 0, 0, 0)),
        scratch_shapes=[pltpu.VMEM((KV_HEADS, rows, 1), F32), pltpu.VMEM((KV_HEADS, rows, 1), F32),
                        pltpu.VMEM((KV_HEADS, rows, HEAD_DIM), F32),
                        pltpu.VMEM((HEAD_TILE, keys, HEAD_DIM), F32), pltpu.VMEM((HEAD_TILE, lw, HEAD_DIM), F32)],
    )
    return pl.pallas_call(
        _nsa_sample_attend_kernel,
        out_shape=jax.ShapeDtypeStruct((n, KV_HEADS, rows, HEAD_DIM), F32),
        grid_spec=grid_spec,
        compiler_params=_params(("parallel", "arbitrary", "arbitrary")),
        name="nsa_sample_attend",
    )(page_table, *([pool] * PAGES_PER_STEP), q, sel, expand, bias_sel, k_new, v_new, bias_new, win_buf, kw_new,
      vw_new, bias_win, bias_win_new, gates, o_cmp)


def _sel_weights(n_cmp_rows, n_sel_cols):
    ratio = SEL_BLOCK // CMP_STRIDE
    span = CMP_BLOCK // CMP_STRIDE
    c = jnp.arange(n_cmp_rows)[:, None]
    j = jnp.arange(n_sel_cols)[None, :]
    o = c - ratio * j + (span - 1)
    cnt = jnp.minimum(o, span - 1) - jnp.maximum(o - (ratio - 1), 0) + 1
    return jnp.where((o >= 0) & (o <= ratio + span - 2), cnt, 0).astype(BF16)


def _layer_b(hp, hs, b, s, n, t, norm_g, w_in, w_out, rel_bias, cmp, pool, page_table, win_buf):
    cmp_pe, cmp_w1, cmp_b1, cmp_w2, cmp_b2 = cmp
    n_kvcol = 6 * NSA_KV
    w_main = w_in[:, :NSA_Q + n_kvcol].astype(BF16)
    w_gate = jnp.pad(w_in[:, NSA_Q + n_kvcol:], ((0, 0), (0, BLK - 3 * N_HEADS))).astype(BF16)
    w1 = cmp_w1.reshape(2, 2, CMP_STRIDE, HEAD_DIM, HEAD_DIM)
    w1r = jnp.concatenate([w1[:, 0], w1[:, 1]], axis=-1).astype(BF16)
    pe_row = jnp.pad(cmp_pe.reshape(2, 1, CMP_BLOCK * HEAD_DIM), ((0, 0), (0, 7), (0, 0))).astype(BF16)
    cmp_w = (pe_row, cmp_w1.reshape(2, CMP_BLOCK * HEAD_DIM, HEAD_DIM).astype(BF16), cmp_b1.reshape(2, 1, HEAD_DIM),
             cmp_w2.astype(BF16), cmp_b2.reshape(2, 1, HEAD_DIM))
    weights = (w_main, w_gate, w_out, w1r, cmp_w)
    hp, new_win_p, new_kv_p = _nsa_prompt_path(hp, b, s, norm_g, weights, rel_bias)
    hs, new_win_s, new_kv_s = _nsa_sample_path(hs, n, t, norm_g, weights, rel_bias, pool, page_table, win_buf)
    return hp, hs, new_win_p, new_win_s, new_kv_p, new_kv_s


def _nsa_prompt_path(hp, b, s, norm_g, weights, rel_bias):
    w_main, w_gate, w_out, w1r, cmp_w = weights
    a_p = _rmsnorm(hp, norm_g, BF16)
    proj_p = _matmul(a_p, w_main)
    gates_p = _matmul(a_p, w_gate, sigmoid=True)
    kv_p = proj_p[:, NSA_Q:].reshape(b, s, 6, KV_HEADS, HEAD_DIM)
    pages_p = proj_p[:, NSA_Q:NSA_Q + 2 * NSA_KV].reshape(b * s, HEAD_TILE, HEAD_DIM)
    table_p = jnp.arange(b * s // BLK, dtype=jnp.int32).reshape(b, s // BLK)
    a_cmp_p = _cmp_proj(pages_p, table_p, w1r)
    nq = s // BLK
    tbl = _bias_table(rel_bias, nq, BLK, BLK, 0, BLK, -1, 1, -(1 << 30), 1 << 30)
    bias_c = _bias_table(rel_bias, 1, s // CMP_STRIDE, s, -(CMP_BLOCK - 1), 0, -CMP_STRIDE, 1, 0, 1 << 30)
    gates_t = gates_p[:, :3 * N_HEADS].reshape(b * s, 3, KV_HEADS, GROUP).transpose(2, 1, 3, 0)
    gates_t = jnp.pad(gates_t.reshape(KV_HEADS, 3 * GROUP, b * s), ((0, 0), (0, 16 - 3 * GROUP), (0, 0)))
    msel_p = _sel_weights(s // CMP_STRIDE, s // SEL_BLOCK).T
    o_t = _nsa_prompt(proj_p, a_cmp_p, gates_t, cmp_w, tbl, bias_c, msel_p, b, s)
    o_p = o_t.transpose(0, 3, 1, 2).reshape(b * s, NSA_Q).astype(BF16)
    hp = _matmul_residual(o_p, w_out, hp)
    keep = min(NSA_WINDOW, s)
    return hp, kv_p[:, s - keep:, 4:], kv_p[:, :, :4]


def _nsa_sample_path(hs, n, t, norm_g, weights, rel_bias, pool, page_table, win_buf):
    w_main, w_gate, w_out, w1r, cmp_w = weights
    past = page_table.shape[1] * BLK
    a_s = _rmsnorm(hs, norm_g, BF16)
    proj_s = _matmul(a_s, w_main)
    gates_s = _matmul(a_s, w_gate, sigmoid=True)
    kv_s = proj_s[:, NSA_Q:].reshape(n, t, 6, KV_HEADS, HEAD_DIM)
    rows = GROUP * 8

    def head_rows(x):
        x = jnp.pad(x.transpose(0, 2, 3, 1, 4), ((0, 0), (0, 0), (0, 0), (0, 8 - t), (0, 0)))
        return x.reshape(n, KV_HEADS, rows, x.shape[-1])

    def new_rows(c):
        return jnp.pad(kv_s[:, :, c].transpose(0, 2, 1, 3), ((0, 0), (0, 0), (0, BLK - t), (0, 0)))

    def head_table(x, cols):
        return x.reshape(KV_HEADS, rows, cols)

    q_s = head_rows(proj_s[:, :NSA_Q].reshape(n, t, KV_HEADS, GROUP, HEAD_DIM))
    pool2 = pool.reshape(pool.shape[0] * BLK, 2 * HEAD_TILE, HEAD_DIM)
    a_cmp_s = _cmp_proj(pool2, page_table, w1r)
    chunks = past // CMP_STRIDE
    big = 1 << 30
    bias_cs = head_table(_bias_table(rel_bias, 1, 8, chunks, past - (CMP_BLOCK - 1), 0, 1, -CMP_STRIDE, 0, big), chunks)
    n_steps = page_table.shape[1] // PAGES_PER_STEP
    n_sel_pad = (n_steps + 1) * BLK
    msel_s = _sel_weights(chunks, n_sel_pad)
    per_step = PAGES_PER_STEP * BLK // SEL_BLOCK
    jj = jnp.arange(n_sel_pad)[None, :, None]
    ll = jnp.arange(BLK)[None, None, :]
    st = jnp.arange(n_steps + 1)[:, None, None]
    regroup = ((jj == st * per_step + ll) & (ll < per_step)).astype(BF16)
    o_cmp_s, sel_s = _nsa_sample_select(q_s, a_cmp_s, cmp_w, bias_cs, msel_s, regroup, past, t)
    expand = (jnp.arange(BLK)[:, None] == jnp.arange(PAGES_PER_STEP * BLK)[None, :] // SEL_BLOCK).astype(BF16)
    bias_sel = head_table(_bias_table(rel_bias, 1, 8, past, past, 0, 1, -1, 0, big), past)
    bias_new = head_table(_bias_table(rel_bias, 1, 8, BLK, 0, 0, 1, -1, 0, big), BLK)
    lw = win_buf.shape[1]
    bias_win = head_table(_bias_table(rel_bias, 1, 8, lw, lw, 0, 1, -1, 0, NSA_WINDOW - 1), lw)
    bias_win_new = head_table(_bias_table(rel_bias, 1, 8, BLK, 0, 0, 1, -1, 0, NSA_WINDOW - 1), BLK)
    g_s = gates_s[:, :3 * N_HEADS].reshape(n, t, 3, KV_HEADS, GROUP, 1)
    g_s = jnp.stack([head_rows(g_s[:, :, c]) for c in range(3)])
    g_s = jnp.broadcast_to(g_s, (3, n, KV_HEADS, rows, HEAD_DIM))
    o_s = _nsa_sample_attend(pool2, page_table, q_s, sel_s, expand, bias_sel, new_rows(2), new_rows(3), bias_new,
                             win_buf.reshape(n, lw, HEAD_TILE, HEAD_DIM), new_rows(4), new_rows(5), bias_win,
                             bias_win_new,
                             g_s, o_cmp_s)
    o_s = o_s.reshape(n, KV_HEADS, GROUP, 8, HEAD_DIM)[:, :, :, :t].transpose(0, 3, 1, 2, 4)
    hs = _matmul_residual(o_s.reshape(n * t, NSA_Q).astype(BF16), w_out, hs)
    new_win_s = jnp.concatenate([win_buf, kv_s[:, :, 4:]], axis=1)[:, t:]
    return hs, new_win_s, kv_s[:, :, :4]


def _ffn_and_ple(hp, hs, b, s, n, t, i, norm_ffn, norm_ple, w_ffn_in, conv_w, conv_b, w_ffn_out, state_conv,
                 p_prompt, p_sample, w_ple_gate, w_ple_proj):
    hu_p = _matmul(_rmsnorm(hp, norm_ffn, BF16), w_ffn_in)
    hu_s = _matmul(_rmsnorm(hs, norm_ffn, BF16), w_ffn_in)
    conv_p = hu_p.reshape(b, s, 2 * D_FF)[:, s - (CONV_W - 1):, :D_FF]
    conv_s = jnp.concatenate([state_conv, hu_s.reshape(n, t, 2 * D_FF)[:, :, :D_FF]], axis=1)[:, t:]
    hp = _ffn_out_prompt(hu_p, conv_w, conv_b, w_ffn_out, hp, s)
    hs = _ffn_out_sample(hu_s, state_conv, conv_w, conv_b, w_ffn_out, hs, t)
    hp = _ple_add(_rmsnorm(hp, norm_ple, BF16), w_ple_gate, p_prompt.astype(BF16), w_ple_proj, hp)
    hs = _ple_add(_rmsnorm(hs, norm_ple, BF16), w_ple_gate, p_sample.astype(BF16), w_ple_proj, hs)
    return hp, hs, conv_p, conv_s


def kernel(x_prompt, x_sample, state_dil_w128, state_dil_w512, state_dil_w2048, state_nsa_win, state_conv,
           cache_nsa_kv, page_table, p_prompt, p_sample, rel_bias, norm_mix, norm_ffn, norm_ple, norm_final,
           w_in_a, w_out_a, w_in_b, w_out_b, cmp_pe, cmp_w1, cmp_b1, cmp_w2, cmp_b2, w_ffn_in, conv_w, conv_b,
           w_ffn_out, w_ple_gate, w_ple_proj):
    b, s, d = x_prompt.shape
    n, t, _ = x_sample.shape
    depth = norm_mix.shape[0]
    hp, hs = x_prompt.reshape(b * s, d), x_sample.reshape(n * t, d)
    dil_p, dil_s = [[] for _ in range(N_DIL)], [[] for _ in range(N_DIL)]
    win_p, win_s, kv_p, kv_s, conv_p, conv_s = [], [], [], [], [], []
    for i in range(depth):
        li = i // 2
        if i % 2 == 0:
            hp, hs, new_p, new_s = _layer_a(
                hp, hs, b, s, n, t, norm_mix[i], w_in_a[li].astype(BF16), w_out_a[li].astype(BF16), rel_bias,
                (state_dil_w128[li], state_dil_w512[li], state_dil_w2048[li]))
            for g in range(N_DIL):
                dil_p[g].append(new_p[g])
                dil_s[g].append(new_s[g])
        else:
            hp, hs, wp, ws, rp, rs = _layer_b(
                hp, hs, b, s, n, t, norm_mix[i], w_in_b[li], w_out_b[li].astype(BF16), rel_bias,
                (cmp_pe[li], cmp_w1[li], cmp_b1[li], cmp_w2[li], cmp_b2[li]), cache_nsa_kv[li], page_table,
                state_nsa_win[li])
            win_p.append(wp)
            win_s.append(ws)
            kv_p.append(rp)
            kv_s.append(rs)
        hp, hs, cp, cs = _ffn_and_ple(
            hp, hs, b, s, n, t, i, norm_ffn[i], norm_ple[i], w_ffn_in[i].astype(BF16), conv_w[i], conv_b[i],
            w_ffn_out[i].astype(BF16), state_conv[i], p_prompt[i].reshape(b * s, -1), p_sample[i].reshape(n * t, -1),
            w_ple_gate[i].astype(BF16), w_ple_proj[i].astype(BF16))
        conv_p.append(cp)
        conv_s.append(cs)
    y_prompt = _rmsnorm(hp, norm_final, F32).reshape(b, s, d)
    y_sample = _rmsnorm(hs, norm_final, F32).reshape(n, t, d)
    return (y_prompt, y_sample,
            jnp.stack(dil_p[0]), jnp.stack(dil_s[0]), jnp.stack(dil_p[1]), jnp.stack(dil_s[1]),
            jnp.stack(dil_p[2]), jnp.stack(dil_s[2]),
            jnp.stack(win_p), jnp.stack(win_s), jnp.stack(conv_p), jnp.stack(conv_s),
            jnp.stack(kv_p), jnp.stack(kv_s))
```

```python
import functools

import jax
import jax.numpy as jnp
from jax import lax
from jax.experimental import pallas as pl
from jax.experimental.pallas import tpu as pltpu

F32 = jnp.float32
BF16 = jnp.bfloat16

D_MODEL = 2048
HEAD_DIM = 128
N_HEADS = 16
DIL_PAIRS = ((128, 1), (512, 4), (2048, 16))
N_DIL = 3
BLK = 128
KV_HEADS = 4
GROUP = 4
CMP_BLOCK = 32
CMP_STRIDE = 16
SEL_BLOCK = 64
SEL_TOPN = 16
NSA_WINDOW = 512
D_FF = 5632
CONV_W = 3
REL_BUCKETS = 32
EPS = 1e-6
NEG = -1e30
FORCED_SCORE = 1e4
SCALE = HEAD_DIM ** -0.5
QKV_A = N_DIL * 3 * N_HEADS * HEAD_DIM
NSA_Q = N_HEADS * HEAD_DIM
NSA_KV = KV_HEADS * HEAD_DIM

BUCKET_START = (1, 2, 3, 4, 5, 6, 7, 8, 9, 10, 11, 12, 13, 14, 15, 16, 22, 30, 40, 54, 73, 99,
                134, 182, 246, 332, 450, 609, 825, 1117, 1513)

VMEM_LIMIT_V7X = 56 * 1024 * 1024


def _params(sem, vmem=VMEM_LIMIT_V7X):
    return pltpu.CompilerParams(dimension_semantics=sem, vmem_limit_bytes=vmem)


def _pick(n, cands):
    for c in cands:
        if n % c == 0:
            return c
    return n


def _rmsnorm_kernel(x_ref, g_ref, o_ref):
    x = x_ref[...]
    ms = jnp.mean(x * x, axis=-1, keepdims=True)
    o_ref[...] = ((x * lax.rsqrt(ms + EPS)) * g_ref[...]).astype(o_ref.dtype)


def _rmsnorm(x, g, out_dtype):
    m, d = x.shape
    tm = _pick(m, (512, 256, 128, 32))
    return pl.pallas_call(
        _rmsnorm_kernel,
        out_shape=jax.ShapeDtypeStruct((m, d), out_dtype),
        grid=(m // tm,),
        in_specs=[pl.BlockSpec((tm, d), lambda i: (i, 0)), pl.BlockSpec((1, d), lambda i: (0, 0))],
        out_specs=pl.BlockSpec((tm, d), lambda i: (i, 0)),
        compiler_params=_params(("parallel",)),
        name="rmsnorm",
    )(x, g.reshape(1, d))


def _mm_kernel(a_ref, w_ref, o_ref, *, sigmoid):
    acc = jnp.dot(a_ref[...], w_ref[...], preferred_element_type=F32)
    if sigmoid:
        acc = jax.nn.sigmoid(acc)
    o_ref[...] = acc.astype(o_ref.dtype)


def _matmul(a, w, out_dtype=F32, sigmoid=False):
    m, k = a.shape
    n = w.shape[1]
    tm = _pick(m, (1024, 512, 256, 128))
    tn = _pick(n, (1024, 512, 256, 128))
    return pl.pallas_call(
        functools.partial(_mm_kernel, sigmoid=sigmoid),
        out_shape=jax.ShapeDtypeStruct((m, n), out_dtype),
        grid=(m // tm, n // tn),
        in_specs=[pl.BlockSpec((tm, k), lambda i, j: (i, 0)), pl.BlockSpec((k, tn), lambda i, j: (0, j))],
        out_specs=pl.BlockSpec((tm, tn), lambda i, j: (i, j)),
        compiler_params=_params(("parallel", "arbitrary")),
        name="matmul",
    )(a, w)


def _mm_res_kernel(a_ref, w_ref, r_ref, o_ref):
    o_ref[...] = r_ref[...] + jnp.dot(a_ref[...], w_ref[...], preferred_element_type=F32)


def _matmul_residual(a, w, res):
    m, k = a.shape
    n = w.shape[1]
    tm = _pick(m, (1024, 512, 256, 128))
    tn = _pick(n, (1024, 512, 256, 128))
    return pl.pallas_call(
        _mm_res_kernel,
        out_shape=jax.ShapeDtypeStruct((m, n), F32),
        grid=(m // tm, n // tn),
        in_specs=[pl.BlockSpec((tm, k), lambda i, j: (i, 0)), pl.BlockSpec((k, tn), lambda i, j: (0, j)),
                  pl.BlockSpec((tm, tn), lambda i, j: (i, j))],
        out_specs=pl.BlockSpec((tm, tn), lambda i, j: (i, j)),
        compiler_params=_params(("parallel", "arbitrary")),
        name="matmul_residual",
    )(a, w, res)


def _conv_gelu_val(g, g1, g2, val, cw_ref, cb_ref):
    c = cb_ref[...] + g2 * cw_ref[0:1, :]
    c = c + g1 * cw_ref[1:2, :]
    c = c + g * cw_ref[2:3, :]
    return (jax.nn.gelu(c) * val).astype(BF16)


def _ffn_out_prompt_kernel(g_ref, halo_ref, v_ref, cw_ref, cb_ref, w_ref, r_ref, o_ref, acc_ref, *,
                           tiles_per_seq):
    i, k = pl.program_id(0), pl.program_id(1)

    @pl.when(k == 0)
    def _():
        acc_ref[...] = jnp.zeros_like(acc_ref)

    g = g_ref[...]
    row = lax.broadcasted_iota(jnp.int32, g.shape, 0)
    halo = jnp.where(i % tiles_per_seq == 0, 0.0, halo_ref[...])
    g1 = jnp.where(row == 0, halo[7:8, :], pltpu.roll(g, 1, 0))
    g2 = jnp.where(row == 0, halo[6:7, :], jnp.where(row == 1, halo[7:8, :], pltpu.roll(g, 2, 0)))
    u = _conv_gelu_val(g, g1, g2, v_ref[...], cw_ref, cb_ref)
    acc_ref[...] += jnp.dot(u, w_ref[...], preferred_element_type=F32)

    @pl.when(k == pl.num_programs(1) - 1)
    def _():
        o_ref[...] = r_ref[...] + acc_ref[...]


def _ffn_out_prompt(hu, conv_w, conv_b, w_out, res, seq):
    m = hu.shape[0]
    tm, tk = 1024, 512
    nk = D_FF // tk
    return pl.pallas_call(
        functools.partial(_ffn_out_prompt_kernel, tiles_per_seq=seq // tm),
        out_shape=jax.ShapeDtypeStruct((m, D_MODEL), F32),
        grid=(m // tm, nk),
        in_specs=[
            pl.BlockSpec((tm, tk), lambda i, k: (i, k)),
            pl.BlockSpec((8, tk), lambda i, k: (jnp.maximum(i * (tm // 8) - 1, 0), k)),
            pl.BlockSpec((tm, tk), lambda i, k: (i, k + nk)),
            pl.BlockSpec((CONV_W, tk), lambda i, k: (0, k)),
            pl.BlockSpec((1, tk), lambda i, k: (0, k)),
            pl.BlockSpec((tk, D_MODEL), lambda i, k: (k, 0)),
            pl.BlockSpec((tm, D_MODEL), lambda i, k: (i, 0)),
        ],
        out_specs=pl.BlockSpec((tm, D_MODEL), lambda i, k: (i, 0)),
        scratch_shapes=[pltpu.VMEM((tm, D_MODEL), F32)],
        compiler_params=_params(("parallel", "arbitrary")),
        name="ffn_out_prompt",
    )(hu, hu, hu, conv_w, conv_b.reshape(1, D_FF), w_out, res)


def _ffn_out_sample_kernel(g_ref, e1_ref, e2_ref, v_ref, cw_ref, cb_ref, w_ref, r_ref, o_ref, acc_ref, *,
                           t_len):
    k = pl.program_id(0)

    @pl.when(k == 0)
    def _():
        acc_ref[...] = jnp.zeros_like(acc_ref)

    g = g_ref[...]
    t = lax.broadcasted_iota(jnp.int32, g.shape, 0) % t_len
    g1 = jnp.where(t == 0, e1_ref[...], pltpu.roll(g, 1, 0))
    g2 = jnp.where(t < 2, e2_ref[...], pltpu.roll(g, 2, 0))
    u = _conv_gelu_val(g, g1, g2, v_ref[...], cw_ref, cb_ref)
    acc_ref[...] += jnp.dot(u, w_ref[...], preferred_element_type=F32)

    @pl.when(k == pl.num_programs(0) - 1)
    def _():
        o_ref[...] = r_ref[...] + acc_ref[...]


def _ffn_out_sample(hu, conv_prev, conv_w, conv_b, w_out, res, t_len):
    m = hu.shape[0]
    n = m // t_len
    tk = 512
    nk = D_FF // tk
    zeros = jnp.zeros((n, t_len - 1, D_FF), F32)
    e1 = jnp.concatenate([conv_prev[:, 1:2], zeros], axis=1).reshape(m, D_FF)
    e2 = jnp.concatenate([conv_prev, zeros[:, 1:]], axis=1).reshape(m, D_FF)
    return pl.pallas_call(
        functools.partial(_ffn_out_sample_kernel, t_len=t_len),
        out_shape=jax.ShapeDtypeStruct((m, D_MODEL), F32),
        grid=(nk,),
        in_specs=[
            pl.BlockSpec((m, tk), lambda k: (0, k)),
            pl.BlockSpec((m, tk), lambda k: (0, k)),
            pl.BlockSpec((m, tk), lambda k: (0, k)),
            pl.BlockSpec((m, tk), lambda k: (0, k + nk)),
            pl.BlockSpec((CONV_W, tk), lambda k: (0, k)),
            pl.BlockSpec((1, tk), lambda k: (0, k)),
            pl.BlockSpec((tk, D_MODEL), lambda k: (k, 0)),
            pl.BlockSpec((m, D_MODEL), lambda k: (0, 0)),
        ],
        out_specs=pl.BlockSpec((m, D_MODEL), lambda k: (0, 0)),
        scratch_shapes=[pltpu.VMEM((m, D_MODEL), F32)],
        compiler_params=_params(("arbitrary",)),
        name="ffn_out_sample",
    )(hu, e1, e2, hu, conv_w, conv_b.reshape(1, D_FF), w_out, res)


def _ple_kernel(a_ref, wg_ref, p_ref, wp_ref, h_ref, o_ref):
    gate = jax.nn.sigmoid(jnp.dot(a_ref[...], wg_ref[...], preferred_element_type=F32))
    proj = jnp.dot(p_ref[...], wp_ref[...], preferred_element_type=F32)
    o_ref[...] = h_ref[...] + gate * proj


def _ple_add(a, w_gate, p, w_proj, h):
    m, d = h.shape
    kp = p.shape[1]
    tm = _pick(m, (1024, 512, 256, 128))
    tn = 1024
    return pl.pallas_call(
        _ple_kernel,
        out_shape=jax.ShapeDtypeStruct((m, d), F32),
        grid=(m // tm, d // tn),
        in_specs=[
            pl.BlockSpec((tm, d), lambda i, j: (i, 0)),
            pl.BlockSpec((d, tn), lambda i, j: (0, j)),
            pl.BlockSpec((tm, kp), lambda i, j: (i, 0)),
            pl.BlockSpec((kp, tn), lambda i, j: (0, j)),
            pl.BlockSpec((tm, tn), lambda i, j: (i, j)),
        ],
        out_specs=pl.BlockSpec((tm, tn), lambda i, j: (i, j)),
        compiler_params=_params(("parallel", "arbitrary")),
        name="ple_add",
    )(a, w_gate, p, w_proj, h)


def _bias_table_kernel(rb_ref, o_ref, *, a0, ag, ar, ac, lo, hi, mod):
    h, g = pl.program_id(0), pl.program_id(1)
    shape = o_ref.shape[2:]
    dist = (a0 + ag * g + ar * lax.broadcasted_iota(jnp.int32, shape, 0)
            + ac * lax.broadcasted_iota(jnp.int32, shape, 1))
    d = jnp.maximum(dist, 0)
    out = jnp.full(shape, rb_ref[0, h], F32)
    for k in range(1, REL_BUCKETS):
        out = jnp.where(d >= BUCKET_START[k - 1], rb_ref[k, h], out)
    ok = (dist >= lo) & (dist <= hi)
    if mod > 1:
        ok = ok & ((d & (mod - 1)) == 0)
    o_ref[0, 0] = jnp.where(ok, out, NEG)


def _bias_table(rel_bias, n_g, n_r, n_c, a0, ag, ar, ac, lo, hi, mod=1):
    assert mod & (mod - 1) == 0
    return pl.pallas_call(
        functools.partial(_bias_table_kernel, a0=a0, ag=ag, ar=ar, ac=ac, lo=lo, hi=hi, mod=mod),
        out_shape=jax.ShapeDtypeStruct((N_HEADS, n_g, n_r, n_c), F32),
        grid=(N_HEADS, n_g),
        in_specs=[pl.BlockSpec(memory_space=pltpu.SMEM)],
        out_specs=pl.BlockSpec((1, 1, n_r, n_c), lambda h, g: (h, g, 0, 0)),
        compiler_params=_params(("parallel", "parallel")),
        name="bias_table",
    )(rel_bias)


def _mm_heads_kernel(a_ref, w_ref, o_ref):
    acc = jnp.dot(a_ref[...], w_ref[...], preferred_element_type=F32)
    for j in range(o_ref.shape[0]):
        o_ref[j] = acc[:, j * HEAD_DIM:(j + 1) * HEAD_DIM]


def _matmul_heads(a, w):
    m, k = a.shape
    n = w.shape[1]
    tm, tn = _pick(m, (1024, 512, 256, 128)), 1024
    return pl.pallas_call(
        _mm_heads_kernel,
        out_shape=jax.ShapeDtypeStruct((n // HEAD_DIM, m, HEAD_DIM), F32),
        grid=(m // tm, n // tn),
        in_specs=[pl.BlockSpec((tm, k), lambda i, j: (i, 0)), pl.BlockSpec((k, tn), lambda i, j: (0, j))],
        out_specs=pl.BlockSpec((tn // HEAD_DIM, tm, HEAD_DIM), lambda i, j: (j, i, 0)),
        compiler_params=_params(("parallel", "arbitrary")),
        name="matmul_heads",
    )(a, w)


def _dil_prompt_kernel(q_ref, kc_ref, kp_ref, vc_ref, vp_ref, bias_ref, o_ref, lse_ref, *, dil, hps):
    first = pl.program_id(1) == 0
    hb = pl.program_id(2)
    col = lax.broadcasted_iota(jnp.int32, (BLK, 2 * BLK), 1)
    edge = jnp.where(first & (col < BLK), NEG, 0.0)
    lane = lax.broadcasted_iota(jnp.int32, (BLK, BLK), 1)

    @pl.when(hb == 0)
    def _():
        lse_ref[...] = jnp.zeros_like(lse_ref)

    def unit(u, carry):
        hh, r = u // dil, u % dil
        rows = pl.ds(r, BLK, stride=dil)
        q = q_ref[hh, rows, :].astype(BF16)
        k = jnp.concatenate([kp_ref[hh, rows, :], kc_ref[hh, rows, :]], axis=0).astype(BF16)
        v = jnp.concatenate([vp_ref[hh, rows, :], vc_ref[hh, rows, :]], axis=0).astype(BF16)
        h = hb * hps + hh
        s = lax.dot_general(q, k, (((1,), (1,)), ((), ())), preferred_element_type=F32)
        s = s * SCALE + bias_ref[h] + edge
        m = jnp.max(s, axis=-1, keepdims=True)
        p = jnp.exp(s - m)
        l = jnp.sum(p, axis=-1, keepdims=True)
        o_ref[hh, rows, :] = jnp.dot(p.astype(BF16), v, preferred_element_type=F32) / l
        lse_ref[rows, :] = jnp.where(lane == h, m + jnp.log(l), lse_ref[rows, :])
        return carry

    lax.fori_loop(0, hps * dil, unit, 0, unroll=4)


def _dil_prompt_group(qkv_hm, bias, grp, dil, b, s):
    span = BLK * dil
    nsp = s // span
    hps = N_HEADS // dil
    nhb = N_HEADS // hps

    def slab(part, prev):
        base = (grp * 3 + part) * N_HEADS // hps
        if prev:
            return lambda bi, sp, hb: (base + hb, bi * nsp + jnp.maximum(sp - 1, 0), 0)
        return lambda bi, sp, hb: (base + hb, bi * nsp + sp, 0)

    blk = (hps, span, HEAD_DIM)
    return pl.pallas_call(
        functools.partial(_dil_prompt_kernel, dil=dil, hps=hps),
        out_shape=(jax.ShapeDtypeStruct((N_HEADS, b * s, HEAD_DIM), F32),
                   jax.ShapeDtypeStruct((b * s, BLK), F32)),
        grid=(b, nsp, nhb),
        in_specs=[pl.BlockSpec(blk, slab(0, False)), pl.BlockSpec(blk, slab(1, False)),
                  pl.BlockSpec(blk, slab(1, True)), pl.BlockSpec(blk, slab(2, False)),
                  pl.BlockSpec(blk, slab(2, True)),
                  pl.BlockSpec((N_HEADS, BLK, 2 * BLK), lambda bi, sp, hb: (0, 0, 0))],
        out_specs=(pl.BlockSpec(blk, lambda bi, sp, hb: (hb, bi * nsp + sp, 0)),
                   pl.BlockSpec((span, BLK), lambda bi, sp, hb: (bi * nsp + sp, 0))),
        compiler_params=_params(("parallel", "parallel", "arbitrary")),
        name=f"dil_attn_prompt_g{grp}",
    )(qkv_hm, qkv_hm, qkv_hm, qkv_hm, qkv_hm, bias)


def _dil_combine_kernel(o0_ref, o1_ref, o2_ref, l0_ref, l1_ref, l2_ref, o_ref, *, head_major):
    l0, l1, l2 = l0_ref[...], l1_ref[...], l2_ref[...]
    mx = jnp.maximum(jnp.maximum(l0, l1), l2)
    e0, e1, e2 = jnp.exp(l0 - mx), jnp.exp(l1 - mx), jnp.exp(l2 - mx)
    den = e0 + e1 + e2
    w0, w1, w2 = e0 / den, e1 / den, e2 / den
    for h in range(N_HEADS):
        sl = slice(h * HEAD_DIM, (h + 1) * HEAD_DIM)
        g0, g1, g2 = ((r[h] for r in (o0_ref, o1_ref, o2_ref)) if head_major
                      else (r[:, sl] for r in (o0_ref, o1_ref, o2_ref)))
        o = (w0[:, h:h + 1] * g0 + w1[:, h:h + 1] * g1) + w2[:, h:h + 1] * g2
        o_ref[:, sl] = o.astype(o_ref.dtype)


def _dil_combine(outs, lses, head_major):
    m = lses[0].shape[0]
    wide = N_HEADS * HEAD_DIM
    tm = _pick(m, (256, 128, 32))
    ob = pl.BlockSpec((tm, wide), lambda i: (i, 0))
    ib = pl.BlockSpec((N_HEADS, tm, HEAD_DIM), lambda i: (0, i, 0)) if head_major else ob
    lb = pl.BlockSpec((tm, BLK), lambda i: (i, 0))
    return pl.pallas_call(
        functools.partial(_dil_combine_kernel, head_major=head_major),
        out_shape=jax.ShapeDtypeStruct((m, wide), BF16),
        grid=(m // tm,),
        in_specs=[ib, ib, ib, lb, lb, lb],
        out_specs=ob,
        compiler_params=_params(("parallel",)),
        name="dil_combine",
    )(*outs, *lses)


HEAD_TILE = 8


def _heads_first(x):
    return pltpu.einshape("mhd->hmd", x)


def _dil_sample_kernel(q_ref, k_ref, v_ref, kn_ref, vn_ref, bias_ref, biasn_ref, o_ref, lse_ref,
                       m_scr, l_scr, acc_scr, k_scr, v_scr):
    ht, c = pl.program_id(1), pl.program_id(2)
    k_scr[...] = _heads_first(k_ref[...])
    v_scr[...] = _heads_first(v_ref[...])

    @pl.when(c == 0)
    def _():
        m_scr[...] = jnp.full_like(m_scr, NEG)
        l_scr[...] = jnp.zeros_like(l_scr)
        acc_scr[...] = jnp.zeros_like(acc_scr)

    @pl.when((c == 0) & (ht == 0))
    def _():
        lse_ref[...] = jnp.zeros_like(lse_ref)

    def step(hh, k, v, bias):
        sl = pl.ds(pl.multiple_of(hh * HEAD_DIM, HEAD_DIM), HEAD_DIM)
        q = q_ref[0, :, sl].astype(BF16)
        s = lax.dot_general(q, k.astype(BF16), (((1,), (1,)), ((), ())), preferred_element_type=F32)
        s = s * SCALE + bias
        m_old = m_scr[hh]
        m_new = jnp.maximum(m_old, jnp.max(s, axis=-1, keepdims=True))
        alpha = jnp.exp(m_old - m_new)
        p = jnp.where(bias > 0.5 * NEG, jnp.exp(s - m_new), 0.0)
        l_scr[hh] = alpha * l_scr[hh] + jnp.sum(p, axis=-1, keepdims=True)
        acc_scr[hh] = alpha * acc_scr[hh] + jnp.dot(p.astype(BF16), v.astype(BF16), preferred_element_type=F32)
        m_scr[hh] = m_new

    def head(hh, carry):
        step(hh, k_scr[hh], v_scr[hh], bias_ref[hh, 0])
        return carry

    lax.fori_loop(0, HEAD_TILE, head, 0, unroll=True)

    @pl.when(c == pl.num_programs(2) - 1)
    def _():
        lane = lax.broadcasted_iota(jnp.int32, (8, BLK), 1)

        def fin(hh, carry):
            sl = pl.ds(pl.multiple_of(hh * HEAD_DIM, HEAD_DIM), HEAD_DIM)
            step(hh, kn_ref[0, :, sl], vn_ref[0, :, sl], biasn_ref[hh])
            l = jnp.maximum(l_scr[hh], 1e-30)
            o_ref[0, :, sl] = acc_scr[hh] / l
            lse_ref[0] = jnp.where(lane == ht * HEAD_TILE + hh, m_scr[hh] + jnp.log(l), lse_ref[0])
            return carry

        lax.fori_loop(0, HEAD_TILE, fin, 0, unroll=True)


def _dil_sample_group(q, kn, vn, buf, bias, bias_new, dil):
    n, lb = buf.shape[:2]
    n_cls = bias.shape[1]
    wide = N_HEADS * HEAD_DIM
    half = wide // 2
    rows = lb // dil
    tiles = 2 * N_HEADS // HEAD_TILE
    view = buf.reshape(n, rows, dil * tiles, HEAD_TILE, HEAD_DIM)
    nht = N_HEADS // HEAD_TILE
    return pl.pallas_call(
        _dil_sample_kernel,
        out_shape=(jax.ShapeDtypeStruct((n, 8, wide), F32), jax.ShapeDtypeStruct((n, 8, BLK), F32)),
        grid=(n, nht, n_cls),
        in_specs=[
            pl.BlockSpec((1, 8, half), lambda i, ht, c: (i, 0, ht)),
            pl.BlockSpec((None, rows, None, HEAD_TILE, HEAD_DIM), lambda i, ht, c: (i, 0, c * tiles + ht, 0, 0)),
            pl.BlockSpec((None, rows, None, HEAD_TILE, HEAD_DIM),
                         lambda i, ht, c: (i, 0, c * tiles + nht + ht, 0, 0)),
            pl.BlockSpec((1, BLK, half), lambda i, ht, c: (i, 0, ht)),
            pl.BlockSpec((1, BLK, half), lambda i, ht, c: (i, 0, ht)),
            pl.BlockSpec((HEAD_TILE, 1, 8, rows), lambda i, ht, c: (ht, c, 0, 0)),
            pl.BlockSpec((HEAD_TILE, 8, BLK), lambda i, ht, c: (ht, 0, 0)),
        ],
        out_specs=(pl.BlockSpec((1, 8, half), lambda i, ht, c: (i, 0, ht)),
                   pl.BlockSpec((1, 8, BLK), lambda i, ht, c: (i, 0, 0))),
        scratch_shapes=[pltpu.VMEM((HEAD_TILE, 8, 1), F32), pltpu.VMEM((HEAD_TILE, 8, 1), F32),
                        pltpu.VMEM((HEAD_TILE, 8, HEAD_DIM), F32),
                        pltpu.VMEM((HEAD_TILE, rows, HEAD_DIM), F32), pltpu.VMEM((HEAD_TILE, rows, HEAD_DIM), F32)],
        compiler_params=_params(("parallel", "arbitrary", "arbitrary")),
        name="dil_attn_sample",
    )(q, view, view, kn, vn, bias, bias_new)


def _pad_rows(x, rows):
    return jnp.pad(x, ((0, 0), (0, rows - x.shape[1]), (0, 0)))


def _layer_a(hp, hs, b, s, n, t, norm_g, w_in, w_out, rel_bias, bufs):
    wide = N_HEADS * HEAD_DIM
    qkv_p = _matmul_heads(_rmsnorm(hp, norm_g, BF16), w_in)
    qkv_s = _matmul(_rmsnorm(hs, norm_g, BF16), w_in).reshape(n, t, QKV_A)
    outs_p, lses_p, outs_s, lses_s, new_p, new_s = [], [], [], [], [], []
    for grp, (win, dil) in enumerate(DIL_PAIRS):
        base = grp * 3 * wide
        bias = _bias_table(rel_bias, 1, BLK, 2 * BLK, BLK * dil, 0, dil, -dil, 0, win).reshape(N_HEADS, BLK, 2 * BLK)
        o, lse = _dil_prompt_group(qkv_p, bias, grp, dil, b, s)
        outs_p.append(o)
        lses_p.append(lse)
        keep = min(win, s)
        kv = qkv_p.reshape(3 * N_DIL, N_HEADS, b, s, HEAD_DIM)[grp * 3 + 1:grp * 3 + 3, :, :, s - keep:]
        new_p.append(kv.transpose(2, 3, 0, 1, 4))
        buf = bufs[grp]
        lb = buf.shape[1]
        n_cls = min(dil, t)
        bias_buf = _bias_table(rel_bias, n_cls, 8, lb // dil, lb, -1, 1, -dil, 0, win, dil)
        bias_new = _bias_table(rel_bias, 1, 8, BLK, 0, 0, 1, -1, 0, win, dil).reshape(N_HEADS, 8, BLK)
        q = _pad_rows(qkv_s[:, :, base:base + wide], 8)
        kn = _pad_rows(qkv_s[:, :, base + wide:base + 2 * wide], BLK)
        vn = _pad_rows(qkv_s[:, :, base + 2 * wide:base + 3 * wide], BLK)
        o, lse = _dil_sample_group(q, kn, vn, buf, bias_buf, bias_new, dil)
        outs_s.append(o[:, :t].reshape(n * t, wide))
        lses_s.append(lse[:, :t].reshape(n * t, BLK))
        kv_new = qkv_s[:, :, base + wide:base + 3 * wide].reshape(n, t, 2, N_HEADS, HEAD_DIM)
        new_s.append(jnp.concatenate([buf, kv_new], axis=1)[:, t:])
    hp = _matmul_residual(_dil_combine(outs_p, lses_p, True), w_out, hp)
    hs = _matmul_residual(_dil_combine(outs_s, lses_s, False), w_out, hs)
    return hp, hs, new_p, new_s


PAGES_PER_STEP = 16
CHUNKS_PER_PAGE = BLK // CMP_STRIDE
NT_DIMS = (((1,), (1,)), ((), ()))
TN_DIMS = (((0,), (0,)), ((), ()))


def _page_spec(p, half):
    return pl.BlockSpec((BLK, HEAD_TILE, HEAD_DIM),
                        lambda i, j, *rest: (rest[-1][i, j * PAGES_PER_STEP + p], half, 0))


def _cmp_proj_kernel(pt_ref, *refs):
    pages = refs[:PAGES_PER_STEP]
    w_ref, o_ref = refs[PAGES_PER_STEP:PAGES_PER_STEP + 2]
    ys = [pltpu.einshape("ctgd->tgcd", pg[...].reshape(CHUNKS_PER_PAGE, CMP_STRIDE, HEAD_TILE, HEAD_DIM))
          for pg in pages]
    for c in range(2):
        acc = jnp.zeros((KV_HEADS * BLK, 2 * HEAD_DIM), F32)
        for t in range(CMP_STRIDE):
            rows = [y[t, c * KV_HEADS + kh] for kh in range(KV_HEADS) for y in ys]
            lhs = jnp.concatenate(rows, axis=0).astype(BF16)
            acc = acc + jnp.dot(lhs, w_ref[c, t], preferred_element_type=F32)
        for kh in range(KV_HEADS):
            o_ref[0, c, kh] = acc[kh * BLK:(kh + 1) * BLK]


def _cmp_proj(pages, page_table, w1r):
    n, n_pages = page_table.shape
    chunks = n_pages * CHUNKS_PER_PAGE
    grid_spec = pltpu.PrefetchScalarGridSpec(
        num_scalar_prefetch=1,
        grid=(n, n_pages // PAGES_PER_STEP),
        in_specs=[_page_spec(p, 0) for p in range(PAGES_PER_STEP)]
        + [pl.BlockSpec((2, CMP_STRIDE, HEAD_DIM, 2 * HEAD_DIM), lambda i, j, pt: (0, 0, 0, 0))],
        out_specs=pl.BlockSpec((1, 2, KV_HEADS, BLK, 2 * HEAD_DIM), lambda i, j, pt: (i, 0, 0, j, 0)),
    )
    return pl.pallas_call(
        _cmp_proj_kernel,
        out_shape=jax.ShapeDtypeStruct((n, 2, KV_HEADS, chunks, 2 * HEAD_DIM), F32),
        grid_spec=grid_spec,
        compiler_params=_params(("parallel", "arbitrary")),
        name="nsa_cmp_proj",
    )(page_table, *([pages] * PAGES_PER_STEP), w1r)


def _finish_compress(a, pe_row, w1f, b1, w2, b2, n_blocks):
    rows = a.shape[0]
    cst = jnp.dot(pe_row, w1f, preferred_element_type=F32)[0:1]
    h = (b1 + cst) + a[:, :HEAD_DIM] + pltpu.roll(a[:, HEAD_DIM:], rows - 1, 0)
    x = jnp.dot(jax.nn.gelu(h).astype(BF16), w2, preferred_element_type=F32) + b2
    return jnp.where(lax.broadcasted_iota(jnp.int32, x.shape, 0) < n_blocks, x, 0.0)


def _split3(x):
    hi = x.astype(BF16)
    r = x - hi.astype(F32)
    mid = r.astype(BF16)
    return hi, mid, (r - mid.astype(F32)).astype(BF16)


def _top_n(score, n, axis):
    idx = lax.broadcasted_iota(jnp.int32, score.shape, axis).astype(F32)
    big = float(score.shape[axis])

    def body(_, carry):
        sc, sel = carry
        mx = jnp.max(sc, axis=axis, keepdims=True)
        first = jnp.min(jnp.where(sc == mx, idx, big), axis=axis, keepdims=True)
        hit = idx == first
        return jnp.where(hit, -jnp.inf, sc), jnp.where(hit, 1.0, sel)

    return lax.fori_loop(0, n, body, (score, jnp.zeros(score.shape, F32)))[1]


def _sel_scores(p_slc, blk, cur, n_blocks):
    forced = (blk == 0) | (blk == cur) | (blk == cur - 1)
    score = jnp.where(forced, FORCED_SCORE, jnp.where(blk <= cur, p_slc, -1.0))
    return jnp.where(blk < n_blocks, score, -2.0)


def _nsa_prompt_kernel(q_ref, a_ref, pe_ref, w1f_ref, b1_ref, w2_ref, b2_ref, ksel_ref, vsel_ref, kwin_ref,
                       vwin_ref, tbl_ref, biasc_ref, gates_ref, msel_ref, o_ref, kc_scr, vc_scr, sel_scr,
                       vselt_ref, vwint_ref, *, n_cmp, n_sel):
    i = pl.program_id(2)

    @pl.when(i == 0)
    def _():
        for c, scr in ((0, kc_scr), (1, vc_scr)):
            scr[...] = _finish_compress(a_ref[0, c, 0], pe_ref[c], w1f_ref[c], b1_ref[c], w2_ref[c], b2_ref[c],
                                        n_cmp).astype(BF16)
        for src, dst in ((vsel_ref, vselt_ref), (vwin_ref, vwint_ref)):
            for c in range(src.shape[0] // BLK):
                dst[:, c * BLK:(c + 1) * BLK] = src[c * BLK:(c + 1) * BLK, :].T.astype(BF16)

    q = q_ref[...]
    qs = jnp.concatenate([q[:, g * HEAD_DIM:(g + 1) * HEAD_DIM] for g in range(GROUP)], axis=0).astype(BF16)
    key_i = lax.broadcasted_iota(jnp.int32, (BLK, BLK), 0)
    tok_i = lax.broadcasted_iota(jnp.int32, (BLK, BLK), 1)

    def lanes4(x):
        return jnp.concatenate([x] * GROUP, axis=1)

    def table(delta):
        return jnp.concatenate([tbl_ref[g, delta] for g in range(GROUP)], axis=1)

    mask_c = lanes4(i * BLK + tok_i - (key_i * CMP_STRIDE + (CMP_BLOCK - 1)) >= 0)
    s = lax.dot_general(kc_scr[...], qs, NT_DIMS, preferred_element_type=F32) * SCALE
    s = jnp.where(mask_c, s + jnp.concatenate([biasc_ref[g, 0] for g in range(GROUP)], axis=1), NEG)
    m = jnp.max(s, axis=0, keepdims=True)
    p = jnp.where(mask_c, jnp.exp(s - m), 0.0)
    pn = p / jnp.maximum(jnp.sum(p, axis=0, keepdims=True), 1e-30)
    o_cmp = lax.dot_general(vc_scr[...], pn.astype(BF16), TN_DIMS, preferred_element_type=F32)
    pc = ((pn[:, 0:BLK] + pn[:, BLK:2 * BLK]) + pn[:, 2 * BLK:3 * BLK]) + pn[:, 3 * BLK:4 * BLK]

    msel = msel_ref[...]
    p_slc = sum(jnp.dot(msel, part, preferred_element_type=F32) for part in _split3(pc))
    blk = lax.broadcasted_iota(jnp.int32, p_slc.shape, 0)
    cur = (i * BLK + lax.broadcasted_iota(jnp.int32, p_slc.shape, 1)) // SEL_BLOCK
    sel_scr[...] = _top_n(_sel_scores(p_slc, blk, cur, n_sel), SEL_TOPN, 0)

    def attend(k_ref, vt_ref, first_blk, n_blk, mask_of, carry):
        m_run, l_run, acc = carry
        off = pl.multiple_of(first_blk * BLK, BLK)
        k = k_ref[pl.ds(off, n_blk * BLK), :].astype(BF16)
        bias, mask = [], []
        for j in range(n_blk):
            delta = i - first_blk - j
            bias.append(table(jnp.maximum(delta, 0)))
            mask.append(lanes4(mask_of(j, delta * BLK + tok_i - key_i)))
        mask = jnp.concatenate(mask, axis=0)
        s = lax.dot_general(k, qs, NT_DIMS, preferred_element_type=F32) * SCALE + jnp.concatenate(bias, axis=0)
        s = jnp.where(mask, s, NEG)
        m_new = jnp.maximum(m_run, jnp.max(s, axis=0, keepdims=True))
        alpha = jnp.exp(m_run - m_new)
        p = jnp.where(mask, jnp.exp(s - m_new), 0.0)
        l_new = alpha * l_run + jnp.sum(p, axis=0, keepdims=True)
        vt = vt_ref[:, pl.ds(off, n_blk * BLK)]
        return m_new, l_new, alpha * acc + jnp.dot(vt, p.astype(BF16), preferred_element_type=F32)

    init = (jnp.full((1, GROUP * BLK), NEG, F32), jnp.zeros((1, GROUP * BLK), F32),
            jnp.zeros((HEAD_DIM, GROUP * BLK), F32))

    sel_span = 4
    per_blk = BLK // SEL_BLOCK

    def sel_step(c, carry):
        rows = sel_scr[pl.ds(pl.multiple_of(c * sel_span * per_blk, 8), sel_span * per_blk), :]

        def mask_of(j, dist):
            picked = jnp.concatenate(
                [jnp.broadcast_to(rows[j * per_blk + u:j * per_blk + u + 1], (SEL_BLOCK, BLK)) for u in range(per_blk)],
                axis=0)
            return (dist >= 0) & (picked > 0.5)

        return attend(ksel_ref, vselt_ref, c * sel_span, sel_span, mask_of, carry)

    _, l_sel, acc_sel = lax.fori_loop(0, i // sel_span + 1, sel_step, init)
    o_sel = acc_sel / jnp.maximum(l_sel, 1e-30)

    n_win = (NSA_WINDOW - 1 + BLK - 1) // BLK + 1
    _, l_win, acc_win = attend(kwin_ref, vwint_ref, jnp.maximum(i - (n_win - 1), 0), n_win,
                               lambda j, dist: (dist >= 0) & (dist <= NSA_WINDOW - 1), init)
    o_win = acc_win / jnp.maximum(l_win, 1e-30)

    gt = gates_ref[0]

    def gate(branch):
        return jnp.concatenate([gt[branch * GROUP + g:branch * GROUP + g + 1, :] for g in range(GROUP)], axis=1)

    o = (gate(0) * o_cmp + gate(1) * o_sel) + gate(2) * o_win
    for g in range(GROUP):
        o_ref[0, g] = o[:, g * BLK:(g + 1) * BLK]


def _nsa_prompt(proj, a_cmp, gates_t, cmp_w, tbl, bias_c, msel, b, s):
    pe, w1f, b1, w2, b2 = cmp_w
    nq = s // BLK
    kcol = NSA_Q // HEAD_DIM
    const = lambda shape: pl.BlockSpec(shape, lambda bi, kh, i: (0,) * len(shape))
    return pl.pallas_call(
        functools.partial(_nsa_prompt_kernel, n_cmp=s // CMP_STRIDE - 1, n_sel=s // SEL_BLOCK),
        out_shape=jax.ShapeDtypeStruct((b, N_HEADS, HEAD_DIM, s), F32),
        grid=(b, KV_HEADS, nq),
        in_specs=[
            pl.BlockSpec((BLK, GROUP * HEAD_DIM), lambda bi, kh, i: (bi * nq + i, kh)),
            pl.BlockSpec((1, 2, 1, s // CMP_STRIDE, 2 * HEAD_DIM), lambda bi, kh, i: (bi, 0, kh, 0, 0)),
            const(pe.shape), const(w1f.shape), const(b1.shape), const(w2.shape), const(b2.shape),
            pl.BlockSpec((s, HEAD_DIM), lambda bi, kh, i: (bi, kcol + 2 * KV_HEADS + kh)),
            pl.BlockSpec((s, HEAD_DIM), lambda bi, kh, i: (bi, kcol + 3 * KV_HEADS + kh)),
            pl.BlockSpec((s, HEAD_DIM), lambda bi, kh, i: (bi, kcol + 4 * KV_HEADS + kh)),
            pl.BlockSpec((s, HEAD_DIM), lambda bi, kh, i: (bi, kcol + 5 * KV_HEADS + kh)),
            pl.BlockSpec((GROUP, nq, BLK, BLK), lambda bi, kh, i: (kh, 0, 0, 0)),
            pl.BlockSpec((GROUP, 1, s // CMP_STRIDE, BLK), lambda bi, kh, i: (kh, 0, 0, i)),
            pl.BlockSpec((1, 16, BLK), lambda bi, kh, i: (kh, 0, bi * nq + i)),
            const(msel.shape),
        ],
        out_specs=pl.BlockSpec((1, GROUP, HEAD_DIM, BLK), lambda bi, kh, i: (bi, kh, 0, i)),
        scratch_shapes=[pltpu.VMEM((s // CMP_STRIDE, HEAD_DIM), BF16), pltpu.VMEM((s // CMP_STRIDE, HEAD_DIM), BF16),
                        pltpu.VMEM((s // SEL_BLOCK, BLK), F32),
                        pltpu.VMEM((HEAD_DIM, s), BF16), pltpu.VMEM((HEAD_DIM, s), BF16)],
        compiler_params=_params(("parallel", "parallel", "arbitrary")),
        name="nsa_prompt",
    )(proj, a_cmp, pe, w1f, b1, w2, b2, proj, proj, proj, proj, tbl, bias_c, gates_t, msel)


def _masked_softmax_rows(s, mask):
    s = jnp.where(mask, s, NEG)
    m = jnp.max(s, axis=-1, keepdims=True)
    p = jnp.where(mask, jnp.exp(s - m), 0.0)
    return p / jnp.maximum(jnp.sum(p, axis=-1, keepdims=True), 1e-30)


def _nsa_sample_select_kernel(q_ref, a_ref, pe_ref, w1f_ref, b1_ref, w2_ref, b2_ref, biasc_ref, msel_ref,
                              regroup_ref, ocmp_ref, sel_ref, *, n_cmp, n_sel, past):
    pcs = []
    for kh in range(KV_HEADS):
        kc, vc = (_finish_compress(a_ref[0, c, kh], pe_ref[c], w1f_ref[c], b1_ref[c], w2_ref[c], b2_ref[c],
                                   n_cmp).astype(BF16) for c in range(2))
        bias = biasc_ref[kh]
        s = lax.dot_general(q_ref[0, kh].astype(BF16), kc, NT_DIMS, preferred_element_type=F32) * SCALE + bias
        pn = _masked_softmax_rows(s, bias > 0.5 * NEG)
        ocmp_ref[0, kh] = jnp.dot(pn.astype(BF16), vc, preferred_element_type=F32)
        pcs.append(((pn[0:8] + pn[8:16]) + pn[16:24]) + pn[24:32])
    pc = jnp.concatenate(pcs, axis=0)
    msel = msel_ref[...]
    p_slc = sum(jnp.dot(part, msel, preferred_element_type=F32) for part in _split3(pc))
    blk = lax.broadcasted_iota(jnp.int32, p_slc.shape, 1)
    cur = (past + lax.broadcasted_iota(jnp.int32, p_slc.shape, 0) % 8) // SEL_BLOCK
    sel = _top_n(_sel_scores(p_slc, blk, cur, n_sel), SEL_TOPN, 1).astype(BF16)
    for j in range(regroup_ref.shape[0]):
        part = jnp.dot(sel, regroup_ref[j], preferred_element_type=F32)
        for kh in range(KV_HEADS):
            sel_ref[0, kh, j] = part[kh * 8:(kh + 1) * 8]


def _nsa_sample_select(q, a_cmp, cmp_w, bias_c, msel, regroup, past, t_len):
    pe, w1f, b1, w2, b2 = cmp_w
    n = q.shape[0]
    chunks = a_cmp.shape[3]
    n_steps = regroup.shape[0]
    const = lambda shape: pl.BlockSpec(shape, lambda i: (0,) * len(shape))
    return pl.pallas_call(
        functools.partial(_nsa_sample_select_kernel, n_cmp=chunks - 1, n_sel=(past + t_len + SEL_BLOCK - 1) // SEL_BLOCK,
                          past=past),
        out_shape=(jax.ShapeDtypeStruct((n, KV_HEADS, GROUP * 8, HEAD_DIM), F32),
                   jax.ShapeDtypeStruct((n, KV_HEADS, n_steps, 8, BLK), F32)),
        grid=(n,),
        in_specs=[
            pl.BlockSpec((1, KV_HEADS, GROUP * 8, HEAD_DIM), lambda i: (i, 0, 0, 0)),
            pl.BlockSpec((1, 2, KV_HEADS, chunks, 2 * HEAD_DIM), lambda i: (i, 0, 0, 0, 0)),
            const(pe.shape), const(w1f.shape), const(b1.shape), const(w2.shape), const(b2.shape),
            const(bias_c.shape), const(msel.shape), const(regroup.shape),
        ],
        out_specs=(pl.BlockSpec((1, KV_HEADS, GROUP * 8, HEAD_DIM), lambda i: (i, 0, 0, 0)),
                   pl.BlockSpec((1, KV_HEADS, n_steps, 8, BLK), lambda i: (i, 0, 0, 0, 0))),
        compiler_params=_params(("parallel",)),
        name="nsa_sample_select",
    )(q, a_cmp, pe, w1f, b1, w2, b2, bias_c, msel, regroup)


def _nsa_sample_attend_kernel(pt_ref, *refs):
    pages = refs[:PAGES_PER_STEP]
    (q_ref, sel_ref, expand_ref, bias_ref, kn_ref, vn_ref, biasn_ref, win_ref, kwn_ref, vwn_ref, biasw_ref,
     biaswn_ref, gates_ref, ocmp_ref, o_ref, m_scr, l_scr, acc_scr, kv_scr, win_scr) = refs[PAGES_PER_STEP:]
    j, kh = pl.program_id(1), pl.program_id(2)
    n_steps = pl.num_programs(1)
    qs = q_ref[0, 0].astype(BF16)

    @pl.when(j == 0)
    def _():
        m_scr[kh] = jnp.full((GROUP * 8, 1), NEG, F32)
        l_scr[kh] = jnp.zeros((GROUP * 8, 1), F32)
        acc_scr[kh] = jnp.zeros((GROUP * 8, HEAD_DIM), F32)

    def picked(step):
        sel = sel_ref[0, 0, step].astype(BF16)
        return jnp.concatenate([sel] * GROUP, axis=0)

    def update(k, v, bias, mask):
        s = lax.dot_general(qs, k, NT_DIMS, preferred_element_type=F32) * SCALE + bias
        s = jnp.where(mask, s, NEG)
        m_old = m_scr[kh]
        m_new = jnp.maximum(m_old, jnp.max(s, axis=-1, keepdims=True))
        alpha = jnp.exp(m_old - m_new)
        p = jnp.where(mask, jnp.exp(s - m_new), 0.0)
        l_scr[kh] = alpha * l_scr[kh] + jnp.sum(p, axis=-1, keepdims=True)
        acc_scr[kh] = alpha * acc_scr[kh] + jnp.dot(p.astype(BF16), v, preferred_element_type=F32)
        m_scr[kh] = m_new

    @pl.when(kh == 0)
    def _():
        for p, pg in enumerate(pages):
            kv_scr[:, p * BLK:(p + 1) * BLK, :] = _heads_first(pg[...])

    in_sel = jnp.dot(picked(j), expand_ref[...], preferred_element_type=F32) > 0.5
    update(kv_scr[kh].astype(BF16), kv_scr[KV_HEADS + kh].astype(BF16), bias_ref[0], in_sel)

    @pl.when(j == n_steps - 1)
    def _():
        biasn = biasn_ref[0]
        new_sel = picked(n_steps)[:, 0:1] > 0.5
        update(kn_ref[0, 0].astype(BF16), vn_ref[0, 0].astype(BF16), biasn, (biasn > 0.5 * NEG) & new_sel)
        o_sel = acc_scr[kh] / jnp.maximum(l_scr[kh], 1e-30)
        @pl.when(kh == 0)
        def _():
            win_scr[...] = _heads_first(win_ref[...])

        biasw = jnp.concatenate([biasw_ref[0], biaswn_ref[0]], axis=1)
        kw = jnp.concatenate([win_scr[kh], kwn_ref[0, 0]], axis=0).astype(BF16)
        vw = jnp.concatenate([win_scr[KV_HEADS + kh], vwn_ref[0, 0]], axis=0).astype(BF16)
        sw = lax.dot_general(qs, kw, NT_DIMS, preferred_element_type=F32) * SCALE + biasw
        pw = _masked_softmax_rows(sw, biasw > 0.5 * NEG)
        o_win = jnp.dot(pw.astype(BF16), vw, preferred_element_type=F32)
        o_ref[0, kh] = (gates_ref[0, 0, 0] * ocmp_ref[0, 0] + gates_ref[1, 0, 0] * o_sel) + gates_ref[2, 0, 0] * o_win


def _nsa_sample_attend(pool, page_table, q, sel, expand, bias_sel, k_new, v_new, bias_new, win_buf, kw_new, vw_new,
                       bias_win, bias_win_new, gates, o_cmp):
    n, n_pages = page_table.shape
    n_steps = n_pages // PAGES_PER_STEP
    keys = PAGES_PER_STEP * BLK
    rows = GROUP * 8
    lw = win_buf.shape[1]
    per = lambda shape: pl.BlockSpec((1, 1) + shape, lambda i, j, kh, pt: (i, kh) + (0,) * len(shape))
    by_head = lambda shape: pl.BlockSpec((1,) + shape, lambda i, j, kh, pt: (kh,) + (0,) * len(shape))
    grid_spec = pltpu.PrefetchScalarGridSpec(
        num_scalar_prefetch=1,
        grid=(n, n_steps, KV_HEADS),
        in_specs=[_page_spec(p, 1) for p in range(PAGES_PER_STEP)] + [
            per((rows, HEAD_DIM)),
            pl.BlockSpec((1, 1, n_steps + 1, 8, BLK), lambda i, j, kh, pt: (i, kh, 0, 0, 0)),
            pl.BlockSpec(expand.shape, lambda i, j, kh, pt: (0, 0)),
            pl.BlockSpec((1, rows, keys), lambda i, j, kh, pt: (kh, 0, j)),
            per((BLK, HEAD_DIM)), per((BLK, HEAD_DIM)), by_head((rows, BLK)),
            pl.BlockSpec((None, lw, HEAD_TILE, HEAD_DIM), lambda i, j, kh, pt: (i, 0, 0, 0)),
            per((BLK, HEAD_DIM)), per((BLK, HEAD_DIM)), by_head((rows, lw)), by_head((rows, BLK)),
            pl.BlockSpec((3, 1, 1, rows, HEAD_DIM), lambda i, j, kh, pt: (0, i, kh, 0, 0)),
            per((rows, HEAD_DIM)),
        ],
        out_specs=pl.BlockSpec((1, KV_HEADS, rows, HEAD_DIM), lambda i, j, kh, pt: (i, 0, 0, 0)),
        scratch_shapes=[pltpu.VMEM((KV_HEADS, rows, 1), F32), pltpu.VMEM((KV_HEADS, rows, 1), F32),
                        pltpu.VMEM((KV_HEADS, rows, HEAD_DIM), F32),
                        pltpu.VMEM((HEAD_TILE, keys, HEAD_DIM), F32), pltpu.VMEM((HEAD_TILE, lw, HEAD_DIM), F32)],
    )
    return pl.pallas_call(
        _nsa_sample_attend_kernel,
        out_shape=jax.ShapeDtypeStruct((n, KV_HEADS, rows, HEAD_DIM), F32),
        grid_spec=grid_spec,
        compiler_params=_params(("parallel", "arbitrary", "arbitrary")),
        name="nsa_sample_attend",
    )(page_table, *([pool] * PAGES_PER_STEP), q, sel, expand, bias_sel, k_new, v_new, bias_new, win_buf, kw_new,
      vw_new, bias_win, bias_win_new, gates, o_cmp)


def _sel_weights(n_cmp_rows, n_sel_cols):
    ratio = SEL_BLOCK // CMP_STRIDE
    span = CMP_BLOCK // CMP_STRIDE
    c = jnp.arange(n_cmp_rows)[:, None]
    j = jnp.arange(n_sel_cols)[None, :]
    o = c - ratio * j + (span - 1)
    cnt = jnp.minimum(o, span - 1) - jnp.maximum(o - (ratio - 1), 0) + 1
    return jnp.where((o >= 0) & (o <= ratio + span - 2), cnt, 0).astype(BF16)


def _layer_b(hp, hs, b, s, n, t, norm_g, w_in, w_out, rel_bias, cmp, pool, page_table, win_buf):
    cmp_pe, cmp_w1, cmp_b1, cmp_w2, cmp_b2 = cmp
    n_kvcol = 6 * NSA_KV
    w_main = w_in[:, :NSA_Q + n_kvcol].astype(BF16)
    w_gate = jnp.pad(w_in[:, NSA_Q + n_kvcol:], ((0, 0), (0, BLK - 3 * N_HEADS))).astype(BF16)
    w1 = cmp_w1.reshape(2, 2, CMP_STRIDE, HEAD_DIM, HEAD_DIM)
    w1r = jnp.concatenate([w1[:, 0], w1[:, 1]], axis=-1).astype(BF16)
    pe_row = jnp.pad(cmp_pe.reshape(2, 1, CMP_BLOCK * HEAD_DIM), ((0, 0), (0, 7), (0, 0))).astype(BF16)
    cmp_w = (pe_row, cmp_w1.reshape(2, CMP_BLOCK * HEAD_DIM, HEAD_DIM).astype(BF16), cmp_b1.reshape(2, 1, HEAD_DIM),
             cmp_w2.astype(BF16), cmp_b2.reshape(2, 1, HEAD_DIM))
    weights = (w_main, w_gate, w_out, w1r, cmp_w)
    hp, new_win_p, new_kv_p = _nsa_prompt_path(hp, b, s, norm_g, weights, rel_bias)
    hs, new_win_s, new_kv_s = _nsa_sample_path(hs, n, t, norm_g, weights, rel_bias, pool, page_table, win_buf)
    return hp, hs, new_win_p, new_win_s, new_kv_p, new_kv_s


def _nsa_prompt_path(hp, b, s, norm_g, weights, rel_bias):
    w_main, w_gate, w_out, w1r, cmp_w = weights
    a_p = _rmsnorm(hp, norm_g, BF16)
    proj_p = _matmul(a_p, w_main)
    gates_p = _matmul(a_p, w_gate, sigmoid=True)
    kv_p = proj_p[:, NSA_Q:].reshape(b, s, 6, KV_HEADS, HEAD_DIM)
    pages_p = proj_p[:, NSA_Q:NSA_Q + 2 * NSA_KV].reshape(b * s, HEAD_TILE, HEAD_DIM)
    table_p = jnp.arange(b * s // BLK, dtype=jnp.int32).reshape(b, s // BLK)
    a_cmp_p = _cmp_proj(pages_p, table_p, w1r)
    nq = s // BLK
    tbl = _bias_table(rel_bias, nq, BLK, BLK, 0, BLK, -1, 1, -(1 << 30), 1 << 30)
    bias_c = _bias_table(rel_bias, 1, s // CMP_STRIDE, s, -(CMP_BLOCK - 1), 0, -CMP_STRIDE, 1, 0, 1 << 30)
    gates_t = gates_p[:, :3 * N_HEADS].reshape(b * s, 3, KV_HEADS, GROUP).transpose(2, 1, 3, 0)
    gates_t = jnp.pad(gates_t.reshape(KV_HEADS, 3 * GROUP, b * s), ((0, 0), (0, 16 - 3 * GROUP), (0, 0)))
    msel_p = _sel_weights(s // CMP_STRIDE, s // SEL_BLOCK).T
    o_t = _nsa_prompt(proj_p, a_cmp_p, gates_t, cmp_w, tbl, bias_c, msel_p, b, s)
    o_p = o_t.transpose(0, 3, 1, 2).reshape(b * s, NSA_Q).astype(BF16)
    hp = _matmul_residual(o_p, w_out, hp)
    keep = min(NSA_WINDOW, s)
    return hp, kv_p[:, s - keep:, 4:], kv_p[:, :, :4]


def _nsa_sample_path(hs, n, t, norm_g, weights, rel_bias, pool, page_table, win_buf):
    w_main, w_gate, w_out, w1r, cmp_w = weights
    past = page_table.shape[1] * BLK
    a_s = _rmsnorm(hs, norm_g, BF16)
    proj_s = _matmul(a_s, w_main)
    gates_s = _matmul(a_s, w_gate, sigmoid=True)
    kv_s = proj_s[:, NSA_Q:].reshape(n, t, 6, KV_HEADS, HEAD_DIM)
    rows = GROUP * 8

    def head_rows(x):
        x = jnp.pad(x.transpose(0, 2, 3, 1, 4), ((0, 0), (0, 0), (0, 0), (0, 8 - t), (0, 0)))
        return x.reshape(n, KV_HEADS, rows, x.shape[-1])

    def new_rows(c):
        return jnp.pad(kv_s[:, :, c].transpose(0, 2, 1, 3), ((0, 0), (0, 0), (0, BLK - t), (0, 0)))

    def head_table(x, cols):
        return x.reshape(KV_HEADS, rows, cols)

    q_s = head_rows(proj_s[:, :NSA_Q].reshape(n, t, KV_HEADS, GROUP, HEAD_DIM))
    pool2 = pool.reshape(pool.shape[0] * BLK, 2 * HEAD_TILE, HEAD_DIM)
    a_cmp_s = _cmp_proj(pool2, page_table, w1r)
    chunks = past // CMP_STRIDE
    big = 1 << 30
    bias_cs = head_table(_bias_table(rel_bias, 1, 8, chunks, past - (CMP_BLOCK - 1), 0, 1, -CMP_STRIDE, 0, big), chunks)
    n_steps = page_table.shape[1] // PAGES_PER_STEP
    n_sel_pad = (n_steps + 1) * BLK
    msel_s = _sel_weights(chunks, n_sel_pad)
    per_step = PAGES_PER_STEP * BLK // SEL_BLOCK
    jj = jnp.arange(n_sel_pad)[None, :, None]
    ll = jnp.arange(BLK)[None, None, :]
    st = jnp.arange(n_steps + 1)[:, None, None]
    regroup = ((jj == st * per_step + ll) & (ll < per_step)).astype(BF16)
    o_cmp_s, sel_s = _nsa_sample_select(q_s, a_cmp_s, cmp_w, bias_cs, msel_s, regroup, past, t)
    expand = (jnp.arange(BLK)[:, None] == jnp.arange(PAGES_PER_STEP * BLK)[None, :] // SEL_BLOCK).astype(BF16)
    bias_sel = head_table(_bias_table(rel_bias, 1, 8, past, past, 0, 1, -1, 0, big), past)
    bias_new = head_table(_bias_table(rel_bias, 1, 8, BLK, 0, 0, 1, -1, 0, big), BLK)
    lw = win_buf.shape[1]
    bias_win = head_table(_bias_table(rel_bias, 1, 8, lw, lw, 0, 1, -1, 0, NSA_WINDOW - 1), lw)
    bias_win_new = head_table(_bias_table(rel_bias, 1, 8, BLK, 0, 0, 1, -1, 0, NSA_WINDOW - 1), BLK)
    g_s = gates_s[:, :3 * N_HEADS].reshape(n, t, 3, KV_HEADS, GROUP, 1)
    g_s = jnp.stack([head_rows(g_s[:, :, c]) for c in range(3)])
    g_s = jnp.broadcast_to(g_s, (3, n, KV_HEADS, rows, HEAD_DIM))
    o_s = _nsa_sample_attend(pool2, page_table, q_s, sel_s, expand, bias_sel, new_rows(2), new_rows(3), bias_new,
                             win_buf.reshape(n, lw, HEAD_TILE, HEAD_DIM), new_rows(4), new_rows(5), bias_win,
                             bias_win_new,
                             g_s, o_cmp_s)
    o_s = o_s.reshape(n, KV_HEADS, GROUP, 8, HEAD_DIM)[:, :, :, :t].transpose(0, 3, 1, 2, 4)
    hs = _matmul_residual(o_s.reshape(n * t, NSA_Q).astype(BF16), w_out, hs)
    new_win_s = jnp.concatenate([win_buf, kv_s[:, :, 4:]], axis=1)[:, t:]
    return hs, new_win_s, kv_s[:, :, :4]


def _ffn_and_ple(hp, hs, b, s, n, t, i, norm_ffn, norm_ple, w_ffn_in, conv_w, conv_b, w_ffn_out, state_conv,
                 p_prompt, p_sample, w_ple_gate, w_ple_proj):
    hu_p = _matmul(_rmsnorm(hp, norm_ffn, BF16), w_ffn_in)
    hu_s = _matmul(_rmsnorm(hs, norm_ffn, BF16), w_ffn_in)
    conv_p = hu_p.reshape(b, s, 2 * D_FF)[:, s - (CONV_W - 1):, :D_FF]
    conv_s = jnp.concatenate([state_conv, hu_s.reshape(n, t, 2 * D_FF)[:, :, :D_FF]], axis=1)[:, t:]
    hp = _ffn_out_prompt(hu_p, conv_w, conv_b, w_ffn_out, hp, s)
    hs = _ffn_out_sample(hu_s, state_conv, conv_w, conv_b, w_ffn_out, hs, t)
    hp = _ple_add(_rmsnorm(hp, norm_ple, BF16), w_ple_gate, p_prompt.astype(BF16), w_ple_proj, hp)
    hs = _ple_add(_rmsnorm(hs, norm_ple, BF16), w_ple_gate, p_sample.astype(BF16), w_ple_proj, hs)
    return hp, hs, conv_p, conv_s


def kernel(x_prompt, x_sample, state_dil_w128, state_dil_w512, state_dil_w2048, state_nsa_win, state_conv,
           cache_nsa_kv, page_table, p_prompt, p_sample, rel_bias, norm_mix, norm_ffn, norm_ple, norm_final,
           w_in_a, w_out_a, w_in_b, w_out_b, cmp_pe, cmp_w1, cmp_b1, cmp_w2, cmp_b2, w_ffn_in, conv_w, conv_b,
           w_ffn_out, w_ple_gate, w_ple_proj):
    b, s, d = x_prompt.shape
    n, t, _ = x_sample.shape
    depth = norm_mix.shape[0]
    hp, hs = x_prompt.reshape(b * s, d), x_sample.reshape(n * t, d)
    dil_p, dil_s = [[] for _ in range(N_DIL)], [[] for _ in range(N_DIL)]
    win_p, win_s, kv_p, kv_s, conv_p, conv_s = [], [], [], [], [], []
    for i in range(depth):
        li = i // 2
        if i % 2 == 0:
            hp, hs, new_p, new_s = _layer_a(
                hp, hs, b, s, n, t, norm_mix[i], w_in_a[li].astype(BF16), w_out_a[li].astype(BF16), rel_bias,
                (state_dil_w128[li], state_dil_w512[li], state_dil_w2048[li]))
            for g in range(N_DIL):
                dil_p[g].append(new_p[g])
                dil_s[g].append(new_s[g])
        else:
            hp, hs, wp, ws, rp, rs = _layer_b(
                hp, hs, b, s, n, t, norm_mix[i], w_in_b[li], w_out_b[li].astype(BF16), rel_bias,
                (cmp_pe[li], cmp_w1[li], cmp_b1[li], cmp_w2[li], cmp_b2[li]), cache_nsa_kv[li], page_table,
                state_nsa_win[li])
            win_p.append(wp)
            win_s.append(ws)
            kv_p.append(rp)
            kv_s.append(rs)
        hp, hs, cp, cs = _ffn_and_ple(
            hp, hs, b, s, n, t, i, norm_ffn[i], norm_ple[i], w_ffn_in[i].astype(BF16), conv_w[i], conv_b[i],
            w_ffn_out[i].astype(BF16), state_conv[i], p_prompt[i].reshape(b * s, -1), p_sample[i].reshape(n * t, -1),
            w_ple_gate[i].astype(BF16), w_ple_proj[i].astype(BF16))
        conv_p.append(cp)
        conv_s.append(cs)
    y_prompt = _rmsnorm(hp, norm_final, F32).reshape(b, s, d)
    y_sample = _rmsnorm(hs, norm_final, F32).reshape(n, t, d)
    return (y_prompt, y_sample,
            jnp.stack(dil_p[0]), jnp.stack(dil_s[0]), jnp.stack(dil_p[1]), jnp.stack(dil_s[1]),
            jnp.stack(dil_p[2]), jnp.stack(dil_s[2]),
            jnp.stack(win_p), jnp.stack(win_s), jnp.stack(conv_p), jnp.stack(conv_s),
            jnp.stack(kv_p), jnp.stack(kv_s))
```

```python
import functools

import jax
import jax.numpy as jnp
from jax import lax
from jax.experimental import pallas as pl
from jax.experimental.pallas import tpu as pltpu

F32 = jnp.float32
BF16 = jnp.bfloat16

D_MODEL = 2048
HEAD_DIM = 128
N_HEADS = 16
DIL_PAIRS = ((128, 1), (512, 4), (2048, 16))
N_DIL = 3
BLK = 128
KV_HEADS = 4
GROUP = 4
CMP_BLOCK = 32
CMP_STRIDE = 16
SEL_BLOCK = 64
SEL_TOPN = 16
NSA_WINDOW = 512
D_FF = 5632
CONV_W = 3
REL_BUCKETS = 32
EPS = 1e-6
NEG = -1e30
FORCED_SCORE = 1e4
SCALE = HEAD_DIM ** -0.5
QKV_A = N_DIL * 3 * N_HEADS * HEAD_DIM
NSA_Q = N_HEADS * HEAD_DIM
NSA_KV = KV_HEADS * HEAD_DIM

BUCKET_START = (1, 2, 3, 4, 5, 6, 7, 8, 9, 10, 11, 12, 13, 14, 15, 16, 22, 30, 40, 54, 73, 99,
                134, 182, 246, 332, 450, 609, 825, 1117, 1513)

VMEM_LIMIT_V7X = 56 * 1024 * 1024


def _params(sem, vmem=VMEM_LIMIT_V7X):
    return pltpu.CompilerParams(dimension_semantics=sem, vmem_limit_bytes=vmem)


def _pick(n, cands):
    for c in cands:
        if n % c == 0:
            return c
    return n


def _rmsnorm_kernel(x_ref, g_ref, o_ref):
    x = x_ref[...]
    ms = jnp.mean(x * x, axis=-1, keepdims=True)
    o_ref[...] = ((x * lax.rsqrt(ms + EPS)) * g_ref[...]).astype(o_ref.dtype)


def _rmsnorm(x, g, out_dtype):
    m, d = x.shape
    tm = _pick(m, (512, 256, 128, 32))
    return pl.pallas_call(
        _rmsnorm_kernel,
        out_shape=jax.ShapeDtypeStruct((m, d), out_dtype),
        grid=(m // tm,),
        in_specs=[pl.BlockSpec((tm, d), lambda i: (i, 0)), pl.BlockSpec((1, d), lambda i: (0, 0))],
        out_specs=pl.BlockSpec((tm, d), lambda i: (i, 0)),
        compiler_params=_params(("parallel",)),
        name="rmsnorm",
    )(x, g.reshape(1, d))


def _normed(x, g):
    ms = jnp.mean(x * x, axis=-1, keepdims=True)
    return ((x * lax.rsqrt(ms + EPS)) * g).astype(BF16)


def _project_kernel(*refs, norm, cast_w, residual, sigmoid, heads):
    refs = list(refs)
    x_ref = refs.pop(0)
    g_ref = refs.pop(0) if norm else None
    w_ref = refs.pop(0)
    r_ref = refs.pop(0) if residual else None
    o_ref = refs.pop(0)
    wb_ref = refs.pop(0) if cast_w else None
    if norm:
        a_scr = refs.pop(0)

        @pl.when(pl.program_id(1) == 0)
        def _():
            a_scr[...] = _normed(x_ref[...], g_ref[...])

        a = a_scr[...]
    else:
        a = x_ref[...]
    w = w_ref[...]
    if cast_w:
        w = w.astype(BF16)
        wb_ref[...] = w
    acc = jnp.dot(a, w, preferred_element_type=F32)
    if sigmoid:
        acc = jax.nn.sigmoid(acc)
    if residual:
        acc = r_ref[...] + acc
    if heads:
        for j in range(o_ref.shape[0]):
            o_ref[j] = acc[:, j * HEAD_DIM:(j + 1) * HEAD_DIM]
    else:
        o_ref[...] = acc


def _project(x, w, *, gain=None, res=None, cast_w=False, sigmoid=False, heads=False):
    m, k = x.shape
    n = w.shape[1]
    tm = _pick(m, (1024, 512, 256, 128))
    tn = _pick(n, (1024, 512, 256, 128))
    assert not cast_w or m == tm
    norm, residual = gain is not None, res is not None
    args, in_specs = [x], [pl.BlockSpec((tm, k), lambda i, j: (i, 0))]
    if norm:
        args.append(gain.reshape(1, k))
        in_specs.append(pl.BlockSpec((1, k), lambda i, j: (0, 0)))
    args.append(w)
    in_specs.append(pl.BlockSpec((k, tn), lambda i, j: (0, j)))
    if residual:
        args.append(res)
        in_specs.append(pl.BlockSpec((tm, tn), lambda i, j: (i, j)))
    if heads:
        out_shape = [jax.ShapeDtypeStruct((n // HEAD_DIM, m, HEAD_DIM), F32)]
        out_specs = [pl.BlockSpec((tn // HEAD_DIM, tm, HEAD_DIM), lambda i, j: (j, i, 0))]
    else:
        out_shape = [jax.ShapeDtypeStruct((m, n), F32)]
        out_specs = [pl.BlockSpec((tm, tn), lambda i, j: (i, j))]
    if cast_w:
        out_shape.append(jax.ShapeDtypeStruct((k, n), BF16))
        out_specs.append(pl.BlockSpec((k, tn), lambda i, j: (0, j)))
    outs = pl.pallas_call(
        functools.partial(_project_kernel, norm=norm, cast_w=cast_w, residual=residual, sigmoid=sigmoid,
                          heads=heads),
        out_shape=out_shape,
        grid=(m // tm, n // tn),
        in_specs=in_specs,
        out_specs=out_specs,
        scratch_shapes=[pltpu.VMEM((tm, k), BF16)] if norm else [],
        compiler_params=_params(("parallel", "arbitrary")),
        name="project",
    )(*args)
    return tuple(outs) if cast_w else outs[0]


def _conv_gelu_val(g, g1, g2, val, cw_ref, cb_ref):
    c = cb_ref[...] + g2 * cw_ref[0:1, :]
    c = c + g1 * cw_ref[1:2, :]
    c = c + g * cw_ref[2:3, :]
    return (jax.nn.gelu(c) * val).astype(BF16)


def _ffn_out_prompt_kernel(g_ref, halo_ref, v_ref, cw_ref, cb_ref, w_ref, r_ref, o_ref, acc_ref, *,
                           tiles_per_seq):
    i, k = pl.program_id(0), pl.program_id(1)

    @pl.when(k == 0)
    def _():
        acc_ref[...] = jnp.zeros_like(acc_ref)

    g = g_ref[...]
    row = lax.broadcasted_iota(jnp.int32, g.shape, 0)
    halo = jnp.where(i % tiles_per_seq == 0, 0.0, halo_ref[...])
    g1 = jnp.where(row == 0, halo[7:8, :], pltpu.roll(g, 1, 0))
    g2 = jnp.where(row == 0, halo[6:7, :], jnp.where(row == 1, halo[7:8, :], pltpu.roll(g, 2, 0)))
    u = _conv_gelu_val(g, g1, g2, v_ref[...], cw_ref, cb_ref)
    acc_ref[...] += jnp.dot(u, w_ref[...], preferred_element_type=F32)

    @pl.when(k == pl.num_programs(1) - 1)
    def _():
        o_ref[...] = r_ref[...] + acc_ref[...]


def _ffn_out_prompt(hu, conv_w, conv_b, w_out, res, seq):
    m = hu.shape[0]
    tm, tk = 1024, 512
    nk = D_FF // tk
    return pl.pallas_call(
        functools.partial(_ffn_out_prompt_kernel, tiles_per_seq=seq // tm),
        out_shape=jax.ShapeDtypeStruct((m, D_MODEL), F32),
        grid=(m // tm, nk),
        in_specs=[
            pl.BlockSpec((tm, tk), lambda i, k: (i, k)),
            pl.BlockSpec((8, tk), lambda i, k: (jnp.maximum(i * (tm // 8) - 1, 0), k)),
            pl.BlockSpec((tm, tk), lambda i, k: (i, k + nk)),
            pl.BlockSpec((CONV_W, tk), lambda i, k: (0, k)),
            pl.BlockSpec((1, tk), lambda i, k: (0, k)),
            pl.BlockSpec((tk, D_MODEL), lambda i, k: (k, 0)),
            pl.BlockSpec((tm, D_MODEL), lambda i, k: (i, 0)),
        ],
        out_specs=pl.BlockSpec((tm, D_MODEL), lambda i, k: (i, 0)),
        scratch_shapes=[pltpu.VMEM((tm, D_MODEL), F32)],
        compiler_params=_params(("parallel", "arbitrary")),
        name="ffn_out_prompt",
    )(hu, hu, hu, conv_w, conv_b.reshape(1, D_FF), w_out, res)


def _ffn_out_sample_kernel(g_ref, e1_ref, e2_ref, v_ref, cw_ref, cb_ref, w_ref, r_ref, o_ref, wb_ref, acc_ref, *,
                           t_len):
    k = pl.program_id(0)

    @pl.when(k == 0)
    def _():
        acc_ref[...] = jnp.zeros_like(acc_ref)

    g = g_ref[...]
    t = lax.broadcasted_iota(jnp.int32, g.shape, 0) % t_len
    g1 = jnp.where(t == 0, e1_ref[...], pltpu.roll(g, 1, 0))
    g2 = jnp.where(t < 2, e2_ref[...], pltpu.roll(g, 2, 0))
    u = _conv_gelu_val(g, g1, g2, v_ref[...], cw_ref, cb_ref)
    w = w_ref[...].astype(BF16)
    wb_ref[...] = w
    acc_ref[...] += jnp.dot(u, w, preferred_element_type=F32)

    @pl.when(k == pl.num_programs(0) - 1)
    def _():
        o_ref[...] = r_ref[...] + acc_ref[...]


def _ffn_out_sample(hu, conv_prev, conv_w, conv_b, w_out, res, t_len):
    m = hu.shape[0]
    n = m // t_len
    tk = 512
    nk = D_FF // tk
    zeros = jnp.zeros((n, t_len - 1, D_FF), F32)
    e1 = jnp.concatenate([conv_prev[:, 1:2], zeros], axis=1).reshape(m, D_FF)
    e2 = jnp.concatenate([conv_prev, zeros[:, 1:]], axis=1).reshape(m, D_FF)
    return pl.pallas_call(
        functools.partial(_ffn_out_sample_kernel, t_len=t_len),
        out_shape=(jax.ShapeDtypeStruct((m, D_MODEL), F32), jax.ShapeDtypeStruct((D_FF, D_MODEL), BF16)),
        grid=(nk,),
        in_specs=[
            pl.BlockSpec((m, tk), lambda k: (0, k)),
            pl.BlockSpec((m, tk), lambda k: (0, k)),
            pl.BlockSpec((m, tk), lambda k: (0, k)),
            pl.BlockSpec((m, tk), lambda k: (0, k + nk)),
            pl.BlockSpec((CONV_W, tk), lambda k: (0, k)),
            pl.BlockSpec((1, tk), lambda k: (0, k)),
            pl.BlockSpec((tk, D_MODEL), lambda k: (k, 0)),
            pl.BlockSpec((m, D_MODEL), lambda k: (0, 0)),
        ],
        out_specs=(pl.BlockSpec((m, D_MODEL), lambda k: (0, 0)), pl.BlockSpec((tk, D_MODEL), lambda k: (k, 0))),
        scratch_shapes=[pltpu.VMEM((m, D_MODEL), F32)],
        compiler_params=_params(("arbitrary",)),
        name="ffn_out_sample",
    )(hu, e1, e2, hu, conv_w, conv_b.reshape(1, D_FF), w_out, res)


def _ple_kernel(h_ref, g_ref, wg_ref, p_ref, wp_ref, o_ref, *rest, cast_w):
    a_scr = rest[-1]
    j = pl.program_id(1)
    tn = o_ref.shape[1]

    @pl.when(j == 0)
    def _():
        a_scr[...] = _normed(h_ref[...], g_ref[...])

    wg, wp = wg_ref[...], wp_ref[...]
    if cast_w:
        wg, wp = wg.astype(BF16), wp.astype(BF16)
        rest[0][...] = wg
        rest[1][...] = wp
    gate = jax.nn.sigmoid(jnp.dot(a_scr[...], wg, preferred_element_type=F32))
    proj = jnp.dot(p_ref[...], wp, preferred_element_type=F32)
    o_ref[...] = h_ref[:, pl.ds(pl.multiple_of(j * tn, tn), tn)] + gate * proj


def _ple_add(h, gain, w_gate, p, w_proj, cast_w=False):
    m, d = h.shape
    kp = p.shape[1]
    tm = _pick(m, (1024, 512, 256, 128))
    tn = 1024
    assert not cast_w or m == tm
    out_shape = [jax.ShapeDtypeStruct((m, d), F32)]
    out_specs = [pl.BlockSpec((tm, tn), lambda i, j: (i, j))]
    if cast_w:
        out_shape += [jax.ShapeDtypeStruct((d, d), BF16), jax.ShapeDtypeStruct((kp, d), BF16)]
        out_specs += [pl.BlockSpec((d, tn), lambda i, j: (0, j)), pl.BlockSpec((kp, tn), lambda i, j: (0, j))]
    outs = pl.pallas_call(
        functools.partial(_ple_kernel, cast_w=cast_w),
        out_shape=out_shape,
        grid=(m // tm, d // tn),
        in_specs=[
            pl.BlockSpec((tm, d), lambda i, j: (i, 0)),
            pl.BlockSpec((1, d), lambda i, j: (0, 0)),
            pl.BlockSpec((d, tn), lambda i, j: (0, j)),
            pl.BlockSpec((tm, kp), lambda i, j: (i, 0)),
            pl.BlockSpec((kp, tn), lambda i, j: (0, j)),
        ],
        out_specs=out_specs,
        scratch_shapes=[pltpu.VMEM((tm, d), BF16)],
        compiler_params=_params(("parallel", "arbitrary")),
        name="ple_add",
    )(h, gain.reshape(1, d), w_gate, p, w_proj)
    return tuple(outs) if cast_w else outs[0]


def _bias_table_kernel(rb_ref, o_ref, *, a0, ag, ar, ac, lo, hi, mod):
    h, g = pl.program_id(0), pl.program_id(1)
    shape = o_ref.shape[2:]
    dist = (a0 + ag * g + ar * lax.broadcasted_iota(jnp.int32, shape, 0)
            + ac * lax.broadcasted_iota(jnp.int32, shape, 1))
    d = jnp.maximum(dist, 0)
    out = jnp.full(shape, rb_ref[0, h], F32)
    for k in range(1, REL_BUCKETS):
        out = jnp.where(d >= BUCKET_START[k - 1], rb_ref[k, h], out)
    ok = (dist >= lo) & (dist <= hi)
    if mod > 1:
        ok = ok & ((d & (mod - 1)) == 0)
    o_ref[0, 0] = jnp.where(ok, out, NEG)


def _bias_table(rel_bias, n_g, n_r, n_c, a0, ag, ar, ac, lo, hi, mod=1):
    assert mod & (mod - 1) == 0
    return pl.pallas_call(
        functools.partial(_bias_table_kernel, a0=a0, ag=ag, ar=ar, ac=ac, lo=lo, hi=hi, mod=mod),
        out_shape=jax.ShapeDtypeStruct((N_HEADS, n_g, n_r, n_c), F32),
        grid=(N_HEADS, n_g),
        in_specs=[pl.BlockSpec(memory_space=pltpu.SMEM)],
        out_specs=pl.BlockSpec((1, 1, n_r, n_c), lambda h, g: (h, g, 0, 0)),
        compiler_params=_params(("parallel", "parallel")),
        name="bias_table",
    )(rel_bias)


def _dil_prompt_kernel(q_ref, kc_ref, kp_ref, vc_ref, vp_ref, bias_ref, o_ref, lse_ref, *, dil, hps):
    first = pl.program_id(1) == 0
    hb = pl.program_id(2)
    col = lax.broadcasted_iota(jnp.int32, (BLK, 2 * BLK), 1)
    edge = jnp.where(first & (col < BLK), NEG, 0.0)
    lane = lax.broadcasted_iota(jnp.int32, (BLK, BLK), 1)

    @pl.when(hb == 0)
    def _():
        lse_ref[...] = jnp.zeros_like(lse_ref)

    def unit(u, carry):
        hh, r = u // dil, u % dil
        rows = pl.ds(r, BLK, stride=dil)
        q = q_ref[hh, rows, :].astype(BF16)
        k = jnp.concatenate([kp_ref[hh, rows, :], kc_ref[hh, rows, :]], axis=0).astype(BF16)
        v = jnp.concatenate([vp_ref[hh, rows, :], vc_ref[hh, rows, :]], axis=0).astype(BF16)
        h = hb * hps + hh
        s = lax.dot_general(q, k, (((1,), (1,)), ((), ())), preferred_element_type=F32)
        s = s * SCALE + bias_ref[h] + edge
        m = jnp.max(s, axis=-1, keepdims=True)
        p = jnp.exp(s - m)
        l = jnp.sum(p, axis=-1, keepdims=True)
        o_ref[hh, rows, :] = jnp.dot(p.astype(BF16), v, preferred_element_type=F32) / l
        lse_ref[rows, :] = jnp.where(lane == h, m + jnp.log(l), lse_ref[rows, :])
        return carry

    lax.fori_loop(0, hps * dil, unit, 0, unroll=4)


def _dil_prompt_group(qkv_hm, bias, grp, dil, b, s):
    span = BLK * dil
    nsp = s // span
    hps = N_HEADS // dil
    nhb = N_HEADS // hps

    def slab(part, prev):
        base = (grp * 3 + part) * N_HEADS // hps
        if prev:
            return lambda bi, sp, hb: (base + hb, bi * nsp + jnp.maximum(sp - 1, 0), 0)
        return lambda bi, sp, hb: (base + hb, bi * nsp + sp, 0)

    blk = (hps, span, HEAD_DIM)
    return pl.pallas_call(
        functools.partial(_dil_prompt_kernel, dil=dil, hps=hps),
        out_shape=(jax.ShapeDtypeStruct((N_HEADS, b * s, HEAD_DIM), F32),
                   jax.ShapeDtypeStruct((b * s, BLK), F32)),
        grid=(b, nsp, nhb),
        in_specs=[pl.BlockSpec(blk, slab(0, False)), pl.BlockSpec(blk, slab(1, False)),
                  pl.BlockSpec(blk, slab(1, True)), pl.BlockSpec(blk, slab(2, False)),
                  pl.BlockSpec(blk, slab(2, True)),
                  pl.BlockSpec((N_HEADS, BLK, 2 * BLK), lambda bi, sp, hb: (0, 0, 0))],
        out_specs=(pl.BlockSpec(blk, lambda bi, sp, hb: (hb, bi * nsp + sp, 0)),
                   pl.BlockSpec((span, BLK), lambda bi, sp, hb: (bi * nsp + sp, 0))),
        compiler_params=_params(("parallel", "parallel", "arbitrary")),
        name=f"dil_attn_prompt_g{grp}",
    )(qkv_hm, qkv_hm, qkv_hm, qkv_hm, qkv_hm, bias)


def _dil_combine_kernel(o0_ref, o1_ref, o2_ref, l0_ref, l1_ref, l2_ref, o_ref, *, head_major):
    l0, l1, l2 = l0_ref[...], l1_ref[...], l2_ref[...]
    mx = jnp.maximum(jnp.maximum(l0, l1), l2)
    e0, e1, e2 = jnp.exp(l0 - mx), jnp.exp(l1 - mx), jnp.exp(l2 - mx)
    den = e0 + e1 + e2
    w0, w1, w2 = e0 / den, e1 / den, e2 / den
    for h in range(N_HEADS):
        sl = slice(h * HEAD_DIM, (h + 1) * HEAD_DIM)
        g0, g1, g2 = ((r[h] for r in (o0_ref, o1_ref, o2_ref)) if head_major
                      else (r[:, sl] for r in (o0_ref, o1_ref, o2_ref)))
        o = (w0[:, h:h + 1] * g0 + w1[:, h:h + 1] * g1) + w2[:, h:h + 1] * g2
        o_ref[:, sl] = o.astype(o_ref.dtype)


def _dil_combine(outs, lses, head_major):
    m = lses[0].shape[0]
    wide = N_HEADS * HEAD_DIM
    tm = _pick(m, (256, 128, 32))
    ob = pl.BlockSpec((tm, wide), lambda i: (i, 0))
    ib = pl.BlockSpec((N_HEADS, tm, HEAD_DIM), lambda i: (0, i, 0)) if head_major else ob
    lb = pl.BlockSpec((tm, BLK), lambda i: (i, 0))
    return pl.pallas_call(
        functools.partial(_dil_combine_kernel, head_major=head_major),
        out_shape=jax.ShapeDtypeStruct((m, wide), BF16),
        grid=(m // tm,),
        in_specs=[ib, ib, ib, lb, lb, lb],
        out_specs=ob,
        compiler_params=_params(("parallel",)),
        name="dil_combine",
    )(*outs, *lses)


HEAD_TILE = 8


def _heads_first(x):
    return pltpu.einshape("mhd->hmd", x)


def _dil_sample_kernel(q_ref, k_ref, v_ref, kn_ref, vn_ref, bias_ref, biasn_ref, o_ref, lse_ref,
                       m_scr, l_scr, acc_scr, k_scr, v_scr):
    ht, c = pl.program_id(1), pl.program_id(2)
    k_scr[...] = _heads_first(k_ref[...])
    v_scr[...] = _heads_first(v_ref[...])

    @pl.when(c == 0)
    def _():
        m_scr[...] = jnp.full_like(m_scr, NEG)
        l_scr[...] = jnp.zeros_like(l_scr)
        acc_scr[...] = jnp.zeros_like(acc_scr)

    @pl.when((c == 0) & (ht == 0))
    def _():
        lse_ref[...] = jnp.zeros_like(lse_ref)

    def step(hh, k, v, bias):
        sl = pl.ds(pl.multiple_of(hh * HEAD_DIM, HEAD_DIM), HEAD_DIM)
        q = q_ref[0, :, sl].astype(BF16)
        s = lax.dot_general(q, k.astype(BF16), (((1,), (1,)), ((), ())), preferred_element_type=F32)
        s = s * SCALE + bias
        m_old = m_scr[hh]
        m_new = jnp.maximum(m_old, jnp.max(s, axis=-1, keepdims=True))
        alpha = jnp.exp(m_old - m_new)
        p = jnp.where(bias > 0.5 * NEG, jnp.exp(s - m_new), 0.0)
        l_scr[hh] = alpha * l_scr[hh] + jnp.sum(p, axis=-1, keepdims=True)
        acc_scr[hh] = alpha * acc_scr[hh] + jnp.dot(p.astype(BF16), v.astype(BF16), preferred_element_type=F32)
        m_scr[hh] = m_new

    def head(hh, carry):
        step(hh, k_scr[hh], v_scr[hh], bias_ref[hh, 0])
        return carry

    lax.fori_loop(0, HEAD_TILE, head, 0, unroll=True)

    @pl.when(c == pl.num_programs(2) - 1)
    def _():
        lane = lax.broadcasted_iota(jnp.int32, (8, BLK), 1)

        def fin(hh, carry):
            sl = pl.ds(pl.multiple_of(hh * HEAD_DIM, HEAD_DIM), HEAD_DIM)
            step(hh, kn_ref[0, :, sl], vn_ref[0, :, sl], biasn_ref[hh])
            l = jnp.maximum(l_scr[hh], 1e-30)
            o_ref[0, :, sl] = acc_scr[hh] / l
            lse_ref[0] = jnp.where(lane == ht * HEAD_TILE + hh, m_scr[hh] + jnp.log(l), lse_ref[0])
            return carry

        lax.fori_loop(0, HEAD_TILE, fin, 0, unroll=True)


def _dil_sample_group(q, kn, vn, buf, bias, bias_new, dil):
    n, lb = buf.shape[:2]
    n_cls = bias.shape[1]
    wide = N_HEADS * HEAD_DIM
    half = wide // 2
    rows = lb // dil
    tiles = 2 * N_HEADS // HEAD_TILE
    view = buf.reshape(n, rows, dil * tiles, HEAD_TILE, HEAD_DIM)
    nht = N_HEADS // HEAD_TILE
    return pl.pallas_call(
        _dil_sample_kernel,
        out_shape=(jax.ShapeDtypeStruct((n, 8, wide), F32), jax.ShapeDtypeStruct((n, 8, BLK), F32)),
        grid=(n, nht, n_cls),
        in_specs=[
            pl.BlockSpec((1, 8, half), lambda i, ht, c: (i, 0, ht)),
            pl.BlockSpec((None, rows, None, HEAD_TILE, HEAD_DIM), lambda i, ht, c: (i, 0, c * tiles + ht, 0, 0)),
            pl.BlockSpec((None, rows, None, HEAD_TILE, HEAD_DIM),
                         lambda i, ht, c: (i, 0, c * tiles + nht + ht, 0, 0)),
            pl.BlockSpec((1, BLK, half), lambda i, ht, c: (i, 0, ht)),
            pl.BlockSpec((1, BLK, half), lambda i, ht, c: (i, 0, ht)),
            pl.BlockSpec((HEAD_TILE, 1, 8, rows), lambda i, ht, c: (ht, c, 0, 0)),
            pl.BlockSpec((HEAD_TILE, 8, BLK), lambda i, ht, c: (ht, 0, 0)),
        ],
        out_specs=(pl.BlockSpec((1, 8, half), lambda i, ht, c: (i, 0, ht)),
                   pl.BlockSpec((1, 8, BLK), lambda i, ht, c: (i, 0, 0))),
        scratch_shapes=[pltpu.VMEM((HEAD_TILE, 8, 1), F32), pltpu.VMEM((HEAD_TILE, 8, 1), F32),
                        pltpu.VMEM((HEAD_TILE, 8, HEAD_DIM), F32),
                        pltpu.VMEM((HEAD_TILE, rows, HEAD_DIM), F32), pltpu.VMEM((HEAD_TILE, rows, HEAD_DIM), F32)],
        compiler_params=_params(("parallel", "arbitrary", "arbitrary")),
        name="dil_attn_sample",
    )(q, view, view, kn, vn, bias, bias_new)


def _pad_rows(x, rows):
    return jnp.pad(x, ((0, 0), (0, rows - x.shape[1]), (0, 0)))


def _layer_a(hp, hs, b, s, n, t, norm_g, w_in, w_out, rel_bias, bufs):
    wide = N_HEADS * HEAD_DIM
    qkv_s, w_in = _project(hs, w_in, gain=norm_g, cast_w=True)
    qkv_s = qkv_s.reshape(n, t, QKV_A)
    qkv_p = _project(hp, w_in, gain=norm_g, heads=True)
    outs_p, lses_p, outs_s, lses_s, new_p, new_s = [], [], [], [], [], []
    for grp, (win, dil) in enumerate(DIL_PAIRS):
        base = grp * 3 * wide
        bias = _bias_table(rel_bias, 1, BLK, 2 * BLK, BLK * dil, 0, dil, -dil, 0, win).reshape(N_HEADS, BLK, 2 * BLK)
        o, lse = _dil_prompt_group(qkv_p, bias, grp, dil, b, s)
        outs_p.append(o)
        lses_p.append(lse)
        keep = min(win, s)
        kv = qkv_p.reshape(3 * N_DIL, N_HEADS, b, s, HEAD_DIM)[grp * 3 + 1:grp * 3 + 3, :, :, s - keep:]
        new_p.append(kv.transpose(2, 3, 0, 1, 4))
        buf = bufs[grp]
        lb = buf.shape[1]
        n_cls = min(dil, t)
        bias_buf = _bias_table(rel_bias, n_cls, 8, lb // dil, lb, -1, 1, -dil, 0, win, dil)
        bias_new = _bias_table(rel_bias, 1, 8, BLK, 0, 0, 1, -1, 0, win, dil).reshape(N_HEADS, 8, BLK)
        q = _pad_rows(qkv_s[:, :, base:base + wide], 8)
        kn = _pad_rows(qkv_s[:, :, base + wide:base + 2 * wide], BLK)
        vn = _pad_rows(qkv_s[:, :, base + 2 * wide:base + 3 * wide], BLK)
        o, lse = _dil_sample_group(q, kn, vn, buf, bias_buf, bias_new, dil)
        outs_s.append(o[:, :t].reshape(n * t, wide))
        lses_s.append(lse[:, :t].reshape(n * t, BLK))
        kv_new = qkv_s[:, :, base + wide:base + 3 * wide].reshape(n, t, 2, N_HEADS, HEAD_DIM)
        new_s.append(jnp.concatenate([buf, kv_new], axis=1)[:, t:])
    hs, w_out = _project(_dil_combine(outs_s, lses_s, False), w_out, res=hs, cast_w=True)
    hp = _project(_dil_combine(outs_p, lses_p, True), w_out, res=hp)
    return hp, hs, new_p, new_s


PAGES_PER_STEP = 16
CHUNKS_PER_PAGE = BLK // CMP_STRIDE
NT_DIMS = (((1,), (1,)), ((), ()))
TN_DIMS = (((0,), (0,)), ((), ()))


def _page_spec(p, half):
    return pl.BlockSpec((BLK, HEAD_TILE, HEAD_DIM),
                        lambda i, j, *rest: (rest[-1][i, j * PAGES_PER_STEP + p], half, 0))


def _cmp_proj_kernel(pt_ref, *refs):
    pages = refs[:PAGES_PER_STEP]
    w_ref, o_ref = refs[PAGES_PER_STEP:PAGES_PER_STEP + 2]
    ys = [pltpu.einshape("ctgd->tgcd", pg[...].reshape(CHUNKS_PER_PAGE, CMP_STRIDE, HEAD_TILE, HEAD_DIM))
          for pg in pages]
    for c in range(2):
        acc = jnp.zeros((KV_HEADS * BLK, 2 * HEAD_DIM), F32)
        for t in range(CMP_STRIDE):
            rows = [y[t, c * KV_HEADS + kh] for kh in range(KV_HEADS) for y in ys]
            lhs = jnp.concatenate(rows, axis=0).astype(BF16)
            acc = acc + jnp.dot(lhs, w_ref[c, t], preferred_element_type=F32)
        for kh in range(KV_HEADS):
            o_ref[0, c, kh] = acc[kh * BLK:(kh + 1) * BLK]


def _cmp_proj(pages, page_table, w1r):
    n, n_pages = page_table.shape
    chunks = n_pages * CHUNKS_PER_PAGE
    grid_spec = pltpu.PrefetchScalarGridSpec(
        num_scalar_prefetch=1,
        grid=(n, n_pages // PAGES_PER_STEP),
        in_specs=[_page_spec(p, 0) for p in range(PAGES_PER_STEP)]
        + [pl.BlockSpec((2, CMP_STRIDE, HEAD_DIM, 2 * HEAD_DIM), lambda i, j, pt: (0, 0, 0, 0))],
        out_specs=pl.BlockSpec((1, 2, KV_HEADS, BLK, 2 * HEAD_DIM), lambda i, j, pt: (i, 0, 0, j, 0)),
    )
    return pl.pallas_call(
        _cmp_proj_kernel,
        out_shape=jax.ShapeDtypeStruct((n, 2, KV_HEADS, chunks, 2 * HEAD_DIM), F32),
        grid_spec=grid_spec,
        compiler_params=_params(("parallel", "arbitrary")),
        name="nsa_cmp_proj",
    )(page_table, *([pages] * PAGES_PER_STEP), w1r)


def _finish_compress(a, pe_row, w1f, b1, w2, b2, n_blocks):
    rows = a.shape[0]
    cst = jnp.dot(pe_row, w1f, preferred_element_type=F32)[0:1]
    h = (b1 + cst) + a[:, :HEAD_DIM] + pltpu.roll(a[:, HEAD_DIM:], rows - 1, 0)
    x = jnp.dot(jax.nn.gelu(h).astype(BF16), w2, preferred_element_type=F32) + b2
    return jnp.where(lax.broadcasted_iota(jnp.int32, x.shape, 0) < n_blocks, x, 0.0)


def _split3(x):
    hi = x.astype(BF16)
    r = x - hi.astype(F32)
    mid = r.astype(BF16)
    return hi, mid, (r - mid.astype(F32)).astype(BF16)


def _top_n(score, n, axis):
    idx = lax.broadcasted_iota(jnp.int32, score.shape, axis).astype(F32)
    big = float(score.shape[axis])

    def body(_, carry):
        sc, sel = carry
        mx = jnp.max(sc, axis=axis, keepdims=True)
        first = jnp.min(jnp.where(sc == mx, idx, big), axis=axis, keepdims=True)
        hit = idx == first
        return jnp.where(hit, -jnp.inf, sc), jnp.where(hit, 1.0, sel)

    return lax.fori_loop(0, n, body, (score, jnp.zeros(score.shape, F32)))[1]


def _sel_scores(p_slc, blk, cur, n_blocks):
    forced = (blk == 0) | (blk == cur) | (blk == cur - 1)
    score = jnp.where(forced, FORCED_SCORE, jnp.where(blk <= cur, p_slc, -1.0))
    return jnp.where(blk < n_blocks, score, -2.0)


def _nsa_prompt_kernel(q_ref, a_ref, pe_ref, w1f_ref, b1_ref, w2_ref, b2_ref, ksel_ref, vsel_ref, kwin_ref,
                       vwin_ref, tbl_ref, tblw_ref, biasc_ref, gates_ref, msel_ref, o_ref, kc_scr, vc_scr, sel_scr,
                       vselt_ref, vwint_ref, *, n_cmp, n_sel):
    i = pl.program_id(2)

    @pl.when(i == 0)
    def _():
        for c, scr in ((0, kc_scr), (1, vc_scr)):
            scr[...] = _finish_compress(a_ref[0, c, 0], pe_ref[c], w1f_ref[c], b1_ref[c], w2_ref[c], b2_ref[c],
                                        n_cmp).astype(BF16)
        for src, dst in ((vsel_ref, vselt_ref), (vwin_ref, vwint_ref)):
            for c in range(src.shape[0] // BLK):
                dst[:, c * BLK:(c + 1) * BLK] = src[c * BLK:(c + 1) * BLK, :].T.astype(BF16)

    q = q_ref[...]
    qs = jnp.concatenate([q[:, g * HEAD_DIM:(g + 1) * HEAD_DIM] for g in range(GROUP)], axis=0).astype(BF16)
    key_i = lax.broadcasted_iota(jnp.int32, (BLK, BLK), 0)
    tok_i = lax.broadcasted_iota(jnp.int32, (BLK, BLK), 1)

    def lanes4(x):
        return jnp.concatenate([x] * GROUP, axis=1)

    mask_c = lanes4(i * BLK + tok_i - (key_i * CMP_STRIDE + (CMP_BLOCK - 1)) >= 0)
    s = lax.dot_general(kc_scr[...], qs, NT_DIMS, preferred_element_type=F32) * SCALE
    s = jnp.where(mask_c, s + jnp.concatenate([biasc_ref[g, 0] for g in range(GROUP)], axis=1), NEG)
    m = jnp.max(s, axis=0, keepdims=True)
    p = jnp.where(mask_c, jnp.exp(s - m), 0.0)
    pn = p / jnp.maximum(jnp.sum(p, axis=0, keepdims=True), 1e-30)
    o_cmp = lax.dot_general(vc_scr[...], pn.astype(BF16), TN_DIMS, preferred_element_type=F32)
    pc = ((pn[:, 0:BLK] + pn[:, BLK:2 * BLK]) + pn[:, 2 * BLK:3 * BLK]) + pn[:, 3 * BLK:4 * BLK]

    msel = msel_ref[...]
    p_slc = sum(jnp.dot(msel, part, preferred_element_type=F32) for part in _split3(pc))
    blk = lax.broadcasted_iota(jnp.int32, p_slc.shape, 0)
    cur = (i * BLK + lax.broadcasted_iota(jnp.int32, p_slc.shape, 1)) // SEL_BLOCK
    sel_scr[...] = (1.0 - _top_n(_sel_scores(p_slc, blk, cur, n_sel), SEL_TOPN, 0)) * NEG

    def attend(k_ref, vt_ref, bias_ref, first_blk, n_blk, extra, carry):
        m_run, l_run, acc = carry
        off = pl.multiple_of(first_blk * BLK, BLK)
        k = k_ref[pl.ds(off, n_blk * BLK), :].astype(BF16)
        bias = []
        for j in range(n_blk):
            idx = jnp.maximum(i - first_blk - j, -1) + 1
            bias.append(jnp.concatenate([bias_ref[g, idx] for g in range(GROUP)], axis=1))
        s = lax.dot_general(k, qs, NT_DIMS, preferred_element_type=F32) * SCALE + jnp.concatenate(bias, axis=0)
        if extra is not None:
            s = s + extra
        m_new = jnp.maximum(m_run, jnp.max(s, axis=0, keepdims=True))
        alpha = jnp.exp(m_run - m_new)
        p = jnp.exp(s - m_new)
        l_new = alpha * l_run + jnp.sum(p, axis=0, keepdims=True)
        vt = vt_ref[:, pl.ds(off, n_blk * BLK)]
        return m_new, l_new, alpha * acc + jnp.dot(vt, p.astype(BF16), preferred_element_type=F32)

    init = (jnp.full((1, GROUP * BLK), NEG, F32), jnp.zeros((1, GROUP * BLK), F32),
            jnp.zeros((HEAD_DIM, GROUP * BLK), F32))

    sel_span = 4
    per_blk = BLK // SEL_BLOCK

    def sel_step(c, carry):
        rows = sel_scr[pl.ds(pl.multiple_of(c * sel_span * per_blk, 8), sel_span * per_blk), :]
        unpicked = jnp.concatenate([jnp.broadcast_to(rows[u:u + 1], (SEL_BLOCK, BLK))
                                    for u in range(sel_span * per_blk)], axis=0)
        return attend(ksel_ref, vselt_ref, tbl_ref, c * sel_span, sel_span, lanes4(unpicked), carry)

    _, l_sel, acc_sel = lax.fori_loop(0, i // sel_span + 1, sel_step, init)
    o_sel = acc_sel / jnp.maximum(l_sel, 1e-30)

    n_win = (NSA_WINDOW - 1 + BLK - 1) // BLK + 1
    _, l_win, acc_win = attend(kwin_ref, vwint_ref, tblw_ref, jnp.maximum(i - (n_win - 1), 0), n_win, None, init)
    o_win = acc_win / jnp.maximum(l_win, 1e-30)

    gt = gates_ref[0]

    def gate(branch):
        return jnp.concatenate([gt[branch * GROUP + g:branch * GROUP + g + 1, :] for g in range(GROUP)], axis=1)

    o = (gate(0) * o_cmp + gate(1) * o_sel) + gate(2) * o_win
    for g in range(GROUP):
        o_ref[0, g] = o[:, g * BLK:(g + 1) * BLK]


def _nsa_prompt(proj, a_cmp, gates_t, cmp_w, tbl, tbl_win, bias_c, msel, b, s):
    pe, w1f, b1, w2, b2 = cmp_w
    nq = s // BLK
    kcol = NSA_Q // HEAD_DIM
    const = lambda shape: pl.BlockSpec(shape, lambda bi, kh, i: (0,) * len(shape))
    return pl.pallas_call(
        functools.partial(_nsa_prompt_kernel, n_cmp=s // CMP_STRIDE - 1, n_sel=s // SEL_BLOCK),
        out_shape=jax.ShapeDtypeStruct((b, N_HEADS, HEAD_DIM, s), F32),
        grid=(b, KV_HEADS, nq),
        in_specs=[
            pl.BlockSpec((BLK, GROUP * HEAD_DIM), lambda bi, kh, i: (bi * nq + i, kh)),
            pl.BlockSpec((1, 2, 1, s // CMP_STRIDE, 2 * HEAD_DIM), lambda bi, kh, i: (bi, 0, kh, 0, 0)),
            const(pe.shape), const(w1f.shape), const(b1.shape), const(w2.shape), const(b2.shape),
            pl.BlockSpec((s, HEAD_DIM), lambda bi, kh, i: (bi, kcol + 2 * KV_HEADS + kh)),
            pl.BlockSpec((s, HEAD_DIM), lambda bi, kh, i: (bi, kcol + 3 * KV_HEADS + kh)),
            pl.BlockSpec((s, HEAD_DIM), lambda bi, kh, i: (bi, kcol + 4 * KV_HEADS + kh)),
            pl.BlockSpec((s, HEAD_DIM), lambda bi, kh, i: (bi, kcol + 5 * KV_HEADS + kh)),
            pl.BlockSpec((GROUP,) + tbl.shape[1:], lambda bi, kh, i: (kh, 0, 0, 0)),
            pl.BlockSpec((GROUP,) + tbl_win.shape[1:], lambda bi, kh, i: (kh, 0, 0, 0)),
            pl.BlockSpec((GROUP, 1, s // CMP_STRIDE, BLK), lambda bi, kh, i: (kh, 0, 0, i)),
            pl.BlockSpec((1, 16, BLK), lambda bi, kh, i: (kh, 0, bi * nq + i)),
            const(msel.shape),
        ],
        out_specs=pl.BlockSpec((1, GROUP, HEAD_DIM, BLK), lambda bi, kh, i: (bi, kh, 0, i)),
        scratch_shapes=[pltpu.VMEM((s // CMP_STRIDE, HEAD_DIM), BF16), pltpu.VMEM((s // CMP_STRIDE, HEAD_DIM), BF16),
                        pltpu.VMEM((s // SEL_BLOCK, BLK), F32),
                        pltpu.VMEM((HEAD_DIM, s), BF16), pltpu.VMEM((HEAD_DIM, s), BF16)],
        compiler_params=_params(("parallel", "parallel", "arbitrary")),
        name="nsa_prompt",
    )(proj, a_cmp, pe, w1f, b1, w2, b2, proj, proj, proj, proj, tbl, tbl_win, bias_c, gates_t, msel)


def _masked_softmax_rows(s, mask):
    s = jnp.where(mask, s, NEG)
    m = jnp.max(s, axis=-1, keepdims=True)
    p = jnp.where(mask, jnp.exp(s - m), 0.0)
    return p / jnp.maximum(jnp.sum(p, axis=-1, keepdims=True), 1e-30)


def _nsa_sample_select_kernel(q_ref, a_ref, pe_ref, w1f_ref, b1_ref, w2_ref, b2_ref, biasc_ref, msel_ref,
                              regroup_ref, ocmp_ref, sel_ref, *, n_cmp, n_sel, past):
    pcs = []
    for kh in range(KV_HEADS):
        kc, vc = (_finish_compress(a_ref[0, c, kh], pe_ref[c], w1f_ref[c], b1_ref[c], w2_ref[c], b2_ref[c],
                                   n_cmp).astype(BF16) for c in range(2))
        bias = biasc_ref[kh]
        s = lax.dot_general(q_ref[0, kh].astype(BF16), kc, NT_DIMS, preferred_element_type=F32) * SCALE + bias
        pn = _masked_softmax_rows(s, bias > 0.5 * NEG)
        ocmp_ref[0, kh] = jnp.dot(pn.astype(BF16), vc, preferred_element_type=F32)
        pcs.append(((pn[0:8] + pn[8:16]) + pn[16:24]) + pn[24:32])
    pc = jnp.concatenate(pcs, axis=0)
    msel = msel_ref[...]
    p_slc = sum(jnp.dot(part, msel, preferred_element_type=F32) for part in _split3(pc))
    blk = lax.broadcasted_iota(jnp.int32, p_slc.shape, 1)
    cur = (past + lax.broadcasted_iota(jnp.int32, p_slc.shape, 0) % 8) // SEL_BLOCK
    sel = _top_n(_sel_scores(p_slc, blk, cur, n_sel), SEL_TOPN, 1).astype(BF16)
    for j in range(regroup_ref.shape[0]):
        part = jnp.dot(sel, regroup_ref[j], preferred_element_type=F32)
        for kh in range(KV_HEADS):
            sel_ref[0, kh, j] = part[kh * 8:(kh + 1) * 8]


def _nsa_sample_select(q, a_cmp, cmp_w, bias_c, msel, regroup, past, t_len):
    pe, w1f, b1, w2, b2 = cmp_w
    n = q.shape[0]
    chunks = a_cmp.shape[3]
    n_steps = regroup.shape[0]
    const = lambda shape: pl.BlockSpec(shape, lambda i: (0,) * len(shape))
    return pl.pallas_call(
        functools.partial(_nsa_sample_select_kernel, n_cmp=chunks - 1, n_sel=(past + t_len + SEL_BLOCK - 1) // SEL_BLOCK,
                          past=past),
        out_shape=(jax.ShapeDtypeStruct((n, KV_HEADS, GROUP * 8, HEAD_DIM), F32),
                   jax.ShapeDtypeStruct((n, KV_HEADS, n_steps, 8, BLK), F32)),
        grid=(n,),
        in_specs=[
            pl.BlockSpec((1, KV_HEADS, GROUP * 8, HEAD_DIM), lambda i: (i, 0, 0, 0)),
            pl.BlockSpec((1, 2, KV_HEADS, chunks, 2 * HEAD_DIM), lambda i: (i, 0, 0, 0, 0)),
            const(pe.shape), const(w1f.shape), const(b1.shape), const(w2.shape), const(b2.shape),
            const(bias_c.shape), const(msel.shape), const(regroup.shape),
        ],
        out_specs=(pl.BlockSpec((1, KV_HEADS, GROUP * 8, HEAD_DIM), lambda i: (i, 0, 0, 0)),
                   pl.BlockSpec((1, KV_HEADS, n_steps, 8, BLK), lambda i: (i, 0, 0, 0, 0))),
        compiler_params=_params(("parallel",)),
        name="nsa_sample_select",
    )(q, a_cmp, pe, w1f, b1, w2, b2, bias_c, msel, regroup)


def _nsa_sample_attend_kernel(pt_ref, *refs):
    pages = refs[:PAGES_PER_STEP]
    (q_ref, sel_ref, expand_ref, bias_ref, kn_ref, vn_ref, biasn_ref, win_ref, kwn_ref, vwn_ref, biasw_ref,
     biaswn_ref, gates_ref, ocmp_ref, o_ref, m_scr, l_scr, acc_scr, kv_scr, win_scr) = refs[PAGES_PER_STEP:]
    j, kh = pl.program_id(1), pl.program_id(2)
    n_steps = pl.num_programs(1)
    qs = q_ref[0, 0].astype(BF16)

    @pl.when(j == 0)
    def _():
        m_scr[kh] = jnp.full((GROUP * 8, 1), NEG, F32)
        l_scr[kh] = jnp.zeros((GROUP * 8, 1), F32)
        acc_scr[kh] = jnp.zeros((GROUP * 8, HEAD_DIM), F32)

    def picked(step):
        sel = sel_ref[0, 0, step].astype(BF16)
        return jnp.concatenate([sel] * GROUP, axis=0)

    def update(k, v, bias, mask):
        s = lax.dot_general(qs, k, NT_DIMS, preferred_element_type=F32) * SCALE + bias
        s = jnp.where(mask, s, NEG)
        m_old = m_scr[kh]
        m_new = jnp.maximum(m_old, jnp.max(s, axis=-1, keepdims=True))
        alpha = jnp.exp(m_old - m_new)
        p = jnp.where(mask, jnp.exp(s - m_new), 0.0)
        l_scr[kh] = alpha * l_scr[kh] + jnp.sum(p, axis=-1, keepdims=True)
        acc_scr[kh] = alpha * acc_scr[kh] + jnp.dot(p.astype(BF16), v, preferred_element_type=F32)
        m_scr[kh] = m_new

    @pl.when(kh == 0)
    def _():
        for p, pg in enumerate(pages):
            kv_scr[:, p * BLK:(p + 1) * BLK, :] = _heads_first(pg[...])

    in_sel = jnp.dot(picked(j), expand_ref[...], preferred_element_type=F32) > 0.5
    update(kv_scr[kh].astype(BF16), kv_scr[KV_HEADS + kh].astype(BF16), bias_ref[0], in_sel)

    @pl.when(j == n_steps - 1)
    def _():
        biasn = biasn_ref[0]
        new_sel = picked(n_steps)[:, 0:1] > 0.5
        update(kn_ref[0, 0].astype(BF16), vn_ref[0, 0].astype(BF16), biasn, (biasn > 0.5 * NEG) & new_sel)
        o_sel = acc_scr[kh] / jnp.maximum(l_scr[kh], 1e-30)
        @pl.when(kh == 0)
        def _():
            win_scr[...] = _heads_first(win_ref[...])

        biasw = jnp.concatenate([biasw_ref[0], biaswn_ref[0]], axis=1)
        kw = jnp.concatenate([win_scr[kh], kwn_ref[0, 0]], axis=0).astype(BF16)
        vw = jnp.concatenate([win_scr[KV_HEADS + kh], vwn_ref[0, 0]], axis=0).astype(BF16)
        sw = lax.dot_general(qs, kw, NT_DIMS, preferred_element_type=F32) * SCALE + biasw
        pw = _masked_softmax_rows(sw, biasw > 0.5 * NEG)
        o_win = jnp.dot(pw.astype(BF16), vw, preferred_element_type=F32)
        o_ref[0, kh] = (gates_ref[0, 0, 0] * ocmp_ref[0, 0] + gates_ref[1, 0, 0] * o_sel) + gates_ref[2, 0, 0] * o_win


def _nsa_sample_attend(pool, page_table, q, sel, expand, bias_sel, k_new, v_new, bias_new, win_buf, kw_new, vw_new,
                       bias_win, bias_win_new, gates, o_cmp):
    n, n_pages = page_table.shape
    n_steps = n_pages // PAGES_PER_STEP
    keys = PAGES_PER_STEP * BLK
    rows = GROUP * 8
    lw = win_buf.shape[1]
    per = lambda shape: pl.BlockSpec((1, 1) + shape, lambda i, j, kh, pt: (i, kh) + (0,) * len(shape))
    by_head = lambda shape: pl.BlockSpec((1,) + shape, lambda i, j, kh, pt: (kh,) + (0,) * len(shape))
    grid_spec = pltpu.PrefetchScalarGridSpec(
        num_scalar_prefetch=1,
        grid=(n, n_steps, KV_HEADS),
        in_specs=[_page_spec(p, 1) for p in range(PAGES_PER_STEP)] + [
            per((rows, HEAD_DIM)),
            pl.BlockSpec((1, 1, n_steps + 1, 8, BLK), lambda i, j, kh, pt: (i, kh, 0, 0, 0)),
            pl.BlockSpec(expand.shape, lambda i, j, kh, pt: (0, 0)),
            pl.BlockSpec((1, rows, keys), lambda i, j, kh, pt: (kh, 0, j)),
            per((BLK, HEAD_DIM)), per((BLK, HEAD_DIM)), by_head((rows, BLK)),
            pl.BlockSpec((None, lw, HEAD_TILE, HEAD_DIM), lambda i, j, kh, pt: (i, 0, 0, 0)),
            per((BLK, HEAD_DIM)), per((BLK, HEAD_DIM)), by_head((rows, lw)), by_head((rows, BLK)),
            pl.BlockSpec((3, 1, 1, rows, HEAD_DIM), lambda i, j, kh, pt: (0, i, kh, 0, 0)),
            per((rows, HEAD_DIM)),
        ],
        out_specs=pl.BlockSpec((1, KV_HEADS, rows, HEAD_DIM), lambda i, j, kh, pt: (i, 0, 0, 0)),
        scratch_shapes=[pltpu.VMEM((KV_HEADS, rows, 1), F32), pltpu.VMEM((KV_HEADS, rows, 1), F32),
                        pltpu.VMEM((KV_HEADS, rows, HEAD_DIM), F32),
                        pltpu.VMEM((HEAD_TILE, keys, HEAD_DIM), F32), pltpu.VMEM((HEAD_TILE, lw, HEAD_DIM), F32)],
    )
    return pl.pallas_call(
        _nsa_sample_attend_kernel,
        out_shape=jax.ShapeDtypeStruct((n, KV_HEADS, rows, HEAD_DIM), F32),
        grid_spec=grid_spec,
        compiler_params=_params(("parallel", "arbitrary", "arbitrary")),
        name="nsa_sample_attend",
    )(page_table, *([pool] * PAGES_PER_STEP), q, sel, expand, bias_sel, k_new, v_new, bias_new, win_buf, kw_new,
      vw_new, bias_win, bias_win_new, gates, o_cmp)


def _sel_weights(n_cmp_rows, n_sel_cols):
    ratio = SEL_BLOCK // CMP_STRIDE
    span = CMP_BLOCK // CMP_STRIDE
    c = jnp.arange(n_cmp_rows)[:, None]
    j = jnp.arange(n_sel_cols)[None, :]
    o = c - ratio * j + (span - 1)
    cnt = jnp.minimum(o, span - 1) - jnp.maximum(o - (ratio - 1), 0) + 1
    return jnp.where((o >= 0) & (o <= ratio + span - 2), cnt, 0).astype(BF16)


def _layer_b(hp, hs, b, s, n, t, norm_g, w_in, w_out, rel_bias, cmp, pool, page_table, win_buf):
    cmp_pe, cmp_w1, cmp_b1, cmp_w2, cmp_b2 = cmp
    n_kvcol = 6 * NSA_KV
    w_main = w_in[:, :NSA_Q + n_kvcol]
    w_gate = jnp.pad(w_in[:, NSA_Q + n_kvcol:], ((0, 0), (0, BLK - 3 * N_HEADS))).astype(BF16)
    w1 = cmp_w1.reshape(2, 2, CMP_STRIDE, HEAD_DIM, HEAD_DIM)
    w1r = jnp.concatenate([w1[:, 0], w1[:, 1]], axis=-1).astype(BF16)
    pe_row = jnp.pad(cmp_pe.reshape(2, 1, CMP_BLOCK * HEAD_DIM), ((0, 0), (0, 7), (0, 0))).astype(BF16)
    cmp_w = (pe_row, cmp_w1.reshape(2, CMP_BLOCK * HEAD_DIM, HEAD_DIM).astype(BF16), cmp_b1.reshape(2, 1, HEAD_DIM),
             cmp_w2.astype(BF16), cmp_b2.reshape(2, 1, HEAD_DIM))
    weights = (w_main, w_gate, w_out, w1r, cmp_w)
    hs, new_win_s, new_kv_s, w_main, w_out = _nsa_sample_path(hs, n, t, norm_g, weights, rel_bias, pool, page_table,
                                                              win_buf)
    weights = (w_main, w_gate, w_out, w1r, cmp_w)
    hp, new_win_p, new_kv_p = _nsa_prompt_path(hp, b, s, norm_g, weights, rel_bias)
    return hp, hs, new_win_p, new_win_s, new_kv_p, new_kv_s


def _nsa_prompt_path(hp, b, s, norm_g, weights, rel_bias):
    w_main, w_gate, w_out, w1r, cmp_w = weights
    proj_p = _project(hp, w_main, gain=norm_g)
    gates_p = _project(hp, w_gate, gain=norm_g, sigmoid=True)
    kv_p = proj_p[:, NSA_Q:].reshape(b, s, 6, KV_HEADS, HEAD_DIM)
    pages_p = proj_p[:, NSA_Q:NSA_Q + 2 * NSA_KV].reshape(b * s, HEAD_TILE, HEAD_DIM)
    table_p = jnp.arange(b * s // BLK, dtype=jnp.int32).reshape(b, s // BLK)
    a_cmp_p = _cmp_proj(pages_p, table_p, w1r)
    nq = s // BLK
    n_win = (NSA_WINDOW - 1 + BLK - 1) // BLK + 1
    tbl = _bias_table(rel_bias, nq + 1, BLK, BLK, -BLK, BLK, -1, 1, 0, 1 << 30)
    tbl_win = _bias_table(rel_bias, n_win + 1, BLK, BLK, -BLK, BLK, -1, 1, 0, NSA_WINDOW - 1)
    bias_c = _bias_table(rel_bias, 1, s // CMP_STRIDE, s, -(CMP_BLOCK - 1), 0, -CMP_STRIDE, 1, 0, 1 << 30)
    gates_t = gates_p[:, :3 * N_HEADS].reshape(b * s, 3, KV_HEADS, GROUP).transpose(2, 1, 3, 0)
    gates_t = jnp.pad(gates_t.reshape(KV_HEADS, 3 * GROUP, b * s), ((0, 0), (0, 16 - 3 * GROUP), (0, 0)))
    msel_p = _sel_weights(s // CMP_STRIDE, s // SEL_BLOCK).T
    o_t = _nsa_prompt(proj_p, a_cmp_p, gates_t, cmp_w, tbl, tbl_win, bias_c, msel_p, b, s)
    o_p = o_t.transpose(0, 3, 1, 2).reshape(b * s, NSA_Q).astype(BF16)
    hp = _project(o_p, w_out, res=hp)
    keep = min(NSA_WINDOW, s)
    return hp, kv_p[:, s - keep:, 4:], kv_p[:, :, :4]


def _nsa_sample_path(hs, n, t, norm_g, weights, rel_bias, pool, page_table, win_buf):
    w_main, w_gate, w_out, w1r, cmp_w = weights
    past = page_table.shape[1] * BLK
    proj_s, w_main = _project(hs, w_main, gain=norm_g, cast_w=True)
    gates_s = _project(hs, w_gate, gain=norm_g, sigmoid=True)
    kv_s = proj_s[:, NSA_Q:].reshape(n, t, 6, KV_HEADS, HEAD_DIM)
    rows = GROUP * 8

    def head_rows(x):
        x = jnp.pad(x.transpose(0, 2, 3, 1, 4), ((0, 0), (0, 0), (0, 0), (0, 8 - t), (0, 0)))
        return x.reshape(n, KV_HEADS, rows, x.shape[-1])

    def new_rows(c):
        return jnp.pad(kv_s[:, :, c].transpose(0, 2, 1, 3), ((0, 0), (0, 0), (0, BLK - t), (0, 0)))

    def head_table(x, cols):
        return x.reshape(KV_HEADS, rows, cols)

    q_s = head_rows(proj_s[:, :NSA_Q].reshape(n, t, KV_HEADS, GROUP, HEAD_DIM))
    pool2 = pool.reshape(pool.shape[0] * BLK, 2 * HEAD_TILE, HEAD_DIM)
    a_cmp_s = _cmp_proj(pool2, page_table, w1r)
    chunks = past // CMP_STRIDE
    big = 1 << 30
    bias_cs = head_table(_bias_table(rel_bias, 1, 8, chunks, past - (CMP_BLOCK - 1), 0, 1, -CMP_STRIDE, 0, big), chunks)
    n_steps = page_table.shape[1] // PAGES_PER_STEP
    n_sel_pad = (n_steps + 1) * BLK
    msel_s = _sel_weights(chunks, n_sel_pad)
    per_step = PAGES_PER_STEP * BLK // SEL_BLOCK
    jj = jnp.arange(n_sel_pad)[None, :, None]
    ll = jnp.arange(BLK)[None, None, :]
    st = jnp.arange(n_steps + 1)[:, None, None]
    regroup = ((jj == st * per_step + ll) & (ll < per_step)).astype(BF16)
    o_cmp_s, sel_s = _nsa_sample_select(q_s, a_cmp_s, cmp_w, bias_cs, msel_s, regroup, past, t)
    expand = (jnp.arange(BLK)[:, None] == jnp.arange(PAGES_PER_STEP * BLK)[None, :] // SEL_BLOCK).astype(BF16)
    bias_sel = head_table(_bias_table(rel_bias, 1, 8, past, past, 0, 1, -1, 0, big), past)
    bias_new = head_table(_bias_table(rel_bias, 1, 8, BLK, 0, 0, 1, -1, 0, big), BLK)
    lw = win_buf.shape[1]
    bias_win = head_table(_bias_table(rel_bias, 1, 8, lw, lw, 0, 1, -1, 0, NSA_WINDOW - 1), lw)
    bias_win_new = head_table(_bias_table(rel_bias, 1, 8, BLK, 0, 0, 1, -1, 0, NSA_WINDOW - 1), BLK)
    g_s = gates_s[:, :3 * N_HEADS].reshape(n, t, 3, KV_HEADS, GROUP, 1)
    g_s = jnp.stack([head_rows(g_s[:, :, c]) for c in range(3)])
    g_s = jnp.broadcast_to(g_s, (3, n, KV_HEADS, rows, HEAD_DIM))
    o_s = _nsa_sample_attend(pool2, page_table, q_s, sel_s, expand, bias_sel, new_rows(2), new_rows(3), bias_new,
                             win_buf.reshape(n, lw, HEAD_TILE, HEAD_DIM), new_rows(4), new_rows(5), bias_win,
                             bias_win_new,
                             g_s, o_cmp_s)
    o_s = o_s.reshape(n, KV_HEADS, GROUP, 8, HEAD_DIM)[:, :, :, :t].transpose(0, 3, 1, 2, 4)
    hs, w_out = _project(o_s.reshape(n * t, NSA_Q).astype(BF16), w_out, res=hs, cast_w=True)
    new_win_s = jnp.concatenate([win_buf, kv_s[:, :, 4:]], axis=1)[:, t:]
    return hs, new_win_s, kv_s[:, :, :4], w_main, w_out


def _ffn_and_ple(hp, hs, b, s, n, t, i, norm_ffn, norm_ple, w_ffn_in, conv_w, conv_b, w_ffn_out, state_conv,
                 p_prompt, p_sample, w_ple_gate, w_ple_proj):
    hu_s, w_ffn_in = _project(hs, w_ffn_in, gain=norm_ffn, cast_w=True)
    hu_p = _project(hp, w_ffn_in, gain=norm_ffn)
    conv_p = hu_p.reshape(b, s, 2 * D_FF)[:, s - (CONV_W - 1):, :D_FF]
    conv_s = jnp.concatenate([state_conv, hu_s.reshape(n, t, 2 * D_FF)[:, :, :D_FF]], axis=1)[:, t:]
    hs, w_ffn_out = _ffn_out_sample(hu_s, state_conv, conv_w, conv_b, w_ffn_out, hs, t)
    hp = _ffn_out_prompt(hu_p, conv_w, conv_b, w_ffn_out, hp, s)
    hs, w_ple_gate, w_ple_proj = _ple_add(hs, norm_ple, w_ple_gate, p_sample.astype(BF16), w_ple_proj, cast_w=True)
    hp = _ple_add(hp, norm_ple, w_ple_gate, p_prompt.astype(BF16), w_ple_proj)
    return hp, hs, conv_p, conv_s


def kernel(x_prompt, x_sample, state_dil_w128, state_dil_w512, state_dil_w2048, state_nsa_win, state_conv,
           cache_nsa_kv, page_table, p_prompt, p_sample, rel_bias, norm_mix, norm_ffn, norm_ple, norm_final,
           w_in_a, w_out_a, w_in_b, w_out_b, cmp_pe, cmp_w1, cmp_b1, cmp_w2, cmp_b2, w_ffn_in, conv_w, conv_b,
           w_ffn_out, w_ple_gate, w_ple_proj):
    b, s, d = x_prompt.shape
    n, t, _ = x_sample.shape
    depth = norm_mix.shape[0]
    hp, hs = x_prompt.reshape(b * s, d), x_sample.reshape(n * t, d)
    dil_p, dil_s = [[] for _ in range(N_DIL)], [[] for _ in range(N_DIL)]
    win_p, win_s, kv_p, kv_s, conv_p, conv_s = [], [], [], [], [], []
    for i in range(depth):
        li = i // 2
        if i % 2 == 0:
            hp, hs, new_p, new_s = _layer_a(
                hp, hs, b, s, n, t, norm_mix[i], w_in_a[li], w_out_a[li], rel_bias,
                (state_dil_w128[li], state_dil_w512[li], state_dil_w2048[li]))
            for g in range(N_DIL):
                dil_p[g].append(new_p[g])
                dil_s[g].append(new_s[g])
        else:
            hp, hs, wp, ws, rp, rs = _layer_b(
                hp, hs, b, s, n, t, norm_mix[i], w_in_b[li], w_out_b[li], rel_bias,
                (cmp_pe[li], cmp_w1[li], cmp_b1[li], cmp_w2[li], cmp_b2[li]), cache_nsa_kv[li], page_table,
                state_nsa_win[li])
            win_p.append(wp)
            win_s.append(ws)
            kv_p.append(rp)
            kv_s.append(rs)
        hp, hs, cp, cs = _ffn_and_ple(
            hp, hs, b, s, n, t, i, norm_ffn[i], norm_ple[i], w_ffn_in[i], conv_w[i], conv_b[i],
            w_ffn_out[i], state_conv[i], p_prompt[i].reshape(b * s, -1), p_sample[i].reshape(n * t, -1),
            w_ple_gate[i], w_ple_proj[i])
        conv_p.append(cp)
        conv_s.append(cs)
    y_prompt = _rmsnorm(hp, norm_final, F32).reshape(b, s, d)
    y_sample = _rmsnorm(hs, norm_final, F32).reshape(n, t, d)
    return (y_prompt, y_sample,
            jnp.stack(dil_p[0]), jnp.stack(dil_s[0]), jnp.stack(dil_p[1]), jnp.stack(dil_s[1]),
            jnp.stack(dil_p[2]), jnp.stack(dil_s[2]),
            jnp.stack(win_p), jnp.stack(win_s), jnp.stack(conv_p), jnp.stack(conv_s),
            jnp.stack(kv_p), jnp.stack(kv_s))
```

```python
import functools

import jax
import jax.numpy as jnp
from jax import lax
from jax.experimental import pallas as pl
from jax.experimental.pallas import tpu as pltpu

F32 = jnp.float32
BF16 = jnp.bfloat16

D_MODEL = 2048
HEAD_DIM = 128
N_HEADS = 16
DIL_PAIRS = ((128, 1), (512, 4), (2048, 16))
N_DIL = 3
BLK = 128
KV_HEADS = 4
GROUP = 4
CMP_BLOCK = 32
CMP_STRIDE = 16
SEL_BLOCK = 64
SEL_TOPN = 16
NSA_WINDOW = 512
D_FF = 5632
CONV_W = 3
REL_BUCKETS = 32
EPS = 1e-6
NEG = -1e30
FORCED_SCORE = 1e4
SCALE = HEAD_DIM ** -0.5
QKV_A = N_DIL * 3 * N_HEADS * HEAD_DIM
NSA_Q = N_HEADS * HEAD_DIM
NSA_KV = KV_HEADS * HEAD_DIM

BUCKET_START = (1, 2, 3, 4, 5, 6, 7, 8, 9, 10, 11, 12, 13, 14, 15, 16, 22, 30, 40, 54, 73, 99,
                134, 182, 246, 332, 450, 609, 825, 1117, 1513)

VMEM_LIMIT_V7X = 56 * 1024 * 1024


def _params(sem, vmem=VMEM_LIMIT_V7X):
    return pltpu.CompilerParams(dimension_semantics=sem, vmem_limit_bytes=vmem)


def _pick(n, cands):
    for c in cands:
        if n % c == 0:
            return c
    return n


def _rmsnorm_kernel(x_ref, g_ref, o_ref):
    x = x_ref[...]
    ms = jnp.mean(x * x, axis=-1, keepdims=True)
    o_ref[...] = ((x * lax.rsqrt(ms + EPS)) * g_ref[...]).astype(o_ref.dtype)


def _rmsnorm(x, g, out_dtype):
    m, d = x.shape
    tm = _pick(m, (512, 256, 128, 32))
    return pl.pallas_call(
        _rmsnorm_kernel,
        out_shape=jax.ShapeDtypeStruct((m, d), out_dtype),
        grid=(m // tm,),
        in_specs=[pl.BlockSpec((tm, d), lambda i: (i, 0)), pl.BlockSpec((1, d), lambda i: (0, 0))],
        out_specs=pl.BlockSpec((tm, d), lambda i: (i, 0)),
        compiler_params=_params(("parallel",)),
        name="rmsnorm",
    )(x, g.reshape(1, d))


def _normed(x, g):
    ms = jnp.mean(x * x, axis=-1, keepdims=True)
    return ((x * lax.rsqrt(ms + EPS)) * g).astype(BF16)


def _project_kernel(*refs, norm, cast_w, residual, sigmoid, heads):
    refs = list(refs)
    x_ref = refs.pop(0)
    g_ref = refs.pop(0) if norm else None
    w_ref = refs.pop(0)
    r_ref = refs.pop(0) if residual else None
    o_ref = refs.pop(0)
    wb_ref = refs.pop(0) if cast_w else None
    if norm:
        a_scr = refs.pop(0)

        @pl.when(pl.program_id(1) == 0)
        def _():
            a_scr[...] = _normed(x_ref[...], g_ref[...])

        a = a_scr[...]
    else:
        a = x_ref[...]
    w = w_ref[...]
    if cast_w:
        w = w.astype(BF16)
        wb_ref[...] = w
    acc = jnp.dot(a, w, preferred_element_type=F32)
    if sigmoid:
        acc = jax.nn.sigmoid(acc)
    if residual:
        acc = r_ref[...] + acc
    if heads:
        for j in range(o_ref.shape[0]):
            o_ref[j] = acc[:, j * HEAD_DIM:(j + 1) * HEAD_DIM]
    else:
        o_ref[...] = acc


def _weight_spec(w, layer, k, tn, index):
    if w.ndim == 2:
        return pl.BlockSpec((k, tn), index)
    return pl.BlockSpec((None, k, tn), lambda *g: (layer,) + index(*g))


def _project(x, w, *, layer=None, n_cols=None, gain=None, res=None, cast_w=False, sigmoid=False, heads=False):
    m, k = x.shape
    n = n_cols or w.shape[-1]
    tm = _pick(m, (1024, 512, 256, 128))
    tn = _pick(n, (1024, 512, 256, 128))
    assert not cast_w or m == tm
    norm, residual = gain is not None, res is not None
    args, in_specs = [x], [pl.BlockSpec((tm, k), lambda i, j: (i, 0))]
    if norm:
        args.append(gain.reshape(1, k))
        in_specs.append(pl.BlockSpec((1, k), lambda i, j: (0, 0)))
    args.append(w)
    in_specs.append(_weight_spec(w, layer, k, tn, lambda i, j: (0, j)))
    if residual:
        args.append(res)
        in_specs.append(pl.BlockSpec((tm, tn), lambda i, j: (i, j)))
    if heads:
        out_shape = [jax.ShapeDtypeStruct((n // HEAD_DIM, m, HEAD_DIM), F32)]
        out_specs = [pl.BlockSpec((tn // HEAD_DIM, tm, HEAD_DIM), lambda i, j: (j, i, 0))]
    else:
        out_shape = [jax.ShapeDtypeStruct((m, n), F32)]
        out_specs = [pl.BlockSpec((tm, tn), lambda i, j: (i, j))]
    if cast_w:
        out_shape.append(jax.ShapeDtypeStruct((k, n), BF16))
        out_specs.append(pl.BlockSpec((k, tn), lambda i, j: (0, j)))
    outs = pl.pallas_call(
        functools.partial(_project_kernel, norm=norm, cast_w=cast_w, residual=residual, sigmoid=sigmoid,
                          heads=heads),
        out_shape=out_shape,
        grid=(m // tm, n // tn),
        in_specs=in_specs,
        out_specs=out_specs,
        scratch_shapes=[pltpu.VMEM((tm, k), BF16)] if norm else [],
        compiler_params=_params(("parallel", "arbitrary")),
        name="project",
    )(*args)
    return tuple(outs) if cast_w else outs[0]


def _conv_gelu_val(g, g1, g2, val, cw_ref, cb_ref):
    c = cb_ref[...] + g2 * cw_ref[0:1, :]
    c = c + g1 * cw_ref[1:2, :]
    c = c + g * cw_ref[2:3, :]
    return (jax.nn.gelu(c) * val).astype(BF16)


def _ffn_out_prompt_kernel(g_ref, halo_ref, v_ref, cw_ref, cb_ref, w_ref, r_ref, o_ref, acc_ref, *,
                           tiles_per_seq):
    i, k = pl.program_id(0), pl.program_id(1)

    @pl.when(k == 0)
    def _():
        acc_ref[...] = jnp.zeros_like(acc_ref)

    g = g_ref[...]
    row = lax.broadcasted_iota(jnp.int32, g.shape, 0)
    halo = jnp.where(i % tiles_per_seq == 0, 0.0, halo_ref[...])
    g1 = jnp.where(row == 0, halo[7:8, :], pltpu.roll(g, 1, 0))
    g2 = jnp.where(row == 0, halo[6:7, :], jnp.where(row == 1, halo[7:8, :], pltpu.roll(g, 2, 0)))
    u = _conv_gelu_val(g, g1, g2, v_ref[...], cw_ref, cb_ref)
    acc_ref[...] += jnp.dot(u, w_ref[...], preferred_element_type=F32)

    @pl.when(k == pl.num_programs(1) - 1)
    def _():
        o_ref[...] = r_ref[...] + acc_ref[...]


def _ffn_out_prompt(hu, conv_w, conv_b, w_out, res, seq):
    m = hu.shape[0]
    tm, tk = 512, D_FF // 4
    nk = D_FF // tk
    return pl.pallas_call(
        functools.partial(_ffn_out_prompt_kernel, tiles_per_seq=seq // tm),
        out_shape=jax.ShapeDtypeStruct((m, D_MODEL), F32),
        grid=(m // tm, nk),
        in_specs=[
            pl.BlockSpec((tm, tk), lambda i, k: (i, k)),
            pl.BlockSpec((8, tk), lambda i, k: (jnp.maximum(i * (tm // 8) - 1, 0), k)),
            pl.BlockSpec((tm, tk), lambda i, k: (i, k + nk)),
            pl.BlockSpec((CONV_W, tk), lambda i, k: (0, k)),
            pl.BlockSpec((1, tk), lambda i, k: (0, k)),
            pl.BlockSpec((tk, D_MODEL), lambda i, k: (k, 0)),
            pl.BlockSpec((tm, D_MODEL), lambda i, k: (i, 0)),
        ],
        out_specs=pl.BlockSpec((tm, D_MODEL), lambda i, k: (i, 0)),
        scratch_shapes=[pltpu.VMEM((tm, D_MODEL), F32)],
        compiler_params=_params(("parallel", "arbitrary")),
        name="ffn_out_prompt",
    )(hu, hu, hu, conv_w, conv_b.reshape(1, D_FF), w_out, res)


def _ffn_out_sample_kernel(g_ref, e1_ref, e2_ref, v_ref, cw_ref, cb_ref, w_ref, r_ref, o_ref, wb_ref, acc_ref, *,
                           t_len):
    k = pl.program_id(0)

    @pl.when(k == 0)
    def _():
        acc_ref[...] = jnp.zeros_like(acc_ref)

    g = g_ref[...]
    t = lax.broadcasted_iota(jnp.int32, g.shape, 0) % t_len
    g1 = jnp.where(t == 0, e1_ref[...], pltpu.roll(g, 1, 0))
    g2 = jnp.where(t < 2, e2_ref[...], pltpu.roll(g, 2, 0))
    u = _conv_gelu_val(g, g1, g2, v_ref[...], cw_ref, cb_ref)
    w = w_ref[...].astype(BF16)
    wb_ref[...] = w
    acc_ref[...] += jnp.dot(u, w, preferred_element_type=F32)

    @pl.when(k == pl.num_programs(0) - 1)
    def _():
        o_ref[...] = r_ref[...] + acc_ref[...]


def _ffn_out_sample(hu, conv_prev, conv_w, conv_b, w_out, layer, res, t_len):
    m = hu.shape[0]
    n = m // t_len
    tk = 512
    nk = D_FF // tk
    zeros = jnp.zeros((n, t_len - 1, D_FF), F32)
    e1 = jnp.concatenate([conv_prev[:, 1:2], zeros], axis=1).reshape(m, D_FF)
    e2 = jnp.concatenate([conv_prev, zeros[:, 1:]], axis=1).reshape(m, D_FF)
    return pl.pallas_call(
        functools.partial(_ffn_out_sample_kernel, t_len=t_len),
        out_shape=(jax.ShapeDtypeStruct((m, D_MODEL), F32), jax.ShapeDtypeStruct((D_FF, D_MODEL), BF16)),
        grid=(nk,),
        in_specs=[
            pl.BlockSpec((m, tk), lambda k: (0, k)),
            pl.BlockSpec((m, tk), lambda k: (0, k)),
            pl.BlockSpec((m, tk), lambda k: (0, k)),
            pl.BlockSpec((m, tk), lambda k: (0, k + nk)),
            pl.BlockSpec((CONV_W, tk), lambda k: (0, k)),
            pl.BlockSpec((1, tk), lambda k: (0, k)),
            pl.BlockSpec((None, tk, D_MODEL), lambda k: (layer, k, 0)),
            pl.BlockSpec((m, D_MODEL), lambda k: (0, 0)),
        ],
        out_specs=(pl.BlockSpec((m, D_MODEL), lambda k: (0, 0)), pl.BlockSpec((tk, D_MODEL), lambda k: (k, 0))),
        scratch_shapes=[pltpu.VMEM((m, D_MODEL), F32)],
        compiler_params=_params(("arbitrary",)),
        name="ffn_out_sample",
    )(hu, e1, e2, hu, conv_w, conv_b.reshape(1, D_FF), w_out, res)


def _ple_kernel(h_ref, g_ref, wg_ref, p_ref, wp_ref, o_ref, *rest, cast_w):
    a_scr = rest[-1]
    j = pl.program_id(1)
    tn = o_ref.shape[1]

    @pl.when(j == 0)
    def _():
        a_scr[...] = _normed(h_ref[...], g_ref[...])

    wg, wp = wg_ref[...], wp_ref[...]
    if cast_w:
        wg, wp = wg.astype(BF16), wp.astype(BF16)
        rest[0][...] = wg
        rest[1][...] = wp
    gate = jax.nn.sigmoid(jnp.dot(a_scr[...], wg, preferred_element_type=F32))
    proj = jnp.dot(p_ref[...], wp, preferred_element_type=F32)
    o_ref[...] = h_ref[:, pl.ds(pl.multiple_of(j * tn, tn), tn)] + gate * proj


def _ple_add(h, gain, w_gate, p, w_proj, layer=None, cast_w=False):
    m, d = h.shape
    kp = p.shape[1]
    tm = _pick(m, (1024, 512, 256, 128))
    tn = 1024
    assert not cast_w or m == tm
    out_shape = [jax.ShapeDtypeStruct((m, d), F32)]
    out_specs = [pl.BlockSpec((tm, tn), lambda i, j: (i, j))]
    if cast_w:
        out_shape += [jax.ShapeDtypeStruct((d, d), BF16), jax.ShapeDtypeStruct((kp, d), BF16)]
        out_specs += [pl.BlockSpec((d, tn), lambda i, j: (0, j)), pl.BlockSpec((kp, tn), lambda i, j: (0, j))]
    outs = pl.pallas_call(
        functools.partial(_ple_kernel, cast_w=cast_w),
        out_shape=out_shape,
        grid=(m // tm, d // tn),
        in_specs=[
            pl.BlockSpec((tm, d), lambda i, j: (i, 0)),
            pl.BlockSpec((1, d), lambda i, j: (0, 0)),
            _weight_spec(w_gate, layer, d, tn, lambda i, j: (0, j)),
            pl.BlockSpec((tm, kp), lambda i, j: (i, 0)),
            _weight_spec(w_proj, layer, kp, tn, lambda i, j: (0, j)),
        ],
        out_specs=out_specs,
        scratch_shapes=[pltpu.VMEM((tm, d), BF16)],
        compiler_params=_params(("parallel", "arbitrary")),
        name="ple_add",
    )(h, gain.reshape(1, d), w_gate, p, w_proj)
    return tuple(outs) if cast_w else outs[0]


def _bias_table_kernel(rb_ref, o_ref, *, a0, ag, ar, ac, lo, hi, mod):
    h, g = pl.program_id(0), pl.program_id(1)
    shape = o_ref.shape[2:]
    dist = (a0 + ag * g + ar * lax.broadcasted_iota(jnp.int32, shape, 0)
            + ac * lax.broadcasted_iota(jnp.int32, shape, 1))
    d = jnp.maximum(dist, 0)
    out = jnp.full(shape, rb_ref[0, h], F32)
    for k in range(1, REL_BUCKETS):
        out = jnp.where(d >= BUCKET_START[k - 1], rb_ref[k, h], out)
    ok = (dist >= lo) & (dist <= hi)
    if mod > 1:
        ok = ok & ((d & (mod - 1)) == 0)
    o_ref[0, 0] = jnp.where(ok, out, NEG)


def _bias_table(rel_bias, n_g, n_r, n_c, a0, ag, ar, ac, lo, hi, mod=1):
    assert mod & (mod - 1) == 0
    return pl.pallas_call(
        functools.partial(_bias_table_kernel, a0=a0, ag=ag, ar=ar, ac=ac, lo=lo, hi=hi, mod=mod),
        out_shape=jax.ShapeDtypeStruct((N_HEADS, n_g, n_r, n_c), F32),
        grid=(N_HEADS, n_g),
        in_specs=[pl.BlockSpec(memory_space=pltpu.SMEM)],
        out_specs=pl.BlockSpec((1, 1, n_r, n_c), lambda h, g: (h, g, 0, 0)),
        compiler_params=_params(("parallel", "parallel")),
        name="bias_table",
    )(rel_bias)


def _dil_prompt_kernel(q_ref, kc_ref, kp_ref, vc_ref, vp_ref, bias_ref, o_ref, lse_ref, *, dil, hps):
    first = pl.program_id(1) == 0
    hb = pl.program_id(2)
    col = lax.broadcasted_iota(jnp.int32, (BLK, 2 * BLK), 1)
    edge = jnp.where(first & (col < BLK), NEG, 0.0)
    lane = lax.broadcasted_iota(jnp.int32, (BLK, BLK), 1)

    @pl.when(hb == 0)
    def _():
        lse_ref[...] = jnp.zeros_like(lse_ref)

    def unit(u, carry):
        hh, r = u // dil, u % dil
        rows = pl.ds(r, BLK, stride=dil)
        q = q_ref[hh, rows, :].astype(BF16)
        k = jnp.concatenate([kp_ref[hh, rows, :], kc_ref[hh, rows, :]], axis=0).astype(BF16)
        v = jnp.concatenate([vp_ref[hh, rows, :], vc_ref[hh, rows, :]], axis=0).astype(BF16)
        h = hb * hps + hh
        s = lax.dot_general(q, k, (((1,), (1,)), ((), ())), preferred_element_type=F32)
        s = s * SCALE + bias_ref[h] + edge
        m = jnp.max(s, axis=-1, keepdims=True)
        p = jnp.exp(s - m)
        l = jnp.sum(p, axis=-1, keepdims=True)
        o_ref[hh, rows, :] = jnp.dot(p.astype(BF16), v, preferred_element_type=F32) / l
        lse_ref[rows, :] = jnp.where(lane == h, m + jnp.log(l), lse_ref[rows, :])
        return carry

    lax.fori_loop(0, hps * dil, unit, 0, unroll=4)


def _dil_prompt_group(qkv_hm, bias, grp, dil, b, s):
    span = BLK * dil
    nsp = s // span
    hps = N_HEADS // dil
    nhb = N_HEADS // hps

    def slab(part, prev):
        base = (grp * 3 + part) * N_HEADS // hps
        if prev:
            return lambda bi, sp, hb: (base + hb, bi * nsp + jnp.maximum(sp - 1, 0), 0)
        return lambda bi, sp, hb: (base + hb, bi * nsp + sp, 0)

    blk = (hps, span, HEAD_DIM)
    return pl.pallas_call(
        functools.partial(_dil_prompt_kernel, dil=dil, hps=hps),
        out_shape=(jax.ShapeDtypeStruct((N_HEADS, b * s, HEAD_DIM), F32),
                   jax.ShapeDtypeStruct((b * s, BLK), F32)),
        grid=(b, nsp, nhb),
        in_specs=[pl.BlockSpec(blk, slab(0, False)), pl.BlockSpec(blk, slab(1, False)),
                  pl.BlockSpec(blk, slab(1, True)), pl.BlockSpec(blk, slab(2, False)),
                  pl.BlockSpec(blk, slab(2, True)),
                  pl.BlockSpec((N_HEADS, BLK, 2 * BLK), lambda bi, sp, hb: (0, 0, 0))],
        out_specs=(pl.BlockSpec(blk, lambda bi, sp, hb: (hb, bi * nsp + sp, 0)),
                   pl.BlockSpec((span, BLK), lambda bi, sp, hb: (bi * nsp + sp, 0))),
        compiler_params=_params(("parallel", "parallel", "arbitrary")),
        name=f"dil_attn_prompt_g{grp}",
    )(qkv_hm, qkv_hm, qkv_hm, qkv_hm, qkv_hm, bias)


def _dil_combine_kernel(o0_ref, o1_ref, o2_ref, l0_ref, l1_ref, l2_ref, o_ref, *, head_major):
    l0, l1, l2 = l0_ref[...], l1_ref[...], l2_ref[...]
    mx = jnp.maximum(jnp.maximum(l0, l1), l2)
    e0, e1, e2 = jnp.exp(l0 - mx), jnp.exp(l1 - mx), jnp.exp(l2 - mx)
    den = e0 + e1 + e2
    w0, w1, w2 = e0 / den, e1 / den, e2 / den
    for h in range(N_HEADS):
        sl = slice(h * HEAD_DIM, (h + 1) * HEAD_DIM)
        g0, g1, g2 = ((r[h] for r in (o0_ref, o1_ref, o2_ref)) if head_major
                      else (r[:, sl] for r in (o0_ref, o1_ref, o2_ref)))
        o = (w0[:, h:h + 1] * g0 + w1[:, h:h + 1] * g1) + w2[:, h:h + 1] * g2
        o_ref[:, sl] = o.astype(o_ref.dtype)


def _dil_combine(outs, lses, head_major):
    m = lses[0].shape[0]
    wide = N_HEADS * HEAD_DIM
    tm = _pick(m, (256, 128, 32))
    ob = pl.BlockSpec((tm, wide), lambda i: (i, 0))
    ib = pl.BlockSpec((N_HEADS, tm, HEAD_DIM), lambda i: (0, i, 0)) if head_major else ob
    lb = pl.BlockSpec((tm, BLK), lambda i: (i, 0))
    return pl.pallas_call(
        functools.partial(_dil_combine_kernel, head_major=head_major),
        out_shape=jax.ShapeDtypeStruct((m, wide), BF16),
        grid=(m // tm,),
        in_specs=[ib, ib, ib, lb, lb, lb],
        out_specs=ob,
        compiler_params=_params(("parallel",)),
        name="dil_combine",
    )(*outs, *lses)


HEAD_TILE = 8


def _heads_first(x):
    return pltpu.einshape("mhd->hmd", x)


def _dil_sample_kernel(q_ref, k_ref, v_ref, kn_ref, vn_ref, bias_ref, biasn_ref, o_ref, lse_ref,
                       m_scr, l_scr, acc_scr, k_scr, v_scr):
    ht, c = pl.program_id(1), pl.program_id(2)
    k_scr[...] = _heads_first(k_ref[...])
    v_scr[...] = _heads_first(v_ref[...])

    @pl.when(c == 0)
    def _():
        m_scr[...] = jnp.full_like(m_scr, NEG)
        l_scr[...] = jnp.zeros_like(l_scr)
        acc_scr[...] = jnp.zeros_like(acc_scr)

    @pl.when((c == 0) & (ht == 0))
    def _():
        lse_ref[...] = jnp.zeros_like(lse_ref)

    def step(hh, k, v, bias):
        sl = pl.ds(pl.multiple_of(hh * HEAD_DIM, HEAD_DIM), HEAD_DIM)
        q = q_ref[0, :, sl].astype(BF16)
        s = lax.dot_general(q, k.astype(BF16), (((1,), (1,)), ((), ())), preferred_element_type=F32)
        s = s * SCALE + bias
        m_old = m_scr[hh]
        m_new = jnp.maximum(m_old, jnp.max(s, axis=-1, keepdims=True))
        alpha = jnp.exp(m_old - m_new)
        p = jnp.where(bias > 0.5 * NEG, jnp.exp(s - m_new), 0.0)
        l_scr[hh] = alpha * l_scr[hh] + jnp.sum(p, axis=-1, keepdims=True)
        acc_scr[hh] = alpha * acc_scr[hh] + jnp.dot(p.astype(BF16), v.astype(BF16), preferred_element_type=F32)
        m_scr[hh] = m_new

    def head(hh, carry):
        step(hh, k_scr[hh], v_scr[hh], bias_ref[hh, 0])
        return carry

    lax.fori_loop(0, HEAD_TILE, head, 0, unroll=True)

    @pl.when(c == pl.num_programs(2) - 1)
    def _():
        lane = lax.broadcasted_iota(jnp.int32, (8, BLK), 1)

        def fin(hh, carry):
            sl = pl.ds(pl.multiple_of(hh * HEAD_DIM, HEAD_DIM), HEAD_DIM)
            step(hh, kn_ref[0, :, sl], vn_ref[0, :, sl], biasn_ref[hh])
            l = jnp.maximum(l_scr[hh], 1e-30)
            o_ref[0, :, sl] = acc_scr[hh] / l
            lse_ref[0] = jnp.where(lane == ht * HEAD_TILE + hh, m_scr[hh] + jnp.log(l), lse_ref[0])
            return carry

        lax.fori_loop(0, HEAD_TILE, fin, 0, unroll=True)


def _dil_sample_group(q, kn, vn, buf, bias, bias_new, dil):
    n, lb = buf.shape[:2]
    n_cls = bias.shape[1]
    wide = N_HEADS * HEAD_DIM
    half = wide // 2
    rows = lb // dil
    tiles = 2 * N_HEADS // HEAD_TILE
    view = buf.reshape(n, rows, dil * tiles, HEAD_TILE, HEAD_DIM)
    nht = N_HEADS // HEAD_TILE
    return pl.pallas_call(
        _dil_sample_kernel,
        out_shape=(jax.ShapeDtypeStruct((n, 8, wide), F32), jax.ShapeDtypeStruct((n, 8, BLK), F32)),
        grid=(n, nht, n_cls),
        in_specs=[
            pl.BlockSpec((1, 8, half), lambda i, ht, c: (i, 0, ht)),
            pl.BlockSpec((None, rows, None, HEAD_TILE, HEAD_DIM), lambda i, ht, c: (i, 0, c * tiles + ht, 0, 0)),
            pl.BlockSpec((None, rows, None, HEAD_TILE, HEAD_DIM),
                         lambda i, ht, c: (i, 0, c * tiles + nht + ht, 0, 0)),
            pl.BlockSpec((1, BLK, half), lambda i, ht, c: (i, 0, ht)),
            pl.BlockSpec((1, BLK, half), lambda i, ht, c: (i, 0, ht)),
            pl.BlockSpec((HEAD_TILE, 1, 8, rows), lambda i, ht, c: (ht, c, 0, 0)),
            pl.BlockSpec((HEAD_TILE, 8, BLK), lambda i, ht, c: (ht, 0, 0)),
        ],
        out_specs=(pl.BlockSpec((1, 8, half), lambda i, ht, c: (i, 0, ht)),
                   pl.BlockSpec((1, 8, BLK), lambda i, ht, c: (i, 0, 0))),
        scratch_shapes=[pltpu.VMEM((HEAD_TILE, 8, 1), F32), pltpu.VMEM((HEAD_TILE, 8, 1), F32),
                        pltpu.VMEM((HEAD_TILE, 8, HEAD_DIM), F32),
                        pltpu.VMEM((HEAD_TILE, rows, HEAD_DIM), F32), pltpu.VMEM((HEAD_TILE, rows, HEAD_DIM), F32)],
        compiler_params=_params(("parallel", "arbitrary", "arbitrary")),
        name="dil_attn_sample",
    )(q, view, view, kn, vn, bias, bias_new)


def _vector_rows_kernel(x_ref, o_ref):
    x = x_ref[...]
    if x.ndim == 2:
        x = jnp.stack([x[:, j * HEAD_DIM:(j + 1) * HEAD_DIM] for j in range(HEAD_TILE)])
    o_ref[...] = pltpu.einshape("hmd->mhd", x)


def _vector_rows(x, tile0, n_tiles, b, s, keep):
    tr = min(keep, 512)
    r0 = (s - keep) // tr
    per_seq = s // tr
    if x.ndim == 3:
        in_spec = pl.BlockSpec((HEAD_TILE, tr, HEAD_DIM), lambda bi, r, c: (tile0 + c, bi * per_seq + r0 + r, 0))
    else:
        in_spec = pl.BlockSpec((tr, HEAD_TILE * HEAD_DIM), lambda bi, r, c: (bi * per_seq + r0 + r, tile0 + c))
    return pl.pallas_call(
        _vector_rows_kernel,
        out_shape=jax.ShapeDtypeStruct((b, keep, n_tiles * HEAD_TILE, HEAD_DIM), F32),
        grid=(b, keep // tr, n_tiles),
        in_specs=[in_spec],
        out_specs=pl.BlockSpec((None, tr, HEAD_TILE, HEAD_DIM), lambda bi, r, c: (bi, r, c, 0)),
        compiler_params=_params(("parallel", "parallel", "parallel")),
        name="vector_rows",
    )(x)


def _pad_rows(x, rows):
    return jnp.pad(x, ((0, 0), (0, rows - x.shape[1]), (0, 0)))


def _layer_a(hp, hs, b, s, n, t, norm_g, w_in, w_out, rel_bias, bufs):
    wide = N_HEADS * HEAD_DIM
    qkv_s, w_in = _project(hs, w_in, gain=norm_g, cast_w=True)
    qkv_s = qkv_s.reshape(n, t, QKV_A)
    qkv_p = _project(hp, w_in, gain=norm_g, heads=True)
    outs_p, lses_p, outs_s, lses_s, new_p, new_s = [], [], [], [], [], []
    for grp, (win, dil) in enumerate(DIL_PAIRS):
        base = grp * 3 * wide
        bias = _bias_table(rel_bias, 1, BLK, 2 * BLK, BLK * dil, 0, dil, -dil, 0, win).reshape(N_HEADS, BLK, 2 * BLK)
        o, lse = _dil_prompt_group(qkv_p, bias, grp, dil, b, s)
        outs_p.append(o)
        lses_p.append(lse)
        keep = min(win, s)
        kv = _vector_rows(qkv_p, (grp * 3 + 1) * N_HEADS // HEAD_TILE, 2 * N_HEADS // HEAD_TILE, b, s, keep)
        new_p.append(kv.reshape(b, keep, 2, N_HEADS, HEAD_DIM))
        buf = bufs[grp]
        lb = buf.shape[1]
        n_cls = min(dil, t)
        bias_buf = _bias_table(rel_bias, n_cls, 8, lb // dil, lb, -1, 1, -dil, 0, win, dil)
        bias_new = _bias_table(rel_bias, 1, 8, BLK, 0, 0, 1, -1, 0, win, dil).reshape(N_HEADS, 8, BLK)
        q = _pad_rows(qkv_s[:, :, base:base + wide], 8)
        kn = _pad_rows(qkv_s[:, :, base + wide:base + 2 * wide], BLK)
        vn = _pad_rows(qkv_s[:, :, base + 2 * wide:base + 3 * wide], BLK)
        o, lse = _dil_sample_group(q, kn, vn, buf, bias_buf, bias_new, dil)
        outs_s.append(o[:, :t].reshape(n * t, wide))
        lses_s.append(lse[:, :t].reshape(n * t, BLK))
        kv_new = qkv_s[:, :, base + wide:base + 3 * wide].reshape(n, t, 2, N_HEADS, HEAD_DIM)
        new_s.append(jnp.concatenate([buf, kv_new], axis=1)[:, t:])
    hs, w_out = _project(_dil_combine(outs_s, lses_s, False), w_out, res=hs, cast_w=True)
    hp = _project(_dil_combine(outs_p, lses_p, True), w_out, res=hp)
    return hp, hs, new_p, new_s


PAGES_PER_STEP = 16
CHUNKS_PER_PAGE = BLK // CMP_STRIDE
NT_DIMS = (((1,), (1,)), ((), ()))
TN_DIMS = (((0,), (0,)), ((), ()))


def _page_spec(p, half):
    return pl.BlockSpec((BLK, HEAD_TILE, HEAD_DIM),
                        lambda i, j, *rest: (rest[-1][i, j * PAGES_PER_STEP + p], half, 0))


def _cmp_proj_kernel(pt_ref, *refs):
    pages = refs[:PAGES_PER_STEP]
    w_ref, o_ref = refs[PAGES_PER_STEP:PAGES_PER_STEP + 2]
    ys = [pltpu.einshape("ctgd->tgcd", pg[...].reshape(CHUNKS_PER_PAGE, CMP_STRIDE, HEAD_TILE, HEAD_DIM))
          for pg in pages]
    for c in range(2):
        acc = jnp.zeros((KV_HEADS * BLK, 2 * HEAD_DIM), F32)
        for t in range(CMP_STRIDE):
            rows = [y[t, c * KV_HEADS + kh] for kh in range(KV_HEADS) for y in ys]
            lhs = jnp.concatenate(rows, axis=0).astype(BF16)
            acc = acc + jnp.dot(lhs, w_ref[c, t], preferred_element_type=F32)
        for kh in range(KV_HEADS):
            o_ref[0, c, kh] = acc[kh * BLK:(kh + 1) * BLK]


def _cmp_proj(pages, page_table, w1r):
    n, n_pages = page_table.shape
    chunks = n_pages * CHUNKS_PER_PAGE
    grid_spec = pltpu.PrefetchScalarGridSpec(
        num_scalar_prefetch=1,
        grid=(n, n_pages // PAGES_PER_STEP),
        in_specs=[_page_spec(p, 0) for p in range(PAGES_PER_STEP)]
        + [pl.BlockSpec((2, CMP_STRIDE, HEAD_DIM, 2 * HEAD_DIM), lambda i, j, pt: (0, 0, 0, 0))],
        out_specs=pl.BlockSpec((1, 2, KV_HEADS, BLK, 2 * HEAD_DIM), lambda i, j, pt: (i, 0, 0, j, 0)),
    )
    return pl.pallas_call(
        _cmp_proj_kernel,
        out_shape=jax.ShapeDtypeStruct((n, 2, KV_HEADS, chunks, 2 * HEAD_DIM), F32),
        grid_spec=grid_spec,
        compiler_params=_params(("parallel", "arbitrary")),
        name="nsa_cmp_proj",
    )(page_table, *([pages] * PAGES_PER_STEP), w1r)


def _finish_compress(a, pe_row, w1f, b1, w2, b2, n_blocks):
    rows = a.shape[0]
    cst = jnp.dot(pe_row, w1f, preferred_element_type=F32)[0:1]
    h = (b1 + cst) + a[:, :HEAD_DIM] + pltpu.roll(a[:, HEAD_DIM:], rows - 1, 0)
    x = jnp.dot(jax.nn.gelu(h).astype(BF16), w2, preferred_element_type=F32) + b2
    return jnp.where(lax.broadcasted_iota(jnp.int32, x.shape, 0) < n_blocks, x, 0.0)


def _split3(x):
    hi = x.astype(BF16)
    r = x - hi.astype(F32)
    mid = r.astype(BF16)
    return hi, mid, (r - mid.astype(F32)).astype(BF16)


def _top_n(score, n, axis):
    idx = lax.broadcasted_iota(jnp.int32, score.shape, axis).astype(F32)
    big = float(score.shape[axis])

    def body(_, carry):
        sc, sel = carry
        mx = jnp.max(sc, axis=axis, keepdims=True)
        first = jnp.min(jnp.where(sc == mx, idx, big), axis=axis, keepdims=True)
        hit = idx == first
        return jnp.where(hit, -jnp.inf, sc), jnp.where(hit, 1.0, sel)

    return lax.fori_loop(0, n, body, (score, jnp.zeros(score.shape, F32)))[1]


def _sel_scores(p_slc, blk, cur, n_blocks):
    forced = (blk == 0) | (blk == cur) | (blk == cur - 1)
    score = jnp.where(forced, FORCED_SCORE, jnp.where(blk <= cur, p_slc, -1.0))
    return jnp.where(blk < n_blocks, score, -2.0)


def _nsa_prompt_kernel(q_ref, a_ref, pe_ref, w1f_ref, b1_ref, w2_ref, b2_ref, ksel_ref, vsel_ref, kwin_ref,
                       vwin_ref, tbl_ref, tblw_ref, biasc_ref, gates_ref, msel_ref, o_ref, kc_scr, vc_scr, sel_scr,
                       vselt_ref, vwint_ref, *, n_cmp, n_sel):
    i = pl.program_id(2)

    @pl.when(i == 0)
    def _():
        for c, scr in ((0, kc_scr), (1, vc_scr)):
            scr[...] = _finish_compress(a_ref[0, c, 0], pe_ref[c], w1f_ref[c], b1_ref[c], w2_ref[c], b2_ref[c],
                                        n_cmp).astype(BF16)
        for src, dst in ((vsel_ref, vselt_ref), (vwin_ref, vwint_ref)):
            for c in range(src.shape[0] // BLK):
                dst[:, c * BLK:(c + 1) * BLK] = src[c * BLK:(c + 1) * BLK, :].T.astype(BF16)

    q = q_ref[...]
    qs = jnp.concatenate([q[:, g * HEAD_DIM:(g + 1) * HEAD_DIM] for g in range(GROUP)], axis=0).astype(BF16)
    key_i = lax.broadcasted_iota(jnp.int32, (BLK, BLK), 0)
    tok_i = lax.broadcasted_iota(jnp.int32, (BLK, BLK), 1)

    def lanes4(x):
        return jnp.concatenate([x] * GROUP, axis=1)

    mask_c = lanes4(i * BLK + tok_i - (key_i * CMP_STRIDE + (CMP_BLOCK - 1)) >= 0)
    s = lax.dot_general(kc_scr[...], qs, NT_DIMS, preferred_element_type=F32) * SCALE
    s = jnp.where(mask_c, s + jnp.concatenate([biasc_ref[g, 0] for g in range(GROUP)], axis=1), NEG)
    m = jnp.max(s, axis=0, keepdims=True)
    p = jnp.where(mask_c, jnp.exp(s - m), 0.0)
    pn = p / jnp.maximum(jnp.sum(p, axis=0, keepdims=True), 1e-30)
    o_cmp = lax.dot_general(vc_scr[...], pn.astype(BF16), TN_DIMS, preferred_element_type=F32)
    pc = ((pn[:, 0:BLK] + pn[:, BLK:2 * BLK]) + pn[:, 2 * BLK:3 * BLK]) + pn[:, 3 * BLK:4 * BLK]

    msel = msel_ref[...]
    p_slc = sum(jnp.dot(msel, part, preferred_element_type=F32) for part in _split3(pc))
    blk = lax.broadcasted_iota(jnp.int32, p_slc.shape, 0)
    cur = (i * BLK + lax.broadcasted_iota(jnp.int32, p_slc.shape, 1)) // SEL_BLOCK
    sel_scr[...] = (1.0 - _top_n(_sel_scores(p_slc, blk, cur, n_sel), SEL_TOPN, 0)) * NEG

    def attend(k_ref, vt_ref, bias_ref, first_blk, n_blk, extra, carry):
        m_run, l_run, acc = carry
        off = pl.multiple_of(first_blk * BLK, BLK)
        k = k_ref[pl.ds(off, n_blk * BLK), :].astype(BF16)
        bias = []
        for j in range(n_blk):
            idx = jnp.maximum(i - first_blk - j, -1) + 1
            bias.append(jnp.concatenate([bias_ref[g, idx] for g in range(GROUP)], axis=1))
        s = lax.dot_general(k, qs, NT_DIMS, preferred_element_type=F32) * SCALE + jnp.concatenate(bias, axis=0)
        if extra is not None:
            s = s + extra
        m_new = jnp.maximum(m_run, jnp.max(s, axis=0, keepdims=True))
        alpha = jnp.exp(m_run - m_new)
        p = jnp.exp(s - m_new)
        l_new = alpha * l_run + jnp.sum(p, axis=0, keepdims=True)
        vt = vt_ref[:, pl.ds(off, n_blk * BLK)]
        return m_new, l_new, alpha * acc + jnp.dot(vt, p.astype(BF16), preferred_element_type=F32)

    init = (jnp.full((1, GROUP * BLK), NEG, F32), jnp.zeros((1, GROUP * BLK), F32),
            jnp.zeros((HEAD_DIM, GROUP * BLK), F32))

    sel_span = 4
    per_blk = BLK // SEL_BLOCK

    def sel_step(c, carry):
        rows = sel_scr[pl.ds(pl.multiple_of(c * sel_span * per_blk, 8), sel_span * per_blk), :]
        unpicked = jnp.concatenate([jnp.broadcast_to(rows[u:u + 1], (SEL_BLOCK, BLK))
                                    for u in range(sel_span * per_blk)], axis=0)
        return attend(ksel_ref, vselt_ref, tbl_ref, c * sel_span, sel_span, lanes4(unpicked), carry)

    _, l_sel, acc_sel = lax.fori_loop(0, i // sel_span + 1, sel_step, init)
    o_sel = acc_sel / jnp.maximum(l_sel, 1e-30)

    n_win = (NSA_WINDOW - 1 + BLK - 1) // BLK + 1
    _, l_win, acc_win = attend(kwin_ref, vwint_ref, tblw_ref, jnp.maximum(i - (n_win - 1), 0), n_win, None, init)
    o_win = acc_win / jnp.maximum(l_win, 1e-30)

    gt = gates_ref[0]

    def gate(branch):
        return jnp.concatenate([gt[branch * GROUP + g:branch * GROUP + g + 1, :] for g in range(GROUP)], axis=1)

    o = (gate(0) * o_cmp + gate(1) * o_sel) + gate(2) * o_win
    for g in range(GROUP):
        o_ref[0, g] = o[:, g * BLK:(g + 1) * BLK]


def _nsa_prompt(proj, a_cmp, gates_t, cmp_w, tbl, tbl_win, bias_c, msel, b, s):
    pe, w1f, b1, w2, b2 = cmp_w
    nq = s // BLK
    kcol = NSA_Q // HEAD_DIM
    const = lambda shape: pl.BlockSpec(shape, lambda bi, kh, i: (0,) * len(shape))
    return pl.pallas_call(
        functools.partial(_nsa_prompt_kernel, n_cmp=s // CMP_STRIDE - 1, n_sel=s // SEL_BLOCK),
        out_shape=jax.ShapeDtypeStruct((b, N_HEADS, HEAD_DIM, s), F32),
        grid=(b, KV_HEADS, nq),
        in_specs=[
            pl.BlockSpec((BLK, GROUP * HEAD_DIM), lambda bi, kh, i: (bi * nq + i, kh)),
            pl.BlockSpec((1, 2, 1, s // CMP_STRIDE, 2 * HEAD_DIM), lambda bi, kh, i: (bi, 0, kh, 0, 0)),
            const(pe.shape), const(w1f.shape), const(b1.shape), const(w2.shape), const(b2.shape),
            pl.BlockSpec((s, HEAD_DIM), lambda bi, kh, i: (bi, kcol + 2 * KV_HEADS + kh)),
            pl.BlockSpec((s, HEAD_DIM), lambda bi, kh, i: (bi, kcol + 3 * KV_HEADS + kh)),
            pl.BlockSpec((s, HEAD_DIM), lambda bi, kh, i: (bi, kcol + 4 * KV_HEADS + kh)),
            pl.BlockSpec((s, HEAD_DIM), lambda bi, kh, i: (bi, kcol + 5 * KV_HEADS + kh)),
            pl.BlockSpec((GROUP,) + tbl.shape[1:], lambda bi, kh, i: (kh, 0, 0, 0)),
            pl.BlockSpec((GROUP,) + tbl_win.shape[1:], lambda bi, kh, i: (kh, 0, 0, 0)),
            pl.BlockSpec((GROUP, 1, s // CMP_STRIDE, BLK), lambda bi, kh, i: (kh, 0, 0, i)),
            pl.BlockSpec((1, 16, BLK), lambda bi, kh, i: (kh, 0, bi * nq + i)),
            const(msel.shape),
        ],
        out_specs=pl.BlockSpec((1, GROUP, HEAD_DIM, BLK), lambda bi, kh, i: (bi, kh, 0, i)),
        scratch_shapes=[pltpu.VMEM((s // CMP_STRIDE, HEAD_DIM), BF16), pltpu.VMEM((s // CMP_STRIDE, HEAD_DIM), BF16),
                        pltpu.VMEM((s // SEL_BLOCK, BLK), F32),
                        pltpu.VMEM((HEAD_DIM, s), BF16), pltpu.VMEM((HEAD_DIM, s), BF16)],
        compiler_params=_params(("parallel", "parallel", "arbitrary")),
        name="nsa_prompt",
    )(proj, a_cmp, pe, w1f, b1, w2, b2, proj, proj, proj, proj, tbl, tbl_win, bias_c, gates_t, msel)


def _masked_softmax_rows(s, mask):
    s = jnp.where(mask, s, NEG)
    m = jnp.max(s, axis=-1, keepdims=True)
    p = jnp.where(mask, jnp.exp(s - m), 0.0)
    return p / jnp.maximum(jnp.sum(p, axis=-1, keepdims=True), 1e-30)


def _nsa_sample_select_kernel(q_ref, a_ref, pe_ref, w1f_ref, b1_ref, w2_ref, b2_ref, biasc_ref, msel_ref,
                              regroup_ref, ocmp_ref, sel_ref, *, n_cmp, n_sel, past):
    pcs = []
    for kh in range(KV_HEADS):
        kc, vc = (_finish_compress(a_ref[0, c, kh], pe_ref[c], w1f_ref[c], b1_ref[c], w2_ref[c], b2_ref[c],
                                   n_cmp).astype(BF16) for c in range(2))
        bias = biasc_ref[kh]
        s = lax.dot_general(q_ref[0, kh].astype(BF16), kc, NT_DIMS, preferred_element_type=F32) * SCALE + bias
        pn = _masked_softmax_rows(s, bias > 0.5 * NEG)
        ocmp_ref[0, kh] = jnp.dot(pn.astype(BF16), vc, preferred_element_type=F32)
        pcs.append(((pn[0:8] + pn[8:16]) + pn[16:24]) + pn[24:32])
    pc = jnp.concatenate(pcs, axis=0)
    msel = msel_ref[...]
    p_slc = sum(jnp.dot(part, msel, preferred_element_type=F32) for part in _split3(pc))
    blk = lax.broadcasted_iota(jnp.int32, p_slc.shape, 1)
    cur = (past + lax.broadcasted_iota(jnp.int32, p_slc.shape, 0) % 8) // SEL_BLOCK
    sel = _top_n(_sel_scores(p_slc, blk, cur, n_sel), SEL_TOPN, 1).astype(BF16)
    for j in range(regroup_ref.shape[0]):
        part = jnp.dot(sel, regroup_ref[j], preferred_element_type=F32)
        for kh in range(KV_HEADS):
            sel_ref[0, kh, j] = part[kh * 8:(kh + 1) * 8]


def _nsa_sample_select(q, a_cmp, cmp_w, bias_c, msel, regroup, past, t_len):
    pe, w1f, b1, w2, b2 = cmp_w
    n = q.shape[0]
    chunks = a_cmp.shape[3]
    n_steps = regroup.shape[0]
    const = lambda shape: pl.BlockSpec(shape, lambda i: (0,) * len(shape))
    return pl.pallas_call(
        functools.partial(_nsa_sample_select_kernel, n_cmp=chunks - 1, n_sel=(past + t_len + SEL_BLOCK - 1) // SEL_BLOCK,
                          past=past),
        out_shape=(jax.ShapeDtypeStruct((n, KV_HEADS, GROUP * 8, HEAD_DIM), F32),
                   jax.ShapeDtypeStruct((n, KV_HEADS, n_steps, 8, BLK), F32)),
        grid=(n,),
        in_specs=[
            pl.BlockSpec((1, KV_HEADS, GROUP * 8, HEAD_DIM), lambda i: (i, 0, 0, 0)),
            pl.BlockSpec((1, 2, KV_HEADS, chunks, 2 * HEAD_DIM), lambda i: (i, 0, 0, 0, 0)),
            const(pe.shape), const(w1f.shape), const(b1.shape), const(w2.shape), const(b2.shape),
            const(bias_c.shape), const(msel.shape), const(regroup.shape),
        ],
        out_specs=(pl.BlockSpec((1, KV_HEADS, GROUP * 8, HEAD_DIM), lambda i: (i, 0, 0, 0)),
                   pl.BlockSpec((1, KV_HEADS, n_steps, 8, BLK), lambda i: (i, 0, 0, 0, 0))),
        compiler_params=_params(("parallel",)),
        name="nsa_sample_select",
    )(q, a_cmp, pe, w1f, b1, w2, b2, bias_c, msel, regroup)


def _nsa_sample_attend_kernel(pt_ref, *refs):
    pages = refs[:PAGES_PER_STEP]
    (q_ref, sel_ref, expand_ref, bias_ref, kn_ref, vn_ref, biasn_ref, win_ref, kwn_ref, vwn_ref, biasw_ref,
     biaswn_ref, gates_ref, ocmp_ref, o_ref, m_scr, l_scr, acc_scr, kv_scr, win_scr) = refs[PAGES_PER_STEP:]
    j, kh = pl.program_id(1), pl.program_id(2)
    n_steps = pl.num_programs(1)
    qs = q_ref[0, 0].astype(BF16)

    @pl.when(j == 0)
    def _():
        m_scr[kh] = jnp.full((GROUP * 8, 1), NEG, F32)
        l_scr[kh] = jnp.zeros((GROUP * 8, 1), F32)
        acc_scr[kh] = jnp.zeros((GROUP * 8, HEAD_DIM), F32)

    def picked(step):
        sel = sel_ref[0, 0, step].astype(BF16)
        return jnp.concatenate([sel] * GROUP, axis=0)

    def update(k, v, bias, mask):
        s = lax.dot_general(qs, k, NT_DIMS, preferred_element_type=F32) * SCALE + bias
        s = jnp.where(mask, s, NEG)
        m_old = m_scr[kh]
        m_new = jnp.maximum(m_old, jnp.max(s, axis=-1, keepdims=True))
        alpha = jnp.exp(m_old - m_new)
        p = jnp.where(mask, jnp.exp(s - m_new), 0.0)
        l_scr[kh] = alpha * l_scr[kh] + jnp.sum(p, axis=-1, keepdims=True)
        acc_scr[kh] = alpha * acc_scr[kh] + jnp.dot(p.astype(BF16), v, preferred_element_type=F32)
        m_scr[kh] = m_new

    @pl.when(kh == 0)
    def _():
        for p, pg in enumerate(pages):
            kv_scr[:, p * BLK:(p + 1) * BLK, :] = _heads_first(pg[...])

    in_sel = jnp.dot(picked(j), expand_ref[...], preferred_element_type=F32) > 0.5
    update(kv_scr[kh].astype(BF16), kv_scr[KV_HEADS + kh].astype(BF16), bias_ref[0], in_sel)

    @pl.when(j == n_steps - 1)
    def _():
        biasn = biasn_ref[0]
        new_sel = picked(n_steps)[:, 0:1] > 0.5
        update(kn_ref[0, 0].astype(BF16), vn_ref[0, 0].astype(BF16), biasn, (biasn > 0.5 * NEG) & new_sel)
        o_sel = acc_scr[kh] / jnp.maximum(l_scr[kh], 1e-30)
        @pl.when(kh == 0)
        def _():
            win_scr[...] = _heads_first(win_ref[...])

        biasw = jnp.concatenate([biasw_ref[0], biaswn_ref[0]], axis=1)
        kw = jnp.concatenate([win_scr[kh], kwn_ref[0, 0]], axis=0).astype(BF16)
        vw = jnp.concatenate([win_scr[KV_HEADS + kh], vwn_ref[0, 0]], axis=0).astype(BF16)
        sw = lax.dot_general(qs, kw, NT_DIMS, preferred_element_type=F32) * SCALE + biasw
        pw = _masked_softmax_rows(sw, biasw > 0.5 * NEG)
        o_win = jnp.dot(pw.astype(BF16), vw, preferred_element_type=F32)
        o_ref[0, kh] = (gates_ref[0, 0, 0] * ocmp_ref[0, 0] + gates_ref[1, 0, 0] * o_sel) + gates_ref[2, 0, 0] * o_win


def _nsa_sample_attend(pool, page_table, q, sel, expand, bias_sel, k_new, v_new, bias_new, win_buf, kw_new, vw_new,
                       bias_win, bias_win_new, gates, o_cmp):
    n, n_pages = page_table.shape
    n_steps = n_pages // PAGES_PER_STEP
    keys = PAGES_PER_STEP * BLK
    rows = GROUP * 8
    lw = win_buf.shape[1]
    per = lambda shape: pl.BlockSpec((1, 1) + shape, lambda i, j, kh, pt: (i, kh) + (0,) * len(shape))
    by_head = lambda shape: pl.BlockSpec((1,) + shape, lambda i, j, kh, pt: (kh,) + (0,) * len(shape))
    grid_spec = pltpu.PrefetchScalarGridSpec(
        num_scalar_prefetch=1,
        grid=(n, n_steps, KV_HEADS),
        in_specs=[_page_spec(p, 1) for p in range(PAGES_PER_STEP)] + [
            per((rows, HEAD_DIM)),
            pl.BlockSpec((1, 1, n_steps + 1, 8, BLK), lambda i, j, kh, pt: (i, kh, 0, 0, 0)),
            pl.BlockSpec(expand.shape, lambda i, j, kh, pt: (0, 0)),
            pl.BlockSpec((1, rows, keys), lambda i, j, kh, pt: (kh, 0, j)),
            per((BLK, HEAD_DIM)), per((BLK, HEAD_DIM)), by_head((rows, BLK)),
            pl.BlockSpec((None, lw, HEAD_TILE, HEAD_DIM), lambda i, j, kh, pt: (i, 0, 0, 0)),
            per((BLK, HEAD_DIM)), per((BLK, HEAD_DIM)), by_head((rows, lw)), by_head((rows, BLK)),
            pl.BlockSpec((3, 1, 1, rows, HEAD_DIM), lambda i, j, kh, pt: (0, i, kh, 0, 0)),
            per((rows, HEAD_DIM)),
        ],
        out_specs=pl.BlockSpec((1, KV_HEADS, rows, HEAD_DIM), lambda i, j, kh, pt: (i, 0, 0, 0)),
        scratch_shapes=[pltpu.VMEM((KV_HEADS, rows, 1), F32), pltpu.VMEM((KV_HEADS, rows, 1), F32),
                        pltpu.VMEM((KV_HEADS, rows, HEAD_DIM), F32),
                        pltpu.VMEM((HEAD_TILE, keys, HEAD_DIM), F32), pltpu.VMEM((HEAD_TILE, lw, HEAD_DIM), F32)],
    )
    return pl.pallas_call(
        _nsa_sample_attend_kernel,
        out_shape=jax.ShapeDtypeStruct((n, KV_HEADS, rows, HEAD_DIM), F32),
        grid_spec=grid_spec,
        compiler_params=_params(("parallel", "arbitrary", "arbitrary")),
        name="nsa_sample_attend",
    )(page_table, *([pool] * PAGES_PER_STEP), q, sel, expand, bias_sel, k_new, v_new, bias_new, win_buf, kw_new,
      vw_new, bias_win, bias_win_new, gates, o_cmp)


def _sel_weights(n_cmp_rows, n_sel_cols):
    ratio = SEL_BLOCK // CMP_STRIDE
    span = CMP_BLOCK // CMP_STRIDE
    c = jnp.arange(n_cmp_rows)[:, None]
    j = jnp.arange(n_sel_cols)[None, :]
    o = c - ratio * j + (span - 1)
    cnt = jnp.minimum(o, span - 1) - jnp.maximum(o - (ratio - 1), 0) + 1
    return jnp.where((o >= 0) & (o <= ratio + span - 2), cnt, 0).astype(BF16)


def _layer_b(hp, hs, b, s, n, t, norm_g, w_in, w_out, rel_bias, cmp, pool, page_table, win_buf):
    cmp_pe, cmp_w1, cmp_b1, cmp_w2, cmp_b2 = cmp
    n_kvcol = 6 * NSA_KV
    w_main = w_in
    w_gate = jnp.pad(w_in[:, NSA_Q + n_kvcol:], ((0, 0), (0, BLK - 3 * N_HEADS))).astype(BF16)
    w1 = cmp_w1.reshape(2, 2, CMP_STRIDE, HEAD_DIM, HEAD_DIM)
    w1r = jnp.concatenate([w1[:, 0], w1[:, 1]], axis=-1).astype(BF16)
    pe_row = jnp.pad(cmp_pe.reshape(2, 1, CMP_BLOCK * HEAD_DIM), ((0, 0), (0, 7), (0, 0))).astype(BF16)
    cmp_w = (pe_row, cmp_w1.reshape(2, CMP_BLOCK * HEAD_DIM, HEAD_DIM).astype(BF16), cmp_b1.reshape(2, 1, HEAD_DIM),
             cmp_w2.astype(BF16), cmp_b2.reshape(2, 1, HEAD_DIM))
    weights = (w_main, w_gate, w_out, w1r, cmp_w)
    hs, new_win_s, new_kv_s, w_main, w_out = _nsa_sample_path(hs, n, t, norm_g, weights, rel_bias, pool, page_table,
                                                              win_buf)
    weights = (w_main, w_gate, w_out, w1r, cmp_w)
    hp, new_win_p, new_kv_p = _nsa_prompt_path(hp, b, s, norm_g, weights, rel_bias)
    return hp, hs, new_win_p, new_win_s, new_kv_p, new_kv_s


def _nsa_prompt_path(hp, b, s, norm_g, weights, rel_bias):
    w_main, w_gate, w_out, w1r, cmp_w = weights
    proj_p = _project(hp, w_main, gain=norm_g)
    gates_p = _project(hp, w_gate, gain=norm_g, sigmoid=True)
    col_tile = HEAD_TILE * HEAD_DIM
    new_kv = _vector_rows(proj_p, NSA_Q // col_tile, 4 * NSA_KV // col_tile, b, s, s)
    keep = min(NSA_WINDOW, s)
    new_win = _vector_rows(proj_p, (NSA_Q + 4 * NSA_KV) // col_tile, 2 * NSA_KV // col_tile, b, s, keep)
    table_p = jnp.arange(b * s // BLK, dtype=jnp.int32).reshape(b, s // BLK)
    a_cmp_p = _cmp_proj(new_kv.reshape(b * s, 2 * HEAD_TILE, HEAD_DIM), table_p, w1r)
    nq = s // BLK
    n_win = (NSA_WINDOW - 1 + BLK - 1) // BLK + 1
    tbl = _bias_table(rel_bias, nq + 1, BLK, BLK, -BLK, BLK, -1, 1, 0, 1 << 30)
    tbl_win = _bias_table(rel_bias, n_win + 1, BLK, BLK, -BLK, BLK, -1, 1, 0, NSA_WINDOW - 1)
    bias_c = _bias_table(rel_bias, 1, s // CMP_STRIDE, s, -(CMP_BLOCK - 1), 0, -CMP_STRIDE, 1, 0, 1 << 30)
    gates_t = gates_p[:, :3 * N_HEADS].reshape(b * s, 3, KV_HEADS, GROUP).transpose(2, 1, 3, 0)
    gates_t = jnp.pad(gates_t.reshape(KV_HEADS, 3 * GROUP, b * s), ((0, 0), (0, 16 - 3 * GROUP), (0, 0)))
    msel_p = _sel_weights(s // CMP_STRIDE, s // SEL_BLOCK).T
    o_t = _nsa_prompt(proj_p, a_cmp_p, gates_t, cmp_w, tbl, tbl_win, bias_c, msel_p, b, s)
    o_p = o_t.transpose(0, 3, 1, 2).reshape(b * s, NSA_Q).astype(BF16)
    hp = _project(o_p, w_out, res=hp)
    return (hp, new_win.reshape(b, keep, 2, KV_HEADS, HEAD_DIM), new_kv.reshape(b, s, 4, KV_HEADS, HEAD_DIM))


def _nsa_sample_path(hs, n, t, norm_g, weights, rel_bias, pool, page_table, win_buf):
    w_main, w_gate, w_out, w1r, cmp_w = weights
    past = page_table.shape[1] * BLK
    proj_s, w_main = _project(hs, w_main, n_cols=NSA_Q + 6 * NSA_KV, gain=norm_g, cast_w=True)
    gates_s = _project(hs, w_gate, gain=norm_g, sigmoid=True)
    kv_s = proj_s[:, NSA_Q:].reshape(n, t, 6, KV_HEADS, HEAD_DIM)
    rows = GROUP * 8

    def head_rows(x):
        x = jnp.pad(x.transpose(0, 2, 3, 1, 4), ((0, 0), (0, 0), (0, 0), (0, 8 - t), (0, 0)))
        return x.reshape(n, KV_HEADS, rows, x.shape[-1])

    def new_rows(c):
        return jnp.pad(kv_s[:, :, c].transpose(0, 2, 1, 3), ((0, 0), (0, 0), (0, BLK - t), (0, 0)))

    def head_table(x, cols):
        return x.reshape(KV_HEADS, rows, cols)

    q_s = head_rows(proj_s[:, :NSA_Q].reshape(n, t, KV_HEADS, GROUP, HEAD_DIM))
    pool2 = pool.reshape(pool.shape[0] * BLK, 2 * HEAD_TILE, HEAD_DIM)
    a_cmp_s = _cmp_proj(pool2, page_table, w1r)
    chunks = past // CMP_STRIDE
    big = 1 << 30
    bias_cs = head_table(_bias_table(rel_bias, 1, 8, chunks, past - (CMP_BLOCK - 1), 0, 1, -CMP_STRIDE, 0, big), chunks)
    n_steps = page_table.shape[1] // PAGES_PER_STEP
    n_sel_pad = (n_steps + 1) * BLK
    msel_s = _sel_weights(chunks, n_sel_pad)
    per_step = PAGES_PER_STEP * BLK // SEL_BLOCK
    jj = jnp.arange(n_sel_pad)[None, :, None]
    ll = jnp.arange(BLK)[None, None, :]
    st = jnp.arange(n_steps + 1)[:, None, None]
    regroup = ((jj == st * per_step + ll) & (ll < per_step)).astype(BF16)
    o_cmp_s, sel_s = _nsa_sample_select(q_s, a_cmp_s, cmp_w, bias_cs, msel_s, regroup, past, t)
    expand = (jnp.arange(BLK)[:, None] == jnp.arange(PAGES_PER_STEP * BLK)[None, :] // SEL_BLOCK).astype(BF16)
    bias_sel = head_table(_bias_table(rel_bias, 1, 8, past, past, 0, 1, -1, 0, big), past)
    bias_new = head_table(_bias_table(rel_bias, 1, 8, BLK, 0, 0, 1, -1, 0, big), BLK)
    lw = win_buf.shape[1]
    bias_win = head_table(_bias_table(rel_bias, 1, 8, lw, lw, 0, 1, -1, 0, NSA_WINDOW - 1), lw)
    bias_win_new = head_table(_bias_table(rel_bias, 1, 8, BLK, 0, 0, 1, -1, 0, NSA_WINDOW - 1), BLK)
    g_s = gates_s[:, :3 * N_HEADS].reshape(n, t, 3, KV_HEADS, GROUP, 1)
    g_s = jnp.stack([head_rows(g_s[:, :, c]) for c in range(3)])
    g_s = jnp.broadcast_to(g_s, (3, n, KV_HEADS, rows, HEAD_DIM))
    o_s = _nsa_sample_attend(pool2, page_table, q_s, sel_s, expand, bias_sel, new_rows(2), new_rows(3), bias_new,
                             win_buf.reshape(n, lw, HEAD_TILE, HEAD_DIM), new_rows(4), new_rows(5), bias_win,
                             bias_win_new,
                             g_s, o_cmp_s)
    o_s = o_s.reshape(n, KV_HEADS, GROUP, 8, HEAD_DIM)[:, :, :, :t].transpose(0, 3, 1, 2, 4)
    hs, w_out = _project(o_s.reshape(n * t, NSA_Q).astype(BF16), w_out, res=hs, cast_w=True)
    new_win_s = jnp.concatenate([win_buf, kv_s[:, :, 4:]], axis=1)[:, t:]
    return hs, new_win_s, kv_s[:, :, :4], w_main, w_out


def _ffn_and_ple(hp, hs, b, s, n, t, i, norm_ffn, norm_ple, w_ffn_in, conv_w, conv_b, w_ffn_out, state_conv,
                 p_prompt, p_sample, w_ple_gate, w_ple_proj):
    hu_s, w_ffn_in = _project(hs, w_ffn_in, layer=i, gain=norm_ffn, cast_w=True)
    hu_p = _project(hp, w_ffn_in, gain=norm_ffn)
    conv_p = hu_p.reshape(b, s, 2 * D_FF)[:, s - (CONV_W - 1):, :D_FF]
    conv_s = jnp.concatenate([state_conv, hu_s.reshape(n, t, 2 * D_FF)[:, :, :D_FF]], axis=1)[:, t:]
    hs, w_ffn_out = _ffn_out_sample(hu_s, state_conv, conv_w, conv_b, w_ffn_out, i, hs, t)
    hp = _ffn_out_prompt(hu_p, conv_w, conv_b, w_ffn_out, hp, s)
    hs, w_ple_gate, w_ple_proj = _ple_add(hs, norm_ple, w_ple_gate, p_sample.astype(BF16), w_ple_proj, layer=i,
                                          cast_w=True)
    hp = _ple_add(hp, norm_ple, w_ple_gate, p_prompt.astype(BF16), w_ple_proj)
    return hp, hs, conv_p, conv_s


def kernel(x_prompt, x_sample, state_dil_w128, state_dil_w512, state_dil_w2048, state_nsa_win, state_conv,
           cache_nsa_kv, page_table, p_prompt, p_sample, rel_bias, norm_mix, norm_ffn, norm_ple, norm_final,
           w_in_a, w_out_a, w_in_b, w_out_b, cmp_pe, cmp_w1, cmp_b1, cmp_w2, cmp_b2, w_ffn_in, conv_w, conv_b,
           w_ffn_out, w_ple_gate, w_ple_proj):
    b, s, d = x_prompt.shape
    n, t, _ = x_sample.shape
    depth = norm_mix.shape[0]
    hp, hs = x_prompt.reshape(b * s, d), x_sample.reshape(n * t, d)
    dil_p, dil_s = [[] for _ in range(N_DIL)], [[] for _ in range(N_DIL)]
    win_p, win_s, kv_p, kv_s, conv_p, conv_s = [], [], [], [], [], []
    for i in range(depth):
        li = i // 2
        if i % 2 == 0:
            hp, hs, new_p, new_s = _layer_a(
                hp, hs, b, s, n, t, norm_mix[i], w_in_a[li], w_out_a[li], rel_bias,
                (state_dil_w128[li], state_dil_w512[li], state_dil_w2048[li]))
            for g in range(N_DIL):
                dil_p[g].append(new_p[g])
                dil_s[g].append(new_s[g])
        else:
            hp, hs, wp, ws, rp, rs = _layer_b(
                hp, hs, b, s, n, t, norm_mix[i], w_in_b[li], w_out_b[li], rel_bias,
                (cmp_pe[li], cmp_w1[li], cmp_b1[li], cmp_w2[li], cmp_b2[li]), cache_nsa_kv[li], page_table,
                state_nsa_win[li])
            win_p.append(wp)
            win_s.append(ws)
            kv_p.append(rp)
            kv_s.append(rs)
        hp, hs, cp, cs = _ffn_and_ple(
            hp, hs, b, s, n, t, i, norm_ffn[i], norm_ple[i], w_ffn_in, conv_w[i], conv_b[i],
            w_ffn_out, state_conv[i], p_prompt[i].reshape(b * s, -1), p_sample[i].reshape(n * t, -1),
            w_ple_gate, w_ple_proj)
        conv_p.append(cp)
        conv_s.append(cs)
    y_prompt = _rmsnorm(hp, norm_final, F32).reshape(b, s, d)
    y_sample = _rmsnorm(hs, norm_final, F32).reshape(n, t, d)
    return (y_prompt, y_sample,
            jnp.stack(dil_p[0]), jnp.stack(dil_s[0]), jnp.stack(dil_p[1]), jnp.stack(dil_s[1]),
            jnp.stack(dil_p[2]), jnp.stack(dil_s[2]),
            jnp.stack(win_p), jnp.stack(win_s), jnp.stack(conv_p), jnp.stack(conv_s),
            jnp.stack(kv_p), jnp.stack(kv_s))
```

```python
import functools

import jax
import jax.numpy as jnp
from jax import lax
from jax.experimental import pallas as pl
from jax.experimental.pallas import tpu as pltpu

F32 = jnp.float32
BF16 = jnp.bfloat16

D_MODEL = 2048
HEAD_DIM = 128
N_HEADS = 16
DIL_PAIRS = ((128, 1), (512, 4), (2048, 16))
N_DIL = 3
BLK = 128
KV_HEADS = 4
GROUP = 4
CMP_BLOCK = 32
CMP_STRIDE = 16
SEL_BLOCK = 64
SEL_TOPN = 16
NSA_WINDOW = 512
D_FF = 5632
CONV_W = 3
REL_BUCKETS = 32
EPS = 1e-6
NEG = -1e30
FORCED_SCORE = 1e4
SCALE = HEAD_DIM ** -0.5
QKV_A = N_DIL * 3 * N_HEADS * HEAD_DIM
NSA_Q = N_HEADS * HEAD_DIM
NSA_KV = KV_HEADS * HEAD_DIM

BUCKET_START = (1, 2, 3, 4, 5, 6, 7, 8, 9, 10, 11, 12, 13, 14, 15, 16, 22, 30, 40, 54, 73, 99,
                134, 182, 246, 332, 450, 609, 825, 1117, 1513)

VMEM_LIMIT_V7X = 56 * 1024 * 1024


def _params(sem, vmem=VMEM_LIMIT_V7X):
    return pltpu.CompilerParams(dimension_semantics=sem, vmem_limit_bytes=vmem)


def _pick(n, cands):
    for c in cands:
        if n % c == 0:
            return c
    return n


def _rmsnorm_kernel(x_ref, g_ref, o_ref):
    x = x_ref[...]
    ms = jnp.mean(x * x, axis=-1, keepdims=True)
    o_ref[...] = ((x * lax.rsqrt(ms + EPS)) * g_ref[...]).astype(o_ref.dtype)


def _rmsnorm(x, g, out_dtype):
    m, d = x.shape
    tm = _pick(m, (512, 256, 128, 32))
    return pl.pallas_call(
        _rmsnorm_kernel,
        out_shape=jax.ShapeDtypeStruct((m, d), out_dtype),
        grid=(m // tm,),
        in_specs=[pl.BlockSpec((tm, d), lambda i: (i, 0)), pl.BlockSpec((1, d), lambda i: (0, 0))],
        out_specs=pl.BlockSpec((tm, d), lambda i: (i, 0)),
        compiler_params=_params(("parallel",)),
        name="rmsnorm",
    )(x, g.reshape(1, d))


def _normed(x, g):
    ms = jnp.mean(x * x, axis=-1, keepdims=True)
    return ((x * lax.rsqrt(ms + EPS)) * g).astype(BF16)


def _project_kernel(*refs, norm, cast_w, residual, sigmoid, heads):
    refs = list(refs)
    x_ref = refs.pop(0)
    g_ref = refs.pop(0) if norm else None
    w_ref = refs.pop(0)
    r_ref = refs.pop(0) if residual else None
    o_ref = refs.pop(0)
    wb_ref = refs.pop(0) if cast_w else None
    if norm:
        a_scr = refs.pop(0)

        @pl.when(pl.program_id(1) == 0)
        def _():
            a_scr[...] = _normed(x_ref[...], g_ref[...])

        a = a_scr[...]
    else:
        a = x_ref[...]
    w = w_ref[...]
    if cast_w:
        w = w.astype(BF16)
        wb_ref[...] = w
    acc = jnp.dot(a, w, preferred_element_type=F32)
    if sigmoid:
        acc = jax.nn.sigmoid(acc)
    if residual:
        acc = r_ref[...] + acc
    if heads:
        for j in range(o_ref.shape[0]):
            o_ref[j] = acc[:, j * HEAD_DIM:(j + 1) * HEAD_DIM]
    else:
        o_ref[...] = acc


def _weight_spec(w, layer, k, tn, index):
    if w.ndim == 2:
        return pl.BlockSpec((k, tn), index)
    return pl.BlockSpec((None, k, tn), lambda *g: (layer,) + index(*g))


def _project(x, w, *, layer=None, n_cols=None, gain=None, res=None, cast_w=False, sigmoid=False, heads=False):
    m, k = x.shape
    n = n_cols or w.shape[-1]
    tm = _pick(m, (1024, 512, 256, 128))
    tn = _pick(n, (1024, 512, 256, 128))
    assert not cast_w or m == tm
    norm, residual = gain is not None, res is not None
    args, in_specs = [x], [pl.BlockSpec((tm, k), lambda i, j: (i, 0))]
    if norm:
        args.append(gain.reshape(1, k))
        in_specs.append(pl.BlockSpec((1, k), lambda i, j: (0, 0)))
    args.append(w)
    in_specs.append(_weight_spec(w, layer, k, tn, lambda i, j: (0, j)))
    if residual:
        args.append(res)
        in_specs.append(pl.BlockSpec((tm, tn), lambda i, j: (i, j)))
    if heads:
        out_shape = [jax.ShapeDtypeStruct((n // HEAD_DIM, m, HEAD_DIM), F32)]
        out_specs = [pl.BlockSpec((tn // HEAD_DIM, tm, HEAD_DIM), lambda i, j: (j, i, 0))]
    else:
        out_shape = [jax.ShapeDtypeStruct((m, n), F32)]
        out_specs = [pl.BlockSpec((tm, tn), lambda i, j: (i, j))]
    if cast_w:
        out_shape.append(jax.ShapeDtypeStruct((k, n), BF16))
        out_specs.append(pl.BlockSpec((k, tn), lambda i, j: (0, j)))
    outs = pl.pallas_call(
        functools.partial(_project_kernel, norm=norm, cast_w=cast_w, residual=residual, sigmoid=sigmoid,
                          heads=heads),
        out_shape=out_shape,
        grid=(m // tm, n // tn),
        in_specs=in_specs,
        out_specs=out_specs,
        scratch_shapes=[pltpu.VMEM((tm, k), BF16)] if norm else [],
        compiler_params=_params(("parallel", "arbitrary")),
        name="project",
    )(*args)
    return tuple(outs) if cast_w else outs[0]


def _conv_gelu_val(g, g1, g2, val, cw_ref, cb_ref):
    c = cb_ref[...] + g2 * cw_ref[0:1, :]
    c = c + g1 * cw_ref[1:2, :]
    c = c + g * cw_ref[2:3, :]
    return (jax.nn.gelu(c) * val).astype(BF16)


def _ffn_out_prompt_kernel(g_ref, halo_ref, v_ref, cw_ref, cb_ref, w_ref, r_ref, o_ref, acc_ref, *,
                           tiles_per_seq):
    i, k = pl.program_id(0), pl.program_id(1)

    @pl.when(k == 0)
    def _():
        acc_ref[...] = jnp.zeros_like(acc_ref)

    g = g_ref[...]
    row = lax.broadcasted_iota(jnp.int32, g.shape, 0)
    halo = jnp.where(i % tiles_per_seq == 0, 0.0, halo_ref[...])
    g1 = jnp.where(row == 0, halo[7:8, :], pltpu.roll(g, 1, 0))
    g2 = jnp.where(row == 0, halo[6:7, :], jnp.where(row == 1, halo[7:8, :], pltpu.roll(g, 2, 0)))
    u = _conv_gelu_val(g, g1, g2, v_ref[...], cw_ref, cb_ref)
    acc_ref[...] += jnp.dot(u, w_ref[...], preferred_element_type=F32)

    @pl.when(k == pl.num_programs(1) - 1)
    def _():
        o_ref[...] = r_ref[...] + acc_ref[...]


def _ffn_out_prompt(hu, conv_w, conv_b, w_out, res, seq):
    m = hu.shape[0]
    tm, tk = 512, D_FF // 4
    nk = D_FF // tk
    return pl.pallas_call(
        functools.partial(_ffn_out_prompt_kernel, tiles_per_seq=seq // tm),
        out_shape=jax.ShapeDtypeStruct((m, D_MODEL), F32),
        grid=(m // tm, nk),
        in_specs=[
            pl.BlockSpec((tm, tk), lambda i, k: (i, k)),
            pl.BlockSpec((8, tk), lambda i, k: (jnp.maximum(i * (tm // 8) - 1, 0), k)),
            pl.BlockSpec((tm, tk), lambda i, k: (i, k + nk)),
            pl.BlockSpec((CONV_W, tk), lambda i, k: (0, k)),
            pl.BlockSpec((1, tk), lambda i, k: (0, k)),
            pl.BlockSpec((tk, D_MODEL), lambda i, k: (k, 0)),
            pl.BlockSpec((tm, D_MODEL), lambda i, k: (i, 0)),
        ],
        out_specs=pl.BlockSpec((tm, D_MODEL), lambda i, k: (i, 0)),
        scratch_shapes=[pltpu.VMEM((tm, D_MODEL), F32)],
        compiler_params=_params(("parallel", "arbitrary")),
        name="ffn_out_prompt",
    )(hu, hu, hu, conv_w, conv_b.reshape(1, D_FF), w_out, res)


def _ffn_out_sample_kernel(g_ref, e1_ref, e2_ref, v_ref, cw_ref, cb_ref, w_ref, r_ref, o_ref, wb_ref, acc_ref, *,
                           t_len):
    k = pl.program_id(0)

    @pl.when(k == 0)
    def _():
        acc_ref[...] = jnp.zeros_like(acc_ref)

    g = g_ref[...]
    t = lax.broadcasted_iota(jnp.int32, g.shape, 0) % t_len
    g1 = jnp.where(t == 0, e1_ref[...], pltpu.roll(g, 1, 0))
    g2 = jnp.where(t < 2, e2_ref[...], pltpu.roll(g, 2, 0))
    u = _conv_gelu_val(g, g1, g2, v_ref[...], cw_ref, cb_ref)
    w = w_ref[...].astype(BF16)
    wb_ref[...] = w
    acc_ref[...] += jnp.dot(u, w, preferred_element_type=F32)

    @pl.when(k == pl.num_programs(0) - 1)
    def _():
        o_ref[...] = r_ref[...] + acc_ref[...]


def _ffn_out_sample(hu, conv_prev, conv_w, conv_b, w_out, layer, res, t_len):
    m = hu.shape[0]
    n = m // t_len
    tk = 512
    nk = D_FF // tk
    zeros = jnp.zeros((n, t_len - 1, D_FF), F32)
    e1 = jnp.concatenate([conv_prev[:, 1:2], zeros], axis=1).reshape(m, D_FF)
    e2 = jnp.concatenate([conv_prev, zeros[:, 1:]], axis=1).reshape(m, D_FF)
    return pl.pallas_call(
        functools.partial(_ffn_out_sample_kernel, t_len=t_len),
        out_shape=(jax.ShapeDtypeStruct((m, D_MODEL), F32), jax.ShapeDtypeStruct((D_FF, D_MODEL), BF16)),
        grid=(nk,),
        in_specs=[
            pl.BlockSpec((m, tk), lambda k: (0, k)),
            pl.BlockSpec((m, tk), lambda k: (0, k)),
            pl.BlockSpec((m, tk), lambda k: (0, k)),
            pl.BlockSpec((m, tk), lambda k: (0, k + nk)),
            pl.BlockSpec((CONV_W, tk), lambda k: (0, k)),
            pl.BlockSpec((1, tk), lambda k: (0, k)),
            pl.BlockSpec((None, tk, D_MODEL), lambda k: (layer, k, 0)),
            pl.BlockSpec((m, D_MODEL), lambda k: (0, 0)),
        ],
        out_specs=(pl.BlockSpec((m, D_MODEL), lambda k: (0, 0)), pl.BlockSpec((tk, D_MODEL), lambda k: (k, 0))),
        scratch_shapes=[pltpu.VMEM((m, D_MODEL), F32)],
        compiler_params=_params(("arbitrary",)),
        name="ffn_out_sample",
    )(hu, e1, e2, hu, conv_w, conv_b.reshape(1, D_FF), w_out, res)


def _ple_kernel(h_ref, g_ref, wg_ref, p_ref, wp_ref, o_ref, *rest, cast_w):
    a_scr = rest[-1]
    j = pl.program_id(1)
    tn = o_ref.shape[1]

    @pl.when(j == 0)
    def _():
        a_scr[...] = _normed(h_ref[...], g_ref[...])

    wg, wp = wg_ref[...], wp_ref[...]
    if cast_w:
        wg, wp = wg.astype(BF16), wp.astype(BF16)
        rest[0][...] = wg
        rest[1][...] = wp
    gate = jax.nn.sigmoid(jnp.dot(a_scr[...], wg, preferred_element_type=F32))
    proj = jnp.dot(p_ref[...], wp, preferred_element_type=F32)
    o_ref[...] = h_ref[:, pl.ds(pl.multiple_of(j * tn, tn), tn)] + gate * proj


def _ple_add(h, gain, w_gate, p, w_proj, layer=None, cast_w=False):
    m, d = h.shape
    kp = p.shape[1]
    tm = _pick(m, (1024, 512, 256, 128))
    tn = 1024
    assert not cast_w or m == tm
    out_shape = [jax.ShapeDtypeStruct((m, d), F32)]
    out_specs = [pl.BlockSpec((tm, tn), lambda i, j: (i, j))]
    if cast_w:
        out_shape += [jax.ShapeDtypeStruct((d, d), BF16), jax.ShapeDtypeStruct((kp, d), BF16)]
        out_specs += [pl.BlockSpec((d, tn), lambda i, j: (0, j)), pl.BlockSpec((kp, tn), lambda i, j: (0, j))]
    outs = pl.pallas_call(
        functools.partial(_ple_kernel, cast_w=cast_w),
        out_shape=out_shape,
        grid=(m // tm, d // tn),
        in_specs=[
            pl.BlockSpec((tm, d), lambda i, j: (i, 0)),
            pl.BlockSpec((1, d), lambda i, j: (0, 0)),
            _weight_spec(w_gate, layer, d, tn, lambda i, j: (0, j)),
            pl.BlockSpec((tm, kp), lambda i, j: (i, 0)),
            _weight_spec(w_proj, layer, kp, tn, lambda i, j: (0, j)),
        ],
        out_specs=out_specs,
        scratch_shapes=[pltpu.VMEM((tm, d), BF16)],
        compiler_params=_params(("parallel", "arbitrary")),
        name="ple_add",
    )(h, gain.reshape(1, d), w_gate, p, w_proj)
    return tuple(outs) if cast_w else outs[0]


def _bias_table_kernel(rb_ref, o_ref, *, a0, ag, ar, ac, lo, hi, mod):
    h, g = pl.program_id(0), pl.program_id(1)
    shape = o_ref.shape[2:]
    dist = (a0 + ag * g + ar * lax.broadcasted_iota(jnp.int32, shape, 0)
            + ac * lax.broadcasted_iota(jnp.int32, shape, 1))
    d = jnp.maximum(dist, 0)
    out = jnp.full(shape, rb_ref[0, h], F32)
    for k in range(1, REL_BUCKETS):
        out = jnp.where(d >= BUCKET_START[k - 1], rb_ref[k, h], out)
    ok = (dist >= lo) & (dist <= hi)
    if mod > 1:
        ok = ok & ((d & (mod - 1)) == 0)
    o_ref[0, 0] = jnp.where(ok, out, NEG)


def _bias_table(rel_bias, n_g, n_r, n_c, a0, ag, ar, ac, lo, hi, mod=1):
    assert mod & (mod - 1) == 0
    return pl.pallas_call(
        functools.partial(_bias_table_kernel, a0=a0, ag=ag, ar=ar, ac=ac, lo=lo, hi=hi, mod=mod),
        out_shape=jax.ShapeDtypeStruct((N_HEADS, n_g, n_r, n_c), F32),
        grid=(N_HEADS, n_g),
        in_specs=[pl.BlockSpec(memory_space=pltpu.SMEM)],
        out_specs=pl.BlockSpec((1, 1, n_r, n_c), lambda h, g: (h, g, 0, 0)),
        compiler_params=_params(("parallel", "parallel")),
        name="bias_table",
    )(rel_bias)


def _dil_prompt_kernel(q_ref, kc_ref, kp_ref, vc_ref, vp_ref, bias_ref, o_ref, lse_ref, *, dil, hps):
    first = pl.program_id(1) == 0
    hb = pl.program_id(2)
    col = lax.broadcasted_iota(jnp.int32, (BLK, 2 * BLK), 1)
    edge = jnp.where(first & (col < BLK), NEG, 0.0)
    lane = lax.broadcasted_iota(jnp.int32, (BLK, BLK), 1)

    @pl.when(hb == 0)
    def _():
        lse_ref[...] = jnp.zeros_like(lse_ref)

    def unit(u, carry):
        hh, r = u // dil, u % dil
        rows = pl.ds(r, BLK, stride=dil)
        q = q_ref[hh, rows, :].astype(BF16)
        k = jnp.concatenate([kp_ref[hh, rows, :], kc_ref[hh, rows, :]], axis=0).astype(BF16)
        v = jnp.concatenate([vp_ref[hh, rows, :], vc_ref[hh, rows, :]], axis=0).astype(BF16)
        h = hb * hps + hh
        s = lax.dot_general(q, k, (((1,), (1,)), ((), ())), preferred_element_type=F32)
        s = s * SCALE + bias_ref[h] + edge
        m = jnp.max(s, axis=-1, keepdims=True)
        p = jnp.exp(s - m)
        l = jnp.sum(p, axis=-1, keepdims=True)
        o_ref[hh, rows, :] = jnp.dot(p.astype(BF16), v, preferred_element_type=F32) / l
        lse_ref[rows, :] = jnp.where(lane == h, m + jnp.log(l), lse_ref[rows, :])
        return carry

    lax.fori_loop(0, hps * dil, unit, 0, unroll=8)


def _dil_prompt_group(qkv_hm, bias, grp, dil, b, s):
    span = BLK * dil
    nsp = s // span
    hps = N_HEADS // dil
    nhb = N_HEADS // hps

    def slab(part, prev):
        base = (grp * 3 + part) * N_HEADS // hps
        if prev:
            return lambda bi, sp, hb: (base + hb, bi * nsp + jnp.maximum(sp - 1, 0), 0)
        return lambda bi, sp, hb: (base + hb, bi * nsp + sp, 0)

    blk = (hps, span, HEAD_DIM)
    return pl.pallas_call(
        functools.partial(_dil_prompt_kernel, dil=dil, hps=hps),
        out_shape=(jax.ShapeDtypeStruct((N_HEADS, b * s, HEAD_DIM), F32),
                   jax.ShapeDtypeStruct((b * s, BLK), F32)),
        grid=(b, nsp, nhb),
        in_specs=[pl.BlockSpec(blk, slab(0, False)), pl.BlockSpec(blk, slab(1, False)),
                  pl.BlockSpec(blk, slab(1, True)), pl.BlockSpec(blk, slab(2, False)),
                  pl.BlockSpec(blk, slab(2, True)),
                  pl.BlockSpec((N_HEADS, BLK, 2 * BLK), lambda bi, sp, hb: (0, 0, 0))],
        out_specs=(pl.BlockSpec(blk, lambda bi, sp, hb: (hb, bi * nsp + sp, 0)),
                   pl.BlockSpec((span, BLK), lambda bi, sp, hb: (bi * nsp + sp, 0))),
        compiler_params=_params(("parallel", "parallel", "arbitrary")),
        name=f"dil_attn_prompt_g{grp}",
    )(qkv_hm, qkv_hm, qkv_hm, qkv_hm, qkv_hm, bias)


def _dil_combine_kernel(o0_ref, o1_ref, o2_ref, l0_ref, l1_ref, l2_ref, o_ref, *, head_major):
    l0, l1, l2 = l0_ref[...], l1_ref[...], l2_ref[...]
    mx = jnp.maximum(jnp.maximum(l0, l1), l2)
    e0, e1, e2 = jnp.exp(l0 - mx), jnp.exp(l1 - mx), jnp.exp(l2 - mx)
    den = e0 + e1 + e2
    w0, w1, w2 = e0 / den, e1 / den, e2 / den
    for h in range(N_HEADS):
        sl = slice(h * HEAD_DIM, (h + 1) * HEAD_DIM)
        g0, g1, g2 = ((r[h] for r in (o0_ref, o1_ref, o2_ref)) if head_major
                      else (r[:, sl] for r in (o0_ref, o1_ref, o2_ref)))
        o = (w0[:, h:h + 1] * g0 + w1[:, h:h + 1] * g1) + w2[:, h:h + 1] * g2
        o_ref[:, sl] = o.astype(o_ref.dtype)


def _dil_combine(outs, lses, head_major):
    m = lses[0].shape[0]
    wide = N_HEADS * HEAD_DIM
    tm = _pick(m, (256, 128, 32))
    ob = pl.BlockSpec((tm, wide), lambda i: (i, 0))
    ib = pl.BlockSpec((N_HEADS, tm, HEAD_DIM), lambda i: (0, i, 0)) if head_major else ob
    lb = pl.BlockSpec((tm, BLK), lambda i: (i, 0))
    return pl.pallas_call(
        functools.partial(_dil_combine_kernel, head_major=head_major),
        out_shape=jax.ShapeDtypeStruct((m, wide), BF16),
        grid=(m // tm,),
        in_specs=[ib, ib, ib, lb, lb, lb],
        out_specs=ob,
        compiler_params=_params(("parallel",)),
        name="dil_combine",
    )(*outs, *lses)


HEAD_TILE = 8


def _heads_first(x):
    return pltpu.einshape("mhd->hmd", x)


def _dil_sample_kernel(q_ref, k_ref, v_ref, kn_ref, vn_ref, bias_ref, biasn_ref, o_ref, lse_ref,
                       m_scr, l_scr, acc_scr, k_scr, v_scr):
    ht, c = pl.program_id(1), pl.program_id(2)
    k_scr[...] = _heads_first(k_ref[...])
    v_scr[...] = _heads_first(v_ref[...])

    @pl.when(c == 0)
    def _():
        m_scr[...] = jnp.full_like(m_scr, NEG)
        l_scr[...] = jnp.zeros_like(l_scr)
        acc_scr[...] = jnp.zeros_like(acc_scr)

    @pl.when((c == 0) & (ht == 0))
    def _():
        lse_ref[...] = jnp.zeros_like(lse_ref)

    def step(hh, k, v, bias):
        sl = pl.ds(pl.multiple_of(hh * HEAD_DIM, HEAD_DIM), HEAD_DIM)
        q = q_ref[0, :, sl].astype(BF16)
        s = lax.dot_general(q, k.astype(BF16), (((1,), (1,)), ((), ())), preferred_element_type=F32)
        s = s * SCALE + bias
        m_old = m_scr[hh]
        m_new = jnp.maximum(m_old, jnp.max(s, axis=-1, keepdims=True))
        alpha = jnp.exp(m_old - m_new)
        p = jnp.where(bias > 0.5 * NEG, jnp.exp(s - m_new), 0.0)
        l_scr[hh] = alpha * l_scr[hh] + jnp.sum(p, axis=-1, keepdims=True)
        acc_scr[hh] = alpha * acc_scr[hh] + jnp.dot(p.astype(BF16), v.astype(BF16), preferred_element_type=F32)
        m_scr[hh] = m_new

    def head(hh, carry):
        step(hh, k_scr[hh], v_scr[hh], bias_ref[hh, 0])
        return carry

    lax.fori_loop(0, HEAD_TILE, head, 0, unroll=True)

    @pl.when(c == pl.num_programs(2) - 1)
    def _():
        lane = lax.broadcasted_iota(jnp.int32, (8, BLK), 1)

        def fin(hh, carry):
            sl = pl.ds(pl.multiple_of(hh * HEAD_DIM, HEAD_DIM), HEAD_DIM)
            step(hh, kn_ref[0, :, sl], vn_ref[0, :, sl], biasn_ref[hh])
            l = jnp.maximum(l_scr[hh], 1e-30)
            o_ref[0, :, sl] = acc_scr[hh] / l
            lse_ref[0] = jnp.where(lane == ht * HEAD_TILE + hh, m_scr[hh] + jnp.log(l), lse_ref[0])
            return carry

        lax.fori_loop(0, HEAD_TILE, fin, 0, unroll=True)


def _dil_sample_group(q, kn, vn, buf, bias, bias_new, dil):
    n, lb = buf.shape[:2]
    n_cls = bias.shape[1]
    wide = N_HEADS * HEAD_DIM
    half = wide // 2
    rows = lb // dil
    tiles = 2 * N_HEADS // HEAD_TILE
    view = buf.reshape(n, rows, dil * tiles, HEAD_TILE, HEAD_DIM)
    nht = N_HEADS // HEAD_TILE
    return pl.pallas_call(
        _dil_sample_kernel,
        out_shape=(jax.ShapeDtypeStruct((n, 8, wide), F32), jax.ShapeDtypeStruct((n, 8, BLK), F32)),
        grid=(n, nht, n_cls),
        in_specs=[
            pl.BlockSpec((1, 8, half), lambda i, ht, c: (i, 0, ht)),
            pl.BlockSpec((None, rows, None, HEAD_TILE, HEAD_DIM), lambda i, ht, c: (i, 0, c * tiles + ht, 0, 0)),
            pl.BlockSpec((None, rows, None, HEAD_TILE, HEAD_DIM),
                         lambda i, ht, c: (i, 0, c * tiles + nht + ht, 0, 0)),
            pl.BlockSpec((1, BLK, half), lambda i, ht, c: (i, 0, ht)),
            pl.BlockSpec((1, BLK, half), lambda i, ht, c: (i, 0, ht)),
            pl.BlockSpec((HEAD_TILE, 1, 8, rows), lambda i, ht, c: (ht, c, 0, 0)),
            pl.BlockSpec((HEAD_TILE, 8, BLK), lambda i, ht, c: (ht, 0, 0)),
        ],
        out_specs=(pl.BlockSpec((1, 8, half), lambda i, ht, c: (i, 0, ht)),
                   pl.BlockSpec((1, 8, BLK), lambda i, ht, c: (i, 0, 0))),
        scratch_shapes=[pltpu.VMEM((HEAD_TILE, 8, 1), F32), pltpu.VMEM((HEAD_TILE, 8, 1), F32),
                        pltpu.VMEM((HEAD_TILE, 8, HEAD_DIM), F32),
                        pltpu.VMEM((HEAD_TILE, rows, HEAD_DIM), F32), pltpu.VMEM((HEAD_TILE, rows, HEAD_DIM), F32)],
        compiler_params=_params(("parallel", "arbitrary", "arbitrary")),
        name="dil_attn_sample",
    )(q, view, view, kn, vn, bias, bias_new)


def _vector_rows_kernel(x_ref, o_ref):
    x = x_ref[...]
    if x.ndim == 2:
        x = jnp.stack([x[:, j * HEAD_DIM:(j + 1) * HEAD_DIM] for j in range(HEAD_TILE)])
    o_ref[...] = pltpu.einshape("hmd->mhd", x)


def _vector_rows(x, tile0, n_tiles, b, s, keep):
    tr = min(keep, 512)
    r0 = (s - keep) // tr
    per_seq = s // tr
    if x.ndim == 3:
        in_spec = pl.BlockSpec((HEAD_TILE, tr, HEAD_DIM), lambda bi, r, c: (tile0 + c, bi * per_seq + r0 + r, 0))
    else:
        in_spec = pl.BlockSpec((tr, HEAD_TILE * HEAD_DIM), lambda bi, r, c: (bi * per_seq + r0 + r, tile0 + c))
    return pl.pallas_call(
        _vector_rows_kernel,
        out_shape=jax.ShapeDtypeStruct((b, keep, n_tiles * HEAD_TILE, HEAD_DIM), F32),
        grid=(b, keep // tr, n_tiles),
        in_specs=[in_spec],
        out_specs=pl.BlockSpec((None, tr, HEAD_TILE, HEAD_DIM), lambda bi, r, c: (bi, r, c, 0)),
        compiler_params=_params(("parallel", "parallel", "parallel")),
        name="vector_rows",
    )(x)


def _pad_rows(x, rows):
    return jnp.pad(x, ((0, 0), (0, rows - x.shape[1]), (0, 0)))


def _layer_a(hp, hs, b, s, n, t, norm_g, w_in, w_out, rel_bias, bufs):
    wide = N_HEADS * HEAD_DIM
    qkv_s, w_in = _project(hs, w_in, gain=norm_g, cast_w=True)
    qkv_s = qkv_s.reshape(n, t, QKV_A)
    qkv_p = _project(hp, w_in, gain=norm_g, heads=True)
    outs_p, lses_p, outs_s, lses_s, new_p, new_s = [], [], [], [], [], []
    for grp, (win, dil) in enumerate(DIL_PAIRS):
        base = grp * 3 * wide
        bias = _bias_table(rel_bias, 1, BLK, 2 * BLK, BLK * dil, 0, dil, -dil, 0, win).reshape(N_HEADS, BLK, 2 * BLK)
        o, lse = _dil_prompt_group(qkv_p, bias, grp, dil, b, s)
        outs_p.append(o)
        lses_p.append(lse)
        keep = min(win, s)
        kv = _vector_rows(qkv_p, (grp * 3 + 1) * N_HEADS // HEAD_TILE, 2 * N_HEADS // HEAD_TILE, b, s, keep)
        new_p.append(kv.reshape(b, keep, 2, N_HEADS, HEAD_DIM))
        buf = bufs[grp]
        lb = buf.shape[1]
        n_cls = min(dil, t)
        bias_buf = _bias_table(rel_bias, n_cls, 8, lb // dil, lb, -1, 1, -dil, 0, win, dil)
        bias_new = _bias_table(rel_bias, 1, 8, BLK, 0, 0, 1, -1, 0, win, dil).reshape(N_HEADS, 8, BLK)
        q = _pad_rows(qkv_s[:, :, base:base + wide], 8)
        kn = _pad_rows(qkv_s[:, :, base + wide:base + 2 * wide], BLK)
        vn = _pad_rows(qkv_s[:, :, base + 2 * wide:base + 3 * wide], BLK)
        o, lse = _dil_sample_group(q, kn, vn, buf, bias_buf, bias_new, dil)
        outs_s.append(o[:, :t].reshape(n * t, wide))
        lses_s.append(lse[:, :t].reshape(n * t, BLK))
        kv_new = qkv_s[:, :, base + wide:base + 3 * wide].reshape(n, t, 2, N_HEADS, HEAD_DIM)
        new_s.append(jnp.concatenate([buf, kv_new], axis=1)[:, t:])
    hs, w_out = _project(_dil_combine(outs_s, lses_s, False), w_out, res=hs, cast_w=True)
    hp = _project(_dil_combine(outs_p, lses_p, True), w_out, res=hp)
    return hp, hs, new_p, new_s


PAGES_PER_STEP = 16
CHUNKS_PER_PAGE = BLK // CMP_STRIDE
NT_DIMS = (((1,), (1,)), ((), ()))
TN_DIMS = (((0,), (0,)), ((), ()))


def _page_spec(p, half):
    return pl.BlockSpec((BLK, HEAD_TILE, HEAD_DIM),
                        lambda i, j, *rest: (rest[-1][i, j * PAGES_PER_STEP + p], half, 0))


def _cmp_proj_kernel(pt_ref, *refs):
    pages = refs[:PAGES_PER_STEP]
    w_ref, o_ref = refs[PAGES_PER_STEP:PAGES_PER_STEP + 2]
    ys = [pltpu.einshape("ctgd->tgcd", pg[...].reshape(CHUNKS_PER_PAGE, CMP_STRIDE, HEAD_TILE, HEAD_DIM))
          for pg in pages]
    for c in range(2):
        lhs = jnp.concatenate(
            [jnp.concatenate([y[t, c * KV_HEADS + kh] for kh in range(KV_HEADS) for y in ys], axis=0)
             for t in range(CMP_STRIDE)], axis=1).astype(BF16)
        acc = jnp.dot(lhs, w_ref[c], preferred_element_type=F32)
        for kh in range(KV_HEADS):
            o_ref[0, c, kh] = acc[kh * BLK:(kh + 1) * BLK]


def _cmp_proj(pages, page_table, w1r):
    n, n_pages = page_table.shape
    chunks = n_pages * CHUNKS_PER_PAGE
    grid_spec = pltpu.PrefetchScalarGridSpec(
        num_scalar_prefetch=1,
        grid=(n, n_pages // PAGES_PER_STEP),
        in_specs=[_page_spec(p, 0) for p in range(PAGES_PER_STEP)]
        + [pl.BlockSpec((2, CMP_STRIDE * HEAD_DIM, 2 * HEAD_DIM), lambda i, j, pt: (0, 0, 0))],
        out_specs=pl.BlockSpec((1, 2, KV_HEADS, BLK, 2 * HEAD_DIM), lambda i, j, pt: (i, 0, 0, j, 0)),
    )
    return pl.pallas_call(
        _cmp_proj_kernel,
        out_shape=jax.ShapeDtypeStruct((n, 2, KV_HEADS, chunks, 2 * HEAD_DIM), F32),
        grid_spec=grid_spec,
        compiler_params=_params(("parallel", "arbitrary")),
        name="nsa_cmp_proj",
    )(page_table, *([pages] * PAGES_PER_STEP), w1r.reshape(2, CMP_STRIDE * HEAD_DIM, 2 * HEAD_DIM))


def _finish_compress(a, pe_row, w1f, b1, w2, b2, n_blocks):
    rows = a.shape[0]
    cst = jnp.dot(pe_row, w1f, preferred_element_type=F32)[0:1]
    h = (b1 + cst) + a[:, :HEAD_DIM] + pltpu.roll(a[:, HEAD_DIM:], rows - 1, 0)
    x = jnp.dot(jax.nn.gelu(h).astype(BF16), w2, preferred_element_type=F32) + b2
    return jnp.where(lax.broadcasted_iota(jnp.int32, x.shape, 0) < n_blocks, x, 0.0)


def _split3(x):
    hi = x.astype(BF16)
    r = x - hi.astype(F32)
    mid = r.astype(BF16)
    return hi, mid, (r - mid.astype(F32)).astype(BF16)


def _top_n(score, n, axis):
    idx = lax.broadcasted_iota(jnp.int32, score.shape, axis).astype(F32)
    big = float(score.shape[axis])

    def body(_, carry):
        sc, sel = carry
        mx = jnp.max(sc, axis=axis, keepdims=True)
        first = jnp.min(jnp.where(sc == mx, idx, big), axis=axis, keepdims=True)
        hit = idx == first
        return jnp.where(hit, -jnp.inf, sc), jnp.where(hit, 1.0, sel)

    return lax.fori_loop(0, n, body, (score, jnp.zeros(score.shape, F32)))[1]


def _sel_scores(p_slc, blk, cur, n_blocks):
    forced = (blk == 0) | (blk == cur) | (blk == cur - 1)
    score = jnp.where(forced, FORCED_SCORE, jnp.where(blk <= cur, p_slc, -1.0))
    return jnp.where(blk < n_blocks, score, -2.0)


def _nsa_prompt_kernel(q_ref, a_ref, pe_ref, w1f_ref, b1_ref, w2_ref, b2_ref, ksel_ref, vsel_ref, kwin_ref,
                       vwin_ref, tbl_ref, tblw_ref, biasc_ref, gates_ref, msel_ref, o_ref, kc_scr, vc_scr, sel_scr,
                       vselt_ref, vwint_ref, *, n_cmp, n_sel):
    i = pl.program_id(2)

    @pl.when(i == 0)
    def _():
        for c, scr in ((0, kc_scr), (1, vc_scr)):
            scr[...] = _finish_compress(a_ref[0, c, 0], pe_ref[c], w1f_ref[c], b1_ref[c], w2_ref[c], b2_ref[c],
                                        n_cmp).astype(BF16)
        for src, dst in ((vsel_ref, vselt_ref), (vwin_ref, vwint_ref)):
            for c in range(src.shape[0] // BLK):
                dst[:, c * BLK:(c + 1) * BLK] = src[c * BLK:(c + 1) * BLK, :].T.astype(BF16)

    q = q_ref[...]
    qs = jnp.concatenate([q[:, g * HEAD_DIM:(g + 1) * HEAD_DIM] for g in range(GROUP)], axis=0).astype(BF16)
    key_i = lax.broadcasted_iota(jnp.int32, (BLK, BLK), 0)
    tok_i = lax.broadcasted_iota(jnp.int32, (BLK, BLK), 1)

    def lanes4(x):
        return jnp.concatenate([x] * GROUP, axis=1)

    mask_c = lanes4(i * BLK + tok_i - (key_i * CMP_STRIDE + (CMP_BLOCK - 1)) >= 0)
    s = lax.dot_general(kc_scr[...], qs, NT_DIMS, preferred_element_type=F32) * SCALE
    s = jnp.where(mask_c, s + jnp.concatenate([biasc_ref[g, 0] for g in range(GROUP)], axis=1), NEG)
    m = jnp.max(s, axis=0, keepdims=True)
    p = jnp.where(mask_c, jnp.exp(s - m), 0.0)
    pn = p / jnp.maximum(jnp.sum(p, axis=0, keepdims=True), 1e-30)
    o_cmp = lax.dot_general(vc_scr[...], pn.astype(BF16), TN_DIMS, preferred_element_type=F32)
    pc = ((pn[:, 0:BLK] + pn[:, BLK:2 * BLK]) + pn[:, 2 * BLK:3 * BLK]) + pn[:, 3 * BLK:4 * BLK]

    msel = msel_ref[...]
    p_slc = sum(jnp.dot(msel, part, preferred_element_type=F32) for part in _split3(pc))
    blk = lax.broadcasted_iota(jnp.int32, p_slc.shape, 0)
    cur = (i * BLK + lax.broadcasted_iota(jnp.int32, p_slc.shape, 1)) // SEL_BLOCK
    sel_scr[...] = (1.0 - _top_n(_sel_scores(p_slc, blk, cur, n_sel), SEL_TOPN, 0)) * NEG

    def attend(k_ref, vt_ref, bias_ref, first_blk, n_blk, extra, carry):
        m_run, l_run, acc = carry
        off = pl.multiple_of(first_blk * BLK, BLK)
        k = k_ref[pl.ds(off, n_blk * BLK), :].astype(BF16)
        bias = []
        for j in range(n_blk):
            idx = jnp.maximum(i - first_blk - j, -1) + 1
            bias.append(jnp.concatenate([bias_ref[g, idx] for g in range(GROUP)], axis=1))
        s = lax.dot_general(k, qs, NT_DIMS, preferred_element_type=F32) * SCALE + jnp.concatenate(bias, axis=0)
        if extra is not None:
            s = s + extra
        m_new = jnp.maximum(m_run, jnp.max(s, axis=0, keepdims=True))
        alpha = jnp.exp(m_run - m_new)
        p = jnp.exp(s - m_new)
        l_new = alpha * l_run + jnp.sum(p, axis=0, keepdims=True)
        vt = vt_ref[:, pl.ds(off, n_blk * BLK)]
        return m_new, l_new, alpha * acc + jnp.dot(vt, p.astype(BF16), preferred_element_type=F32)

    init = (jnp.full((1, GROUP * BLK), NEG, F32), jnp.zeros((1, GROUP * BLK), F32),
            jnp.zeros((HEAD_DIM, GROUP * BLK), F32))

    sel_span = 4
    per_blk = BLK // SEL_BLOCK

    def sel_step(c, carry):
        rows = sel_scr[pl.ds(pl.multiple_of(c * sel_span * per_blk, 8), sel_span * per_blk), :]
        unpicked = jnp.concatenate([jnp.broadcast_to(rows[u:u + 1], (SEL_BLOCK, BLK))
                                    for u in range(sel_span * per_blk)], axis=0)
        return attend(ksel_ref, vselt_ref, tbl_ref, c * sel_span, sel_span, lanes4(unpicked), carry)

    _, l_sel, acc_sel = lax.fori_loop(0, i // sel_span + 1, sel_step, init)
    o_sel = acc_sel / jnp.maximum(l_sel, 1e-30)

    n_win = (NSA_WINDOW - 1 + BLK - 1) // BLK + 1
    _, l_win, acc_win = attend(kwin_ref, vwint_ref, tblw_ref, jnp.maximum(i - (n_win - 1), 0), n_win, None, init)
    o_win = acc_win / jnp.maximum(l_win, 1e-30)

    gt = gates_ref[0]

    def gate(branch):
        return jnp.concatenate([gt[branch * GROUP + g:branch * GROUP + g + 1, :] for g in range(GROUP)], axis=1)

    o = (gate(0) * o_cmp + gate(1) * o_sel) + gate(2) * o_win
    for g in range(GROUP):
        o_ref[0, g] = o[:, g * BLK:(g + 1) * BLK]


def _nsa_prompt(proj, a_cmp, gates_t, cmp_w, tbl, tbl_win, bias_c, msel, b, s):
    pe, w1f, b1, w2, b2 = cmp_w
    nq = s // BLK
    kcol = NSA_Q // HEAD_DIM
    const = lambda shape: pl.BlockSpec(shape, lambda bi, kh, i: (0,) * len(shape))
    return pl.pallas_call(
        functools.partial(_nsa_prompt_kernel, n_cmp=s // CMP_STRIDE - 1, n_sel=s // SEL_BLOCK),
        out_shape=jax.ShapeDtypeStruct((b, N_HEADS, HEAD_DIM, s), F32),
        grid=(b, KV_HEADS, nq),
        in_specs=[
            pl.BlockSpec((BLK, GROUP * HEAD_DIM), lambda bi, kh, i: (bi * nq + i, kh)),
            pl.BlockSpec((1, 2, 1, s // CMP_STRIDE, 2 * HEAD_DIM), lambda bi, kh, i: (bi, 0, kh, 0, 0)),
            const(pe.shape), const(w1f.shape), const(b1.shape), const(w2.shape), const(b2.shape),
            pl.BlockSpec((s, HEAD_DIM), lambda bi, kh, i: (bi, kcol + 2 * KV_HEADS + kh)),
            pl.BlockSpec((s, HEAD_DIM), lambda bi, kh, i: (bi, kcol + 3 * KV_HEADS + kh)),
            pl.BlockSpec((s, HEAD_DIM), lambda bi, kh, i: (bi, kcol + 4 * KV_HEADS + kh)),
            pl.BlockSpec((s, HEAD_DIM), lambda bi, kh, i: (bi, kcol + 5 * KV_HEADS + kh)),
            pl.BlockSpec((GROUP,) + tbl.shape[1:], lambda bi, kh, i: (kh, 0, 0, 0)),
            pl.BlockSpec((GROUP,) + tbl_win.shape[1:], lambda bi, kh, i: (kh, 0, 0, 0)),
            pl.BlockSpec((GROUP, 1, s // CMP_STRIDE, BLK), lambda bi, kh, i: (kh, 0, 0, i)),
            pl.BlockSpec((1, 16, BLK), lambda bi, kh, i: (kh, 0, bi * nq + i)),
            const(msel.shape),
        ],
        out_specs=pl.BlockSpec((1, GROUP, HEAD_DIM, BLK), lambda bi, kh, i: (bi, kh, 0, i)),
        scratch_shapes=[pltpu.VMEM((s // CMP_STRIDE, HEAD_DIM), BF16), pltpu.VMEM((s // CMP_STRIDE, HEAD_DIM), BF16),
                        pltpu.VMEM((s // SEL_BLOCK, BLK), F32),
                        pltpu.VMEM((HEAD_DIM, s), BF16), pltpu.VMEM((HEAD_DIM, s), BF16)],
        compiler_params=_params(("parallel", "parallel", "arbitrary")),
        name="nsa_prompt",
    )(proj, a_cmp, pe, w1f, b1, w2, b2, proj, proj, proj, proj, tbl, tbl_win, bias_c, gates_t, msel)


def _masked_softmax_rows(s, mask):
    s = jnp.where(mask, s, NEG)
    m = jnp.max(s, axis=-1, keepdims=True)
    p = jnp.where(mask, jnp.exp(s - m), 0.0)
    return p / jnp.maximum(jnp.sum(p, axis=-1, keepdims=True), 1e-30)


def _nsa_sample_select_kernel(q_ref, a_ref, pe_ref, w1f_ref, b1_ref, w2_ref, b2_ref, biasc_ref, msel_ref,
                              regroup_ref, ocmp_ref, sel_ref, *, n_cmp, n_sel, past):
    pcs = []
    for kh in range(KV_HEADS):
        kc, vc = (_finish_compress(a_ref[0, c, kh], pe_ref[c], w1f_ref[c], b1_ref[c], w2_ref[c], b2_ref[c],
                                   n_cmp).astype(BF16) for c in range(2))
        bias = biasc_ref[kh]
        s = lax.dot_general(q_ref[0, kh].astype(BF16), kc, NT_DIMS, preferred_element_type=F32) * SCALE + bias
        pn = _masked_softmax_rows(s, bias > 0.5 * NEG)
        ocmp_ref[0, kh] = jnp.dot(pn.astype(BF16), vc, preferred_element_type=F32)
        pcs.append(((pn[0:8] + pn[8:16]) + pn[16:24]) + pn[24:32])
    pc = jnp.concatenate(pcs, axis=0)
    msel = msel_ref[...]
    p_slc = sum(jnp.dot(part, msel, preferred_element_type=F32) for part in _split3(pc))
    blk = lax.broadcasted_iota(jnp.int32, p_slc.shape, 1)
    cur = (past + lax.broadcasted_iota(jnp.int32, p_slc.shape, 0) % 8) // SEL_BLOCK
    sel = _top_n(_sel_scores(p_slc, blk, cur, n_sel), SEL_TOPN, 1).astype(BF16)
    for j in range(regroup_ref.shape[0]):
        part = jnp.dot(sel, regroup_ref[j], preferred_element_type=F32)
        for kh in range(KV_HEADS):
            sel_ref[0, kh, j] = part[kh * 8:(kh + 1) * 8]


def _nsa_sample_select(q, a_cmp, cmp_w, bias_c, msel, regroup, past, t_len):
    pe, w1f, b1, w2, b2 = cmp_w
    n = q.shape[0]
    chunks = a_cmp.shape[3]
    n_steps = regroup.shape[0]
    const = lambda shape: pl.BlockSpec(shape, lambda i: (0,) * len(shape))
    return pl.pallas_call(
        functools.partial(_nsa_sample_select_kernel, n_cmp=chunks - 1, n_sel=(past + t_len + SEL_BLOCK - 1) // SEL_BLOCK,
                          past=past),
        out_shape=(jax.ShapeDtypeStruct((n, KV_HEADS, GROUP * 8, HEAD_DIM), F32),
                   jax.ShapeDtypeStruct((n, KV_HEADS, n_steps, 8, BLK), F32)),
        grid=(n,),
        in_specs=[
            pl.BlockSpec((1, KV_HEADS, GROUP * 8, HEAD_DIM), lambda i: (i, 0, 0, 0)),
            pl.BlockSpec((1, 2, KV_HEADS, chunks, 2 * HEAD_DIM), lambda i: (i, 0, 0, 0, 0)),
            const(pe.shape), const(w1f.shape), const(b1.shape), const(w2.shape), const(b2.shape),
            const(bias_c.shape), const(msel.shape), const(regroup.shape),
        ],
        out_specs=(pl.BlockSpec((1, KV_HEADS, GROUP * 8, HEAD_DIM), lambda i: (i, 0, 0, 0)),
                   pl.BlockSpec((1, KV_HEADS, n_steps, 8, BLK), lambda i: (i, 0, 0, 0, 0))),
        compiler_params=_params(("parallel",)),
        name="nsa_sample_select",
    )(q, a_cmp, pe, w1f, b1, w2, b2, bias_c, msel, regroup)


def _nsa_sample_attend_kernel(pt_ref, *refs):
    pages = refs[:PAGES_PER_STEP]
    (q_ref, sel_ref, expand_ref, bias_ref, kn_ref, vn_ref, biasn_ref, win_ref, kwn_ref, vwn_ref, biasw_ref,
     biaswn_ref, gates_ref, ocmp_ref, o_ref, m_scr, l_scr, acc_scr) = refs[PAGES_PER_STEP:]
    j = pl.program_id(1)
    n_steps = pl.num_programs(1)

    @pl.when(j == 0)
    def _():
        m_scr[...] = jnp.full_like(m_scr, NEG)
        l_scr[...] = jnp.zeros_like(l_scr)
        acc_scr[...] = jnp.zeros_like(acc_scr)

    def picked(kh, step):
        sel = sel_ref[0, kh, step].astype(BF16)
        return jnp.concatenate([sel] * GROUP, axis=0)

    def update(kh, k, v, bias, mask):
        qs = q_ref[0, kh].astype(BF16)
        s = lax.dot_general(qs, k, NT_DIMS, preferred_element_type=F32) * SCALE + bias
        s = jnp.where(mask, s, NEG)
        m_old = m_scr[kh]
        m_new = jnp.maximum(m_old, jnp.max(s, axis=-1, keepdims=True))
        alpha = jnp.exp(m_old - m_new)
        p = jnp.where(mask, jnp.exp(s - m_new), 0.0)
        l_scr[kh] = alpha * l_scr[kh] + jnp.sum(p, axis=-1, keepdims=True)
        acc_scr[kh] = alpha * acc_scr[kh] + jnp.dot(p.astype(BF16), v, preferred_element_type=F32)
        m_scr[kh] = m_new

    tiles = [_heads_first(pg[...]) for pg in pages]
    expand = expand_ref[...]
    for kh in range(KV_HEADS):
        k = jnp.concatenate([tl[kh] for tl in tiles], axis=0).astype(BF16)
        v = jnp.concatenate([tl[KV_HEADS + kh] for tl in tiles], axis=0).astype(BF16)
        in_sel = jnp.dot(picked(kh, j), expand, preferred_element_type=F32) > 0.5
        update(kh, k, v, bias_ref[kh], in_sel)

    @pl.when(j == n_steps - 1)
    def _():
        win = _heads_first(win_ref[...])
        for kh in range(KV_HEADS):
            biasn = biasn_ref[kh]
            new_sel = picked(kh, n_steps)[:, 0:1] > 0.5
            update(kh, kn_ref[0, kh].astype(BF16), vn_ref[0, kh].astype(BF16), biasn, (biasn > 0.5 * NEG) & new_sel)
            o_sel = acc_scr[kh] / jnp.maximum(l_scr[kh], 1e-30)
            biasw = jnp.concatenate([biasw_ref[kh], biaswn_ref[kh]], axis=1)
            kw = jnp.concatenate([win[kh], kwn_ref[0, kh]], axis=0).astype(BF16)
            vw = jnp.concatenate([win[KV_HEADS + kh], vwn_ref[0, kh]], axis=0).astype(BF16)
            sw = lax.dot_general(q_ref[0, kh].astype(BF16), kw, NT_DIMS, preferred_element_type=F32) * SCALE + biasw
            pw = _masked_softmax_rows(sw, biasw > 0.5 * NEG)
            o_win = jnp.dot(pw.astype(BF16), vw, preferred_element_type=F32)
            o_ref[0, kh] = ((gates_ref[0, 0, kh] * ocmp_ref[0, kh] + gates_ref[1, 0, kh] * o_sel)
                            + gates_ref[2, 0, kh] * o_win)


def _nsa_sample_attend(pool, page_table, q, sel, expand, bias_sel, k_new, v_new, bias_new, win_buf, kw_new, vw_new,
                       bias_win, bias_win_new, gates, o_cmp):
    n, n_pages = page_table.shape
    n_steps = n_pages // PAGES_PER_STEP
    keys = PAGES_PER_STEP * BLK
    rows = GROUP * 8
    lw = win_buf.shape[1]
    per = lambda shape: pl.BlockSpec((1, KV_HEADS) + shape, lambda i, j, pt: (i, 0) + (0,) * len(shape))
    whole = lambda shape: pl.BlockSpec(shape, lambda i, j, pt: (0,) * len(shape))
    grid_spec = pltpu.PrefetchScalarGridSpec(
        num_scalar_prefetch=1,
        grid=(n, n_steps),
        in_specs=[_page_spec(p, 1) for p in range(PAGES_PER_STEP)] + [
            per((rows, HEAD_DIM)),
            per((n_steps + 1, 8, BLK)),
            whole(expand.shape),
            pl.BlockSpec((KV_HEADS, rows, keys), lambda i, j, pt: (0, 0, j)),
            per((BLK, HEAD_DIM)), per((BLK, HEAD_DIM)), whole((KV_HEADS, rows, BLK)),
            pl.BlockSpec((None, lw, HEAD_TILE, HEAD_DIM), lambda i, j, pt: (i, 0, 0, 0)),
            per((BLK, HEAD_DIM)), per((BLK, HEAD_DIM)), whole((KV_HEADS, rows, lw)), whole((KV_HEADS, rows, BLK)),
            pl.BlockSpec((3, 1, KV_HEADS, rows, HEAD_DIM), lambda i, j, pt: (0, i, 0, 0, 0)),
            per((rows, HEAD_DIM)),
        ],
        out_specs=per((rows, HEAD_DIM)),
        scratch_shapes=[pltpu.VMEM((KV_HEADS, rows, 1), F32), pltpu.VMEM((KV_HEADS, rows, 1), F32),
                        pltpu.VMEM((KV_HEADS, rows, HEAD_DIM), F32)],
    )
    return pl.pallas_call(
        _nsa_sample_attend_kernel,
        out_shape=jax.ShapeDtypeStruct((n, KV_HEADS, rows, HEAD_DIM), F32),
        grid_spec=grid_spec,
        compiler_params=_params(("parallel", "arbitrary")),
        name="nsa_sample_attend",
    )(page_table, *([pool] * PAGES_PER_STEP), q, sel, expand, bias_sel, k_new, v_new, bias_new, win_buf, kw_new,
      vw_new, bias_win, bias_win_new, gates, o_cmp)


def _sel_weights(n_cmp_rows, n_sel_cols):
    ratio = SEL_BLOCK // CMP_STRIDE
    span = CMP_BLOCK // CMP_STRIDE
    c = jnp.arange(n_cmp_rows)[:, None]
    j = jnp.arange(n_sel_cols)[None, :]
    o = c - ratio * j + (span - 1)
    cnt = jnp.minimum(o, span - 1) - jnp.maximum(o - (ratio - 1), 0) + 1
    return jnp.where((o >= 0) & (o <= ratio + span - 2), cnt, 0).astype(BF16)


def _layer_b(hp, hs, b, s, n, t, norm_g, w_in, w_out, rel_bias, cmp, pool, page_table, win_buf):
    cmp_pe, cmp_w1, cmp_b1, cmp_w2, cmp_b2 = cmp
    n_kvcol = 6 * NSA_KV
    w_main = w_in
    w_gate = jnp.pad(w_in[:, NSA_Q + n_kvcol:], ((0, 0), (0, BLK - 3 * N_HEADS))).astype(BF16)
    w1 = cmp_w1.reshape(2, 2, CMP_STRIDE, HEAD_DIM, HEAD_DIM)
    w1r = jnp.concatenate([w1[:, 0], w1[:, 1]], axis=-1).astype(BF16)
    pe_row = jnp.pad(cmp_pe.reshape(2, 1, CMP_BLOCK * HEAD_DIM), ((0, 0), (0, 7), (0, 0))).astype(BF16)
    cmp_w = (pe_row, cmp_w1.reshape(2, CMP_BLOCK * HEAD_DIM, HEAD_DIM).astype(BF16), cmp_b1.reshape(2, 1, HEAD_DIM),
             cmp_w2.astype(BF16), cmp_b2.reshape(2, 1, HEAD_DIM))
    weights = (w_main, w_gate, w_out, w1r, cmp_w)
    hs, new_win_s, new_kv_s, w_main, w_out = _nsa_sample_path(hs, n, t, norm_g, weights, rel_bias, pool, page_table,
                                                              win_buf)
    weights = (w_main, w_gate, w_out, w1r, cmp_w)
    hp, new_win_p, new_kv_p = _nsa_prompt_path(hp, b, s, norm_g, weights, rel_bias)
    return hp, hs, new_win_p, new_win_s, new_kv_p, new_kv_s


def _nsa_prompt_path(hp, b, s, norm_g, weights, rel_bias):
    w_main, w_gate, w_out, w1r, cmp_w = weights
    proj_p = _project(hp, w_main, gain=norm_g)
    gates_p = _project(hp, w_gate, gain=norm_g, sigmoid=True)
    col_tile = HEAD_TILE * HEAD_DIM
    new_kv = _vector_rows(proj_p, NSA_Q // col_tile, 4 * NSA_KV // col_tile, b, s, s)
    keep = min(NSA_WINDOW, s)
    new_win = _vector_rows(proj_p, (NSA_Q + 4 * NSA_KV) // col_tile, 2 * NSA_KV // col_tile, b, s, keep)
    table_p = jnp.arange(b * s // BLK, dtype=jnp.int32).reshape(b, s // BLK)
    a_cmp_p = _cmp_proj(new_kv.reshape(b * s, 2 * HEAD_TILE, HEAD_DIM), table_p, w1r)
    nq = s // BLK
    n_win = (NSA_WINDOW - 1 + BLK - 1) // BLK + 1
    tbl = _bias_table(rel_bias, nq + 1, BLK, BLK, -BLK, BLK, -1, 1, 0, 1 << 30)
    tbl_win = _bias_table(rel_bias, n_win + 1, BLK, BLK, -BLK, BLK, -1, 1, 0, NSA_WINDOW - 1)
    bias_c = _bias_table(rel_bias, 1, s // CMP_STRIDE, s, -(CMP_BLOCK - 1), 0, -CMP_STRIDE, 1, 0, 1 << 30)
    gates_t = gates_p[:, :3 * N_HEADS].reshape(b * s, 3, KV_HEADS, GROUP).transpose(2, 1, 3, 0)
    gates_t = jnp.pad(gates_t.reshape(KV_HEADS, 3 * GROUP, b * s), ((0, 0), (0, 16 - 3 * GROUP), (0, 0)))
    msel_p = _sel_weights(s // CMP_STRIDE, s // SEL_BLOCK).T
    o_t = _nsa_prompt(proj_p, a_cmp_p, gates_t, cmp_w, tbl, tbl_win, bias_c, msel_p, b, s)
    o_p = o_t.transpose(0, 3, 1, 2).reshape(b * s, NSA_Q).astype(BF16)
    hp = _project(o_p, w_out, res=hp)
    return (hp, new_win.reshape(b, keep, 2, KV_HEADS, HEAD_DIM), new_kv.reshape(b, s, 4, KV_HEADS, HEAD_DIM))


def _nsa_sample_path(hs, n, t, norm_g, weights, rel_bias, pool, page_table, win_buf):
    w_main, w_gate, w_out, w1r, cmp_w = weights
    past = page_table.shape[1] * BLK
    proj_s, w_main = _project(hs, w_main, n_cols=NSA_Q + 6 * NSA_KV, gain=norm_g, cast_w=True)
    gates_s = _project(hs, w_gate, gain=norm_g, sigmoid=True)
    kv_s = proj_s[:, NSA_Q:].reshape(n, t, 6, KV_HEADS, HEAD_DIM)
    rows = GROUP * 8

    def head_rows(x):
        x = jnp.pad(x.transpose(0, 2, 3, 1, 4), ((0, 0), (0, 0), (0, 0), (0, 8 - t), (0, 0)))
        return x.reshape(n, KV_HEADS, rows, x.shape[-1])

    def new_rows(c):
        return jnp.pad(kv_s[:, :, c].transpose(0, 2, 1, 3), ((0, 0), (0, 0), (0, BLK - t), (0, 0)))

    def head_table(x, cols):
        return x.reshape(KV_HEADS, rows, cols)

    q_s = head_rows(proj_s[:, :NSA_Q].reshape(n, t, KV_HEADS, GROUP, HEAD_DIM))
    pool2 = pool.reshape(pool.shape[0] * BLK, 2 * HEAD_TILE, HEAD_DIM)
    a_cmp_s = _cmp_proj(pool2, page_table, w1r)
    chunks = past // CMP_STRIDE
    big = 1 << 30
    bias_cs = head_table(_bias_table(rel_bias, 1, 8, chunks, past - (CMP_BLOCK - 1), 0, 1, -CMP_STRIDE, 0, big), chunks)
    n_steps = page_table.shape[1] // PAGES_PER_STEP
    n_sel_pad = (n_steps + 1) * BLK
    msel_s = _sel_weights(chunks, n_sel_pad)
    per_step = PAGES_PER_STEP * BLK // SEL_BLOCK
    jj = jnp.arange(n_sel_pad)[None, :, None]
    ll = jnp.arange(BLK)[None, None, :]
    st = jnp.arange(n_steps + 1)[:, None, None]
    regroup = ((jj == st * per_step + ll) & (ll < per_step)).astype(BF16)
    o_cmp_s, sel_s = _nsa_sample_select(q_s, a_cmp_s, cmp_w, bias_cs, msel_s, regroup, past, t)
    expand = (jnp.arange(BLK)[:, None] == jnp.arange(PAGES_PER_STEP * BLK)[None, :] // SEL_BLOCK).astype(BF16)
    bias_sel = head_table(_bias_table(rel_bias, 1, 8, past, past, 0, 1, -1, 0, big), past)
    bias_new = head_table(_bias_table(rel_bias, 1, 8, BLK, 0, 0, 1, -1, 0, big), BLK)
    lw = win_buf.shape[1]
    bias_win = head_table(_bias_table(rel_bias, 1, 8, lw, lw, 0, 1, -1, 0, NSA_WINDOW - 1), lw)
    bias_win_new = head_table(_bias_table(rel_bias, 1, 8, BLK, 0, 0, 1, -1, 0, NSA_WINDOW - 1), BLK)
    g_s = gates_s[:, :3 * N_HEADS].reshape(n, t, 3, KV_HEADS, GROUP, 1)
    g_s = jnp.stack([head_rows(g_s[:, :, c]) for c in range(3)])
    g_s = jnp.broadcast_to(g_s, (3, n, KV_HEADS, rows, HEAD_DIM))
    o_s = _nsa_sample_attend(pool2, page_table, q_s, sel_s, expand, bias_sel, new_rows(2), new_rows(3), bias_new,
                             win_buf.reshape(n, lw, HEAD_TILE, HEAD_DIM), new_rows(4), new_rows(5), bias_win,
                             bias_win_new,
                             g_s, o_cmp_s)
    o_s = o_s.reshape(n, KV_HEADS, GROUP, 8, HEAD_DIM)[:, :, :, :t].transpose(0, 3, 1, 2, 4)
    hs, w_out = _project(o_s.reshape(n * t, NSA_Q).astype(BF16), w_out, res=hs, cast_w=True)
    new_win_s = jnp.concatenate([win_buf, kv_s[:, :, 4:]], axis=1)[:, t:]
    return hs, new_win_s, kv_s[:, :, :4], w_main, w_out


def _ffn_and_ple(hp, hs, b, s, n, t, i, norm_ffn, norm_ple, w_ffn_in, conv_w, conv_b, w_ffn_out, state_conv,
                 p_prompt, p_sample, w_ple_gate, w_ple_proj):
    hu_s, w_ffn_in = _project(hs, w_ffn_in, layer=i, gain=norm_ffn, cast_w=True)
    hu_p = _project(hp, w_ffn_in, gain=norm_ffn)
    conv_p = hu_p.reshape(b, s, 2 * D_FF)[:, s - (CONV_W - 1):, :D_FF]
    conv_s = jnp.concatenate([state_conv, hu_s.reshape(n, t, 2 * D_FF)[:, :, :D_FF]], axis=1)[:, t:]
    hs, w_ffn_out = _ffn_out_sample(hu_s, state_conv, conv_w, conv_b, w_ffn_out, i, hs, t)
    hp = _ffn_out_prompt(hu_p, conv_w, conv_b, w_ffn_out, hp, s)
    hs, w_ple_gate, w_ple_proj = _ple_add(hs, norm_ple, w_ple_gate, p_sample.astype(BF16), w_ple_proj, layer=i,
                                          cast_w=True)
    hp = _ple_add(hp, norm_ple, w_ple_gate, p_prompt.astype(BF16), w_ple_proj)
    return hp, hs, conv_p, conv_s


def kernel(x_prompt, x_sample, state_dil_w128, state_dil_w512, state_dil_w2048, state_nsa_win, state_conv,
           cache_nsa_kv, page_table, p_prompt, p_sample, rel_bias, norm_mix, norm_ffn, norm_ple, norm_final,
           w_in_a, w_out_a, w_in_b, w_out_b, cmp_pe, cmp_w1, cmp_b1, cmp_w2, cmp_b2, w_ffn_in, conv_w, conv_b,
           w_ffn_out, w_ple_gate, w_ple_proj):
    b, s, d = x_prompt.shape
    n, t, _ = x_sample.shape
    depth = norm_mix.shape[0]
    hp, hs = x_prompt.reshape(b * s, d), x_sample.reshape(n * t, d)
    dil_p, dil_s = [[] for _ in range(N_DIL)], [[] for _ in range(N_DIL)]
    win_p, win_s, kv_p, kv_s, conv_p, conv_s = [], [], [], [], [], []
    for i in range(depth):
        li = i // 2
        if i % 2 == 0:
            hp, hs, new_p, new_s = _layer_a(
                hp, hs, b, s, n, t, norm_mix[i], w_in_a[li], w_out_a[li], rel_bias,
                (state_dil_w128[li], state_dil_w512[li], state_dil_w2048[li]))
            for g in range(N_DIL):
                dil_p[g].append(new_p[g])
                dil_s[g].append(new_s[g])
        else:
            hp, hs, wp, ws, rp, rs = _layer_b(
                hp, hs, b, s, n, t, norm_mix[i], w_in_b[li], w_out_b[li], rel_bias,
                (cmp_pe[li], cmp_w1[li], cmp_b1[li], cmp_w2[li], cmp_b2[li]), cache_nsa_kv[li], page_table,
                state_nsa_win[li])
            win_p.append(wp)
            win_s.append(ws)
            kv_p.append(rp)
            kv_s.append(rs)
        hp, hs, cp, cs = _ffn_and_ple(
            hp, hs, b, s, n, t, i, norm_ffn[i], norm_ple[i], w_ffn_in, conv_w[i], conv_b[i],
            w_ffn_out, state_conv[i], p_prompt[i].reshape(b * s, -1), p_sample[i].reshape(n * t, -1),
            w_ple_gate, w_ple_proj)
        conv_p.append(cp)
        conv_s.append(cs)
    y_prompt = _rmsnorm(hp, norm_final, F32).reshape(b, s, d)
    y_sample = _rmsnorm(hs, norm_final, F32).reshape(n, t, d)
    return (y_prompt, y_sample,
            jnp.stack(dil_p[0]), jnp.stack(dil_s[0]), jnp.stack(dil_p[1]), jnp.stack(dil_s[1]),
            jnp.stack(dil_p[2]), jnp.stack(dil_s[2]),
            jnp.stack(win_p), jnp.stack(win_s), jnp.stack(conv_p), jnp.stack(conv_s),
            jnp.stack(kv_p), jnp.stack(kv_s))
```

```python
import functools

import jax
import jax.numpy as jnp
from jax import lax
from jax.experimental import pallas as pl
from jax.experimental.pallas import tpu as pltpu

F32 = jnp.float32
BF16 = jnp.bfloat16

D_MODEL = 2048
HEAD_DIM = 128
N_HEADS = 16
DIL_PAIRS = ((128, 1), (512, 4), (2048, 16))
N_DIL = 3
BLK = 128
KV_HEADS = 4
GROUP = 4
CMP_BLOCK = 32
CMP_STRIDE = 16
SEL_BLOCK = 64
SEL_TOPN = 16
NSA_WINDOW = 512
D_FF = 5632
CONV_W = 3
REL_BUCKETS = 32
EPS = 1e-6
NEG = -1e30
FORCED_SCORE = 1e4
SCALE = HEAD_DIM ** -0.5
QKV_A = N_DIL * 3 * N_HEADS * HEAD_DIM
NSA_Q = N_HEADS * HEAD_DIM
NSA_KV = KV_HEADS * HEAD_DIM

BUCKET_START = (1, 2, 3, 4, 5, 6, 7, 8, 9, 10, 11, 12, 13, 14, 15, 16, 22, 30, 40, 54, 73, 99,
                134, 182, 246, 332, 450, 609, 825, 1117, 1513)

VMEM_LIMIT_V7X = 56 * 1024 * 1024


def _params(sem, vmem=VMEM_LIMIT_V7X):
    return pltpu.CompilerParams(dimension_semantics=sem, vmem_limit_bytes=vmem)


def _pick(n, cands):
    for c in cands:
        if n % c == 0:
            return c
    return n


def _rmsnorm_kernel(x_ref, g_ref, o_ref):
    x = x_ref[...]
    ms = jnp.mean(x * x, axis=-1, keepdims=True)
    o_ref[...] = ((x * lax.rsqrt(ms + EPS)) * g_ref[...]).astype(o_ref.dtype)


def _rmsnorm(x, g, out_dtype):
    m, d = x.shape
    tm = _pick(m, (512, 256, 128, 32))
    return pl.pallas_call(
        _rmsnorm_kernel,
        out_shape=jax.ShapeDtypeStruct((m, d), out_dtype),
        grid=(m // tm,),
        in_specs=[pl.BlockSpec((tm, d), lambda i: (i, 0)), pl.BlockSpec((1, d), lambda i: (0, 0))],
        out_specs=pl.BlockSpec((tm, d), lambda i: (i, 0)),
        compiler_params=_params(("parallel",)),
        name="rmsnorm",
    )(x, g.reshape(1, d))


def _normed(x, g):
    ms = jnp.mean(x * x, axis=-1, keepdims=True)
    return ((x * lax.rsqrt(ms + EPS)) * g).astype(BF16)


def _project_kernel(*refs, norm, cast_w, residual, sigmoid, heads):
    refs = list(refs)
    x_ref = refs.pop(0)
    g_ref = refs.pop(0) if norm else None
    w_ref = refs.pop(0)
    r_ref = refs.pop(0) if residual else None
    o_ref = refs.pop(0)
    wb_ref = refs.pop(0) if cast_w else None
    if norm:
        a_scr = refs.pop(0)

        @pl.when(pl.program_id(1) == 0)
        def _():
            a_scr[...] = _normed(x_ref[...], g_ref[...])

        a = a_scr[...]
    else:
        a = x_ref[...]
    w = w_ref[...]
    if cast_w:
        w = w.astype(BF16)
        wb_ref[...] = w
    acc = jnp.dot(a, w, preferred_element_type=F32)
    if sigmoid:
        acc = jax.nn.sigmoid(acc)
    if residual:
        acc = r_ref[...] + acc
    if heads:
        for j in range(o_ref.shape[0]):
            o_ref[j] = acc[:, j * HEAD_DIM:(j + 1) * HEAD_DIM]
    else:
        o_ref[...] = acc


def _weight_spec(w, layer, k, tn, index):
    if w.ndim == 2:
        return pl.BlockSpec((k, tn), index)
    return pl.BlockSpec((None, k, tn), lambda *g: (layer,) + index(*g))


def _project(x, w, *, layer=None, n_cols=None, gain=None, res=None, cast_w=False, sigmoid=False, heads=False):
    m, k = x.shape
    n = n_cols or w.shape[-1]
    tm = _pick(m, (1024, 512, 256, 128))
    tn = _pick(n, (1024, 512, 256, 128))
    assert not cast_w or m == tm
    norm, residual = gain is not None, res is not None
    args, in_specs = [x], [pl.BlockSpec((tm, k), lambda i, j: (i, 0))]
    if norm:
        args.append(gain.reshape(1, k))
        in_specs.append(pl.BlockSpec((1, k), lambda i, j: (0, 0)))
    args.append(w)
    in_specs.append(_weight_spec(w, layer, k, tn, lambda i, j: (0, j)))
    if residual:
        args.append(res)
        in_specs.append(pl.BlockSpec((tm, tn), lambda i, j: (i, j)))
    if heads:
        out_shape = [jax.ShapeDtypeStruct((n // HEAD_DIM, m, HEAD_DIM), F32)]
        out_specs = [pl.BlockSpec((tn // HEAD_DIM, tm, HEAD_DIM), lambda i, j: (j, i, 0))]
    else:
        out_shape = [jax.ShapeDtypeStruct((m, n), F32)]
        out_specs = [pl.BlockSpec((tm, tn), lambda i, j: (i, j))]
    if cast_w:
        out_shape.append(jax.ShapeDtypeStruct((k, n), BF16))
        out_specs.append(pl.BlockSpec((k, tn), lambda i, j: (0, j)))
    outs = pl.pallas_call(
        functools.partial(_project_kernel, norm=norm, cast_w=cast_w, residual=residual, sigmoid=sigmoid,
                          heads=heads),
        out_shape=out_shape,
        grid=(m // tm, n // tn),
        in_specs=in_specs,
        out_specs=out_specs,
        scratch_shapes=[pltpu.VMEM((tm, k), BF16)] if norm else [],
        compiler_params=_params(("parallel", "arbitrary")),
        name="project",
    )(*args)
    return tuple(outs) if cast_w else outs[0]


def _conv_gelu_val(g, g1, g2, val, cw_ref, cb_ref):
    c = cb_ref[...] + g2 * cw_ref[0:1, :]
    c = c + g1 * cw_ref[1:2, :]
    c = c + g * cw_ref[2:3, :]
    return (jax.nn.gelu(c) * val).astype(BF16)


def _ffn_out_prompt_kernel(g_ref, halo_ref, v_ref, cw_ref, cb_ref, w_ref, r_ref, o_ref, acc_ref, *,
                           tiles_per_seq):
    i, k = pl.program_id(0), pl.program_id(1)

    @pl.when(k == 0)
    def _():
        acc_ref[...] = jnp.zeros_like(acc_ref)

    g = g_ref[...]
    row = lax.broadcasted_iota(jnp.int32, g.shape, 0)
    halo = jnp.where(i % tiles_per_seq == 0, 0.0, halo_ref[...])
    g1 = jnp.where(row == 0, halo[7:8, :], pltpu.roll(g, 1, 0))
    g2 = jnp.where(row == 0, halo[6:7, :], jnp.where(row == 1, halo[7:8, :], pltpu.roll(g, 2, 0)))
    u = _conv_gelu_val(g, g1, g2, v_ref[...], cw_ref, cb_ref)
    acc_ref[...] += jnp.dot(u, w_ref[...], preferred_element_type=F32)

    @pl.when(k == pl.num_programs(1) - 1)
    def _():
        o_ref[...] = r_ref[...] + acc_ref[...]


def _ffn_out_prompt(hu, conv_w, conv_b, w_out, res, seq):
    m = hu.shape[0]
    tm, tk = 512, D_FF // 4
    nk = D_FF // tk
    return pl.pallas_call(
        functools.partial(_ffn_out_prompt_kernel, tiles_per_seq=seq // tm),
        out_shape=jax.ShapeDtypeStruct((m, D_MODEL), F32),
        grid=(m // tm, nk),
        in_specs=[
            pl.BlockSpec((tm, tk), lambda i, k: (i, k)),
            pl.BlockSpec((8, tk), lambda i, k: (jnp.maximum(i * (tm // 8) - 1, 0), k)),
            pl.BlockSpec((tm, tk), lambda i, k: (i, k + nk)),
            pl.BlockSpec((CONV_W, tk), lambda i, k: (0, k)),
            pl.BlockSpec((1, tk), lambda i, k: (0, k)),
            pl.BlockSpec((tk, D_MODEL), lambda i, k: (k, 0)),
            pl.BlockSpec((tm, D_MODEL), lambda i, k: (i, 0)),
        ],
        out_specs=pl.BlockSpec((tm, D_MODEL), lambda i, k: (i, 0)),
        scratch_shapes=[pltpu.VMEM((tm, D_MODEL), F32)],
        compiler_params=_params(("parallel", "arbitrary")),
        name="ffn_out_prompt",
    )(hu, hu, hu, conv_w, conv_b.reshape(1, D_FF), w_out, res)


def _ffn_out_sample_kernel(g_ref, e1_ref, e2_ref, v_ref, cw_ref, cb_ref, w_ref, r_ref, o_ref, wb_ref, acc_ref, *,
                           t_len):
    k = pl.program_id(0)

    @pl.when(k == 0)
    def _():
        acc_ref[...] = jnp.zeros_like(acc_ref)

    g = g_ref[...]
    t = lax.broadcasted_iota(jnp.int32, g.shape, 0) % t_len
    g1 = jnp.where(t == 0, e1_ref[...], pltpu.roll(g, 1, 0))
    g2 = jnp.where(t < 2, e2_ref[...], pltpu.roll(g, 2, 0))
    u = _conv_gelu_val(g, g1, g2, v_ref[...], cw_ref, cb_ref)
    w = w_ref[...].astype(BF16)
    wb_ref[...] = w
    acc_ref[...] += jnp.dot(u, w, preferred_element_type=F32)

    @pl.when(k == pl.num_programs(0) - 1)
    def _():
        o_ref[...] = r_ref[...] + acc_ref[...]


def _ffn_out_sample(hu, conv_prev, conv_w, conv_b, w_out, layer, res, t_len):
    m = hu.shape[0]
    n = m // t_len
    tk = 512
    nk = D_FF // tk
    zeros = jnp.zeros((n, t_len - 1, D_FF), F32)
    e1 = jnp.concatenate([conv_prev[:, 1:2], zeros], axis=1).reshape(m, D_FF)
    e2 = jnp.concatenate([conv_prev, zeros[:, 1:]], axis=1).reshape(m, D_FF)
    return pl.pallas_call(
        functools.partial(_ffn_out_sample_kernel, t_len=t_len),
        out_shape=(jax.ShapeDtypeStruct((m, D_MODEL), F32), jax.ShapeDtypeStruct((D_FF, D_MODEL), BF16)),
        grid=(nk,),
        in_specs=[
            pl.BlockSpec((m, tk), lambda k: (0, k)),
            pl.BlockSpec((m, tk), lambda k: (0, k)),
            pl.BlockSpec((m, tk), lambda k: (0, k)),
            pl.BlockSpec((m, tk), lambda k: (0, k + nk)),
            pl.BlockSpec((CONV_W, tk), lambda k: (0, k)),
            pl.BlockSpec((1, tk), lambda k: (0, k)),
            pl.BlockSpec((None, tk, D_MODEL), lambda k: (layer, k, 0)),
            pl.BlockSpec((m, D_MODEL), lambda k: (0, 0)),
        ],
        out_specs=(pl.BlockSpec((m, D_MODEL), lambda k: (0, 0)), pl.BlockSpec((tk, D_MODEL), lambda k: (k, 0))),
        scratch_shapes=[pltpu.VMEM((m, D_MODEL), F32)],
        compiler_params=_params(("arbitrary",)),
        name="ffn_out_sample",
    )(hu, e1, e2, hu, conv_w, conv_b.reshape(1, D_FF), w_out, res)


def _ple_kernel(h_ref, g_ref, wg_ref, p_ref, wp_ref, o_ref, *rest, cast_w):
    a_scr = rest[-1]
    j = pl.program_id(1)
    tn = o_ref.shape[1]

    @pl.when(j == 0)
    def _():
        a_scr[...] = _normed(h_ref[...], g_ref[...])

    wg, wp = wg_ref[...], wp_ref[...]
    if cast_w:
        wg, wp = wg.astype(BF16), wp.astype(BF16)
        rest[0][...] = wg
        rest[1][...] = wp
    gate = jax.nn.sigmoid(jnp.dot(a_scr[...], wg, preferred_element_type=F32))
    proj = jnp.dot(p_ref[...], wp, preferred_element_type=F32)
    o_ref[...] = h_ref[:, pl.ds(pl.multiple_of(j * tn, tn), tn)] + gate * proj


def _ple_add(h, gain, w_gate, p, w_proj, layer=None, cast_w=False):
    m, d = h.shape
    kp = p.shape[1]
    tm = _pick(m, (1024, 512, 256, 128))
    tn = 1024
    assert not cast_w or m == tm
    out_shape = [jax.ShapeDtypeStruct((m, d), F32)]
    out_specs = [pl.BlockSpec((tm, tn), lambda i, j: (i, j))]
    if cast_w:
        out_shape += [jax.ShapeDtypeStruct((d, d), BF16), jax.ShapeDtypeStruct((kp, d), BF16)]
        out_specs += [pl.BlockSpec((d, tn), lambda i, j: (0, j)), pl.BlockSpec((kp, tn), lambda i, j: (0, j))]
    outs = pl.pallas_call(
        functools.partial(_ple_kernel, cast_w=cast_w),
        out_shape=out_shape,
        grid=(m // tm, d // tn),
        in_specs=[
            pl.BlockSpec((tm, d), lambda i, j: (i, 0)),
            pl.BlockSpec((1, d), lambda i, j: (0, 0)),
            _weight_spec(w_gate, layer, d, tn, lambda i, j: (0, j)),
            pl.BlockSpec((tm, kp), lambda i, j: (i, 0)),
            _weight_spec(w_proj, layer, kp, tn, lambda i, j: (0, j)),
        ],
        out_specs=out_specs,
        scratch_shapes=[pltpu.VMEM((tm, d), BF16)],
        compiler_params=_params(("parallel", "arbitrary")),
        name="ple_add",
    )(h, gain.reshape(1, d), w_gate, p, w_proj)
    return tuple(outs) if cast_w else outs[0]


def _bias_table_kernel(rbt_ref, o_ref, *, a0, ag, ar, ac, lo, hi, mod):
    g = pl.program_id(0)
    n_r, n_c = o_ref.shape[2:]
    shape = (n_r, BLK)
    rows = ar * lax.broadcasted_iota(jnp.int32, shape, 0)
    cols = ac * lax.broadcasted_iota(jnp.int32, shape, 1)
    for c0 in range(0, n_c, BLK):
        dist = (a0 + ac * c0) + ag * g + rows + cols
        d = jnp.maximum(dist, 0)
        bucket = jnp.zeros(shape, jnp.int32)
        for start in BUCKET_START:
            bucket = bucket + (d >= start).astype(jnp.int32)
        ok = (dist >= lo) & (dist <= hi)
        if mod > 1:
            ok = ok & ((d & (mod - 1)) == 0)
        for h in range(N_HEADS):
            row = jnp.broadcast_to(rbt_ref[h:h + 1, :], shape)
            o_ref[h, 0, :, c0:c0 + BLK] = jnp.where(ok, jnp.take_along_axis(row, bucket, axis=1), NEG)


def _bias_table(rel_bias, n_g, n_r, n_c, a0, ag, ar, ac, lo, hi, mod=1):
    assert mod & (mod - 1) == 0 and n_c % BLK == 0
    rbt = jnp.pad(rel_bias.T, ((0, 0), (0, BLK - REL_BUCKETS)))
    return pl.pallas_call(
        functools.partial(_bias_table_kernel, a0=a0, ag=ag, ar=ar, ac=ac, lo=lo, hi=hi, mod=mod),
        out_shape=jax.ShapeDtypeStruct((N_HEADS, n_g, n_r, n_c), F32),
        grid=(n_g,),
        in_specs=[pl.BlockSpec((N_HEADS, BLK), lambda g: (0, 0))],
        out_specs=pl.BlockSpec((N_HEADS, 1, n_r, n_c), lambda g: (0, g, 0, 0)),
        compiler_params=_params(("parallel",)),
        name="bias_table",
    )(rbt)


def _dil_prompt_kernel(q_ref, kc_ref, kp_ref, vc_ref, vp_ref, bias_ref, o_ref, lse_ref, *, dil, hps):
    first = pl.program_id(1) == 0
    hb = pl.program_id(2)
    col = lax.broadcasted_iota(jnp.int32, (BLK, 2 * BLK), 1)
    edge = jnp.where(first & (col < BLK), NEG, 0.0)
    lane = lax.broadcasted_iota(jnp.int32, (BLK, BLK), 1)

    @pl.when(hb == 0)
    def _():
        lse_ref[...] = jnp.zeros_like(lse_ref)

    for r in range(dil):
        rows = pl.ds(r, BLK, stride=dil)
        lse_rows = lse_ref[rows, :]
        for hh in range(hps):
            q = q_ref[hh, rows, :].astype(BF16)
            k = jnp.concatenate([kp_ref[hh, rows, :], kc_ref[hh, rows, :]], axis=0).astype(BF16)
            v = jnp.concatenate([vp_ref[hh, rows, :], vc_ref[hh, rows, :]], axis=0).astype(BF16)
            h = hb * hps + hh
            s = lax.dot_general(q, k, (((1,), (1,)), ((), ())), preferred_element_type=F32)
            s = s * SCALE + bias_ref[h] + edge
            m = jnp.max(s, axis=-1, keepdims=True)
            p = jnp.exp(s - m)
            l = jnp.sum(p, axis=-1, keepdims=True)
            o_ref[hh, rows, :] = jnp.dot(p.astype(BF16), v, preferred_element_type=F32) / l
            lse_rows = jnp.where(lane == h, m + jnp.log(l), lse_rows)
        lse_ref[rows, :] = lse_rows


def _dil_prompt_group(qkv_hm, bias, grp, dil, b, s):
    span = BLK * dil
    nsp = s // span
    hps = N_HEADS // dil
    nhb = N_HEADS // hps

    def slab(part, prev):
        base = (grp * 3 + part) * N_HEADS // hps
        if prev:
            return lambda bi, sp, hb: (base + hb, bi * nsp + jnp.maximum(sp - 1, 0), 0)
        return lambda bi, sp, hb: (base + hb, bi * nsp + sp, 0)

    blk = (hps, span, HEAD_DIM)
    return pl.pallas_call(
        functools.partial(_dil_prompt_kernel, dil=dil, hps=hps),
        out_shape=(jax.ShapeDtypeStruct((N_HEADS, b * s, HEAD_DIM), F32),
                   jax.ShapeDtypeStruct((b * s, BLK), F32)),
        grid=(b, nsp, nhb),
        in_specs=[pl.BlockSpec(blk, slab(0, False)), pl.BlockSpec(blk, slab(1, False)),
                  pl.BlockSpec(blk, slab(1, True)), pl.BlockSpec(blk, slab(2, False)),
                  pl.BlockSpec(blk, slab(2, True)),
                  pl.BlockSpec((N_HEADS, BLK, 2 * BLK), lambda bi, sp, hb: (0, 0, 0))],
        out_specs=(pl.BlockSpec(blk, lambda bi, sp, hb: (hb, bi * nsp + sp, 0)),
                   pl.BlockSpec((span, BLK), lambda bi, sp, hb: (bi * nsp + sp, 0))),
        compiler_params=_params(("parallel", "parallel", "arbitrary")),
        name=f"dil_attn_prompt_g{grp}",
    )(qkv_hm, qkv_hm, qkv_hm, qkv_hm, qkv_hm, bias)


def _dil_combine_kernel(o0_ref, o1_ref, o2_ref, l0_ref, l1_ref, l2_ref, o_ref, *, head_major):
    l0, l1, l2 = l0_ref[...], l1_ref[...], l2_ref[...]
    mx = jnp.maximum(jnp.maximum(l0, l1), l2)
    e0, e1, e2 = jnp.exp(l0 - mx), jnp.exp(l1 - mx), jnp.exp(l2 - mx)
    den = e0 + e1 + e2
    w0, w1, w2 = e0 / den, e1 / den, e2 / den
    for h in range(N_HEADS):
        sl = slice(h * HEAD_DIM, (h + 1) * HEAD_DIM)
        g0, g1, g2 = ((r[h] for r in (o0_ref, o1_ref, o2_ref)) if head_major
                      else (r[:, sl] for r in (o0_ref, o1_ref, o2_ref)))
        o = (w0[:, h:h + 1] * g0 + w1[:, h:h + 1] * g1) + w2[:, h:h + 1] * g2
        o_ref[:, sl] = o.astype(o_ref.dtype)


def _dil_combine(outs, lses, head_major):
    m = lses[0].shape[0]
    wide = N_HEADS * HEAD_DIM
    tm = _pick(m, (256, 128, 32))
    ob = pl.BlockSpec((tm, wide), lambda i: (i, 0))
    ib = pl.BlockSpec((N_HEADS, tm, HEAD_DIM), lambda i: (0, i, 0)) if head_major else ob
    lb = pl.BlockSpec((tm, BLK), lambda i: (i, 0))
    return pl.pallas_call(
        functools.partial(_dil_combine_kernel, head_major=head_major),
        out_shape=jax.ShapeDtypeStruct((m, wide), BF16),
        grid=(m // tm,),
        in_specs=[ib, ib, ib, lb, lb, lb],
        out_specs=ob,
        compiler_params=_params(("parallel",)),
        name="dil_combine",
    )(*outs, *lses)


HEAD_TILE = 8


def _heads_first(x):
    return pltpu.einshape("mhd->hmd", x)


def _dil_sample_kernel(q_ref, k_ref, v_ref, kn_ref, vn_ref, bias_ref, biasn_ref, o_ref, lse_ref,
                       m_scr, l_scr, acc_scr):
    ht, c = pl.program_id(1), pl.program_id(2)
    k_all = _heads_first(k_ref[...])
    v_all = _heads_first(v_ref[...])

    @pl.when(c == 0)
    def _():
        m_scr[...] = jnp.full_like(m_scr, NEG)
        l_scr[...] = jnp.zeros_like(l_scr)
        acc_scr[...] = jnp.zeros_like(acc_scr)

    @pl.when((c == 0) & (ht == 0))
    def _():
        lse_ref[...] = jnp.zeros_like(lse_ref)

    def step(hh, k, v, bias):
        q = q_ref[0, :, hh * HEAD_DIM:(hh + 1) * HEAD_DIM].astype(BF16)
        s = lax.dot_general(q, k.astype(BF16), (((1,), (1,)), ((), ())), preferred_element_type=F32)
        s = s * SCALE + bias
        m_old = m_scr[hh]
        m_new = jnp.maximum(m_old, jnp.max(s, axis=-1, keepdims=True))
        alpha = jnp.exp(m_old - m_new)
        p = jnp.where(bias > 0.5 * NEG, jnp.exp(s - m_new), 0.0)
        l_scr[hh] = alpha * l_scr[hh] + jnp.sum(p, axis=-1, keepdims=True)
        acc_scr[hh] = alpha * acc_scr[hh] + jnp.dot(p.astype(BF16), v.astype(BF16), preferred_element_type=F32)
        m_scr[hh] = m_new

    for hh in range(HEAD_TILE):
        step(hh, k_all[hh], v_all[hh], bias_ref[hh, 0])

    @pl.when(c == pl.num_programs(2) - 1)
    def _():
        lane = lax.broadcasted_iota(jnp.int32, (8, BLK), 1)
        lse = lse_ref[0]
        for hh in range(HEAD_TILE):
            sl = slice(hh * HEAD_DIM, (hh + 1) * HEAD_DIM)
            step(hh, kn_ref[0, :, sl], vn_ref[0, :, sl], biasn_ref[hh])
            l = jnp.maximum(l_scr[hh], 1e-30)
            o_ref[0, :, sl] = acc_scr[hh] / l
            lse = jnp.where(lane == ht * HEAD_TILE + hh, m_scr[hh] + jnp.log(l), lse)
        lse_ref[0] = lse


def _dil_sample_group(q, kn, vn, buf, bias, bias_new, dil):
    n, lb = buf.shape[:2]
    n_cls = bias.shape[1]
    wide = N_HEADS * HEAD_DIM
    half = wide // 2
    rows = lb // dil
    tiles = 2 * N_HEADS // HEAD_TILE
    view = buf.reshape(n, rows, dil * tiles, HEAD_TILE, HEAD_DIM)
    nht = N_HEADS // HEAD_TILE
    return pl.pallas_call(
        _dil_sample_kernel,
        out_shape=(jax.ShapeDtypeStruct((n, 8, wide), F32), jax.ShapeDtypeStruct((n, 8, BLK), F32)),
        grid=(n, nht, n_cls),
        in_specs=[
            pl.BlockSpec((1, 8, half), lambda i, ht, c: (i, 0, ht)),
            pl.BlockSpec((None, rows, None, HEAD_TILE, HEAD_DIM), lambda i, ht, c: (i, 0, c * tiles + ht, 0, 0)),
            pl.BlockSpec((None, rows, None, HEAD_TILE, HEAD_DIM),
                         lambda i, ht, c: (i, 0, c * tiles + nht + ht, 0, 0)),
            pl.BlockSpec((1, BLK, half), lambda i, ht, c: (i, 0, ht)),
            pl.BlockSpec((1, BLK, half), lambda i, ht, c: (i, 0, ht)),
            pl.BlockSpec((HEAD_TILE, 1, 8, rows), lambda i, ht, c: (ht, c, 0, 0)),
            pl.BlockSpec((HEAD_TILE, 8, BLK), lambda i, ht, c: (ht, 0, 0)),
        ],
        out_specs=(pl.BlockSpec((1, 8, half), lambda i, ht, c: (i, 0, ht)),
                   pl.BlockSpec((1, 8, BLK), lambda i, ht, c: (i, 0, 0))),
        scratch_shapes=[pltpu.VMEM((HEAD_TILE, 8, 1), F32), pltpu.VMEM((HEAD_TILE, 8, 1), F32),
                        pltpu.VMEM((HEAD_TILE, 8, HEAD_DIM), F32)],
        compiler_params=_params(("parallel", "arbitrary", "arbitrary")),
        name="dil_attn_sample",
    )(q, view, view, kn, vn, bias, bias_new)


def _vector_rows_kernel(x_ref, o_ref):
    x = x_ref[...]
    if x.ndim == 2:
        x = jnp.stack([x[:, j * HEAD_DIM:(j + 1) * HEAD_DIM] for j in range(HEAD_TILE)])
    o_ref[...] = pltpu.einshape("hmd->mhd", x)


def _vector_rows(x, tile0, n_tiles, b, s, keep):
    tr = min(keep, 512)
    r0 = (s - keep) // tr
    per_seq = s // tr
    if x.ndim == 3:
        in_spec = pl.BlockSpec((HEAD_TILE, tr, HEAD_DIM), lambda bi, r, c: (tile0 + c, bi * per_seq + r0 + r, 0))
    else:
        in_spec = pl.BlockSpec((tr, HEAD_TILE * HEAD_DIM), lambda bi, r, c: (bi * per_seq + r0 + r, tile0 + c))
    return pl.pallas_call(
        _vector_rows_kernel,
        out_shape=jax.ShapeDtypeStruct((b, keep, n_tiles * HEAD_TILE, HEAD_DIM), F32),
        grid=(b, keep // tr, n_tiles),
        in_specs=[in_spec],
        out_specs=pl.BlockSpec((None, tr, HEAD_TILE, HEAD_DIM), lambda bi, r, c: (bi, r, c, 0)),
        compiler_params=_params(("parallel", "parallel", "parallel")),
        name="vector_rows",
    )(x)


def _pad_rows(x, rows):
    return jnp.pad(x, ((0, 0), (0, rows - x.shape[1]), (0, 0)))


def _layer_a(hp, hs, b, s, n, t, norm_g, w_in, w_out, rel_bias, bufs):
    wide = N_HEADS * HEAD_DIM
    qkv_s, w_in = _project(hs, w_in, gain=norm_g, cast_w=True)
    qkv_s = qkv_s.reshape(n, t, QKV_A)
    qkv_p = _project(hp, w_in, gain=norm_g, heads=True)
    outs_p, lses_p, outs_s, lses_s, new_p, new_s = [], [], [], [], [], []
    for grp, (win, dil) in enumerate(DIL_PAIRS):
        base = grp * 3 * wide
        bias = _bias_table(rel_bias, 1, BLK, 2 * BLK, BLK * dil, 0, dil, -dil, 0, win).reshape(N_HEADS, BLK, 2 * BLK)
        o, lse = _dil_prompt_group(qkv_p, bias, grp, dil, b, s)
        outs_p.append(o)
        lses_p.append(lse)
        keep = min(win, s)
        kv = _vector_rows(qkv_p, (grp * 3 + 1) * N_HEADS // HEAD_TILE, 2 * N_HEADS // HEAD_TILE, b, s, keep)
        new_p.append(kv.reshape(b, keep, 2, N_HEADS, HEAD_DIM))
        buf = bufs[grp]
        lb = buf.shape[1]
        n_cls = min(dil, t)
        bias_buf = _bias_table(rel_bias, n_cls, 8, lb // dil, lb, -1, 1, -dil, 0, win, dil)
        bias_new = _bias_table(rel_bias, 1, 8, BLK, 0, 0, 1, -1, 0, win, dil).reshape(N_HEADS, 8, BLK)
        q = _pad_rows(qkv_s[:, :, base:base + wide], 8)
        kn = _pad_rows(qkv_s[:, :, base + wide:base + 2 * wide], BLK)
        vn = _pad_rows(qkv_s[:, :, base + 2 * wide:base + 3 * wide], BLK)
        o, lse = _dil_sample_group(q, kn, vn, buf, bias_buf, bias_new, dil)
        outs_s.append(o[:, :t].reshape(n * t, wide))
        lses_s.append(lse[:, :t].reshape(n * t, BLK))
        kv_new = qkv_s[:, :, base + wide:base + 3 * wide].reshape(n, t, 2, N_HEADS, HEAD_DIM)
        new_s.append(jnp.concatenate([buf, kv_new], axis=1)[:, t:])
    hs, w_out = _project(_dil_combine(outs_s, lses_s, False), w_out, res=hs, cast_w=True)
    hp = _project(_dil_combine(outs_p, lses_p, True), w_out, res=hp)
    return hp, hs, new_p, new_s


PAGES_PER_STEP = 16
CHUNKS_PER_PAGE = BLK // CMP_STRIDE
NT_DIMS = (((1,), (1,)), ((), ()))
TN_DIMS = (((0,), (0,)), ((), ()))


def _page_spec(p, half):
    return pl.BlockSpec((BLK, HEAD_TILE, HEAD_DIM),
                        lambda i, j, *rest: (rest[-1][i, j * PAGES_PER_STEP + p], half, 0))


def _cmp_proj_kernel(pt_ref, *refs):
    pages = refs[:PAGES_PER_STEP]
    w_ref, o_ref = refs[PAGES_PER_STEP:PAGES_PER_STEP + 2]
    ys = [pltpu.einshape("ctgd->tgcd", pg[...].reshape(CHUNKS_PER_PAGE, CMP_STRIDE, HEAD_TILE, HEAD_DIM))
          for pg in pages]
    for c in range(2):
        lhs = jnp.concatenate(
            [jnp.concatenate([y[t, c * KV_HEADS + kh] for kh in range(KV_HEADS) for y in ys], axis=0)
             for t in range(CMP_STRIDE)], axis=1).astype(BF16)
        acc = jnp.dot(lhs, w_ref[c], preferred_element_type=F32)
        for kh in range(KV_HEADS):
            o_ref[0, c, kh] = acc[kh * BLK:(kh + 1) * BLK]


def _cmp_proj(pages, page_table, w1r):
    n, n_pages = page_table.shape
    chunks = n_pages * CHUNKS_PER_PAGE
    grid_spec = pltpu.PrefetchScalarGridSpec(
        num_scalar_prefetch=1,
        grid=(n, n_pages // PAGES_PER_STEP),
        in_specs=[_page_spec(p, 0) for p in range(PAGES_PER_STEP)]
        + [pl.BlockSpec((2, CMP_STRIDE * HEAD_DIM, 2 * HEAD_DIM), lambda i, j, pt: (0, 0, 0))],
        out_specs=pl.BlockSpec((1, 2, KV_HEADS, BLK, 2 * HEAD_DIM), lambda i, j, pt: (i, 0, 0, j, 0)),
    )
    return pl.pallas_call(
        _cmp_proj_kernel,
        out_shape=jax.ShapeDtypeStruct((n, 2, KV_HEADS, chunks, 2 * HEAD_DIM), F32),
        grid_spec=grid_spec,
        compiler_params=_params(("parallel", "arbitrary")),
        name="nsa_cmp_proj",
    )(page_table, *([pages] * PAGES_PER_STEP), w1r.reshape(2, CMP_STRIDE * HEAD_DIM, 2 * HEAD_DIM))


def _finish_compress(a, pe_row, w1f, b1, w2, b2, n_blocks):
    rows = a.shape[0]
    cst = jnp.dot(pe_row, w1f, preferred_element_type=F32)[0:1]
    h = (b1 + cst) + a[:, :HEAD_DIM] + pltpu.roll(a[:, HEAD_DIM:], rows - 1, 0)
    x = jnp.dot(jax.nn.gelu(h).astype(BF16), w2, preferred_element_type=F32) + b2
    return jnp.where(lax.broadcasted_iota(jnp.int32, x.shape, 0) < n_blocks, x, 0.0)


def _split3(x):
    hi = x.astype(BF16)
    r = x - hi.astype(F32)
    mid = r.astype(BF16)
    return hi, mid, (r - mid.astype(F32)).astype(BF16)


def _top_n(score, n, axis):
    idx = lax.broadcasted_iota(jnp.int32, score.shape, axis).astype(F32)
    big = float(score.shape[axis])

    def body(_, carry):
        sc, sel = carry
        mx = jnp.max(sc, axis=axis, keepdims=True)
        first = jnp.min(jnp.where(sc == mx, idx, big), axis=axis, keepdims=True)
        hit = idx == first
        return jnp.where(hit, -jnp.inf, sc), jnp.where(hit, 1.0, sel)

    return lax.fori_loop(0, n, body, (score, jnp.zeros(score.shape, F32)))[1]


def _sel_scores(p_slc, blk, cur, n_blocks):
    forced = (blk == 0) | (blk == cur) | (blk == cur - 1)
    score = jnp.where(forced, FORCED_SCORE, jnp.where(blk <= cur, p_slc, -1.0))
    return jnp.where(blk < n_blocks, score, -2.0)


def _nsa_prompt_kernel(q_ref, a_ref, pe_ref, w1f_ref, b1_ref, w2_ref, b2_ref, ksel_ref, vsel_ref, kwin_ref,
                       vwin_ref, tbl_ref, tblw_ref, biasc_ref, gates_ref, msel_ref, o_ref, kc_scr, vc_scr, sel_scr,
                       vselt_ref, vwint_ref, *, n_cmp, n_sel):
    i = pl.program_id(2)

    @pl.when(i == 0)
    def _():
        for c, scr in ((0, kc_scr), (1, vc_scr)):
            scr[...] = _finish_compress(a_ref[0, c, 0], pe_ref[c], w1f_ref[c], b1_ref[c], w2_ref[c], b2_ref[c],
                                        n_cmp).astype(BF16)
        for src, dst in ((vsel_ref, vselt_ref), (vwin_ref, vwint_ref)):
            for c in range(src.shape[0] // BLK):
                dst[:, c * BLK:(c + 1) * BLK] = src[c * BLK:(c + 1) * BLK, :].T.astype(BF16)

    q = q_ref[...]
    qs = jnp.concatenate([q[:, g * HEAD_DIM:(g + 1) * HEAD_DIM] for g in range(GROUP)], axis=0).astype(BF16)
    key_i = lax.broadcasted_iota(jnp.int32, (BLK, BLK), 0)
    tok_i = lax.broadcasted_iota(jnp.int32, (BLK, BLK), 1)

    def lanes4(x):
        return jnp.concatenate([x] * GROUP, axis=1)

    mask_c = lanes4(i * BLK + tok_i - (key_i * CMP_STRIDE + (CMP_BLOCK - 1)) >= 0)
    s = lax.dot_general(kc_scr[...], qs, NT_DIMS, preferred_element_type=F32) * SCALE
    s = jnp.where(mask_c, s + jnp.concatenate([biasc_ref[g, 0] for g in range(GROUP)], axis=1), NEG)
    m = jnp.max(s, axis=0, keepdims=True)
    p = jnp.where(mask_c, jnp.exp(s - m), 0.0)
    pn = p / jnp.maximum(jnp.sum(p, axis=0, keepdims=True), 1e-30)
    o_cmp = lax.dot_general(vc_scr[...], pn.astype(BF16), TN_DIMS, preferred_element_type=F32)
    pc = ((pn[:, 0:BLK] + pn[:, BLK:2 * BLK]) + pn[:, 2 * BLK:3 * BLK]) + pn[:, 3 * BLK:4 * BLK]

    msel = msel_ref[...]
    p_slc = sum(jnp.dot(msel, part, preferred_element_type=F32) for part in _split3(pc))
    blk = lax.broadcasted_iota(jnp.int32, p_slc.shape, 0)
    cur = (i * BLK + lax.broadcasted_iota(jnp.int32, p_slc.shape, 1)) // SEL_BLOCK
    sel_scr[...] = (1.0 - _top_n(_sel_scores(p_slc, blk, cur, n_sel), SEL_TOPN, 0)) * NEG

    def attend(k_ref, vt_ref, bias_ref, first_blk, n_blk, extra, carry):
        m_run, l_run, acc = carry
        off = pl.multiple_of(first_blk * BLK, BLK)
        k = k_ref[pl.ds(off, n_blk * BLK), :].astype(BF16)
        bias = []
        for j in range(n_blk):
            idx = jnp.maximum(i - first_blk - j, -1) + 1
            bias.append(jnp.concatenate([bias_ref[g, idx] for g in range(GROUP)], axis=1))
        s = lax.dot_general(k, qs, NT_DIMS, preferred_element_type=F32) * SCALE + jnp.concatenate(bias, axis=0)
        if extra is not None:
            s = s + extra
        m_new = jnp.maximum(m_run, jnp.max(s, axis=0, keepdims=True))
        alpha = jnp.exp(m_run - m_new)
        p = jnp.exp(s - m_new)
        l_new = alpha * l_run + jnp.sum(p, axis=0, keepdims=True)
        vt = vt_ref[:, pl.ds(off, n_blk * BLK)]
        return m_new, l_new, alpha * acc + jnp.dot(vt, p.astype(BF16), preferred_element_type=F32)

    init = (jnp.full((1, GROUP * BLK), NEG, F32), jnp.zeros((1, GROUP * BLK), F32),
            jnp.zeros((HEAD_DIM, GROUP * BLK), F32))

    sel_span = 4
    per_blk = BLK // SEL_BLOCK

    def sel_step(c, carry):
        rows = sel_scr[pl.ds(pl.multiple_of(c * sel_span * per_blk, 8), sel_span * per_blk), :]
        unpicked = jnp.concatenate([jnp.broadcast_to(rows[u:u + 1], (SEL_BLOCK, BLK))
                                    for u in range(sel_span * per_blk)], axis=0)
        return attend(ksel_ref, vselt_ref, tbl_ref, c * sel_span, sel_span, lanes4(unpicked), carry)

    _, l_sel, acc_sel = lax.fori_loop(0, i // sel_span + 1, sel_step, init)
    o_sel = acc_sel / jnp.maximum(l_sel, 1e-30)

    n_win = (NSA_WINDOW - 1 + BLK - 1) // BLK + 1
    _, l_win, acc_win = attend(kwin_ref, vwint_ref, tblw_ref, jnp.maximum(i - (n_win - 1), 0), n_win, None, init)
    o_win = acc_win / jnp.maximum(l_win, 1e-30)

    gt = gates_ref[0]

    def gate(branch):
        return jnp.concatenate([gt[branch * GROUP + g:branch * GROUP + g + 1, :] for g in range(GROUP)], axis=1)

    o = (gate(0) * o_cmp + gate(1) * o_sel) + gate(2) * o_win
    for g in range(GROUP):
        o_ref[0, g] = o[:, g * BLK:(g + 1) * BLK]


def _nsa_prompt(proj, a_cmp, gates_t, cmp_w, tbl, tbl_win, bias_c, msel, b, s):
    pe, w1f, b1, w2, b2 = cmp_w
    nq = s // BLK
    kcol = NSA_Q // HEAD_DIM
    const = lambda shape: pl.BlockSpec(shape, lambda bi, kh, i: (0,) * len(shape))
    return pl.pallas_call(
        functools.partial(_nsa_prompt_kernel, n_cmp=s // CMP_STRIDE - 1, n_sel=s // SEL_BLOCK),
        out_shape=jax.ShapeDtypeStruct((b, N_HEADS, HEAD_DIM, s), F32),
        grid=(b, KV_HEADS, nq),
        in_specs=[
            pl.BlockSpec((BLK, GROUP * HEAD_DIM), lambda bi, kh, i: (bi * nq + i, kh)),
            pl.BlockSpec((1, 2, 1, s // CMP_STRIDE, 2 * HEAD_DIM), lambda bi, kh, i: (bi, 0, kh, 0, 0)),
            const(pe.shape), const(w1f.shape), const(b1.shape), const(w2.shape), const(b2.shape),
            pl.BlockSpec((s, HEAD_DIM), lambda bi, kh, i: (bi, kcol + 2 * KV_HEADS + kh)),
            pl.BlockSpec((s, HEAD_DIM), lambda bi, kh, i: (bi, kcol + 3 * KV_HEADS + kh)),
            pl.BlockSpec((s, HEAD_DIM), lambda bi, kh, i: (bi, kcol + 4 * KV_HEADS + kh)),
            pl.BlockSpec((s, HEAD_DIM), lambda bi, kh, i: (bi, kcol + 5 * KV_HEADS + kh)),
            pl.BlockSpec((GROUP,) + tbl.shape[1:], lambda bi, kh, i: (kh, 0, 0, 0)),
            pl.BlockSpec((GROUP,) + tbl_win.shape[1:], lambda bi, kh, i: (kh, 0, 0, 0)),
            pl.BlockSpec((GROUP, 1, s // CMP_STRIDE, BLK), lambda bi, kh, i: (kh, 0, 0, i)),
            pl.BlockSpec((1, 16, BLK), lambda bi, kh, i: (kh, 0, bi * nq + i)),
            const(msel.shape),
        ],
        out_specs=pl.BlockSpec((1, GROUP, HEAD_DIM, BLK), lambda bi, kh, i: (bi, kh, 0, i)),
        scratch_shapes=[pltpu.VMEM((s // CMP_STRIDE, HEAD_DIM), BF16), pltpu.VMEM((s // CMP_STRIDE, HEAD_DIM), BF16),
                        pltpu.VMEM((s // SEL_BLOCK, BLK), F32),
                        pltpu.VMEM((HEAD_DIM, s), BF16), pltpu.VMEM((HEAD_DIM, s), BF16)],
        compiler_params=_params(("parallel", "parallel", "arbitrary")),
        name="nsa_prompt",
    )(proj, a_cmp, pe, w1f, b1, w2, b2, proj, proj, proj, proj, tbl, tbl_win, bias_c, gates_t, msel)


def _masked_softmax_rows(s, mask):
    s = jnp.where(mask, s, NEG)
    m = jnp.max(s, axis=-1, keepdims=True)
    p = jnp.where(mask, jnp.exp(s - m), 0.0)
    return p / jnp.maximum(jnp.sum(p, axis=-1, keepdims=True), 1e-30)


def _nsa_sample_select_kernel(q_ref, a_ref, pe_ref, w1f_ref, b1_ref, w2_ref, b2_ref, biasc_ref, msel_ref,
                              regroup_ref, ocmp_ref, sel_ref, *, n_cmp, n_sel, past):
    pcs = []
    for kh in range(KV_HEADS):
        kc, vc = (_finish_compress(a_ref[0, c, kh], pe_ref[c], w1f_ref[c], b1_ref[c], w2_ref[c], b2_ref[c],
                                   n_cmp).astype(BF16) for c in range(2))
        bias = biasc_ref[kh]
        s = lax.dot_general(q_ref[0, kh].astype(BF16), kc, NT_DIMS, preferred_element_type=F32) * SCALE + bias
        pn = _masked_softmax_rows(s, bias > 0.5 * NEG)
        ocmp_ref[0, kh] = jnp.dot(pn.astype(BF16), vc, preferred_element_type=F32)
        pcs.append(((pn[0:8] + pn[8:16]) + pn[16:24]) + pn[24:32])
    pc = jnp.concatenate(pcs, axis=0)
    msel = msel_ref[...]
    p_slc = sum(jnp.dot(part, msel, preferred_element_type=F32) for part in _split3(pc))
    blk = lax.broadcasted_iota(jnp.int32, p_slc.shape, 1)
    cur = (past + lax.broadcasted_iota(jnp.int32, p_slc.shape, 0) % 8) // SEL_BLOCK
    sel = _top_n(_sel_scores(p_slc, blk, cur, n_sel), SEL_TOPN, 1).astype(BF16)
    for j in range(regroup_ref.shape[0]):
        part = jnp.dot(sel, regroup_ref[j], preferred_element_type=F32)
        for kh in range(KV_HEADS):
            sel_ref[0, kh, j] = part[kh * 8:(kh + 1) * 8]


def _nsa_sample_select(q, a_cmp, cmp_w, bias_c, msel, regroup, past, t_len):
    pe, w1f, b1, w2, b2 = cmp_w
    n = q.shape[0]
    chunks = a_cmp.shape[3]
    n_steps = regroup.shape[0]
    const = lambda shape: pl.BlockSpec(shape, lambda i: (0,) * len(shape))
    return pl.pallas_call(
        functools.partial(_nsa_sample_select_kernel, n_cmp=chunks - 1, n_sel=(past + t_len + SEL_BLOCK - 1) // SEL_BLOCK,
                          past=past),
        out_shape=(jax.ShapeDtypeStruct((n, KV_HEADS, GROUP * 8, HEAD_DIM), F32),
                   jax.ShapeDtypeStruct((n, KV_HEADS, n_steps, 8, BLK), F32)),
        grid=(n,),
        in_specs=[
            pl.BlockSpec((1, KV_HEADS, GROUP * 8, HEAD_DIM), lambda i: (i, 0, 0, 0)),
            pl.BlockSpec((1, 2, KV_HEADS, chunks, 2 * HEAD_DIM), lambda i: (i, 0, 0, 0, 0)),
            const(pe.shape), const(w1f.shape), const(b1.shape), const(w2.shape), const(b2.shape),
            const(bias_c.shape), const(msel.shape), const(regroup.shape),
        ],
        out_specs=(pl.BlockSpec((1, KV_HEADS, GROUP * 8, HEAD_DIM), lambda i: (i, 0, 0, 0)),
                   pl.BlockSpec((1, KV_HEADS, n_steps, 8, BLK), lambda i: (i, 0, 0, 0, 0))),
        compiler_params=_params(("parallel",)),
        name="nsa_sample_select",
    )(q, a_cmp, pe, w1f, b1, w2, b2, bias_c, msel, regroup)


def _nsa_sample_attend_kernel(pt_ref, *refs):
    pages = refs[:PAGES_PER_STEP]
    (q_ref, sel_ref, expand_ref, bias_ref, kn_ref, vn_ref, biasn_ref, win_ref, kwn_ref, vwn_ref, biasw_ref,
     biaswn_ref, gates_ref, ocmp_ref, o_ref, m_scr, l_scr, acc_scr) = refs[PAGES_PER_STEP:]
    j = pl.program_id(1)
    n_steps = pl.num_programs(1)

    @pl.when(j == 0)
    def _():
        m_scr[...] = jnp.full_like(m_scr, NEG)
        l_scr[...] = jnp.zeros_like(l_scr)
        acc_scr[...] = jnp.zeros_like(acc_scr)

    def picked(kh, step):
        sel = sel_ref[0, kh, step].astype(BF16)
        return jnp.concatenate([sel] * GROUP, axis=0)

    def update(kh, k, v, bias, mask):
        qs = q_ref[0, kh].astype(BF16)
        s = lax.dot_general(qs, k, NT_DIMS, preferred_element_type=F32) * SCALE + bias
        s = jnp.where(mask, s, NEG)
        m_old = m_scr[kh]
        m_new = jnp.maximum(m_old, jnp.max(s, axis=-1, keepdims=True))
        alpha = jnp.exp(m_old - m_new)
        p = jnp.where(mask, jnp.exp(s - m_new), 0.0)
        l_scr[kh] = alpha * l_scr[kh] + jnp.sum(p, axis=-1, keepdims=True)
        acc_scr[kh] = alpha * acc_scr[kh] + jnp.dot(p.astype(BF16), v, preferred_element_type=F32)
        m_scr[kh] = m_new

    tiles = [_heads_first(pg[...]) for pg in pages]
    expand = expand_ref[...]
    for kh in range(KV_HEADS):
        k = jnp.concatenate([tl[kh] for tl in tiles], axis=0).astype(BF16)
        v = jnp.concatenate([tl[KV_HEADS + kh] for tl in tiles], axis=0).astype(BF16)
        in_sel = jnp.dot(picked(kh, j), expand, preferred_element_type=F32) > 0.5
        update(kh, k, v, bias_ref[kh], in_sel)

    @pl.when(j == n_steps - 1)
    def _():
        win = _heads_first(win_ref[...])
        for kh in range(KV_HEADS):
            biasn = biasn_ref[kh]
            new_sel = picked(kh, n_steps)[:, 0:1] > 0.5
            update(kh, kn_ref[0, kh].astype(BF16), vn_ref[0, kh].astype(BF16), biasn, (biasn > 0.5 * NEG) & new_sel)
            o_sel = acc_scr[kh] / jnp.maximum(l_scr[kh], 1e-30)
            biasw = jnp.concatenate([biasw_ref[kh], biaswn_ref[kh]], axis=1)
            kw = jnp.concatenate([win[kh], kwn_ref[0, kh]], axis=0).astype(BF16)
            vw = jnp.concatenate([win[KV_HEADS + kh], vwn_ref[0, kh]], axis=0).astype(BF16)
            sw = lax.dot_general(q_ref[0, kh].astype(BF16), kw, NT_DIMS, preferred_element_type=F32) * SCALE + biasw
            pw = _masked_softmax_rows(sw, biasw > 0.5 * NEG)
            o_win = jnp.dot(pw.astype(BF16), vw, preferred_element_type=F32)
            o_ref[0, kh] = ((gates_ref[0, 0, kh] * ocmp_ref[0, kh] + gates_ref[1, 0, kh] * o_sel)
                            + gates_ref[2, 0, kh] * o_win)


def _nsa_sample_attend(pool, page_table, q, sel, expand, bias_sel, k_new, v_new, bias_new, win_buf, kw_new, vw_new,
                       bias_win, bias_win_new, gates, o_cmp):
    n, n_pages = page_table.shape
    n_steps = n_pages // PAGES_PER_STEP
    keys = PAGES_PER_STEP * BLK
    rows = GROUP * 8
    lw = win_buf.shape[1]
    per = lambda shape: pl.BlockSpec((1, KV_HEADS) + shape, lambda i, j, pt: (i, 0) + (0,) * len(shape))
    whole = lambda shape: pl.BlockSpec(shape, lambda i, j, pt: (0,) * len(shape))
    grid_spec = pltpu.PrefetchScalarGridSpec(
        num_scalar_prefetch=1,
        grid=(n, n_steps),
        in_specs=[_page_spec(p, 1) for p in range(PAGES_PER_STEP)] + [
            per((rows, HEAD_DIM)),
            per((n_steps + 1, 8, BLK)),
            whole(expand.shape),
            pl.BlockSpec((KV_HEADS, rows, keys), lambda i, j, pt: (0, 0, j)),
            per((BLK, HEAD_DIM)), per((BLK, HEAD_DIM)), whole((KV_HEADS, rows, BLK)),
            pl.BlockSpec((None, lw, HEAD_TILE, HEAD_DIM), lambda i, j, pt: (i, 0, 0, 0)),
            per((BLK, HEAD_DIM)), per((BLK, HEAD_DIM)), whole((KV_HEADS, rows, lw)), whole((KV_HEADS, rows, BLK)),
            pl.BlockSpec((3, 1, KV_HEADS, rows, HEAD_DIM), lambda i, j, pt: (0, i, 0, 0, 0)),
            per((rows, HEAD_DIM)),
        ],
        out_specs=per((rows, HEAD_DIM)),
        scratch_shapes=[pltpu.VMEM((KV_HEADS, rows, 1), F32), pltpu.VMEM((KV_HEADS, rows, 1), F32),
                        pltpu.VMEM((KV_HEADS, rows, HEAD_DIM), F32)],
    )
    return pl.pallas_call(
        _nsa_sample_attend_kernel,
        out_shape=jax.ShapeDtypeStruct((n, KV_HEADS, rows, HEAD_DIM), F32),
        grid_spec=grid_spec,
        compiler_params=_params(("parallel", "arbitrary")),
        name="nsa_sample_attend",
    )(page_table, *([pool] * PAGES_PER_STEP), q, sel, expand, bias_sel, k_new, v_new, bias_new, win_buf, kw_new,
      vw_new, bias_win, bias_win_new, gates, o_cmp)


def _sel_weights(n_cmp_rows, n_sel_cols):
    ratio = SEL_BLOCK // CMP_STRIDE
    span = CMP_BLOCK // CMP_STRIDE
    c = jnp.arange(n_cmp_rows)[:, None]
    j = jnp.arange(n_sel_cols)[None, :]
    o = c - ratio * j + (span - 1)
    cnt = jnp.minimum(o, span - 1) - jnp.maximum(o - (ratio - 1), 0) + 1
    return jnp.where((o >= 0) & (o <= ratio + span - 2), cnt, 0).astype(BF16)


def _layer_b(hp, hs, b, s, n, t, norm_g, w_in, w_out, rel_bias, cmp, pool, page_table, win_buf):
    cmp_pe, cmp_w1, cmp_b1, cmp_w2, cmp_b2 = cmp
    n_kvcol = 6 * NSA_KV
    w_main = w_in
    w_gate = jnp.pad(w_in[:, NSA_Q + n_kvcol:], ((0, 0), (0, BLK - 3 * N_HEADS))).astype(BF16)
    w1 = cmp_w1.reshape(2, 2, CMP_STRIDE, HEAD_DIM, HEAD_DIM)
    w1r = jnp.concatenate([w1[:, 0], w1[:, 1]], axis=-1).astype(BF16)
    pe_row = jnp.pad(cmp_pe.reshape(2, 1, CMP_BLOCK * HEAD_DIM), ((0, 0), (0, 7), (0, 0))).astype(BF16)
    cmp_w = (pe_row, cmp_w1.reshape(2, CMP_BLOCK * HEAD_DIM, HEAD_DIM).astype(BF16), cmp_b1.reshape(2, 1, HEAD_DIM),
             cmp_w2.astype(BF16), cmp_b2.reshape(2, 1, HEAD_DIM))
    weights = (w_main, w_gate, w_out, w1r, cmp_w)
    hs, new_win_s, new_kv_s, w_main, w_out = _nsa_sample_path(hs, n, t, norm_g, weights, rel_bias, pool, page_table,
                                                              win_buf)
    weights = (w_main, w_gate, w_out, w1r, cmp_w)
    hp, new_win_p, new_kv_p = _nsa_prompt_path(hp, b, s, norm_g, weights, rel_bias)
    return hp, hs, new_win_p, new_win_s, new_kv_p, new_kv_s


def _nsa_prompt_path(hp, b, s, norm_g, weights, rel_bias):
    w_main, w_gate, w_out, w1r, cmp_w = weights
    proj_p = _project(hp, w_main, gain=norm_g)
    gates_p = _project(hp, w_gate, gain=norm_g, sigmoid=True)
    col_tile = HEAD_TILE * HEAD_DIM
    new_kv = _vector_rows(proj_p, NSA_Q // col_tile, 4 * NSA_KV // col_tile, b, s, s)
    keep = min(NSA_WINDOW, s)
    new_win = _vector_rows(proj_p, (NSA_Q + 4 * NSA_KV) // col_tile, 2 * NSA_KV // col_tile, b, s, keep)
    table_p = jnp.arange(b * s // BLK, dtype=jnp.int32).reshape(b, s // BLK)
    a_cmp_p = _cmp_proj(new_kv.reshape(b * s, 2 * HEAD_TILE, HEAD_DIM), table_p, w1r)
    nq = s // BLK
    n_win = (NSA_WINDOW - 1 + BLK - 1) // BLK + 1
    tbl = _bias_table(rel_bias, nq + 1, BLK, BLK, -BLK, BLK, -1, 1, 0, 1 << 30)
    tbl_win = _bias_table(rel_bias, n_win + 1, BLK, BLK, -BLK, BLK, -1, 1, 0, NSA_WINDOW - 1)
    bias_c = _bias_table(rel_bias, 1, s // CMP_STRIDE, s, -(CMP_BLOCK - 1), 0, -CMP_STRIDE, 1, 0, 1 << 30)
    gates_t = gates_p[:, :3 * N_HEADS].reshape(b * s, 3, KV_HEADS, GROUP).transpose(2, 1, 3, 0)
    gates_t = jnp.pad(gates_t.reshape(KV_HEADS, 3 * GROUP, b * s), ((0, 0), (0, 16 - 3 * GROUP), (0, 0)))
    msel_p = _sel_weights(s // CMP_STRIDE, s // SEL_BLOCK).T
    o_t = _nsa_prompt(proj_p, a_cmp_p, gates_t, cmp_w, tbl, tbl_win, bias_c, msel_p, b, s)
    o_p = o_t.transpose(0, 3, 1, 2).reshape(b * s, NSA_Q).astype(BF16)
    hp = _project(o_p, w_out, res=hp)
    return (hp, new_win.reshape(b, keep, 2, KV_HEADS, HEAD_DIM), new_kv.reshape(b, s, 4, KV_HEADS, HEAD_DIM))


def _nsa_sample_path(hs, n, t, norm_g, weights, rel_bias, pool, page_table, win_buf):
    w_main, w_gate, w_out, w1r, cmp_w = weights
    past = page_table.shape[1] * BLK
    proj_s, w_main = _project(hs, w_main, n_cols=NSA_Q + 6 * NSA_KV, gain=norm_g, cast_w=True)
    gates_s = _project(hs, w_gate, gain=norm_g, sigmoid=True)
    kv_s = proj_s[:, NSA_Q:].reshape(n, t, 6, KV_HEADS, HEAD_DIM)
    rows = GROUP * 8

    def head_rows(x):
        x = jnp.pad(x.transpose(0, 2, 3, 1, 4), ((0, 0), (0, 0), (0, 0), (0, 8 - t), (0, 0)))
        return x.reshape(n, KV_HEADS, rows, x.shape[-1])

    def new_rows(c):
        return jnp.pad(kv_s[:, :, c].transpose(0, 2, 1, 3), ((0, 0), (0, 0), (0, BLK - t), (0, 0)))

    def head_table(x, cols):
        return x.reshape(KV_HEADS, rows, cols)

    q_s = head_rows(proj_s[:, :NSA_Q].reshape(n, t, KV_HEADS, GROUP, HEAD_DIM))
    pool2 = pool.reshape(pool.shape[0] * BLK, 2 * HEAD_TILE, HEAD_DIM)
    a_cmp_s = _cmp_proj(pool2, page_table, w1r)
    chunks = past // CMP_STRIDE
    big = 1 << 30
    bias_cs = head_table(_bias_table(rel_bias, 1, 8, chunks, past - (CMP_BLOCK - 1), 0, 1, -CMP_STRIDE, 0, big), chunks)
    n_steps = page_table.shape[1] // PAGES_PER_STEP
    n_sel_pad = (n_steps + 1) * BLK
    msel_s = _sel_weights(chunks, n_sel_pad)
    per_step = PAGES_PER_STEP * BLK // SEL_BLOCK
    jj = jnp.arange(n_sel_pad)[None, :, None]
    ll = jnp.arange(BLK)[None, None, :]
    st = jnp.arange(n_steps + 1)[:, None, None]
    regroup = ((jj == st * per_step + ll) & (ll < per_step)).astype(BF16)
    o_cmp_s, sel_s = _nsa_sample_select(q_s, a_cmp_s, cmp_w, bias_cs, msel_s, regroup, past, t)
    expand = (jnp.arange(BLK)[:, None] == jnp.arange(PAGES_PER_STEP * BLK)[None, :] // SEL_BLOCK).astype(BF16)
    bias_sel = head_table(_bias_table(rel_bias, 1, 8, past, past, 0, 1, -1, 0, big), past)
    bias_new = head_table(_bias_table(rel_bias, 1, 8, BLK, 0, 0, 1, -1, 0, big), BLK)
    lw = win_buf.shape[1]
    bias_win = head_table(_bias_table(rel_bias, 1, 8, lw, lw, 0, 1, -1, 0, NSA_WINDOW - 1), lw)
    bias_win_new = head_table(_bias_table(rel_bias, 1, 8, BLK, 0, 0, 1, -1, 0, NSA_WINDOW - 1), BLK)
    g_s = gates_s[:, :3 * N_HEADS].reshape(n, t, 3, KV_HEADS, GROUP, 1)
    g_s = jnp.stack([head_rows(g_s[:, :, c]) for c in range(3)])
    g_s = jnp.broadcast_to(g_s, (3, n, KV_HEADS, rows, HEAD_DIM))
    o_s = _nsa_sample_attend(pool2, page_table, q_s, sel_s, expand, bias_sel, new_rows(2), new_rows(3), bias_new,
                             win_buf.reshape(n, lw, HEAD_TILE, HEAD_DIM), new_rows(4), new_rows(5), bias_win,
                             bias_win_new,
                             g_s, o_cmp_s)
    o_s = o_s.reshape(n, KV_HEADS, GROUP, 8, HEAD_DIM)[:, :, :, :t].transpose(0, 3, 1, 2, 4)
    hs, w_out = _project(o_s.reshape(n * t, NSA_Q).astype(BF16), w_out, res=hs, cast_w=True)
    new_win_s = jnp.concatenate([win_buf, kv_s[:, :, 4:]], axis=1)[:, t:]
    return hs, new_win_s, kv_s[:, :, :4], w_main, w_out


def _ffn_and_ple(hp, hs, b, s, n, t, i, norm_ffn, norm_ple, w_ffn_in, conv_w, conv_b, w_ffn_out, state_conv,
                 p_prompt, p_sample, w_ple_gate, w_ple_proj):
    hu_s, w_ffn_in = _project(hs, w_ffn_in, layer=i, gain=norm_ffn, cast_w=True)
    hu_p = _project(hp, w_ffn_in, gain=norm_ffn)
    conv_p = hu_p.reshape(b, s, 2 * D_FF)[:, s - (CONV_W - 1):, :D_FF]
    conv_s = jnp.concatenate([state_conv, hu_s.reshape(n, t, 2 * D_FF)[:, :, :D_FF]], axis=1)[:, t:]
    hs, w_ffn_out = _ffn_out_sample(hu_s, state_conv, conv_w, conv_b, w_ffn_out, i, hs, t)
    hp = _ffn_out_prompt(hu_p, conv_w, conv_b, w_ffn_out, hp, s)
    hs, w_ple_gate, w_ple_proj = _ple_add(hs, norm_ple, w_ple_gate, p_sample.astype(BF16), w_ple_proj, layer=i,
                                          cast_w=True)
    hp = _ple_add(hp, norm_ple, w_ple_gate, p_prompt.astype(BF16), w_ple_proj)
    return hp, hs, conv_p, conv_s


def kernel(x_prompt, x_sample, state_dil_w128, state_dil_w512, state_dil_w2048, state_nsa_win, state_conv,
           cache_nsa_kv, page_table, p_prompt, p_sample, rel_bias, norm_mix, norm_ffn, norm_ple, norm_final,
           w_in_a, w_out_a, w_in_b, w_out_b, cmp_pe, cmp_w1, cmp_b1, cmp_w2, cmp_b2, w_ffn_in, conv_w, conv_b,
           w_ffn_out, w_ple_gate, w_ple_proj):
    b, s, d = x_prompt.shape
    n, t, _ = x_sample.shape
    depth = norm_mix.shape[0]
    hp, hs = x_prompt.reshape(b * s, d), x_sample.reshape(n * t, d)
    dil_p, dil_s = [[] for _ in range(N_DIL)], [[] for _ in range(N_DIL)]
    win_p, win_s, kv_p, kv_s, conv_p, conv_s = [], [], [], [], [], []
    for i in range(depth):
        li = i // 2
        if i % 2 == 0:
            hp, hs, new_p, new_s = _layer_a(
                hp, hs, b, s, n, t, norm_mix[i], w_in_a[li], w_out_a[li], rel_bias,
                (state_dil_w128[li], state_dil_w512[li], state_dil_w2048[li]))
            for g in range(N_DIL):
                dil_p[g].append(new_p[g])
                dil_s[g].append(new_s[g])
        else:
            hp, hs, wp, ws, rp, rs = _layer_b(
                hp, hs, b, s, n, t, norm_mix[i], w_in_b[li], w_out_b[li], rel_bias,
                (cmp_pe[li], cmp_w1[li], cmp_b1[li], cmp_w2[li], cmp_b2[li]), cache_nsa_kv[li], page_table,
                state_nsa_win[li])
            win_p.append(wp)
            win_s.append(ws)
            kv_p.append(rp)
            kv_s.append(rs)
        hp, hs, cp, cs = _ffn_and_ple(
            hp, hs, b, s, n, t, i, norm_ffn[i], norm_ple[i], w_ffn_in, conv_w[i], conv_b[i],
            w_ffn_out, state_conv[i], p_prompt[i].reshape(b * s, -1), p_sample[i].reshape(n * t, -1),
            w_ple_gate, w_ple_proj)
        conv_p.append(cp)
        conv_s.append(cs)
    y_prompt = _rmsnorm(hp, norm_final, F32).reshape(b, s, d)
    y_sample = _rmsnorm(hs, norm_final, F32).reshape(n, t, d)
    return (y_prompt, y_sample,
            jnp.stack(dil_p[0]), jnp.stack(dil_s[0]), jnp.stack(dil_p[1]), jnp.stack(dil_s[1]),
            jnp.stack(dil_p[2]), jnp.stack(dil_s[2]),
            jnp.stack(win_p), jnp.stack(win_s), jnp.stack(conv_p), jnp.stack(conv_s),
            jnp.stack(kv_p), jnp.stack(kv_s))
```

```python
import functools

import jax
import jax.numpy as jnp
from jax import lax
from jax.experimental import pallas as pl
from jax.experimental.pallas import tpu as pltpu

F32 = jnp.float32
BF16 = jnp.bfloat16

D_MODEL = 2048
HEAD_DIM = 128
N_HEADS = 16
DIL_PAIRS = ((128, 1), (512, 4), (2048, 16))
N_DIL = 3
BLK = 128
KV_HEADS = 4
GROUP = 4
CMP_BLOCK = 32
CMP_STRIDE = 16
SEL_BLOCK = 64
SEL_TOPN = 16
NSA_WINDOW = 512
D_FF = 5632
CONV_W = 3
REL_BUCKETS = 32
EPS = 1e-6
NEG = -1e30
FORCED_SCORE = 1e4
SCALE = HEAD_DIM ** -0.5
QKV_A = N_DIL * 3 * N_HEADS * HEAD_DIM
NSA_Q = N_HEADS * HEAD_DIM
NSA_KV = KV_HEADS * HEAD_DIM

BUCKET_START = (1, 2, 3, 4, 5, 6, 7, 8, 9, 10, 11, 12, 13, 14, 15, 16, 22, 30, 40, 54, 73, 99,
                134, 182, 246, 332, 450, 609, 825, 1117, 1513)

VMEM_LIMIT_V7X = 56 * 1024 * 1024


def _params(sem, vmem=VMEM_LIMIT_V7X):
    return pltpu.CompilerParams(dimension_semantics=sem, vmem_limit_bytes=vmem)


def _pick(n, cands):
    for c in cands:
        if n % c == 0:
            return c
    return n


def _rmsnorm_kernel(x_ref, g_ref, o_ref):
    x = x_ref[...]
    ms = jnp.mean(x * x, axis=-1, keepdims=True)
    o_ref[...] = ((x * lax.rsqrt(ms + EPS)) * g_ref[...]).astype(o_ref.dtype)


def _rmsnorm(x, g, out_dtype):
    m, d = x.shape
    tm = _pick(m, (512, 256, 128, 32))
    return pl.pallas_call(
        _rmsnorm_kernel,
        out_shape=jax.ShapeDtypeStruct((m, d), out_dtype),
        grid=(m // tm,),
        in_specs=[pl.BlockSpec((tm, d), lambda i: (i, 0)), pl.BlockSpec((1, d), lambda i: (0, 0))],
        out_specs=pl.BlockSpec((tm, d), lambda i: (i, 0)),
        compiler_params=_params(("parallel",)),
        name="rmsnorm",
    )(x, g.reshape(1, d))


def _normed(x, g):
    ms = jnp.mean(x * x, axis=-1, keepdims=True)
    return ((x * lax.rsqrt(ms + EPS)) * g).astype(BF16)


def _project_kernel(*refs, norm, cast_w, residual, sigmoid, heads):
    refs = list(refs)
    x_ref = refs.pop(0)
    g_ref = refs.pop(0) if norm else None
    w_ref = refs.pop(0)
    r_ref = refs.pop(0) if residual else None
    o_ref = refs.pop(0)
    wb_ref = refs.pop(0) if cast_w else None
    if norm:
        a_scr = refs.pop(0)

        @pl.when(pl.program_id(1) == 0)
        def _():
            a_scr[...] = _normed(x_ref[...], g_ref[...])

        a = a_scr[...]
    else:
        a = x_ref[...]
    w = w_ref[...]
    if cast_w:
        w = w.astype(BF16)
        wb_ref[...] = w
    acc = jnp.dot(a, w, preferred_element_type=F32)
    if sigmoid:
        acc = jax.nn.sigmoid(acc)
    if residual:
        acc = r_ref[...] + acc
    if heads:
        for j in range(o_ref.shape[0]):
            o_ref[j] = acc[:, j * HEAD_DIM:(j + 1) * HEAD_DIM]
    else:
        o_ref[...] = acc


def _weight_spec(w, layer, k, tn, index):
    if w.ndim == 2:
        return pl.BlockSpec((k, tn), index)
    return pl.BlockSpec((None, k, tn), lambda *g: (layer,) + index(*g))


def _project(x, w, *, layer=None, n_cols=None, gain=None, res=None, cast_w=False, sigmoid=False, heads=False):
    m, k = x.shape
    n = n_cols or w.shape[-1]
    tm = _pick(m, (1024, 512, 256, 128))
    tn = _pick(n, (1024, 512, 256, 128))
    assert not cast_w or m == tm
    norm, residual = gain is not None, res is not None
    args, in_specs = [x], [pl.BlockSpec((tm, k), lambda i, j: (i, 0))]
    if norm:
        args.append(gain.reshape(1, k))
        in_specs.append(pl.BlockSpec((1, k), lambda i, j: (0, 0)))
    args.append(w)
    in_specs.append(_weight_spec(w, layer, k, tn, lambda i, j: (0, j)))
    if residual:
        args.append(res)
        in_specs.append(pl.BlockSpec((tm, tn), lambda i, j: (i, j)))
    if heads:
        out_shape = [jax.ShapeDtypeStruct((n // HEAD_DIM, m, HEAD_DIM), F32)]
        out_specs = [pl.BlockSpec((tn // HEAD_DIM, tm, HEAD_DIM), lambda i, j: (j, i, 0))]
    else:
        out_shape = [jax.ShapeDtypeStruct((m, n), F32)]
        out_specs = [pl.BlockSpec((tm, tn), lambda i, j: (i, j))]
    if cast_w:
        out_shape.append(jax.ShapeDtypeStruct((k, n), BF16))
        out_specs.append(pl.BlockSpec((k, tn), lambda i, j: (0, j)))
    outs = pl.pallas_call(
        functools.partial(_project_kernel, norm=norm, cast_w=cast_w, residual=residual, sigmoid=sigmoid,
                          heads=heads),
        out_shape=out_shape,
        grid=(m // tm, n // tn),
        in_specs=in_specs,
        out_specs=out_specs,
        scratch_shapes=[pltpu.VMEM((tm, k), BF16)] if norm else [],
        compiler_params=_params(("parallel", "arbitrary")),
        name="project",
    )(*args)
    return tuple(outs) if cast_w else outs[0]


def _conv_gelu_val(g, g1, g2, val, cw_ref, cb_ref):
    c = cb_ref[...] + g2 * cw_ref[0:1, :]
    c = c + g1 * cw_ref[1:2, :]
    c = c + g * cw_ref[2:3, :]
    return (jax.nn.gelu(c) * val).astype(BF16)


def _ffn_out_prompt_kernel(g_ref, halo_ref, v_ref, cw_ref, cb_ref, w_ref, r_ref, o_ref, acc_ref, *,
                           tiles_per_seq):
    i, k = pl.program_id(0), pl.program_id(1)

    @pl.when(k == 0)
    def _():
        acc_ref[...] = jnp.zeros_like(acc_ref)

    g = g_ref[...]
    row = lax.broadcasted_iota(jnp.int32, g.shape, 0)
    halo = jnp.where(i % tiles_per_seq == 0, 0.0, halo_ref[...])
    g1 = jnp.where(row == 0, halo[7:8, :], pltpu.roll(g, 1, 0))
    g2 = jnp.where(row == 0, halo[6:7, :], jnp.where(row == 1, halo[7:8, :], pltpu.roll(g, 2, 0)))
    u = _conv_gelu_val(g, g1, g2, v_ref[...], cw_ref, cb_ref)
    acc_ref[...] += jnp.dot(u, w_ref[...], preferred_element_type=F32)

    @pl.when(k == pl.num_programs(1) - 1)
    def _():
        o_ref[...] = r_ref[...] + acc_ref[...]


def _ffn_out_prompt(hu, conv_w, conv_b, w_out, res, seq):
    m = hu.shape[0]
    tm, tk = 512, D_FF // 4
    nk = D_FF // tk
    return pl.pallas_call(
        functools.partial(_ffn_out_prompt_kernel, tiles_per_seq=seq // tm),
        out_shape=jax.ShapeDtypeStruct((m, D_MODEL), F32),
        grid=(m // tm, nk),
        in_specs=[
            pl.BlockSpec((tm, tk), lambda i, k: (i, k)),
            pl.BlockSpec((8, tk), lambda i, k: (jnp.maximum(i * (tm // 8) - 1, 0), k)),
            pl.BlockSpec((tm, tk), lambda i, k: (i, k + nk)),
            pl.BlockSpec((CONV_W, tk), lambda i, k: (0, k)),
            pl.BlockSpec((1, tk), lambda i, k: (0, k)),
            pl.BlockSpec((tk, D_MODEL), lambda i, k: (k, 0)),
            pl.BlockSpec((tm, D_MODEL), lambda i, k: (i, 0)),
        ],
        out_specs=pl.BlockSpec((tm, D_MODEL), lambda i, k: (i, 0)),
        scratch_shapes=[pltpu.VMEM((tm, D_MODEL), F32)],
        compiler_params=_params(("parallel", "arbitrary")),
        name="ffn_out_prompt",
    )(hu, hu, hu, conv_w, conv_b.reshape(1, D_FF), w_out, res)


def _ffn_out_sample_kernel(g_ref, e1_ref, e2_ref, v_ref, cw_ref, cb_ref, w_ref, r_ref, o_ref, wb_ref, acc_ref, *,
                           t_len):
    k = pl.program_id(0)

    @pl.when(k == 0)
    def _():
        acc_ref[...] = jnp.zeros_like(acc_ref)

    g = g_ref[...]
    t = lax.broadcasted_iota(jnp.int32, g.shape, 0) % t_len
    g1 = jnp.where(t == 0, e1_ref[...], pltpu.roll(g, 1, 0))
    g2 = jnp.where(t < 2, e2_ref[...], pltpu.roll(g, 2, 0))
    u = _conv_gelu_val(g, g1, g2, v_ref[...], cw_ref, cb_ref)
    w = w_ref[...].astype(BF16)
    wb_ref[...] = w
    acc_ref[...] += jnp.dot(u, w, preferred_element_type=F32)

    @pl.when(k == pl.num_programs(0) - 1)
    def _():
        o_ref[...] = r_ref[...] + acc_ref[...]


def _ffn_out_sample(hu, conv_prev, conv_w, conv_b, w_out, layer, res, t_len):
    m = hu.shape[0]
    n = m // t_len
    tk = 512
    nk = D_FF // tk
    zeros = jnp.zeros((n, t_len - 1, D_FF), F32)
    e1 = jnp.concatenate([conv_prev[:, 1:2], zeros], axis=1).reshape(m, D_FF)
    e2 = jnp.concatenate([conv_prev, zeros[:, 1:]], axis=1).reshape(m, D_FF)
    return pl.pallas_call(
        functools.partial(_ffn_out_sample_kernel, t_len=t_len),
        out_shape=(jax.ShapeDtypeStruct((m, D_MODEL), F32), jax.ShapeDtypeStruct((D_FF, D_MODEL), BF16)),
        grid=(nk,),
        in_specs=[
            pl.BlockSpec((m, tk), lambda k: (0, k)),
            pl.BlockSpec((m, tk), lambda k: (0, k)),
            pl.BlockSpec((m, tk), lambda k: (0, k)),
            pl.BlockSpec((m, tk), lambda k: (0, k + nk)),
            pl.BlockSpec((CONV_W, tk), lambda k: (0, k)),
            pl.BlockSpec((1, tk), lambda k: (0, k)),
            pl.BlockSpec((None, tk, D_MODEL), lambda k: (layer, k, 0)),
            pl.BlockSpec((m, D_MODEL), lambda k: (0, 0)),
        ],
        out_specs=(pl.BlockSpec((m, D_MODEL), lambda k: (0, 0)), pl.BlockSpec((tk, D_MODEL), lambda k: (k, 0))),
        scratch_shapes=[pltpu.VMEM((m, D_MODEL), F32)],
        compiler_params=_params(("arbitrary",)),
        name="ffn_out_sample",
    )(hu, e1, e2, hu, conv_w, conv_b.reshape(1, D_FF), w_out, res)


def _ple_kernel(h_ref, g_ref, wg_ref, p_ref, wp_ref, o_ref, *rest, cast_w):
    a_scr = rest[-1]
    j = pl.program_id(1)
    tn = o_ref.shape[1]

    @pl.when(j == 0)
    def _():
        a_scr[...] = _normed(h_ref[...], g_ref[...])

    wg, wp = wg_ref[...], wp_ref[...]
    if cast_w:
        wg, wp = wg.astype(BF16), wp.astype(BF16)
        rest[0][...] = wg
        rest[1][...] = wp
    gate = jax.nn.sigmoid(jnp.dot(a_scr[...], wg, preferred_element_type=F32))
    proj = jnp.dot(p_ref[...], wp, preferred_element_type=F32)
    o_ref[...] = h_ref[:, pl.ds(pl.multiple_of(j * tn, tn), tn)] + gate * proj


def _ple_add(h, gain, w_gate, p, w_proj, layer=None, cast_w=False):
    m, d = h.shape
    kp = p.shape[1]
    tm = _pick(m, (1024, 512, 256, 128))
    tn = 1024
    assert not cast_w or m == tm
    out_shape = [jax.ShapeDtypeStruct((m, d), F32)]
    out_specs = [pl.BlockSpec((tm, tn), lambda i, j: (i, j))]
    if cast_w:
        out_shape += [jax.ShapeDtypeStruct((d, d), BF16), jax.ShapeDtypeStruct((kp, d), BF16)]
        out_specs += [pl.BlockSpec((d, tn), lambda i, j: (0, j)), pl.BlockSpec((kp, tn), lambda i, j: (0, j))]
    outs = pl.pallas_call(
        functools.partial(_ple_kernel, cast_w=cast_w),
        out_shape=out_shape,
        grid=(m // tm, d // tn),
        in_specs=[
            pl.BlockSpec((tm, d), lambda i, j: (i, 0)),
            pl.BlockSpec((1, d), lambda i, j: (0, 0)),
            _weight_spec(w_gate, layer, d, tn, lambda i, j: (0, j)),
            pl.BlockSpec((tm, kp), lambda i, j: (i, 0)),
            _weight_spec(w_proj, layer, kp, tn, lambda i, j: (0, j)),
        ],
        out_specs=out_specs,
        scratch_shapes=[pltpu.VMEM((tm, d), BF16)],
        compiler_params=_params(("parallel", "arbitrary")),
        name="ple_add",
    )(h, gain.reshape(1, d), w_gate, p, w_proj)
    return tuple(outs) if cast_w else outs[0]


def _bias_table_kernel(rbt_ref, o_ref, *, a0, ag, ar, ac, lo, hi, mod):
    g = pl.program_id(0)
    n_r, n_c = o_ref.shape[2:]
    shape = (n_r, BLK)
    rows = ar * lax.broadcasted_iota(jnp.int32, shape, 0)
    cols = ac * lax.broadcasted_iota(jnp.int32, shape, 1)
    for c0 in range(0, n_c, BLK):
        dist = (a0 + ac * c0) + ag * g + rows + cols
        d = jnp.maximum(dist, 0)
        bucket = jnp.zeros(shape, jnp.int32)
        for start in BUCKET_START:
            bucket = bucket + (d >= start).astype(jnp.int32)
        ok = (dist >= lo) & (dist <= hi)
        if mod > 1:
            ok = ok & ((d & (mod - 1)) == 0)
        for h in range(N_HEADS):
            row = jnp.broadcast_to(rbt_ref[h:h + 1, :], shape)
            o_ref[h, 0, :, c0:c0 + BLK] = jnp.where(ok, jnp.take_along_axis(row, bucket, axis=1), NEG)


def _bias_table(rel_bias, n_g, n_r, n_c, a0, ag, ar, ac, lo, hi, mod=1):
    assert mod & (mod - 1) == 0 and n_c % BLK == 0
    rbt = jnp.pad(rel_bias.T, ((0, 0), (0, BLK - REL_BUCKETS)))
    return pl.pallas_call(
        functools.partial(_bias_table_kernel, a0=a0, ag=ag, ar=ar, ac=ac, lo=lo, hi=hi, mod=mod),
        out_shape=jax.ShapeDtypeStruct((N_HEADS, n_g, n_r, n_c), F32),
        grid=(n_g,),
        in_specs=[pl.BlockSpec((N_HEADS, BLK), lambda g: (0, 0))],
        out_specs=pl.BlockSpec((N_HEADS, 1, n_r, n_c), lambda g: (0, g, 0, 0)),
        compiler_params=_params(("parallel",)),
        name="bias_table",
    )(rbt)


def _dil_prompt_kernel(q_ref, kc_ref, kp_ref, vc_ref, vp_ref, bias_ref, o_ref, lse_ref, *, dil, hps):
    first = pl.program_id(1) == 0
    hb = pl.program_id(2)
    col = lax.broadcasted_iota(jnp.int32, (BLK, 2 * BLK), 1)
    edge = jnp.where(first & (col < BLK), NEG, 0.0)
    lane = lax.broadcasted_iota(jnp.int32, (BLK, BLK), 1)

    @pl.when(hb == 0)
    def _():
        lse_ref[...] = jnp.zeros_like(lse_ref)

    for r in range(dil):
        rows = pl.ds(r, BLK, stride=dil)
        lse_rows = lse_ref[rows, :]
        for hh in range(hps):
            q = q_ref[hh, rows, :].astype(BF16)
            k = jnp.concatenate([kp_ref[hh, rows, :], kc_ref[hh, rows, :]], axis=0).astype(BF16)
            v = jnp.concatenate([vp_ref[hh, rows, :], vc_ref[hh, rows, :]], axis=0).astype(BF16)
            h = hb * hps + hh
            s = lax.dot_general(q, k, (((1,), (1,)), ((), ())), preferred_element_type=F32)
            s = s * SCALE + bias_ref[h] + edge
            m = jnp.max(s, axis=-1, keepdims=True)
            p = jnp.exp(s - m)
            l = jnp.sum(p, axis=-1, keepdims=True)
            o_ref[hh, rows, :] = jnp.dot(p.astype(BF16), v, preferred_element_type=F32) / l
            lse_rows = jnp.where(lane == h, m + jnp.log(l), lse_rows)
        lse_ref[rows, :] = lse_rows


def _dil_prompt_group(qkv_hm, bias, grp, dil, b, s):
    span = BLK * dil
    nsp = s // span
    hps = N_HEADS // dil
    nhb = N_HEADS // hps

    def slab(part, prev):
        base = (grp * 3 + part) * N_HEADS // hps
        if prev:
            return lambda bi, sp, hb: (base + hb, bi * nsp + jnp.maximum(sp - 1, 0), 0)
        return lambda bi, sp, hb: (base + hb, bi * nsp + sp, 0)

    blk = (hps, span, HEAD_DIM)
    return pl.pallas_call(
        functools.partial(_dil_prompt_kernel, dil=dil, hps=hps),
        out_shape=(jax.ShapeDtypeStruct((N_HEADS, b * s, HEAD_DIM), F32),
                   jax.ShapeDtypeStruct((b * s, BLK), F32)),
        grid=(b, nsp, nhb),
        in_specs=[pl.BlockSpec(blk, slab(0, False)), pl.BlockSpec(blk, slab(1, False)),
                  pl.BlockSpec(blk, slab(1, True)), pl.BlockSpec(blk, slab(2, False)),
                  pl.BlockSpec(blk, slab(2, True)),
                  pl.BlockSpec((N_HEADS, BLK, 2 * BLK), lambda bi, sp, hb: (0, 0, 0))],
        out_specs=(pl.BlockSpec(blk, lambda bi, sp, hb: (hb, bi * nsp + sp, 0)),
                   pl.BlockSpec((span, BLK), lambda bi, sp, hb: (bi * nsp + sp, 0))),
        compiler_params=_params(("parallel", "parallel", "arbitrary")),
        name=f"dil_attn_prompt_g{grp}",
    )(qkv_hm, qkv_hm, qkv_hm, qkv_hm, qkv_hm, bias)


def _dil_combine_kernel(o0_ref, o1_ref, o2_ref, l0_ref, l1_ref, l2_ref, o_ref, *, head_major):
    l0, l1, l2 = l0_ref[...], l1_ref[...], l2_ref[...]
    mx = jnp.maximum(jnp.maximum(l0, l1), l2)
    e0, e1, e2 = jnp.exp(l0 - mx), jnp.exp(l1 - mx), jnp.exp(l2 - mx)
    den = e0 + e1 + e2
    w0, w1, w2 = e0 / den, e1 / den, e2 / den
    for h in range(N_HEADS):
        sl = slice(h * HEAD_DIM, (h + 1) * HEAD_DIM)
        g0, g1, g2 = ((r[h] for r in (o0_ref, o1_ref, o2_ref)) if head_major
                      else (r[:, sl] for r in (o0_ref, o1_ref, o2_ref)))
        o = (w0[:, h:h + 1] * g0 + w1[:, h:h + 1] * g1) + w2[:, h:h + 1] * g2
        o_ref[:, sl] = o.astype(o_ref.dtype)


def _dil_combine(outs, lses, head_major):
    m = lses[0].shape[0]
    wide = N_HEADS * HEAD_DIM
    tm = _pick(m, (256, 128, 32))
    ob = pl.BlockSpec((tm, wide), lambda i: (i, 0))
    ib = pl.BlockSpec((N_HEADS, tm, HEAD_DIM), lambda i: (0, i, 0)) if head_major else ob
    lb = pl.BlockSpec((tm, BLK), lambda i: (i, 0))
    return pl.pallas_call(
        functools.partial(_dil_combine_kernel, head_major=head_major),
        out_shape=jax.ShapeDtypeStruct((m, wide), BF16),
        grid=(m // tm,),
        in_specs=[ib, ib, ib, lb, lb, lb],
        out_specs=ob,
        compiler_params=_params(("parallel",)),
        name="dil_combine",
    )(*outs, *lses)


HEAD_TILE = 8


def _heads_first(x):
    return pltpu.einshape("mhd->hmd", x)


def _dil_sample_kernel(q_ref, k_ref, v_ref, kn_ref, vn_ref, bias_ref, biasn_ref, o_ref, lse_ref,
                       m_scr, l_scr, acc_scr):
    ht, c = pl.program_id(1), pl.program_id(2)
    k_all = _heads_first(k_ref[...])
    v_all = _heads_first(v_ref[...])

    @pl.when(c == 0)
    def _():
        m_scr[...] = jnp.full_like(m_scr, NEG)
        l_scr[...] = jnp.zeros_like(l_scr)
        acc_scr[...] = jnp.zeros_like(acc_scr)

    @pl.when((c == 0) & (ht == 0))
    def _():
        lse_ref[...] = jnp.zeros_like(lse_ref)

    def update(ks, vs, bias):
        s = jnp.concatenate(
            [lax.dot_general(q_ref[0, :, hh * HEAD_DIM:(hh + 1) * HEAD_DIM].astype(BF16), ks[hh].astype(BF16),
                             (((1,), (1,)), ((), ())), preferred_element_type=F32) for hh in range(HEAD_TILE)], axis=0)
        s = s * SCALE + bias
        m_old = m_scr[...]
        m_new = jnp.maximum(m_old, jnp.max(s, axis=-1, keepdims=True))
        alpha = jnp.exp(m_old - m_new)
        p = jnp.where(bias > 0.5 * NEG, jnp.exp(s - m_new), 0.0)
        l_scr[...] = alpha * l_scr[...] + jnp.sum(p, axis=-1, keepdims=True)
        pv = jnp.concatenate(
            [jnp.dot(p[hh * 8:(hh + 1) * 8].astype(BF16), vs[hh].astype(BF16), preferred_element_type=F32)
             for hh in range(HEAD_TILE)], axis=0)
        acc_scr[...] = alpha * acc_scr[...] + pv
        m_scr[...] = m_new

    n_keys = k_ref.shape[0]
    update(k_all, v_all, bias_ref[:, 0].reshape(HEAD_TILE * 8, n_keys))

    @pl.when(c == pl.num_programs(2) - 1)
    def _():
        cols = [slice(hh * HEAD_DIM, (hh + 1) * HEAD_DIM) for hh in range(HEAD_TILE)]
        update([kn_ref[0, :, sl] for sl in cols], [vn_ref[0, :, sl] for sl in cols],
               biasn_ref[...].reshape(HEAD_TILE * 8, BLK))
        l = jnp.maximum(l_scr[...], 1e-30)
        o = acc_scr[...] / l
        lse_rows = m_scr[...] + jnp.log(l)
        lane = lax.broadcasted_iota(jnp.int32, (8, BLK), 1)
        lse = lse_ref[0]
        for hh in range(HEAD_TILE):
            o_ref[0, :, cols[hh]] = o[hh * 8:(hh + 1) * 8]
            lse = jnp.where(lane == ht * HEAD_TILE + hh, lse_rows[hh * 8:(hh + 1) * 8], lse)
        lse_ref[0] = lse


def _dil_sample_group(q, kn, vn, buf, bias, bias_new, dil):
    n, lb = buf.shape[:2]
    n_cls = bias.shape[1]
    wide = N_HEADS * HEAD_DIM
    half = wide // 2
    rows = lb // dil
    tiles = 2 * N_HEADS // HEAD_TILE
    view = buf.reshape(n, rows, dil * tiles, HEAD_TILE, HEAD_DIM)
    nht = N_HEADS // HEAD_TILE
    return pl.pallas_call(
        _dil_sample_kernel,
        out_shape=(jax.ShapeDtypeStruct((n, 8, wide), F32), jax.ShapeDtypeStruct((n, 8, BLK), F32)),
        grid=(n, nht, n_cls),
        in_specs=[
            pl.BlockSpec((1, 8, half), lambda i, ht, c: (i, 0, ht)),
            pl.BlockSpec((None, rows, None, HEAD_TILE, HEAD_DIM), lambda i, ht, c: (i, 0, c * tiles + ht, 0, 0)),
            pl.BlockSpec((None, rows, None, HEAD_TILE, HEAD_DIM),
                         lambda i, ht, c: (i, 0, c * tiles + nht + ht, 0, 0)),
            pl.BlockSpec((1, BLK, half), lambda i, ht, c: (i, 0, ht)),
            pl.BlockSpec((1, BLK, half), lambda i, ht, c: (i, 0, ht)),
            pl.BlockSpec((HEAD_TILE, 1, 8, rows), lambda i, ht, c: (ht, c, 0, 0)),
            pl.BlockSpec((HEAD_TILE, 8, BLK), lambda i, ht, c: (ht, 0, 0)),
        ],
        out_specs=(pl.BlockSpec((1, 8, half), lambda i, ht, c: (i, 0, ht)),
                   pl.BlockSpec((1, 8, BLK), lambda i, ht, c: (i, 0, 0))),
        scratch_shapes=[pltpu.VMEM((HEAD_TILE * 8, 1), F32), pltpu.VMEM((HEAD_TILE * 8, 1), F32),
                        pltpu.VMEM((HEAD_TILE * 8, HEAD_DIM), F32)],
        compiler_params=_params(("parallel", "arbitrary", "arbitrary")),
        name="dil_attn_sample",
    )(q, view, view, kn, vn, bias, bias_new)


def _vector_rows_kernel(x_ref, o_ref):
    x = x_ref[...]
    if x.ndim == 2:
        x = jnp.stack([x[:, j * HEAD_DIM:(j + 1) * HEAD_DIM] for j in range(HEAD_TILE)])
    o_ref[...] = pltpu.einshape("hmd->mhd", x)


def _vector_rows(x, tile0, n_tiles, b, s, keep):
    tr = min(keep, 512)
    r0 = (s - keep) // tr
    per_seq = s // tr
    if x.ndim == 3:
        in_spec = pl.BlockSpec((HEAD_TILE, tr, HEAD_DIM), lambda bi, r, c: (tile0 + c, bi * per_seq + r0 + r, 0))
    else:
        in_spec = pl.BlockSpec((tr, HEAD_TILE * HEAD_DIM), lambda bi, r, c: (bi * per_seq + r0 + r, tile0 + c))
    return pl.pallas_call(
        _vector_rows_kernel,
        out_shape=jax.ShapeDtypeStruct((b, keep, n_tiles * HEAD_TILE, HEAD_DIM), F32),
        grid=(b, keep // tr, n_tiles),
        in_specs=[in_spec],
        out_specs=pl.BlockSpec((None, tr, HEAD_TILE, HEAD_DIM), lambda bi, r, c: (bi, r, c, 0)),
        compiler_params=_params(("parallel", "parallel", "parallel")),
        name="vector_rows",
    )(x)


def _pad_rows(x, rows):
    return jnp.pad(x, ((0, 0), (0, rows - x.shape[1]), (0, 0)))


def _layer_a(hp, hs, b, s, n, t, norm_g, w_in, w_out, rel_bias, bufs):
    wide = N_HEADS * HEAD_DIM
    qkv_s, w_in = _project(hs, w_in, gain=norm_g, cast_w=True)
    qkv_s = qkv_s.reshape(n, t, QKV_A)
    qkv_p = _project(hp, w_in, gain=norm_g, heads=True)
    outs_p, lses_p, outs_s, lses_s, new_p, new_s = [], [], [], [], [], []
    for grp, (win, dil) in enumerate(DIL_PAIRS):
        base = grp * 3 * wide
        bias = _bias_table(rel_bias, 1, BLK, 2 * BLK, BLK * dil, 0, dil, -dil, 0, win).reshape(N_HEADS, BLK, 2 * BLK)
        o, lse = _dil_prompt_group(qkv_p, bias, grp, dil, b, s)
        outs_p.append(o)
        lses_p.append(lse)
        keep = min(win, s)
        kv = _vector_rows(qkv_p, (grp * 3 + 1) * N_HEADS // HEAD_TILE, 2 * N_HEADS // HEAD_TILE, b, s, keep)
        new_p.append(kv.reshape(b, keep, 2, N_HEADS, HEAD_DIM))
        buf = bufs[grp]
        lb = buf.shape[1]
        n_cls = min(dil, t)
        bias_buf = _bias_table(rel_bias, n_cls, 8, lb // dil, lb, -1, 1, -dil, 0, win, dil)
        bias_new = _bias_table(rel_bias, 1, 8, BLK, 0, 0, 1, -1, 0, win, dil).reshape(N_HEADS, 8, BLK)
        q = _pad_rows(qkv_s[:, :, base:base + wide], 8)
        kn = _pad_rows(qkv_s[:, :, base + wide:base + 2 * wide], BLK)
        vn = _pad_rows(qkv_s[:, :, base + 2 * wide:base + 3 * wide], BLK)
        o, lse = _dil_sample_group(q, kn, vn, buf, bias_buf, bias_new, dil)
        outs_s.append(o[:, :t].reshape(n * t, wide))
        lses_s.append(lse[:, :t].reshape(n * t, BLK))
        kv_new = qkv_s[:, :, base + wide:base + 3 * wide].reshape(n, t, 2, N_HEADS, HEAD_DIM)
        new_s.append(jnp.concatenate([buf, kv_new], axis=1)[:, t:])
    hs, w_out = _project(_dil_combine(outs_s, lses_s, False), w_out, res=hs, cast_w=True)
    hp = _project(_dil_combine(outs_p, lses_p, True), w_out, res=hp)
    return hp, hs, new_p, new_s


PAGES_PER_STEP = 16
CHUNKS_PER_PAGE = BLK // CMP_STRIDE
NT_DIMS = (((1,), (1,)), ((), ()))
TN_DIMS = (((0,), (0,)), ((), ()))


def _page_spec(p, half):
    return pl.BlockSpec((BLK, HEAD_TILE, HEAD_DIM),
                        lambda i, j, *rest: (rest[-1][i, j * PAGES_PER_STEP + p], half, 0))


def _cmp_proj_kernel(pt_ref, *refs):
    pages = refs[:PAGES_PER_STEP]
    w_ref, o_ref = refs[PAGES_PER_STEP:PAGES_PER_STEP + 2]
    ys = [pltpu.einshape("ctgd->tgcd", pg[...].reshape(CHUNKS_PER_PAGE, CMP_STRIDE, HEAD_TILE, HEAD_DIM))
          for pg in pages]
    for c in range(2):
        lhs = jnp.concatenate(
            [jnp.concatenate([y[t, c * KV_HEADS + kh] for kh in range(KV_HEADS) for y in ys], axis=0)
             for t in range(CMP_STRIDE)], axis=1).astype(BF16)
        acc = jnp.dot(lhs, w_ref[c], preferred_element_type=F32)
        for kh in range(KV_HEADS):
            o_ref[0, c, kh] = acc[kh * BLK:(kh + 1) * BLK]


def _cmp_proj(pages, page_table, w1r):
    n, n_pages = page_table.shape
    chunks = n_pages * CHUNKS_PER_PAGE
    grid_spec = pltpu.PrefetchScalarGridSpec(
        num_scalar_prefetch=1,
        grid=(n, n_pages // PAGES_PER_STEP),
        in_specs=[_page_spec(p, 0) for p in range(PAGES_PER_STEP)]
        + [pl.BlockSpec((2, CMP_STRIDE * HEAD_DIM, 2 * HEAD_DIM), lambda i, j, pt: (0, 0, 0))],
        out_specs=pl.BlockSpec((1, 2, KV_HEADS, BLK, 2 * HEAD_DIM), lambda i, j, pt: (i, 0, 0, j, 0)),
    )
    return pl.pallas_call(
        _cmp_proj_kernel,
        out_shape=jax.ShapeDtypeStruct((n, 2, KV_HEADS, chunks, 2 * HEAD_DIM), F32),
        grid_spec=grid_spec,
        compiler_params=_params(("parallel", "arbitrary")),
        name="nsa_cmp_proj",
    )(page_table, *([pages] * PAGES_PER_STEP), w1r.reshape(2, CMP_STRIDE * HEAD_DIM, 2 * HEAD_DIM))


def _finish_compress(a, pe_row, w1f, b1, w2, b2, n_blocks):
    rows = a.shape[0]
    cst = jnp.dot(pe_row, w1f, preferred_element_type=F32)[0:1]
    h = (b1 + cst) + a[:, :HEAD_DIM] + pltpu.roll(a[:, HEAD_DIM:], rows - 1, 0)
    x = jnp.dot(jax.nn.gelu(h).astype(BF16), w2, preferred_element_type=F32) + b2
    return jnp.where(lax.broadcasted_iota(jnp.int32, x.shape, 0) < n_blocks, x, 0.0)


def _split3(x):
    hi = x.astype(BF16)
    r = x - hi.astype(F32)
    mid = r.astype(BF16)
    return hi, mid, (r - mid.astype(F32)).astype(BF16)


def _top_n(score, n, axis):
    idx = lax.broadcasted_iota(jnp.int32, score.shape, axis).astype(F32)
    big = float(score.shape[axis])

    def body(_, carry):
        sc, sel = carry
        mx = jnp.max(sc, axis=axis, keepdims=True)
        first = jnp.min(jnp.where(sc == mx, idx, big), axis=axis, keepdims=True)
        hit = idx == first
        return jnp.where(hit, -jnp.inf, sc), jnp.where(hit, 1.0, sel)

    return lax.fori_loop(0, n, body, (score, jnp.zeros(score.shape, F32)))[1]


def _sel_scores(p_slc, blk, cur, n_blocks):
    forced = (blk == 0) | (blk == cur) | (blk == cur - 1)
    score = jnp.where(forced, FORCED_SCORE, jnp.where(blk <= cur, p_slc, -1.0))
    return jnp.where(blk < n_blocks, score, -2.0)


def _nsa_prompt_kernel(q_ref, a_ref, pe_ref, w1f_ref, b1_ref, w2_ref, b2_ref, ksel_ref, vsel_ref, kwin_ref,
                       vwin_ref, tbl_ref, tblw_ref, biasc_ref, gates_ref, msel_ref, o_ref, kc_scr, vc_scr, sel_scr,
                       vselt_ref, vwint_ref, *, n_cmp, n_sel):
    i = pl.program_id(2)

    @pl.when(i == 0)
    def _():
        for c, scr in ((0, kc_scr), (1, vc_scr)):
            scr[...] = _finish_compress(a_ref[0, c, 0], pe_ref[c], w1f_ref[c], b1_ref[c], w2_ref[c], b2_ref[c],
                                        n_cmp).astype(BF16)
        for src, dst in ((vsel_ref, vselt_ref), (vwin_ref, vwint_ref)):
            for c in range(src.shape[0] // BLK):
                dst[:, c * BLK:(c + 1) * BLK] = src[c * BLK:(c + 1) * BLK, :].T.astype(BF16)

    q = q_ref[...]
    qs = jnp.concatenate([q[:, g * HEAD_DIM:(g + 1) * HEAD_DIM] for g in range(GROUP)], axis=0).astype(BF16)
    key_i = lax.broadcasted_iota(jnp.int32, (BLK, BLK), 0)
    tok_i = lax.broadcasted_iota(jnp.int32, (BLK, BLK), 1)

    def lanes4(x):
        return jnp.concatenate([x] * GROUP, axis=1)

    mask_c = lanes4(i * BLK + tok_i - (key_i * CMP_STRIDE + (CMP_BLOCK - 1)) >= 0)
    s = lax.dot_general(kc_scr[...], qs, NT_DIMS, preferred_element_type=F32) * SCALE
    s = jnp.where(mask_c, s + jnp.concatenate([biasc_ref[g, 0] for g in range(GROUP)], axis=1), NEG)
    m = jnp.max(s, axis=0, keepdims=True)
    p = jnp.where(mask_c, jnp.exp(s - m), 0.0)
    pn = p / jnp.maximum(jnp.sum(p, axis=0, keepdims=True), 1e-30)
    o_cmp = lax.dot_general(vc_scr[...], pn.astype(BF16), TN_DIMS, preferred_element_type=F32)
    pc = ((pn[:, 0:BLK] + pn[:, BLK:2 * BLK]) + pn[:, 2 * BLK:3 * BLK]) + pn[:, 3 * BLK:4 * BLK]

    msel = msel_ref[...]
    p_slc = sum(jnp.dot(msel, part, preferred_element_type=F32) for part in _split3(pc))
    blk = lax.broadcasted_iota(jnp.int32, p_slc.shape, 0)
    cur = (i * BLK + lax.broadcasted_iota(jnp.int32, p_slc.shape, 1)) // SEL_BLOCK
    sel_scr[...] = (1.0 - _top_n(_sel_scores(p_slc, blk, cur, n_sel), SEL_TOPN, 0)) * NEG

    def attend(k_ref, vt_ref, bias_ref, first_blk, n_blk, extra, carry):
        m_run, l_run, acc = carry
        off = pl.multiple_of(first_blk * BLK, BLK)
        k = k_ref[pl.ds(off, n_blk * BLK), :].astype(BF16)
        bias = []
        for j in range(n_blk):
            idx = jnp.maximum(i - first_blk - j, -1) + 1
            bias.append(jnp.concatenate([bias_ref[g, idx] for g in range(GROUP)], axis=1))
        s = lax.dot_general(k, qs, NT_DIMS, preferred_element_type=F32) * SCALE + jnp.concatenate(bias, axis=0)
        if extra is not None:
            s = s + extra
        m_new = jnp.maximum(m_run, jnp.max(s, axis=0, keepdims=True))
        alpha = jnp.exp(m_run - m_new)
        p = jnp.exp(s - m_new)
        l_new = alpha * l_run + jnp.sum(p, axis=0, keepdims=True)
        vt = vt_ref[:, pl.ds(off, n_blk * BLK)]
        return m_new, l_new, alpha * acc + jnp.dot(vt, p.astype(BF16), preferred_element_type=F32)

    init = (jnp.full((1, GROUP * BLK), NEG, F32), jnp.zeros((1, GROUP * BLK), F32),
            jnp.zeros((HEAD_DIM, GROUP * BLK), F32))

    sel_span = 4
    per_blk = BLK // SEL_BLOCK

    def sel_step(c, carry):
        rows = sel_scr[pl.ds(pl.multiple_of(c * sel_span * per_blk, 8), sel_span * per_blk), :]
        unpicked = jnp.concatenate([jnp.broadcast_to(rows[u:u + 1], (SEL_BLOCK, BLK))
                                    for u in range(sel_span * per_blk)], axis=0)
        return attend(ksel_ref, vselt_ref, tbl_ref, c * sel_span, sel_span, lanes4(unpicked), carry)

    _, l_sel, acc_sel = lax.fori_loop(0, i // sel_span + 1, sel_step, init)
    o_sel = acc_sel / jnp.maximum(l_sel, 1e-30)

    n_win = (NSA_WINDOW - 1 + BLK - 1) // BLK + 1
    _, l_win, acc_win = attend(kwin_ref, vwint_ref, tblw_ref, jnp.maximum(i - (n_win - 1), 0), n_win, None, init)
    o_win = acc_win / jnp.maximum(l_win, 1e-30)

    gt = gates_ref[0]

    def gate(branch):
        return jnp.concatenate([gt[branch * GROUP + g:branch * GROUP + g + 1, :] for g in range(GROUP)], axis=1)

    o = (gate(0) * o_cmp + gate(1) * o_sel) + gate(2) * o_win
    for g in range(GROUP):
        o_ref[0, g] = o[:, g * BLK:(g + 1) * BLK]


def _nsa_prompt(proj, a_cmp, gates_t, cmp_w, tbl, tbl_win, bias_c, msel, b, s):
    pe, w1f, b1, w2, b2 = cmp_w
    nq = s // BLK
    kcol = NSA_Q // HEAD_DIM
    const = lambda shape: pl.BlockSpec(shape, lambda bi, kh, i: (0,) * len(shape))
    return pl.pallas_call(
        functools.partial(_nsa_prompt_kernel, n_cmp=s // CMP_STRIDE - 1, n_sel=s // SEL_BLOCK),
        out_shape=jax.ShapeDtypeStruct((b, N_HEADS, HEAD_DIM, s), F32),
        grid=(b, KV_HEADS, nq),
        in_specs=[
            pl.BlockSpec((BLK, GROUP * HEAD_DIM), lambda bi, kh, i: (bi * nq + i, kh)),
            pl.BlockSpec((1, 2, 1, s // CMP_STRIDE, 2 * HEAD_DIM), lambda bi, kh, i: (bi, 0, kh, 0, 0)),
            const(pe.shape), const(w1f.shape), const(b1.shape), const(w2.shape), const(b2.shape),
            pl.BlockSpec((s, HEAD_DIM), lambda bi, kh, i: (bi, kcol + 2 * KV_HEADS + kh)),
            pl.BlockSpec((s, HEAD_DIM), lambda bi, kh, i: (bi, kcol + 3 * KV_HEADS + kh)),
            pl.BlockSpec((s, HEAD_DIM), lambda bi, kh, i: (bi, kcol + 4 * KV_HEADS + kh)),
            pl.BlockSpec((s, HEAD_DIM), lambda bi, kh, i: (bi, kcol + 5 * KV_HEADS + kh)),
            pl.BlockSpec((GROUP,) + tbl.shape[1:], lambda bi, kh, i: (kh, 0, 0, 0)),
            pl.BlockSpec((GROUP,) + tbl_win.shape[1:], lambda bi, kh, i: (kh, 0, 0, 0)),
            pl.BlockSpec((GROUP, 1, s // CMP_STRIDE, BLK), lambda bi, kh, i: (kh, 0, 0, i)),
            pl.BlockSpec((1, 16, BLK), lambda bi, kh, i: (kh, 0, bi * nq + i)),
            const(msel.shape),
        ],
        out_specs=pl.BlockSpec((1, GROUP, HEAD_DIM, BLK), lambda bi, kh, i: (bi, kh, 0, i)),
        scratch_shapes=[pltpu.VMEM((s // CMP_STRIDE, HEAD_DIM), BF16), pltpu.VMEM((s // CMP_STRIDE, HEAD_DIM), BF16),
                        pltpu.VMEM((s // SEL_BLOCK, BLK), F32),
                        pltpu.VMEM((HEAD_DIM, s), BF16), pltpu.VMEM((HEAD_DIM, s), BF16)],
        compiler_params=_params(("parallel", "parallel", "arbitrary")),
        name="nsa_prompt",
    )(proj, a_cmp, pe, w1f, b1, w2, b2, proj, proj, proj, proj, tbl, tbl_win, bias_c, gates_t, msel)


def _masked_softmax_rows(s, mask):
    s = jnp.where(mask, s, NEG)
    m = jnp.max(s, axis=-1, keepdims=True)
    p = jnp.where(mask, jnp.exp(s - m), 0.0)
    return p / jnp.maximum(jnp.sum(p, axis=-1, keepdims=True), 1e-30)


def _nsa_sample_select_kernel(q_ref, a_ref, pe_ref, w1f_ref, b1_ref, w2_ref, b2_ref, biasc_ref, msel_ref,
                              regroup_ref, ocmp_ref, sel_ref, *, n_cmp, n_sel, past):
    pcs = []
    for kh in range(KV_HEADS):
        kc, vc = (_finish_compress(a_ref[0, c, kh], pe_ref[c], w1f_ref[c], b1_ref[c], w2_ref[c], b2_ref[c],
                                   n_cmp).astype(BF16) for c in range(2))
        bias = biasc_ref[kh]
        s = lax.dot_general(q_ref[0, kh].astype(BF16), kc, NT_DIMS, preferred_element_type=F32) * SCALE + bias
        pn = _masked_softmax_rows(s, bias > 0.5 * NEG)
        ocmp_ref[0, kh] = jnp.dot(pn.astype(BF16), vc, preferred_element_type=F32)
        pcs.append(((pn[0:8] + pn[8:16]) + pn[16:24]) + pn[24:32])
    pc = jnp.concatenate(pcs, axis=0)
    msel = msel_ref[...]
    p_slc = sum(jnp.dot(part, msel, preferred_element_type=F32) for part in _split3(pc))
    blk = lax.broadcasted_iota(jnp.int32, p_slc.shape, 1)
    cur = (past + lax.broadcasted_iota(jnp.int32, p_slc.shape, 0) % 8) // SEL_BLOCK
    sel = _top_n(_sel_scores(p_slc, blk, cur, n_sel), SEL_TOPN, 1).astype(BF16)
    for j in range(regroup_ref.shape[0]):
        part = jnp.dot(sel, regroup_ref[j], preferred_element_type=F32)
        for kh in range(KV_HEADS):
            sel_ref[0, kh, j] = part[kh * 8:(kh + 1) * 8]


def _nsa_sample_select(q, a_cmp, cmp_w, bias_c, msel, regroup, past, t_len):
    pe, w1f, b1, w2, b2 = cmp_w
    n = q.shape[0]
    chunks = a_cmp.shape[3]
    n_steps = regroup.shape[0]
    const = lambda shape: pl.BlockSpec(shape, lambda i: (0,) * len(shape))
    return pl.pallas_call(
        functools.partial(_nsa_sample_select_kernel, n_cmp=chunks - 1, n_sel=(past + t_len + SEL_BLOCK - 1) // SEL_BLOCK,
                          past=past),
        out_shape=(jax.ShapeDtypeStruct((n, KV_HEADS, GROUP * 8, HEAD_DIM), F32),
                   jax.ShapeDtypeStruct((n, KV_HEADS, n_steps, 8, BLK), F32)),
        grid=(n,),
        in_specs=[
            pl.BlockSpec((1, KV_HEADS, GROUP * 8, HEAD_DIM), lambda i: (i, 0, 0, 0)),
            pl.BlockSpec((1, 2, KV_HEADS, chunks, 2 * HEAD_DIM), lambda i: (i, 0, 0, 0, 0)),
            const(pe.shape), const(w1f.shape), const(b1.shape), const(w2.shape), const(b2.shape),
            const(bias_c.shape), const(msel.shape), const(regroup.shape),
        ],
        out_specs=(pl.BlockSpec((1, KV_HEADS, GROUP * 8, HEAD_DIM), lambda i: (i, 0, 0, 0)),
                   pl.BlockSpec((1, KV_HEADS, n_steps, 8, BLK), lambda i: (i, 0, 0, 0, 0))),
        compiler_params=_params(("parallel",)),
        name="nsa_sample_select",
    )(q, a_cmp, pe, w1f, b1, w2, b2, bias_c, msel, regroup)


def _nsa_sample_attend_kernel(pt_ref, *refs):
    pages = refs[:PAGES_PER_STEP]
    (q_ref, sel_ref, expand_ref, bias_ref, kn_ref, vn_ref, biasn_ref, win_ref, kwn_ref, vwn_ref, biasw_ref,
     biaswn_ref, gates_ref, ocmp_ref, o_ref, m_scr, l_scr, acc_scr) = refs[PAGES_PER_STEP:]
    j = pl.program_id(1)
    n_steps = pl.num_programs(1)

    @pl.when(j == 0)
    def _():
        m_scr[...] = jnp.full_like(m_scr, NEG)
        l_scr[...] = jnp.zeros_like(l_scr)
        acc_scr[...] = jnp.zeros_like(acc_scr)

    def picked(kh, step):
        sel = sel_ref[0, kh, step].astype(BF16)
        return jnp.concatenate([sel] * GROUP, axis=0)

    def update(kh, k, v, bias, mask):
        qs = q_ref[0, kh].astype(BF16)
        s = lax.dot_general(qs, k, NT_DIMS, preferred_element_type=F32) * SCALE + bias
        s = jnp.where(mask, s, NEG)
        m_old = m_scr[kh]
        m_new = jnp.maximum(m_old, jnp.max(s, axis=-1, keepdims=True))
        alpha = jnp.exp(m_old - m_new)
        p = jnp.where(mask, jnp.exp(s - m_new), 0.0)
        l_scr[kh] = alpha * l_scr[kh] + jnp.sum(p, axis=-1, keepdims=True)
        acc_scr[kh] = alpha * acc_scr[kh] + jnp.dot(p.astype(BF16), v, preferred_element_type=F32)
        m_scr[kh] = m_new

    tiles = [_heads_first(pg[...]) for pg in pages]
    expand = expand_ref[...]
    for kh in range(KV_HEADS):
        k = jnp.concatenate([tl[kh] for tl in tiles], axis=0).astype(BF16)
        v = jnp.concatenate([tl[KV_HEADS + kh] for tl in tiles], axis=0).astype(BF16)
        in_sel = jnp.dot(picked(kh, j), expand, preferred_element_type=F32) > 0.5
        update(kh, k, v, bias_ref[kh], in_sel)

    @pl.when(j == n_steps - 1)
    def _():
        win = _heads_first(win_ref[...])
        for kh in range(KV_HEADS):
            biasn = biasn_ref[kh]
            new_sel = picked(kh, n_steps)[:, 0:1] > 0.5
            update(kh, kn_ref[0, kh].astype(BF16), vn_ref[0, kh].astype(BF16), biasn, (biasn > 0.5 * NEG) & new_sel)
            o_sel = acc_scr[kh] / jnp.maximum(l_scr[kh], 1e-30)
            biasw = jnp.concatenate([biasw_ref[kh], biaswn_ref[kh]], axis=1)
            kw = jnp.concatenate([win[kh], kwn_ref[0, kh]], axis=0).astype(BF16)
            vw = jnp.concatenate([win[KV_HEADS + kh], vwn_ref[0, kh]], axis=0).astype(BF16)
            sw = lax.dot_general(q_ref[0, kh].astype(BF16), kw, NT_DIMS, preferred_element_type=F32) * SCALE + biasw
            pw = _masked_softmax_rows(sw, biasw > 0.5 * NEG)
            o_win = jnp.dot(pw.astype(BF16), vw, preferred_element_type=F32)
            o_ref[0, kh] = ((gates_ref[0, 0, kh] * ocmp_ref[0, kh] + gates_ref[1, 0, kh] * o_sel)
                            + gates_ref[2, 0, kh] * o_win)


def _nsa_sample_attend(pool, page_table, q, sel, expand, bias_sel, k_new, v_new, bias_new, win_buf, kw_new, vw_new,
                       bias_win, bias_win_new, gates, o_cmp):
    n, n_pages = page_table.shape
    n_steps = n_pages // PAGES_PER_STEP
    keys = PAGES_PER_STEP * BLK
    rows = GROUP * 8
    lw = win_buf.shape[1]
    per = lambda shape: pl.BlockSpec((1, KV_HEADS) + shape, lambda i, j, pt: (i, 0) + (0,) * len(shape))
    whole = lambda shape: pl.BlockSpec(shape, lambda i, j, pt: (0,) * len(shape))
    grid_spec = pltpu.PrefetchScalarGridSpec(
        num_scalar_prefetch=1,
        grid=(n, n_steps),
        in_specs=[_page_spec(p, 1) for p in range(PAGES_PER_STEP)] + [
            per((rows, HEAD_DIM)),
            per((n_steps + 1, 8, BLK)),
            whole(expand.shape),
            pl.BlockSpec((KV_HEADS, rows, keys), lambda i, j, pt: (0, 0, j)),
            per((BLK, HEAD_DIM)), per((BLK, HEAD_DIM)), whole((KV_HEADS, rows, BLK)),
            pl.BlockSpec((None, lw, HEAD_TILE, HEAD_DIM), lambda i, j, pt: (i, 0, 0, 0)),
            per((BLK, HEAD_DIM)), per((BLK, HEAD_DIM)), whole((KV_HEADS, rows, lw)), whole((KV_HEADS, rows, BLK)),
            pl.BlockSpec((3, 1, KV_HEADS, rows, HEAD_DIM), lambda i, j, pt: (0, i, 0, 0, 0)),
            per((rows, HEAD_DIM)),
        ],
        out_specs=per((rows, HEAD_DIM)),
        scratch_shapes=[pltpu.VMEM((KV_HEADS, rows, 1), F32), pltpu.VMEM((KV_HEADS, rows, 1), F32),
                        pltpu.VMEM((KV_HEADS, rows, HEAD_DIM), F32)],
    )
    return pl.pallas_call(
        _nsa_sample_attend_kernel,
        out_shape=jax.ShapeDtypeStruct((n, KV_HEADS, rows, HEAD_DIM), F32),
        grid_spec=grid_spec,
        compiler_params=_params(("parallel", "arbitrary")),
        name="nsa_sample_attend",
    )(page_table, *([pool] * PAGES_PER_STEP), q, sel, expand, bias_sel, k_new, v_new, bias_new, win_buf, kw_new,
      vw_new, bias_win, bias_win_new, gates, o_cmp)


def _sel_weights(n_cmp_rows, n_sel_cols):
    ratio = SEL_BLOCK // CMP_STRIDE
    span = CMP_BLOCK // CMP_STRIDE
    c = jnp.arange(n_cmp_rows)[:, None]
    j = jnp.arange(n_sel_cols)[None, :]
    o = c - ratio * j + (span - 1)
    cnt = jnp.minimum(o, span - 1) - jnp.maximum(o - (ratio - 1), 0) + 1
    return jnp.where((o >= 0) & (o <= ratio + span - 2), cnt, 0).astype(BF16)


def _layer_b(hp, hs, b, s, n, t, norm_g, w_in, w_out, rel_bias, cmp, pool, page_table, win_buf):
    cmp_pe, cmp_w1, cmp_b1, cmp_w2, cmp_b2 = cmp
    n_kvcol = 6 * NSA_KV
    w_main = w_in
    w_gate = jnp.pad(w_in[:, NSA_Q + n_kvcol:], ((0, 0), (0, BLK - 3 * N_HEADS))).astype(BF16)
    w1 = cmp_w1.reshape(2, 2, CMP_STRIDE, HEAD_DIM, HEAD_DIM)
    w1r = jnp.concatenate([w1[:, 0], w1[:, 1]], axis=-1).astype(BF16)
    pe_row = jnp.pad(cmp_pe.reshape(2, 1, CMP_BLOCK * HEAD_DIM), ((0, 0), (0, 7), (0, 0))).astype(BF16)
    cmp_w = (pe_row, cmp_w1.reshape(2, CMP_BLOCK * HEAD_DIM, HEAD_DIM).astype(BF16), cmp_b1.reshape(2, 1, HEAD_DIM),
             cmp_w2.astype(BF16), cmp_b2.reshape(2, 1, HEAD_DIM))
    weights = (w_main, w_gate, w_out, w1r, cmp_w)
    hs, new_win_s, new_kv_s, w_main, w_out = _nsa_sample_path(hs, n, t, norm_g, weights, rel_bias, pool, page_table,
                                                              win_buf)
    weights = (w_main, w_gate, w_out, w1r, cmp_w)
    hp, new_win_p, new_kv_p = _nsa_prompt_path(hp, b, s, norm_g, weights, rel_bias)
    return hp, hs, new_win_p, new_win_s, new_kv_p, new_kv_s


def _nsa_prompt_path(hp, b, s, norm_g, weights, rel_bias):
    w_main, w_gate, w_out, w1r, cmp_w = weights
    proj_p = _project(hp, w_main, gain=norm_g)
    gates_p = _project(hp, w_gate, gain=norm_g, sigmoid=True)
    col_tile = HEAD_TILE * HEAD_DIM
    new_kv = _vector_rows(proj_p, NSA_Q // col_tile, 4 * NSA_KV // col_tile, b, s, s)
    keep = min(NSA_WINDOW, s)
    new_win = _vector_rows(proj_p, (NSA_Q + 4 * NSA_KV) // col_tile, 2 * NSA_KV // col_tile, b, s, keep)
    table_p = jnp.arange(b * s // BLK, dtype=jnp.int32).reshape(b, s // BLK)
    a_cmp_p = _cmp_proj(new_kv.reshape(b * s, 2 * HEAD_TILE, HEAD_DIM), table_p, w1r)
    nq = s // BLK
    n_win = (NSA_WINDOW - 1 + BLK - 1) // BLK + 1
    tbl = _bias_table(rel_bias, nq + 1, BLK, BLK, -BLK, BLK, -1, 1, 0, 1 << 30)
    tbl_win = _bias_table(rel_bias, n_win + 1, BLK, BLK, -BLK, BLK, -1, 1, 0, NSA_WINDOW - 1)
    bias_c = _bias_table(rel_bias, 1, s // CMP_STRIDE, s, -(CMP_BLOCK - 1), 0, -CMP_STRIDE, 1, 0, 1 << 30)
    gates_t = gates_p[:, :3 * N_HEADS].reshape(b * s, 3, KV_HEADS, GROUP).transpose(2, 1, 3, 0)
    gates_t = jnp.pad(gates_t.reshape(KV_HEADS, 3 * GROUP, b * s), ((0, 0), (0, 16 - 3 * GROUP), (0, 0)))
    msel_p = _sel_weights(s // CMP_STRIDE, s // SEL_BLOCK).T
    o_t = _nsa_prompt(proj_p, a_cmp_p, gates_t, cmp_w, tbl, tbl_win, bias_c, msel_p, b, s)
    o_p = o_t.transpose(0, 3, 1, 2).reshape(b * s, NSA_Q).astype(BF16)
    hp = _project(o_p, w_out, res=hp)
    return (hp, new_win.reshape(b, keep, 2, KV_HEADS, HEAD_DIM), new_kv.reshape(b, s, 4, KV_HEADS, HEAD_DIM))


def _nsa_sample_path(hs, n, t, norm_g, weights, rel_bias, pool, page_table, win_buf):
    w_main, w_gate, w_out, w1r, cmp_w = weights
    past = page_table.shape[1] * BLK
    proj_s, w_main = _project(hs, w_main, n_cols=NSA_Q + 6 * NSA_KV, gain=norm_g, cast_w=True)
    gates_s = _project(hs, w_gate, gain=norm_g, sigmoid=True)
    kv_s = proj_s[:, NSA_Q:].reshape(n, t, 6, KV_HEADS, HEAD_DIM)
    rows = GROUP * 8

    def head_rows(x):
        x = jnp.pad(x.transpose(0, 2, 3, 1, 4), ((0, 0), (0, 0), (0, 0), (0, 8 - t), (0, 0)))
        return x.reshape(n, KV_HEADS, rows, x.shape[-1])

    def new_rows(c):
        return jnp.pad(kv_s[:, :, c].transpose(0, 2, 1, 3), ((0, 0), (0, 0), (0, BLK - t), (0, 0)))

    def head_table(x, cols):
        return x.reshape(KV_HEADS, rows, cols)

    q_s = head_rows(proj_s[:, :NSA_Q].reshape(n, t, KV_HEADS, GROUP, HEAD_DIM))
    pool2 = pool.reshape(pool.shape[0] * BLK, 2 * HEAD_TILE, HEAD_DIM)
    a_cmp_s = _cmp_proj(pool2, page_table, w1r)
    chunks = past // CMP_STRIDE
    big = 1 << 30
    bias_cs = head_table(_bias_table(rel_bias, 1, 8, chunks, past - (CMP_BLOCK - 1), 0, 1, -CMP_STRIDE, 0, big), chunks)
    n_steps = page_table.shape[1] // PAGES_PER_STEP
    n_sel_pad = (n_steps + 1) * BLK
    msel_s = _sel_weights(chunks, n_sel_pad)
    per_step = PAGES_PER_STEP * BLK // SEL_BLOCK
    jj = jnp.arange(n_sel_pad)[None, :, None]
    ll = jnp.arange(BLK)[None, None, :]
    st = jnp.arange(n_steps + 1)[:, None, None]
    regroup = ((jj == st * per_step + ll) & (ll < per_step)).astype(BF16)
    o_cmp_s, sel_s = _nsa_sample_select(q_s, a_cmp_s, cmp_w, bias_cs, msel_s, regroup, past, t)
    expand = (jnp.arange(BLK)[:, None] == jnp.arange(PAGES_PER_STEP * BLK)[None, :] // SEL_BLOCK).astype(BF16)
    bias_sel = head_table(_bias_table(rel_bias, 1, 8, past, past, 0, 1, -1, 0, big), past)
    bias_new = head_table(_bias_table(rel_bias, 1, 8, BLK, 0, 0, 1, -1, 0, big), BLK)
    lw = win_buf.shape[1]
    bias_win = head_table(_bias_table(rel_bias, 1, 8, lw, lw, 0, 1, -1, 0, NSA_WINDOW - 1), lw)
    bias_win_new = head_table(_bias_table(rel_bias, 1, 8, BLK, 0, 0, 1, -1, 0, NSA_WINDOW - 1), BLK)
    g_s = gates_s[:, :3 * N_HEADS].reshape(n, t, 3, KV_HEADS, GROUP, 1)
    g_s = jnp.stack([head_rows(g_s[:, :, c]) for c in range(3)])
    g_s = jnp.broadcast_to(g_s, (3, n, KV_HEADS, rows, HEAD_DIM))
    o_s = _nsa_sample_attend(pool2, page_table, q_s, sel_s, expand, bias_sel, new_rows(2), new_rows(3), bias_new,
                             win_buf.reshape(n, lw, HEAD_TILE, HEAD_DIM), new_rows(4), new_rows(5), bias_win,
                             bias_win_new,
                             g_s, o_cmp_s)
    o_s = o_s.reshape(n, KV_HEADS, GROUP, 8, HEAD_DIM)[:, :, :, :t].transpose(0, 3, 1, 2, 4)
    hs, w_out = _project(o_s.reshape(n * t, NSA_Q).astype(BF16), w_out, res=hs, cast_w=True)
    new_win_s = jnp.concatenate([win_buf, kv_s[:, :, 4:]], axis=1)[:, t:]
    return hs, new_win_s, kv_s[:, :, :4], w_main, w_out


def _ffn_and_ple(hp, hs, b, s, n, t, i, norm_ffn, norm_ple, w_ffn_in, conv_w, conv_b, w_ffn_out, state_conv,
                 p_prompt, p_sample, w_ple_gate, w_ple_proj):
    hu_s, w_ffn_in = _project(hs, w_ffn_in, layer=i, gain=norm_ffn, cast_w=True)
    hu_p = _project(hp, w_ffn_in, gain=norm_ffn)
    conv_p = hu_p.reshape(b, s, 2 * D_FF)[:, s - (CONV_W - 1):, :D_FF]
    conv_s = jnp.concatenate([state_conv, hu_s.reshape(n, t, 2 * D_FF)[:, :, :D_FF]], axis=1)[:, t:]
    hs, w_ffn_out = _ffn_out_sample(hu_s, state_conv, conv_w, conv_b, w_ffn_out, i, hs, t)
    hp = _ffn_out_prompt(hu_p, conv_w, conv_b, w_ffn_out, hp, s)
    hs, w_ple_gate, w_ple_proj = _ple_add(hs, norm_ple, w_ple_gate, p_sample.astype(BF16), w_ple_proj, layer=i,
                                          cast_w=True)
    hp = _ple_add(hp, norm_ple, w_ple_gate, p_prompt.astype(BF16), w_ple_proj)
    return hp, hs, conv_p, conv_s


def kernel(x_prompt, x_sample, state_dil_w128, state_dil_w512, state_dil_w2048, state_nsa_win, state_conv,
           cache_nsa_kv, page_table, p_prompt, p_sample, rel_bias, norm_mix, norm_ffn, norm_ple, norm_final,
           w_in_a, w_out_a, w_in_b, w_out_b, cmp_pe, cmp_w1, cmp_b1, cmp_w2, cmp_b2, w_ffn_in, conv_w, conv_b,
           w_ffn_out, w_ple_gate, w_ple_proj):
    b, s, d = x_prompt.shape
    n, t, _ = x_sample.shape
    depth = norm_mix.shape[0]
    hp, hs = x_prompt.reshape(b * s, d), x_sample.reshape(n * t, d)
    dil_p, dil_s = [[] for _ in range(N_DIL)], [[] for _ in range(N_DIL)]
    win_p, win_s, kv_p, kv_s, conv_p, conv_s = [], [], [], [], [], []
    for i in range(depth):
        li = i // 2
        if i % 2 == 0:
            hp, hs, new_p, new_s = _layer_a(
                hp, hs, b, s, n, t, norm_mix[i], w_in_a[li], w_out_a[li], rel_bias,
                (state_dil_w128[li], state_dil_w512[li], state_dil_w2048[li]))
            for g in range(N_DIL):
                dil_p[g].append(new_p[g])
                dil_s[g].append(new_s[g])
        else:
            hp, hs, wp, ws, rp, rs = _layer_b(
                hp, hs, b, s, n, t, norm_mix[i], w_in_b[li], w_out_b[li], rel_bias,
                (cmp_pe[li], cmp_w1[li], cmp_b1[li], cmp_w2[li], cmp_b2[li]), cache_nsa_kv[li], page_table,
                state_nsa_win[li])
            win_p.append(wp)
            win_s.append(ws)
            kv_p.append(rp)
            kv_s.append(rs)
        hp, hs, cp, cs = _ffn_and_ple(
            hp, hs, b, s, n, t, i, norm_ffn[i], norm_ple[i], w_ffn_in, conv_w[i], conv_b[i],
            w_ffn_out, state_conv[i], p_prompt[i].reshape(b * s, -1), p_sample[i].reshape(n * t, -1),
            w_ple_gate, w_ple_proj)
        conv_p.append(cp)
        conv_s.append(cs)
    y_prompt = _rmsnorm(hp, norm_final, F32).reshape(b, s, d)
    y_sample = _rmsnorm(hs, norm_final, F32).reshape(n, t, d)
    return (y_prompt, y_sample,
            jnp.stack(dil_p[0]), jnp.stack(dil_s[0]), jnp.stack(dil_p[1]), jnp.stack(dil_s[1]),
            jnp.stack(dil_p[2]), jnp.stack(dil_s[2]),
            jnp.stack(win_p), jnp.stack(win_s), jnp.stack(conv_p), jnp.stack(conv_s),
            jnp.stack(kv_p), jnp.stack(kv_s))
```

```python
import functools

import jax
import jax.numpy as jnp
from jax import lax
from jax.experimental import pallas as pl
from jax.experimental.pallas import tpu as pltpu

F32 = jnp.float32
BF16 = jnp.bfloat16

D_MODEL = 2048
HEAD_DIM = 128
N_HEADS = 16
DIL_PAIRS = ((128, 1), (512, 4), (2048, 16))
N_DIL = 3
BLK = 128
KV_HEADS = 4
GROUP = 4
CMP_BLOCK = 32
CMP_STRIDE = 16
SEL_BLOCK = 64
SEL_TOPN = 16
NSA_WINDOW = 512
D_FF = 5632
CONV_W = 3
REL_BUCKETS = 32
EPS = 1e-6
NEG = -1e30
FORCED_SCORE = 1e4
SCALE = HEAD_DIM ** -0.5
LOG2E = 1.4426950408889634
QKV_A = N_DIL * 3 * N_HEADS * HEAD_DIM
NSA_Q = N_HEADS * HEAD_DIM
NSA_KV = KV_HEADS * HEAD_DIM

BUCKET_START = (1, 2, 3, 4, 5, 6, 7, 8, 9, 10, 11, 12, 13, 14, 15, 16, 22, 30, 40, 54, 73, 99,
                134, 182, 246, 332, 450, 609, 825, 1117, 1513)

VMEM_LIMIT_V7X = 56 * 1024 * 1024


def _params(sem, vmem=VMEM_LIMIT_V7X):
    return pltpu.CompilerParams(dimension_semantics=sem, vmem_limit_bytes=vmem)


def _pick(n, cands):
    for c in cands:
        if n % c == 0:
            return c
    return n


def _rmsnorm_kernel(x_ref, g_ref, o_ref):
    x = x_ref[...]
    ms = jnp.mean(x * x, axis=-1, keepdims=True)
    o_ref[...] = ((x * lax.rsqrt(ms + EPS)) * g_ref[...]).astype(o_ref.dtype)


def _rmsnorm(x, g, out_dtype):
    m, d = x.shape
    tm = _pick(m, (512, 256, 128, 32))
    return pl.pallas_call(
        _rmsnorm_kernel,
        out_shape=jax.ShapeDtypeStruct((m, d), out_dtype),
        grid=(m // tm,),
        in_specs=[pl.BlockSpec((tm, d), lambda i: (i, 0)), pl.BlockSpec((1, d), lambda i: (0, 0))],
        out_specs=pl.BlockSpec((tm, d), lambda i: (i, 0)),
        compiler_params=_params(("parallel",)),
        name="rmsnorm",
    )(x, g.reshape(1, d))


def _normed(x, g):
    ms = jnp.mean(x * x, axis=-1, keepdims=True)
    return ((x * lax.rsqrt(ms + EPS)) * g).astype(BF16)


def _project_kernel(*refs, norm, cast_w, residual, sigmoid, heads):
    refs = list(refs)
    x_ref = refs.pop(0)
    g_ref = refs.pop(0) if norm else None
    w_ref = refs.pop(0)
    r_ref = refs.pop(0) if residual else None
    o_ref = refs.pop(0)
    wb_ref = refs.pop(0) if cast_w else None
    if norm:
        a_scr = refs.pop(0)

        @pl.when(pl.program_id(1) == 0)
        def _():
            a_scr[...] = _normed(x_ref[...], g_ref[...])

        a = a_scr[...]
    else:
        a = x_ref[...]
    w = w_ref[...]
    if cast_w:
        w = w.astype(BF16)
        wb_ref[...] = w
    acc = jnp.dot(a, w, preferred_element_type=F32)
    if sigmoid:
        acc = jax.nn.sigmoid(acc)
    if residual:
        acc = r_ref[...] + acc
    if heads:
        for j in range(o_ref.shape[0]):
            o_ref[j] = acc[:, j * HEAD_DIM:(j + 1) * HEAD_DIM]
    else:
        o_ref[...] = acc


def _weight_spec(w, layer, k, tn, index):
    if w.ndim == 2:
        return pl.BlockSpec((k, tn), index)
    return pl.BlockSpec((None, k, tn), lambda *g: (layer,) + index(*g))


def _project(x, w, *, layer=None, n_cols=None, gain=None, res=None, cast_w=False, sigmoid=False, heads=False):
    m, k = x.shape
    n = n_cols or w.shape[-1]
    tm = _pick(m, (1024, 512, 256, 128))
    tn = _pick(n, (1024, 512, 256, 128))
    assert not cast_w or m == tm
    norm, residual = gain is not None, res is not None
    args, in_specs = [x], [pl.BlockSpec((tm, k), lambda i, j: (i, 0))]
    if norm:
        args.append(gain.reshape(1, k))
        in_specs.append(pl.BlockSpec((1, k), lambda i, j: (0, 0)))
    args.append(w)
    in_specs.append(_weight_spec(w, layer, k, tn, lambda i, j: (0, j)))
    if residual:
        args.append(res)
        in_specs.append(pl.BlockSpec((tm, tn), lambda i, j: (i, j)))
    if heads:
        out_shape = [jax.ShapeDtypeStruct((n // HEAD_DIM, m, HEAD_DIM), F32)]
        out_specs = [pl.BlockSpec((tn // HEAD_DIM, tm, HEAD_DIM), lambda i, j: (j, i, 0))]
    else:
        out_shape = [jax.ShapeDtypeStruct((m, n), F32)]
        out_specs = [pl.BlockSpec((tm, tn), lambda i, j: (i, j))]
    if cast_w:
        out_shape.append(jax.ShapeDtypeStruct((k, n), BF16))
        out_specs.append(pl.BlockSpec((k, tn), lambda i, j: (0, j)))
    outs = pl.pallas_call(
        functools.partial(_project_kernel, norm=norm, cast_w=cast_w, residual=residual, sigmoid=sigmoid,
                          heads=heads),
        out_shape=out_shape,
        grid=(m // tm, n // tn),
        in_specs=in_specs,
        out_specs=out_specs,
        scratch_shapes=[pltpu.VMEM((tm, k), BF16)] if norm else [],
        compiler_params=_params(("parallel", "arbitrary")),
        name="project",
    )(*args)
    return tuple(outs) if cast_w else outs[0]


def _conv_gelu_val(g, g1, g2, val, cw_ref, cb_ref):
    c = cb_ref[...] + g2 * cw_ref[0:1, :]
    c = c + g1 * cw_ref[1:2, :]
    c = c + g * cw_ref[2:3, :]
    return (jax.nn.gelu(c) * val).astype(BF16)


def _ffn_out_prompt_kernel(g_ref, halo_ref, v_ref, cw_ref, cb_ref, w_ref, r_ref, o_ref, acc_ref, *,
                           tiles_per_seq):
    i, k = pl.program_id(0), pl.program_id(1)

    @pl.when(k == 0)
    def _():
        acc_ref[...] = jnp.zeros_like(acc_ref)

    g = g_ref[...]
    row = lax.broadcasted_iota(jnp.int32, g.shape, 0)
    halo = jnp.where(i % tiles_per_seq == 0, 0.0, halo_ref[...])
    g1 = jnp.where(row == 0, halo[7:8, :], pltpu.roll(g, 1, 0))
    g2 = jnp.where(row == 0, halo[6:7, :], jnp.where(row == 1, halo[7:8, :], pltpu.roll(g, 2, 0)))
    u = _conv_gelu_val(g, g1, g2, v_ref[...], cw_ref, cb_ref)
    acc_ref[...] += jnp.dot(u, w_ref[...], preferred_element_type=F32)

    @pl.when(k == pl.num_programs(1) - 1)
    def _():
        o_ref[...] = r_ref[...] + acc_ref[...]


def _ffn_out_prompt(hu, conv_w, conv_b, w_out, res, seq):
    m = hu.shape[0]
    tm, tk = 512, D_FF // 4
    nk = D_FF // tk
    return pl.pallas_call(
        functools.partial(_ffn_out_prompt_kernel, tiles_per_seq=seq // tm),
        out_shape=jax.ShapeDtypeStruct((m, D_MODEL), F32),
        grid=(m // tm, nk),
        in_specs=[
            pl.BlockSpec((tm, tk), lambda i, k: (i, k)),
            pl.BlockSpec((8, tk), lambda i, k: (jnp.maximum(i * (tm // 8) - 1, 0), k)),
            pl.BlockSpec((tm, tk), lambda i, k: (i, k + nk)),
            pl.BlockSpec((CONV_W, tk), lambda i, k: (0, k)),
            pl.BlockSpec((1, tk), lambda i, k: (0, k)),
            pl.BlockSpec((tk, D_MODEL), lambda i, k: (k, 0)),
            pl.BlockSpec((tm, D_MODEL), lambda i, k: (i, 0)),
        ],
        out_specs=pl.BlockSpec((tm, D_MODEL), lambda i, k: (i, 0)),
        scratch_shapes=[pltpu.VMEM((tm, D_MODEL), F32)],
        compiler_params=_params(("parallel", "arbitrary")),
        name="ffn_out_prompt",
    )(hu, hu, hu, conv_w, conv_b.reshape(1, D_FF), w_out, res)


def _ffn_out_sample_kernel(g_ref, e1_ref, e2_ref, v_ref, cw_ref, cb_ref, w_ref, r_ref, o_ref, wb_ref, acc_ref, *,
                           t_len):
    k = pl.program_id(0)

    @pl.when(k == 0)
    def _():
        acc_ref[...] = jnp.zeros_like(acc_ref)

    g = g_ref[...]
    t = lax.broadcasted_iota(jnp.int32, g.shape, 0) % t_len
    g1 = jnp.where(t == 0, e1_ref[...], pltpu.roll(g, 1, 0))
    g2 = jnp.where(t < 2, e2_ref[...], pltpu.roll(g, 2, 0))
    u = _conv_gelu_val(g, g1, g2, v_ref[...], cw_ref, cb_ref)
    w = w_ref[...].astype(BF16)
    wb_ref[...] = w
    acc_ref[...] += jnp.dot(u, w, preferred_element_type=F32)

    @pl.when(k == pl.num_programs(0) - 1)
    def _():
        o_ref[...] = r_ref[...] + acc_ref[...]


def _ffn_out_sample(hu, conv_prev, conv_w, conv_b, w_out, layer, res, t_len):
    m = hu.shape[0]
    n = m // t_len
    tk = 512
    nk = D_FF // tk
    zeros = jnp.zeros((n, t_len - 1, D_FF), F32)
    e1 = jnp.concatenate([conv_prev[:, 1:2], zeros], axis=1).reshape(m, D_FF)
    e2 = jnp.concatenate([conv_prev, zeros[:, 1:]], axis=1).reshape(m, D_FF)
    return pl.pallas_call(
        functools.partial(_ffn_out_sample_kernel, t_len=t_len),
        out_shape=(jax.ShapeDtypeStruct((m, D_MODEL), F32), jax.ShapeDtypeStruct((D_FF, D_MODEL), BF16)),
        grid=(nk,),
        in_specs=[
            pl.BlockSpec((m, tk), lambda k: (0, k)),
            pl.BlockSpec((m, tk), lambda k: (0, k)),
            pl.BlockSpec((m, tk), lambda k: (0, k)),
            pl.BlockSpec((m, tk), lambda k: (0, k + nk)),
            pl.BlockSpec((CONV_W, tk), lambda k: (0, k)),
            pl.BlockSpec((1, tk), lambda k: (0, k)),
            pl.BlockSpec((None, tk, D_MODEL), lambda k: (layer, k, 0)),
            pl.BlockSpec((m, D_MODEL), lambda k: (0, 0)),
        ],
        out_specs=(pl.BlockSpec((m, D_MODEL), lambda k: (0, 0)), pl.BlockSpec((tk, D_MODEL), lambda k: (k, 0))),
        scratch_shapes=[pltpu.VMEM((m, D_MODEL), F32)],
        compiler_params=_params(("arbitrary",)),
        name="ffn_out_sample",
    )(hu, e1, e2, hu, conv_w, conv_b.reshape(1, D_FF), w_out, res)


def _ple_kernel(h_ref, g_ref, wg_ref, p_ref, wp_ref, o_ref, *rest, cast_w):
    a_scr = rest[-1]
    j = pl.program_id(1)
    tn = o_ref.shape[1]

    @pl.when(j == 0)
    def _():
        a_scr[...] = _normed(h_ref[...], g_ref[...])

    wg, wp = wg_ref[...], wp_ref[...]
    if cast_w:
        wg, wp = wg.astype(BF16), wp.astype(BF16)
        rest[0][...] = wg
        rest[1][...] = wp
    gate = jax.nn.sigmoid(jnp.dot(a_scr[...], wg, preferred_element_type=F32))
    proj = jnp.dot(p_ref[...], wp, preferred_element_type=F32)
    o_ref[...] = h_ref[:, pl.ds(pl.multiple_of(j * tn, tn), tn)] + gate * proj


def _ple_add(h, gain, w_gate, p, w_proj, layer=None, cast_w=False):
    m, d = h.shape
    kp = p.shape[1]
    tm = _pick(m, (1024, 512, 256, 128))
    tn = 1024
    assert not cast_w or m == tm
    out_shape = [jax.ShapeDtypeStruct((m, d), F32)]
    out_specs = [pl.BlockSpec((tm, tn), lambda i, j: (i, j))]
    if cast_w:
        out_shape += [jax.ShapeDtypeStruct((d, d), BF16), jax.ShapeDtypeStruct((kp, d), BF16)]
        out_specs += [pl.BlockSpec((d, tn), lambda i, j: (0, j)), pl.BlockSpec((kp, tn), lambda i, j: (0, j))]
    outs = pl.pallas_call(
        functools.partial(_ple_kernel, cast_w=cast_w),
        out_shape=out_shape,
        grid=(m // tm, d // tn),
        in_specs=[
            pl.BlockSpec((tm, d), lambda i, j: (i, 0)),
            pl.BlockSpec((1, d), lambda i, j: (0, 0)),
            _weight_spec(w_gate, layer, d, tn, lambda i, j: (0, j)),
            pl.BlockSpec((tm, kp), lambda i, j: (i, 0)),
            _weight_spec(w_proj, layer, kp, tn, lambda i, j: (0, j)),
        ],
        out_specs=out_specs,
        scratch_shapes=[pltpu.VMEM((tm, d), BF16)],
        compiler_params=_params(("parallel", "arbitrary")),
        name="ple_add",
    )(h, gain.reshape(1, d), w_gate, p, w_proj)
    return tuple(outs) if cast_w else outs[0]


def _bias_table_kernel(rbt_ref, o_ref, *, a0, ag, ar, ac, lo, hi, mod, scale):
    g = pl.program_id(0)
    n_r, n_c = o_ref.shape[2:]
    shape = (n_r, BLK)
    rows = ar * lax.broadcasted_iota(jnp.int32, shape, 0)
    cols = ac * lax.broadcasted_iota(jnp.int32, shape, 1)
    for c0 in range(0, n_c, BLK):
        dist = (a0 + ac * c0) + ag * g + rows + cols
        d = jnp.maximum(dist, 0)
        bucket = jnp.zeros(shape, jnp.int32)
        for start in BUCKET_START:
            bucket = bucket + (d >= start).astype(jnp.int32)
        ok = (dist >= lo) & (dist <= hi)
        if mod > 1:
            ok = ok & ((d & (mod - 1)) == 0)
        for h in range(N_HEADS):
            row = jnp.broadcast_to(rbt_ref[h:h + 1, :], shape)
            if scale != 1.0:
                row = row * scale
            o_ref[h, 0, :, c0:c0 + BLK] = jnp.where(ok, jnp.take_along_axis(row, bucket, axis=1), NEG)


def _bias_table(rel_bias, n_g, n_r, n_c, a0, ag, ar, ac, lo, hi, mod=1, scale=1.0):
    assert mod & (mod - 1) == 0 and n_c % BLK == 0
    rbt = jnp.pad(rel_bias.T, ((0, 0), (0, BLK - REL_BUCKETS)))
    return pl.pallas_call(
        functools.partial(_bias_table_kernel, a0=a0, ag=ag, ar=ar, ac=ac, lo=lo, hi=hi, mod=mod, scale=scale),
        out_shape=jax.ShapeDtypeStruct((N_HEADS, n_g, n_r, n_c), F32),
        grid=(n_g,),
        in_specs=[pl.BlockSpec((N_HEADS, BLK), lambda g: (0, 0))],
        out_specs=pl.BlockSpec((N_HEADS, 1, n_r, n_c), lambda g: (0, g, 0, 0)),
        compiler_params=_params(("parallel",)),
        name="bias_table",
    )(rbt)


def _dil_prompt_kernel(q_ref, kc_ref, kp_ref, vc_ref, vp_ref, bias_ref, o_ref, lse_ref, *, dil, hps):
    first = pl.program_id(1) == 0
    hb = pl.program_id(2)
    col = lax.broadcasted_iota(jnp.int32, (BLK, 2 * BLK), 1)
    edge = jnp.where(first & (col < BLK), NEG, 0.0)
    lane = lax.broadcasted_iota(jnp.int32, (BLK, BLK), 1)

    @pl.when(hb == 0)
    def _():
        lse_ref[...] = jnp.zeros_like(lse_ref)

    for r in range(dil):
        rows = pl.ds(r, BLK, stride=dil)
        lse_rows = lse_ref[rows, :]
        for hh in range(hps):
            q = q_ref[hh, rows, :].astype(BF16)
            k = jnp.concatenate([kp_ref[hh, rows, :], kc_ref[hh, rows, :]], axis=0).astype(BF16)
            v = jnp.concatenate([vp_ref[hh, rows, :], vc_ref[hh, rows, :]], axis=0).astype(BF16)
            h = hb * hps + hh
            s = lax.dot_general(q, k, (((1,), (1,)), ((), ())), preferred_element_type=F32)
            s = s * SCALE + bias_ref[h] + edge
            m = jnp.max(s, axis=-1, keepdims=True)
            p = jnp.exp(s - m)
            l = jnp.sum(p, axis=-1, keepdims=True)
            o_ref[hh, rows, :] = jnp.dot(p.astype(BF16), v, preferred_element_type=F32) / l
            lse_rows = jnp.where(lane == h, m + jnp.log(l), lse_rows)
        lse_ref[rows, :] = lse_rows


def _dil_prompt_group(qkv_hm, bias, grp, dil, b, s):
    span = BLK * dil
    nsp = s // span
    hps = N_HEADS // dil
    nhb = N_HEADS // hps

    def slab(part, prev):
        base = (grp * 3 + part) * N_HEADS // hps
        if prev:
            return lambda bi, sp, hb: (base + hb, bi * nsp + jnp.maximum(sp - 1, 0), 0)
        return lambda bi, sp, hb: (base + hb, bi * nsp + sp, 0)

    blk = (hps, span, HEAD_DIM)
    return pl.pallas_call(
        functools.partial(_dil_prompt_kernel, dil=dil, hps=hps),
        out_shape=(jax.ShapeDtypeStruct((N_HEADS, b * s, HEAD_DIM), F32),
                   jax.ShapeDtypeStruct((b * s, BLK), F32)),
        grid=(b, nsp, nhb),
        in_specs=[pl.BlockSpec(blk, slab(0, False)), pl.BlockSpec(blk, slab(1, False)),
                  pl.BlockSpec(blk, slab(1, True)), pl.BlockSpec(blk, slab(2, False)),
                  pl.BlockSpec(blk, slab(2, True)),
                  pl.BlockSpec((N_HEADS, BLK, 2 * BLK), lambda bi, sp, hb: (0, 0, 0))],
        out_specs=(pl.BlockSpec(blk, lambda bi, sp, hb: (hb, bi * nsp + sp, 0)),
                   pl.BlockSpec((span, BLK), lambda bi, sp, hb: (bi * nsp + sp, 0))),
        compiler_params=_params(("parallel", "parallel", "arbitrary")),
        name=f"dil_attn_prompt_g{grp}",
    )(qkv_hm, qkv_hm, qkv_hm, qkv_hm, qkv_hm, bias)


def _dil_combine_kernel(o0_ref, o1_ref, o2_ref, l0_ref, l1_ref, l2_ref, o_ref, *, head_major):
    l0, l1, l2 = l0_ref[...], l1_ref[...], l2_ref[...]
    mx = jnp.maximum(jnp.maximum(l0, l1), l2)
    e0, e1, e2 = jnp.exp(l0 - mx), jnp.exp(l1 - mx), jnp.exp(l2 - mx)
    den = e0 + e1 + e2
    w0, w1, w2 = e0 / den, e1 / den, e2 / den
    for h in range(N_HEADS):
        sl = slice(h * HEAD_DIM, (h + 1) * HEAD_DIM)
        g0, g1, g2 = ((r[h] for r in (o0_ref, o1_ref, o2_ref)) if head_major
                      else (r[:, sl] for r in (o0_ref, o1_ref, o2_ref)))
        o = (w0[:, h:h + 1] * g0 + w1[:, h:h + 1] * g1) + w2[:, h:h + 1] * g2
        o_ref[:, sl] = o.astype(o_ref.dtype)


def _dil_combine(outs, lses, head_major):
    m = lses[0].shape[0]
    wide = N_HEADS * HEAD_DIM
    tm = _pick(m, (256, 128, 32))
    ob = pl.BlockSpec((tm, wide), lambda i: (i, 0))
    ib = pl.BlockSpec((N_HEADS, tm, HEAD_DIM), lambda i: (0, i, 0)) if head_major else ob
    lb = pl.BlockSpec((tm, BLK), lambda i: (i, 0))
    return pl.pallas_call(
        functools.partial(_dil_combine_kernel, head_major=head_major),
        out_shape=jax.ShapeDtypeStruct((m, wide), BF16),
        grid=(m // tm,),
        in_specs=[ib, ib, ib, lb, lb, lb],
        out_specs=ob,
        compiler_params=_params(("parallel",)),
        name="dil_combine",
    )(*outs, *lses)


HEAD_TILE = 8


def _heads_first(x):
    return pltpu.einshape("mhd->hmd", x)


def _dil_sample_kernel(q_ref, k_ref, v_ref, kn_ref, vn_ref, bias_ref, biasn_ref, o_ref, lse_ref,
                       m_scr, l_scr, acc_scr):
    ht, c = pl.program_id(1), pl.program_id(2)
    k_all = _heads_first(k_ref[...])
    v_all = _heads_first(v_ref[...])

    @pl.when(c == 0)
    def _():
        m_scr[...] = jnp.full_like(m_scr, NEG)
        l_scr[...] = jnp.zeros_like(l_scr)
        acc_scr[...] = jnp.zeros_like(acc_scr)

    @pl.when((c == 0) & (ht == 0))
    def _():
        lse_ref[...] = jnp.zeros_like(lse_ref)

    def update(ks, vs, bias):
        s = jnp.concatenate(
            [lax.dot_general(q_ref[0, :, hh * HEAD_DIM:(hh + 1) * HEAD_DIM].astype(BF16), ks[hh].astype(BF16),
                             (((1,), (1,)), ((), ())), preferred_element_type=F32) for hh in range(HEAD_TILE)], axis=0)
        s = s * SCALE + bias
        m_old = m_scr[...]
        m_new = jnp.maximum(m_old, jnp.max(s, axis=-1, keepdims=True))
        alpha = jnp.exp(m_old - m_new)
        p = jnp.where(bias > 0.5 * NEG, jnp.exp(s - m_new), 0.0)
        l_scr[...] = alpha * l_scr[...] + jnp.sum(p, axis=-1, keepdims=True)
        pv = jnp.concatenate(
            [jnp.dot(p[hh * 8:(hh + 1) * 8].astype(BF16), vs[hh].astype(BF16), preferred_element_type=F32)
             for hh in range(HEAD_TILE)], axis=0)
        acc_scr[...] = alpha * acc_scr[...] + pv
        m_scr[...] = m_new

    n_keys = k_ref.shape[0]
    update(k_all, v_all, bias_ref[:, 0].reshape(HEAD_TILE * 8, n_keys))

    @pl.when(c == pl.num_programs(2) - 1)
    def _():
        cols = [slice(hh * HEAD_DIM, (hh + 1) * HEAD_DIM) for hh in range(HEAD_TILE)]
        update([kn_ref[0, :, sl] for sl in cols], [vn_ref[0, :, sl] for sl in cols],
               biasn_ref[...].reshape(HEAD_TILE * 8, BLK))
        l = jnp.maximum(l_scr[...], 1e-30)
        o = acc_scr[...] / l
        lse_rows = m_scr[...] + jnp.log(l)
        lane = lax.broadcasted_iota(jnp.int32, (8, BLK), 1)
        lse = lse_ref[0]
        for hh in range(HEAD_TILE):
            o_ref[0, :, cols[hh]] = o[hh * 8:(hh + 1) * 8]
            lse = jnp.where(lane == ht * HEAD_TILE + hh, lse_rows[hh * 8:(hh + 1) * 8], lse)
        lse_ref[0] = lse


def _dil_sample_group(q, kn, vn, buf, bias, bias_new, dil):
    n, lb = buf.shape[:2]
    n_cls = bias.shape[1]
    wide = N_HEADS * HEAD_DIM
    half = wide // 2
    rows = lb // dil
    tiles = 2 * N_HEADS // HEAD_TILE
    view = buf.reshape(n, rows, dil * tiles, HEAD_TILE, HEAD_DIM)
    nht = N_HEADS // HEAD_TILE
    return pl.pallas_call(
        _dil_sample_kernel,
        out_shape=(jax.ShapeDtypeStruct((n, 8, wide), F32), jax.ShapeDtypeStruct((n, 8, BLK), F32)),
        grid=(n, nht, n_cls),
        in_specs=[
            pl.BlockSpec((1, 8, half), lambda i, ht, c: (i, 0, ht)),
            pl.BlockSpec((None, rows, None, HEAD_TILE, HEAD_DIM), lambda i, ht, c: (i, 0, c * tiles + ht, 0, 0)),
            pl.BlockSpec((None, rows, None, HEAD_TILE, HEAD_DIM),
                         lambda i, ht, c: (i, 0, c * tiles + nht + ht, 0, 0)),
            pl.BlockSpec((1, BLK, half), lambda i, ht, c: (i, 0, ht)),
            pl.BlockSpec((1, BLK, half), lambda i, ht, c: (i, 0, ht)),
            pl.BlockSpec((HEAD_TILE, 1, 8, rows), lambda i, ht, c: (ht, c, 0, 0)),
            pl.BlockSpec((HEAD_TILE, 8, BLK), lambda i, ht, c: (ht, 0, 0)),
        ],
        out_specs=(pl.BlockSpec((1, 8, half), lambda i, ht, c: (i, 0, ht)),
                   pl.BlockSpec((1, 8, BLK), lambda i, ht, c: (i, 0, 0))),
        scratch_shapes=[pltpu.VMEM((HEAD_TILE * 8, 1), F32), pltpu.VMEM((HEAD_TILE * 8, 1), F32),
                        pltpu.VMEM((HEAD_TILE * 8, HEAD_DIM), F32)],
        compiler_params=_params(("parallel", "arbitrary", "arbitrary")),
        name="dil_attn_sample",
    )(q, view, view, kn, vn, bias, bias_new)


def _vector_rows_kernel(x_ref, o_ref):
    x = x_ref[...]
    if x.ndim == 2:
        x = jnp.stack([x[:, j * HEAD_DIM:(j + 1) * HEAD_DIM] for j in range(HEAD_TILE)])
    o_ref[...] = pltpu.einshape("hmd->mhd", x)


def _vector_rows(x, tile0, n_tiles, b, s, keep):
    tr = min(keep, 512)
    r0 = (s - keep) // tr
    per_seq = s // tr
    if x.ndim == 3:
        in_spec = pl.BlockSpec((HEAD_TILE, tr, HEAD_DIM), lambda bi, r, c: (tile0 + c, bi * per_seq + r0 + r, 0))
    else:
        in_spec = pl.BlockSpec((tr, HEAD_TILE * HEAD_DIM), lambda bi, r, c: (bi * per_seq + r0 + r, tile0 + c))
    return pl.pallas_call(
        _vector_rows_kernel,
        out_shape=jax.ShapeDtypeStruct((b, keep, n_tiles * HEAD_TILE, HEAD_DIM), F32),
        grid=(b, keep // tr, n_tiles),
        in_specs=[in_spec],
        out_specs=pl.BlockSpec((None, tr, HEAD_TILE, HEAD_DIM), lambda bi, r, c: (bi, r, c, 0)),
        compiler_params=_params(("parallel", "parallel", "parallel")),
        name="vector_rows",
    )(x)


def _pad_rows(x, rows):
    return jnp.pad(x, ((0, 0), (0, rows - x.shape[1]), (0, 0)))


def _layer_a(hp, hs, b, s, n, t, norm_g, w_in, w_out, rel_bias, bufs):
    wide = N_HEADS * HEAD_DIM
    qkv_s, w_in = _project(hs, w_in, gain=norm_g, cast_w=True)
    qkv_s = qkv_s.reshape(n, t, QKV_A)
    qkv_p = _project(hp, w_in, gain=norm_g, heads=True)
    outs_p, lses_p, outs_s, lses_s, new_p, new_s = [], [], [], [], [], []
    for grp, (win, dil) in enumerate(DIL_PAIRS):
        base = grp * 3 * wide
        bias = _bias_table(rel_bias, 1, BLK, 2 * BLK, BLK * dil, 0, dil, -dil, 0, win).reshape(N_HEADS, BLK, 2 * BLK)
        o, lse = _dil_prompt_group(qkv_p, bias, grp, dil, b, s)
        outs_p.append(o)
        lses_p.append(lse)
        keep = min(win, s)
        kv = _vector_rows(qkv_p, (grp * 3 + 1) * N_HEADS // HEAD_TILE, 2 * N_HEADS // HEAD_TILE, b, s, keep)
        new_p.append(kv.reshape(b, keep, 2, N_HEADS, HEAD_DIM))
        buf = bufs[grp]
        lb = buf.shape[1]
        n_cls = min(dil, t)
        bias_buf = _bias_table(rel_bias, n_cls, 8, lb // dil, lb, -1, 1, -dil, 0, win, dil)
        bias_new = _bias_table(rel_bias, 1, 8, BLK, 0, 0, 1, -1, 0, win, dil).reshape(N_HEADS, 8, BLK)
        q = _pad_rows(qkv_s[:, :, base:base + wide], 8)
        kn = _pad_rows(qkv_s[:, :, base + wide:base + 2 * wide], BLK)
        vn = _pad_rows(qkv_s[:, :, base + 2 * wide:base + 3 * wide], BLK)
        o, lse = _dil_sample_group(q, kn, vn, buf, bias_buf, bias_new, dil)
        outs_s.append(o[:, :t].reshape(n * t, wide))
        lses_s.append(lse[:, :t].reshape(n * t, BLK))
        kv_new = qkv_s[:, :, base + wide:base + 3 * wide].reshape(n, t, 2, N_HEADS, HEAD_DIM)
        new_s.append(jnp.concatenate([buf, kv_new], axis=1)[:, t:])
    hs, w_out = _project(_dil_combine(outs_s, lses_s, False), w_out, res=hs, cast_w=True)
    hp = _project(_dil_combine(outs_p, lses_p, True), w_out, res=hp)
    return hp, hs, new_p, new_s


PAGES_PER_STEP = 16
CHUNKS_PER_PAGE = BLK // CMP_STRIDE
NT_DIMS = (((1,), (1,)), ((), ()))
TN_DIMS = (((0,), (0,)), ((), ()))


def _page_spec(p, half):
    return pl.BlockSpec((BLK, HEAD_TILE, HEAD_DIM),
                        lambda i, j, *rest: (rest[-1][i, j * PAGES_PER_STEP + p], half, 0))


def _cmp_proj_kernel(pt_ref, *refs):
    pages = refs[:PAGES_PER_STEP]
    w_ref, o_ref = refs[PAGES_PER_STEP:PAGES_PER_STEP + 2]
    ys = [pltpu.einshape("ctgd->tgcd", pg[...].reshape(CHUNKS_PER_PAGE, CMP_STRIDE, HEAD_TILE, HEAD_DIM))
          for pg in pages]
    for c in range(2):
        lhs = jnp.concatenate(
            [jnp.concatenate([y[t, c * KV_HEADS + kh] for kh in range(KV_HEADS) for y in ys], axis=0)
             for t in range(CMP_STRIDE)], axis=1).astype(BF16)
        acc = jnp.dot(lhs, w_ref[c], preferred_element_type=F32)
        for kh in range(KV_HEADS):
            o_ref[0, c, kh] = acc[kh * BLK:(kh + 1) * BLK]


def _cmp_proj(pages, page_table, w1r):
    n, n_pages = page_table.shape
    chunks = n_pages * CHUNKS_PER_PAGE
    grid_spec = pltpu.PrefetchScalarGridSpec(
        num_scalar_prefetch=1,
        grid=(n, n_pages // PAGES_PER_STEP),
        in_specs=[_page_spec(p, 0) for p in range(PAGES_PER_STEP)]
        + [pl.BlockSpec((2, CMP_STRIDE * HEAD_DIM, 2 * HEAD_DIM), lambda i, j, pt: (0, 0, 0))],
        out_specs=pl.BlockSpec((1, 2, KV_HEADS, BLK, 2 * HEAD_DIM), lambda i, j, pt: (i, 0, 0, j, 0)),
    )
    return pl.pallas_call(
        _cmp_proj_kernel,
        out_shape=jax.ShapeDtypeStruct((n, 2, KV_HEADS, chunks, 2 * HEAD_DIM), F32),
        grid_spec=grid_spec,
        compiler_params=_params(("parallel", "arbitrary")),
        name="nsa_cmp_proj",
    )(page_table, *([pages] * PAGES_PER_STEP), w1r.reshape(2, CMP_STRIDE * HEAD_DIM, 2 * HEAD_DIM))


def _finish_compress(a, pe_row, w1f, b1, w2, b2, n_blocks):
    rows = a.shape[0]
    cst = jnp.dot(pe_row, w1f, preferred_element_type=F32)[0:1]
    h = (b1 + cst) + a[:, :HEAD_DIM] + pltpu.roll(a[:, HEAD_DIM:], rows - 1, 0)
    x = jnp.dot(jax.nn.gelu(h).astype(BF16), w2, preferred_element_type=F32) + b2
    return jnp.where(lax.broadcasted_iota(jnp.int32, x.shape, 0) < n_blocks, x, 0.0)


def _split3(x):
    hi = x.astype(BF16)
    r = x - hi.astype(F32)
    mid = r.astype(BF16)
    return hi, mid, (r - mid.astype(F32)).astype(BF16)


def _top_n(score, n, axis):
    idx = lax.broadcasted_iota(jnp.int32, score.shape, axis).astype(F32)
    big = float(score.shape[axis])

    def body(_, carry):
        sc, sel = carry
        mx = jnp.max(sc, axis=axis, keepdims=True)
        first = jnp.min(jnp.where(sc == mx, idx, big), axis=axis, keepdims=True)
        hit = idx == first
        return jnp.where(hit, -jnp.inf, sc), jnp.where(hit, 1.0, sel)

    return lax.fori_loop(0, n, body, (score, jnp.zeros(score.shape, F32)))[1]


def _sel_scores(p_slc, blk, cur, n_blocks):
    forced = (blk == 0) | (blk == cur) | (blk == cur - 1)
    score = jnp.where(forced, FORCED_SCORE, jnp.where(blk <= cur, p_slc, -1.0))
    return jnp.where(blk < n_blocks, score, -2.0)


def _nsa_prompt_kernel(q_ref, a_ref, pe_ref, w1f_ref, b1_ref, w2_ref, b2_ref, ksel_ref, vsel_ref, kwin_ref,
                       vwin_ref, tbl_ref, tblw_ref, biasc_ref, gates_ref, msel_ref, o_ref, kc_scr, vc_scr, sel_scr,
                       vselt_ref, vwint_ref, *, n_cmp, n_sel):
    i = pl.program_id(2)

    @pl.when(i == 0)
    def _():
        for c, scr in ((0, kc_scr), (1, vc_scr)):
            scr[...] = _finish_compress(a_ref[0, c, 0], pe_ref[c], w1f_ref[c], b1_ref[c], w2_ref[c], b2_ref[c],
                                        n_cmp).astype(BF16)
        for src, dst in ((vsel_ref, vselt_ref), (vwin_ref, vwint_ref)):
            for c in range(src.shape[0] // BLK):
                dst[:, c * BLK:(c + 1) * BLK] = src[c * BLK:(c + 1) * BLK, :].T.astype(BF16)

    q = q_ref[...]
    qs = jnp.concatenate([q[:, g * HEAD_DIM:(g + 1) * HEAD_DIM] for g in range(GROUP)], axis=0).astype(BF16)
    key_i = lax.broadcasted_iota(jnp.int32, (BLK, BLK), 0)
    tok_i = lax.broadcasted_iota(jnp.int32, (BLK, BLK), 1)

    def lanes4(x):
        return jnp.concatenate([x] * GROUP, axis=1)

    qs2 = jnp.concatenate([q[:, g * HEAD_DIM:(g + 1) * HEAD_DIM] for g in range(GROUP)],
                          axis=0) * (SCALE * LOG2E)
    qs2 = qs2.astype(BF16)

    mask_c = lanes4(i * BLK + tok_i - (key_i * CMP_STRIDE + (CMP_BLOCK - 1)) >= 0)
    s = lax.dot_general(kc_scr[...], qs, NT_DIMS, preferred_element_type=F32) * SCALE
    s = jnp.where(mask_c, s + jnp.concatenate([biasc_ref[g, 0] for g in range(GROUP)], axis=1), NEG)
    m = jnp.max(s, axis=0, keepdims=True)
    p = jnp.where(mask_c, jnp.exp(s - m), 0.0)
    pn = p / jnp.maximum(jnp.sum(p, axis=0, keepdims=True), 1e-30)
    o_cmp = lax.dot_general(vc_scr[...], pn.astype(BF16), TN_DIMS, preferred_element_type=F32)
    pc = ((pn[:, 0:BLK] + pn[:, BLK:2 * BLK]) + pn[:, 2 * BLK:3 * BLK]) + pn[:, 3 * BLK:4 * BLK]

    msel = msel_ref[...]
    p_slc = sum(jnp.dot(msel, part, preferred_element_type=F32) for part in _split3(pc))
    blk = lax.broadcasted_iota(jnp.int32, p_slc.shape, 0)
    cur = (i * BLK + lax.broadcasted_iota(jnp.int32, p_slc.shape, 1)) // SEL_BLOCK
    sel_scr[...] = (1.0 - _top_n(_sel_scores(p_slc, blk, cur, n_sel), SEL_TOPN, 0)) * NEG

    def attend(k_ref, vt_ref, bias_ref, first_blk, n_blk, extra, carry):
        m_run, l_run, acc = carry
        off = pl.multiple_of(first_blk * BLK, BLK)
        k = k_ref[pl.ds(off, n_blk * BLK), :].astype(BF16)
        bias = []
        for j in range(n_blk):
            idx = jnp.maximum(i - first_blk - j, -1) + 1
            bias.append(jnp.concatenate([bias_ref[g, idx] for g in range(GROUP)], axis=1))
        s = lax.dot_general(k, qs2, NT_DIMS, preferred_element_type=F32) + jnp.concatenate(bias, axis=0)
        if extra is not None:
            s = s + extra
        m_new = jnp.maximum(m_run, jnp.max(s, axis=0, keepdims=True))
        alpha = jnp.exp2(m_run - m_new)
        p = jnp.exp2(s - m_new)
        l_new = alpha * l_run + jnp.sum(p, axis=0, keepdims=True)
        vt = vt_ref[:, pl.ds(off, n_blk * BLK)]
        return m_new, l_new, alpha * acc + jnp.dot(vt, p.astype(BF16), preferred_element_type=F32)

    init = (jnp.full((1, GROUP * BLK), NEG, F32), jnp.zeros((1, GROUP * BLK), F32),
            jnp.zeros((HEAD_DIM, GROUP * BLK), F32))

    sel_span = 4
    per_blk = BLK // SEL_BLOCK

    def sel_step(c, carry):
        rows = sel_scr[pl.ds(pl.multiple_of(c * sel_span * per_blk, 8), sel_span * per_blk), :]
        unpicked = jnp.concatenate([jnp.broadcast_to(rows[u:u + 1], (SEL_BLOCK, BLK))
                                    for u in range(sel_span * per_blk)], axis=0)
        return attend(ksel_ref, vselt_ref, tbl_ref, c * sel_span, sel_span, lanes4(unpicked), carry)

    _, l_sel, acc_sel = lax.fori_loop(0, i // sel_span + 1, sel_step, init)
    o_sel = acc_sel / jnp.maximum(l_sel, 1e-30)

    n_win = (NSA_WINDOW - 1 + BLK - 1) // BLK + 1
    _, l_win, acc_win = attend(kwin_ref, vwint_ref, tblw_ref, jnp.maximum(i - (n_win - 1), 0), n_win, None, init)
    o_win = acc_win / jnp.maximum(l_win, 1e-30)

    gt = gates_ref[0]

    def gate(branch):
        return jnp.concatenate([gt[branch * GROUP + g:branch * GROUP + g + 1, :] for g in range(GROUP)], axis=1)

    o = (gate(0) * o_cmp + gate(1) * o_sel) + gate(2) * o_win
    for g in range(GROUP):
        o_ref[0, g] = o[:, g * BLK:(g + 1) * BLK]


def _nsa_prompt(proj, a_cmp, gates_t, cmp_w, tbl, tbl_win, bias_c, msel, b, s):
    pe, w1f, b1, w2, b2 = cmp_w
    nq = s // BLK
    kcol = NSA_Q // HEAD_DIM
    const = lambda shape: pl.BlockSpec(shape, lambda bi, kh, i: (0,) * len(shape))
    return pl.pallas_call(
        functools.partial(_nsa_prompt_kernel, n_cmp=s // CMP_STRIDE - 1, n_sel=s // SEL_BLOCK),
        out_shape=jax.ShapeDtypeStruct((b, N_HEADS, HEAD_DIM, s), F32),
        grid=(b, KV_HEADS, nq),
        in_specs=[
            pl.BlockSpec((BLK, GROUP * HEAD_DIM), lambda bi, kh, i: (bi * nq + i, kh)),
            pl.BlockSpec((1, 2, 1, s // CMP_STRIDE, 2 * HEAD_DIM), lambda bi, kh, i: (bi, 0, kh, 0, 0)),
            const(pe.shape), const(w1f.shape), const(b1.shape), const(w2.shape), const(b2.shape),
            pl.BlockSpec((s, HEAD_DIM), lambda bi, kh, i: (bi, kcol + 2 * KV_HEADS + kh)),
            pl.BlockSpec((s, HEAD_DIM), lambda bi, kh, i: (bi, kcol + 3 * KV_HEADS + kh)),
            pl.BlockSpec((s, HEAD_DIM), lambda bi, kh, i: (bi, kcol + 4 * KV_HEADS + kh)),
            pl.BlockSpec((s, HEAD_DIM), lambda bi, kh, i: (bi, kcol + 5 * KV_HEADS + kh)),
            pl.BlockSpec((GROUP,) + tbl.shape[1:], lambda bi, kh, i: (kh, 0, 0, 0)),
            pl.BlockSpec((GROUP,) + tbl_win.shape[1:], lambda bi, kh, i: (kh, 0, 0, 0)),
            pl.BlockSpec((GROUP, 1, s // CMP_STRIDE, BLK), lambda bi, kh, i: (kh, 0, 0, i)),
            pl.BlockSpec((1, 16, BLK), lambda bi, kh, i: (kh, 0, bi * nq + i)),
            const(msel.shape),
        ],
        out_specs=pl.BlockSpec((1, GROUP, HEAD_DIM, BLK), lambda bi, kh, i: (bi, kh, 0, i)),
        scratch_shapes=[pltpu.VMEM((s // CMP_STRIDE, HEAD_DIM), BF16), pltpu.VMEM((s // CMP_STRIDE, HEAD_DIM), BF16),
                        pltpu.VMEM((s // SEL_BLOCK, BLK), F32),
                        pltpu.VMEM((HEAD_DIM, s), BF16), pltpu.VMEM((HEAD_DIM, s), BF16)],
        compiler_params=_params(("parallel", "parallel", "arbitrary")),
        name="nsa_prompt",
    )(proj, a_cmp, pe, w1f, b1, w2, b2, proj, proj, proj, proj, tbl, tbl_win, bias_c, gates_t, msel)


def _masked_softmax_rows(s, mask):
    s = jnp.where(mask, s, NEG)
    m = jnp.max(s, axis=-1, keepdims=True)
    p = jnp.where(mask, jnp.exp(s - m), 0.0)
    return p / jnp.maximum(jnp.sum(p, axis=-1, keepdims=True), 1e-30)


def _nsa_sample_select_kernel(q_ref, a_ref, pe_ref, w1f_ref, b1_ref, w2_ref, b2_ref, biasc_ref, msel_ref,
                              regroup_ref, ocmp_ref, sel_ref, *, n_cmp, n_sel, past):
    pcs = []
    for kh in range(KV_HEADS):
        kc, vc = (_finish_compress(a_ref[0, c, kh], pe_ref[c], w1f_ref[c], b1_ref[c], w2_ref[c], b2_ref[c],
                                   n_cmp).astype(BF16) for c in range(2))
        bias = biasc_ref[kh]
        s = lax.dot_general(q_ref[0, kh].astype(BF16), kc, NT_DIMS, preferred_element_type=F32) * SCALE + bias
        pn = _masked_softmax_rows(s, bias > 0.5 * NEG)
        ocmp_ref[0, kh] = jnp.dot(pn.astype(BF16), vc, preferred_element_type=F32)
        pcs.append(((pn[0:8] + pn[8:16]) + pn[16:24]) + pn[24:32])
    pc = jnp.concatenate(pcs, axis=0)
    msel = msel_ref[...]
    p_slc = sum(jnp.dot(part, msel, preferred_element_type=F32) for part in _split3(pc))
    blk = lax.broadcasted_iota(jnp.int32, p_slc.shape, 1)
    cur = (past + lax.broadcasted_iota(jnp.int32, p_slc.shape, 0) % 8) // SEL_BLOCK
    sel = _top_n(_sel_scores(p_slc, blk, cur, n_sel), SEL_TOPN, 1).astype(BF16)
    for j in range(regroup_ref.shape[0]):
        part = jnp.dot(sel, regroup_ref[j], preferred_element_type=F32)
        for kh in range(KV_HEADS):
            sel_ref[0, kh, j] = part[kh * 8:(kh + 1) * 8]


def _nsa_sample_select(q, a_cmp, cmp_w, bias_c, msel, regroup, past, t_len):
    pe, w1f, b1, w2, b2 = cmp_w
    n = q.shape[0]
    chunks = a_cmp.shape[3]
    n_steps = regroup.shape[0]
    const = lambda shape: pl.BlockSpec(shape, lambda i: (0,) * len(shape))
    return pl.pallas_call(
        functools.partial(_nsa_sample_select_kernel, n_cmp=chunks - 1, n_sel=(past + t_len + SEL_BLOCK - 1) // SEL_BLOCK,
                          past=past),
        out_shape=(jax.ShapeDtypeStruct((n, KV_HEADS, GROUP * 8, HEAD_DIM), F32),
                   jax.ShapeDtypeStruct((n, KV_HEADS, n_steps, 8, BLK), F32)),
        grid=(n,),
        in_specs=[
            pl.BlockSpec((1, KV_HEADS, GROUP * 8, HEAD_DIM), lambda i: (i, 0, 0, 0)),
            pl.BlockSpec((1, 2, KV_HEADS, chunks, 2 * HEAD_DIM), lambda i: (i, 0, 0, 0, 0)),
            const(pe.shape), const(w1f.shape), const(b1.shape), const(w2.shape), const(b2.shape),
            const(bias_c.shape), const(msel.shape), const(regroup.shape),
        ],
        out_specs=(pl.BlockSpec((1, KV_HEADS, GROUP * 8, HEAD_DIM), lambda i: (i, 0, 0, 0)),
                   pl.BlockSpec((1, KV_HEADS, n_steps, 8, BLK), lambda i: (i, 0, 0, 0, 0))),
        compiler_params=_params(("parallel",)),
        name="nsa_sample_select",
    )(q, a_cmp, pe, w1f, b1, w2, b2, bias_c, msel, regroup)


def _nsa_sample_attend_kernel(pt_ref, *refs):
    pages = refs[:PAGES_PER_STEP]
    (q_ref, sel_ref, expand_ref, bias_ref, kn_ref, vn_ref, biasn_ref, win_ref, kwn_ref, vwn_ref, biasw_ref,
     biaswn_ref, gates_ref, ocmp_ref, o_ref, m_scr, l_scr, acc_scr) = refs[PAGES_PER_STEP:]
    j = pl.program_id(1)
    n_steps = pl.num_programs(1)

    @pl.when(j == 0)
    def _():
        m_scr[...] = jnp.full_like(m_scr, NEG)
        l_scr[...] = jnp.zeros_like(l_scr)
        acc_scr[...] = jnp.zeros_like(acc_scr)

    def picked(kh, step):
        sel = sel_ref[0, kh, step].astype(BF16)
        return jnp.concatenate([sel] * GROUP, axis=0)

    def update(kh, k, v, bias, mask):
        qs = q_ref[0, kh].astype(BF16)
        s = lax.dot_general(qs, k, NT_DIMS, preferred_element_type=F32) * SCALE + bias
        s = jnp.where(mask, s, NEG)
        m_old = m_scr[kh]
        m_new = jnp.maximum(m_old, jnp.max(s, axis=-1, keepdims=True))
        alpha = jnp.exp(m_old - m_new)
        p = jnp.where(mask, jnp.exp(s - m_new), 0.0)
        l_scr[kh] = alpha * l_scr[kh] + jnp.sum(p, axis=-1, keepdims=True)
        acc_scr[kh] = alpha * acc_scr[kh] + jnp.dot(p.astype(BF16), v, preferred_element_type=F32)
        m_scr[kh] = m_new

    tiles = [_heads_first(pg[...]) for pg in pages]
    expand = expand_ref[...]
    for kh in range(KV_HEADS):
        k = jnp.concatenate([tl[kh] for tl in tiles], axis=0).astype(BF16)
        v = jnp.concatenate([tl[KV_HEADS + kh] for tl in tiles], axis=0).astype(BF16)
        in_sel = jnp.dot(picked(kh, j), expand, preferred_element_type=F32) > 0.5
        update(kh, k, v, bias_ref[kh], in_sel)

    @pl.when(j == n_steps - 1)
    def _():
        win = _heads_first(win_ref[...])
        for kh in range(KV_HEADS):
            biasn = biasn_ref[kh]
            new_sel = picked(kh, n_steps)[:, 0:1] > 0.5
            update(kh, kn_ref[0, kh].astype(BF16), vn_ref[0, kh].astype(BF16), biasn, (biasn > 0.5 * NEG) & new_sel)
            o_sel = acc_scr[kh] / jnp.maximum(l_scr[kh], 1e-30)
            biasw = jnp.concatenate([biasw_ref[kh], biaswn_ref[kh]], axis=1)
            kw = jnp.concatenate([win[kh], kwn_ref[0, kh]], axis=0).astype(BF16)
            vw = jnp.concatenate([win[KV_HEADS + kh], vwn_ref[0, kh]], axis=0).astype(BF16)
            sw = lax.dot_general(q_ref[0, kh].astype(BF16), kw, NT_DIMS, preferred_element_type=F32) * SCALE + biasw
            pw = _masked_softmax_rows(sw, biasw > 0.5 * NEG)
            o_win = jnp.dot(pw.astype(BF16), vw, preferred_element_type=F32)
            o_ref[0, kh] = ((gates_ref[0, 0, kh] * ocmp_ref[0, kh] + gates_ref[1, 0, kh] * o_sel)
                            + gates_ref[2, 0, kh] * o_win)


def _nsa_sample_attend(pool, page_table, q, sel, expand, bias_sel, k_new, v_new, bias_new, win_buf, kw_new, vw_new,
                       bias_win, bias_win_new, gates, o_cmp):
    n, n_pages = page_table.shape
    n_steps = n_pages // PAGES_PER_STEP
    keys = PAGES_PER_STEP * BLK
    rows = GROUP * 8
    lw = win_buf.shape[1]
    per = lambda shape: pl.BlockSpec((1, KV_HEADS) + shape, lambda i, j, pt: (i, 0) + (0,) * len(shape))
    whole = lambda shape: pl.BlockSpec(shape, lambda i, j, pt: (0,) * len(shape))
    grid_spec = pltpu.PrefetchScalarGridSpec(
        num_scalar_prefetch=1,
        grid=(n, n_steps),
        in_specs=[_page_spec(p, 1) for p in range(PAGES_PER_STEP)] + [
            per((rows, HEAD_DIM)),
            per((n_steps + 1, 8, BLK)),
            whole(expand.shape),
            pl.BlockSpec((KV_HEADS, rows, keys), lambda i, j, pt: (0, 0, j)),
            per((BLK, HEAD_DIM)), per((BLK, HEAD_DIM)), whole((KV_HEADS, rows, BLK)),
            pl.BlockSpec((None, lw, HEAD_TILE, HEAD_DIM), lambda i, j, pt: (i, 0, 0, 0)),
            per((BLK, HEAD_DIM)), per((BLK, HEAD_DIM)), whole((KV_HEADS, rows, lw)), whole((KV_HEADS, rows, BLK)),
            pl.BlockSpec((3, 1, KV_HEADS, rows, HEAD_DIM), lambda i, j, pt: (0, i, 0, 0, 0)),
            per((rows, HEAD_DIM)),
        ],
        out_specs=per((rows, HEAD_DIM)),
        scratch_shapes=[pltpu.VMEM((KV_HEADS, rows, 1), F32), pltpu.VMEM((KV_HEADS, rows, 1), F32),
                        pltpu.VMEM((KV_HEADS, rows, HEAD_DIM), F32)],
    )
    return pl.pallas_call(
        _nsa_sample_attend_kernel,
        out_shape=jax.ShapeDtypeStruct((n, KV_HEADS, rows, HEAD_DIM), F32),
        grid_spec=grid_spec,
        compiler_params=_params(("parallel", "arbitrary")),
        name="nsa_sample_attend",
    )(page_table, *([pool] * PAGES_PER_STEP), q, sel, expand, bias_sel, k_new, v_new, bias_new, win_buf, kw_new,
      vw_new, bias_win, bias_win_new, gates, o_cmp)


def _sel_weights(n_cmp_rows, n_sel_cols):
    ratio = SEL_BLOCK // CMP_STRIDE
    span = CMP_BLOCK // CMP_STRIDE
    c = jnp.arange(n_cmp_rows)[:, None]
    j = jnp.arange(n_sel_cols)[None, :]
    o = c - ratio * j + (span - 1)
    cnt = jnp.minimum(o, span - 1) - jnp.maximum(o - (ratio - 1), 0) + 1
    return jnp.where((o >= 0) & (o <= ratio + span - 2), cnt, 0).astype(BF16)


def _layer_b(hp, hs, b, s, n, t, norm_g, w_in, w_out, rel_bias, cmp, pool, page_table, win_buf):
    cmp_pe, cmp_w1, cmp_b1, cmp_w2, cmp_b2 = cmp
    n_kvcol = 6 * NSA_KV
    w_main = w_in
    w_gate = jnp.pad(w_in[:, NSA_Q + n_kvcol:], ((0, 0), (0, BLK - 3 * N_HEADS))).astype(BF16)
    w1 = cmp_w1.reshape(2, 2, CMP_STRIDE, HEAD_DIM, HEAD_DIM)
    w1r = jnp.concatenate([w1[:, 0], w1[:, 1]], axis=-1).astype(BF16)
    pe_row = jnp.pad(cmp_pe.reshape(2, 1, CMP_BLOCK * HEAD_DIM), ((0, 0), (0, 7), (0, 0))).astype(BF16)
    cmp_w = (pe_row, cmp_w1.reshape(2, CMP_BLOCK * HEAD_DIM, HEAD_DIM).astype(BF16), cmp_b1.reshape(2, 1, HEAD_DIM),
             cmp_w2.astype(BF16), cmp_b2.reshape(2, 1, HEAD_DIM))
    weights = (w_main, w_gate, w_out, w1r, cmp_w)
    hs, new_win_s, new_kv_s, w_main, w_out = _nsa_sample_path(hs, n, t, norm_g, weights, rel_bias, pool, page_table,
                                                              win_buf)
    weights = (w_main, w_gate, w_out, w1r, cmp_w)
    hp, new_win_p, new_kv_p = _nsa_prompt_path(hp, b, s, norm_g, weights, rel_bias)
    return hp, hs, new_win_p, new_win_s, new_kv_p, new_kv_s


def _nsa_prompt_path(hp, b, s, norm_g, weights, rel_bias):
    w_main, w_gate, w_out, w1r, cmp_w = weights
    proj_p = _project(hp, w_main, gain=norm_g)
    gates_p = _project(hp, w_gate, gain=norm_g, sigmoid=True)
    col_tile = HEAD_TILE * HEAD_DIM
    new_kv = _vector_rows(proj_p, NSA_Q // col_tile, 4 * NSA_KV // col_tile, b, s, s)
    keep = min(NSA_WINDOW, s)
    new_win = _vector_rows(proj_p, (NSA_Q + 4 * NSA_KV) // col_tile, 2 * NSA_KV // col_tile, b, s, keep)
    table_p = jnp.arange(b * s // BLK, dtype=jnp.int32).reshape(b, s // BLK)
    a_cmp_p = _cmp_proj(new_kv.reshape(b * s, 2 * HEAD_TILE, HEAD_DIM), table_p, w1r)
    nq = s // BLK
    n_win = (NSA_WINDOW - 1 + BLK - 1) // BLK + 1
    tbl = _bias_table(rel_bias, nq + 1, BLK, BLK, -BLK, BLK, -1, 1, 0, 1 << 30, scale=LOG2E)
    tbl_win = _bias_table(rel_bias, n_win + 1, BLK, BLK, -BLK, BLK, -1, 1, 0, NSA_WINDOW - 1, scale=LOG2E)
    bias_c = _bias_table(rel_bias, 1, s // CMP_STRIDE, s, -(CMP_BLOCK - 1), 0, -CMP_STRIDE, 1, 0, 1 << 30)
    gates_t = gates_p[:, :3 * N_HEADS].reshape(b * s, 3, KV_HEADS, GROUP).transpose(2, 1, 3, 0)
    gates_t = jnp.pad(gates_t.reshape(KV_HEADS, 3 * GROUP, b * s), ((0, 0), (0, 16 - 3 * GROUP), (0, 0)))
    msel_p = _sel_weights(s // CMP_STRIDE, s // SEL_BLOCK).T
    o_t = _nsa_prompt(proj_p, a_cmp_p, gates_t, cmp_w, tbl, tbl_win, bias_c, msel_p, b, s)
    o_p = o_t.transpose(0, 3, 1, 2).reshape(b * s, NSA_Q).astype(BF16)
    hp = _project(o_p, w_out, res=hp)
    return (hp, new_win.reshape(b, keep, 2, KV_HEADS, HEAD_DIM), new_kv.reshape(b, s, 4, KV_HEADS, HEAD_DIM))


def _nsa_sample_path(hs, n, t, norm_g, weights, rel_bias, pool, page_table, win_buf):
    w_main, w_gate, w_out, w1r, cmp_w = weights
    past = page_table.shape[1] * BLK
    proj_s, w_main = _project(hs, w_main, n_cols=NSA_Q + 6 * NSA_KV, gain=norm_g, cast_w=True)
    gates_s = _project(hs, w_gate, gain=norm_g, sigmoid=True)
    kv_s = proj_s[:, NSA_Q:].reshape(n, t, 6, KV_HEADS, HEAD_DIM)
    rows = GROUP * 8

    def head_rows(x):
        x = jnp.pad(x.transpose(0, 2, 3, 1, 4), ((0, 0), (0, 0), (0, 0), (0, 8 - t), (0, 0)))
        return x.reshape(n, KV_HEADS, rows, x.shape[-1])

    def new_rows(c):
        return jnp.pad(kv_s[:, :, c].transpose(0, 2, 1, 3), ((0, 0), (0, 0), (0, BLK - t), (0, 0)))

    def head_table(x, cols):
        return x.reshape(KV_HEADS, rows, cols)

    q_s = head_rows(proj_s[:, :NSA_Q].reshape(n, t, KV_HEADS, GROUP, HEAD_DIM))
    pool2 = pool.reshape(pool.shape[0] * BLK, 2 * HEAD_TILE, HEAD_DIM)
    a_cmp_s = _cmp_proj(pool2, page_table, w1r)
    chunks = past // CMP_STRIDE
    big = 1 << 30
    bias_cs = head_table(_bias_table(rel_bias, 1, 8, chunks, past - (CMP_BLOCK - 1), 0, 1, -CMP_STRIDE, 0, big), chunks)
    n_steps = page_table.shape[1] // PAGES_PER_STEP
    n_sel_pad = (n_steps + 1) * BLK
    msel_s = _sel_weights(chunks, n_sel_pad)
    per_step = PAGES_PER_STEP * BLK // SEL_BLOCK
    jj = jnp.arange(n_sel_pad)[None, :, None]
    ll = jnp.arange(BLK)[None, None, :]
    st = jnp.arange(n_steps + 1)[:, None, None]
    regroup = ((jj == st * per_step + ll) & (ll < per_step)).astype(BF16)
    o_cmp_s, sel_s = _nsa_sample_select(q_s, a_cmp_s, cmp_w, bias_cs, msel_s, regroup, past, t)
    expand = (jnp.arange(BLK)[:, None] == jnp.arange(PAGES_PER_STEP * BLK)[None, :] // SEL_BLOCK).astype(BF16)
    bias_sel = head_table(_bias_table(rel_bias, 1, 8, past, past, 0, 1, -1, 0, big), past)
    bias_new = head_table(_bias_table(rel_bias, 1, 8, BLK, 0, 0, 1, -1, 0, big), BLK)
    lw = win_buf.shape[1]
    bias_win = head_table(_bias_table(rel_bias, 1, 8, lw, lw, 0, 1, -1, 0, NSA_WINDOW - 1), lw)
    bias_win_new = head_table(_bias_table(rel_bias, 1, 8, BLK, 0, 0, 1, -1, 0, NSA_WINDOW - 1), BLK)
    g_s = gates_s[:, :3 * N_HEADS].reshape(n, t, 3, KV_HEADS, GROUP, 1)
    g_s = jnp.stack([head_rows(g_s[:, :, c]) for c in range(3)])
    g_s = jnp.broadcast_to(g_s, (3, n, KV_HEADS, rows, HEAD_DIM))
    o_s = _nsa_sample_attend(pool2, page_table, q_s, sel_s, expand, bias_sel, new_rows(2), new_rows(3), bias_new,
                             win_buf.reshape(n, lw, HEAD_TILE, HEAD_DIM), new_rows(4), new_rows(5), bias_win,
                             bias_win_new,
                             g_s, o_cmp_s)
    o_s = o_s.reshape(n, KV_HEADS, GROUP, 8, HEAD_DIM)[:, :, :, :t].transpose(0, 3, 1, 2, 4)
    hs, w_out = _project(o_s.reshape(n * t, NSA_Q).astype(BF16), w_out, res=hs, cast_w=True)
    new_win_s = jnp.concatenate([win_buf, kv_s[:, :, 4:]], axis=1)[:, t:]
    return hs, new_win_s, kv_s[:, :, :4], w_main, w_out


def _ffn_and_ple(hp, hs, b, s, n, t, i, norm_ffn, norm_ple, w_ffn_in, conv_w, conv_b, w_ffn_out, state_conv,
                 p_prompt, p_sample, w_ple_gate, w_ple_proj):
    hu_s, w_ffn_in = _project(hs, w_ffn_in, layer=i, gain=norm_ffn, cast_w=True)
    hu_p = _project(hp, w_ffn_in, gain=norm_ffn)
    conv_p = hu_p.reshape(b, s, 2 * D_FF)[:, s - (CONV_W - 1):, :D_FF]
    conv_s = jnp.concatenate([state_conv, hu_s.reshape(n, t, 2 * D_FF)[:, :, :D_FF]], axis=1)[:, t:]
    hs, w_ffn_out = _ffn_out_sample(hu_s, state_conv, conv_w, conv_b, w_ffn_out, i, hs, t)
    hp = _ffn_out_prompt(hu_p, conv_w, conv_b, w_ffn_out, hp, s)
    hs, w_ple_gate, w_ple_proj = _ple_add(hs, norm_ple, w_ple_gate, p_sample.astype(BF16), w_ple_proj, layer=i,
                                          cast_w=True)
    hp = _ple_add(hp, norm_ple, w_ple_gate, p_prompt.astype(BF16), w_ple_proj)
    return hp, hs, conv_p, conv_s


def kernel(x_prompt, x_sample, state_dil_w128, state_dil_w512, state_dil_w2048, state_nsa_win, state_conv,
           cache_nsa_kv, page_table, p_prompt, p_sample, rel_bias, norm_mix, norm_ffn, norm_ple, norm_final,
           w_in_a, w_out_a, w_in_b, w_out_b, cmp_pe, cmp_w1, cmp_b1, cmp_w2, cmp_b2, w_ffn_in, conv_w, conv_b,
           w_ffn_out, w_ple_gate, w_ple_proj):
    b, s, d = x_prompt.shape
    n, t, _ = x_sample.shape
    depth = norm_mix.shape[0]
    hp, hs = x_prompt.reshape(b * s, d), x_sample.reshape(n * t, d)
    dil_p, dil_s = [[] for _ in range(N_DIL)], [[] for _ in range(N_DIL)]
    win_p, win_s, kv_p, kv_s, conv_p, conv_s = [], [], [], [], [], []
    for i in range(depth):
        li = i // 2
        if i % 2 == 0:
            hp, hs, new_p, new_s = _layer_a(
                hp, hs, b, s, n, t, norm_mix[i], w_in_a[li], w_out_a[li], rel_bias,
                (state_dil_w128[li], state_dil_w512[li], state_dil_w2048[li]))
            for g in range(N_DIL):
                dil_p[g].append(new_p[g])
                dil_s[g].append(new_s[g])
        else:
            hp, hs, wp, ws, rp, rs = _layer_b(
                hp, hs, b, s, n, t, norm_mix[i], w_in_b[li], w_out_b[li], rel_bias,
                (cmp_pe[li], cmp_w1[li], cmp_b1[li], cmp_w2[li], cmp_b2[li]), cache_nsa_kv[li], page_table,
                state_nsa_win[li])
            win_p.append(wp)
            win_s.append(ws)
            kv_p.append(rp)
            kv_s.append(rs)
        hp, hs, cp, cs = _ffn_and_ple(
            hp, hs, b, s, n, t, i, norm_ffn[i], norm_ple[i], w_ffn_in, conv_w[i], conv_b[i],
            w_ffn_out, state_conv[i], p_prompt[i].reshape(b * s, -1), p_sample[i].reshape(n * t, -1),
            w_ple_gate, w_ple_proj)
        conv_p.append(cp)
        conv_s.append(cs)
    y_prompt = _rmsnorm(hp, norm_final, F32).reshape(b, s, d)
    y_sample = _rmsnorm(hs, norm_final, F32).reshape(n, t, d)
    return (y_prompt, y_sample,
            jnp.stack(dil_p[0]), jnp.stack(dil_s[0]), jnp.stack(dil_p[1]), jnp.stack(dil_s[1]),
            jnp.stack(dil_p[2]), jnp.stack(dil_s[2]),
            jnp.stack(win_p), jnp.stack(win_s), jnp.stack(conv_p), jnp.stack(conv_s),
            jnp.stack(kv_p), jnp.stack(kv_s))
```

```python
import functools

import jax
import jax.numpy as jnp
from jax import lax
from jax.experimental import pallas as pl
from jax.experimental.pallas import tpu as pltpu

F32 = jnp.float32
BF16 = jnp.bfloat16

D_MODEL = 2048
HEAD_DIM = 128
N_HEADS = 16
DIL_PAIRS = ((128, 1), (512, 4), (2048, 16))
N_DIL = 3
BLK = 128
KV_HEADS = 4
GROUP = 4
CMP_BLOCK = 32
CMP_STRIDE = 16
SEL_BLOCK = 64
SEL_TOPN = 16
NSA_WINDOW = 512
D_FF = 5632
CONV_W = 3
REL_BUCKETS = 32
EPS = 1e-6
NEG = -1e30
FORCED_SCORE = 1e4
SCALE = HEAD_DIM ** -0.5
LOG2E = 1.4426950408889634
QKV_A = N_DIL * 3 * N_HEADS * HEAD_DIM
NSA_Q = N_HEADS * HEAD_DIM
NSA_KV = KV_HEADS * HEAD_DIM

BUCKET_START = (1, 2, 3, 4, 5, 6, 7, 8, 9, 10, 11, 12, 13, 14, 15, 16, 22, 30, 40, 54, 73, 99,
                134, 182, 246, 332, 450, 609, 825, 1117, 1513)

VMEM_LIMIT_V7X = 56 * 1024 * 1024


def _params(sem, vmem=VMEM_LIMIT_V7X):
    return pltpu.CompilerParams(dimension_semantics=sem, vmem_limit_bytes=vmem)


def _pick(n, cands):
    for c in cands:
        if n % c == 0:
            return c
    return n


def _rmsnorm_kernel(x_ref, g_ref, o_ref):
    x = x_ref[...]
    ms = jnp.mean(x * x, axis=-1, keepdims=True)
    o_ref[...] = ((x * lax.rsqrt(ms + EPS)) * g_ref[...]).astype(o_ref.dtype)


def _rmsnorm(x, g, out_dtype):
    m, d = x.shape
    tm = _pick(m, (512, 256, 128, 32))
    return pl.pallas_call(
        _rmsnorm_kernel,
        out_shape=jax.ShapeDtypeStruct((m, d), out_dtype),
        grid=(m // tm,),
        in_specs=[pl.BlockSpec((tm, d), lambda i: (i, 0)), pl.BlockSpec((1, d), lambda i: (0, 0))],
        out_specs=pl.BlockSpec((tm, d), lambda i: (i, 0)),
        compiler_params=_params(("parallel",)),
        name="rmsnorm",
    )(x, g.reshape(1, d))


def _normed(x, g):
    ms = jnp.mean(x * x, axis=-1, keepdims=True)
    return ((x * lax.rsqrt(ms + EPS)) * g).astype(BF16)


def _project_kernel(*refs, norm, cast_w, residual, sigmoid, heads):
    refs = list(refs)
    x_ref = refs.pop(0)
    g_ref = refs.pop(0) if norm else None
    w_ref = refs.pop(0)
    r_ref = refs.pop(0) if residual else None
    o_ref = refs.pop(0)
    wb_ref = refs.pop(0) if cast_w else None
    if norm:
        a_scr = refs.pop(0)

        @pl.when(pl.program_id(1) == 0)
        def _():
            a_scr[...] = _normed(x_ref[...], g_ref[...])

        a = a_scr[...]
    else:
        a = x_ref[...]
    w = w_ref[...]
    if cast_w:
        w = w.astype(BF16)
        wb_ref[...] = w
    acc = jnp.dot(a, w, preferred_element_type=F32)
    if sigmoid:
        acc = jax.nn.sigmoid(acc)
    if residual:
        acc = r_ref[...] + acc
    if heads:
        for j in range(o_ref.shape[0]):
            o_ref[j] = acc[:, j * HEAD_DIM:(j + 1) * HEAD_DIM]
    else:
        o_ref[...] = acc


def _weight_spec(w, layer, k, tn, index):
    if w.ndim == 2:
        return pl.BlockSpec((k, tn), index)
    return pl.BlockSpec((None, k, tn), lambda *g: (layer,) + index(*g))


def _project(x, w, *, layer=None, n_cols=None, gain=None, res=None, cast_w=False, sigmoid=False, heads=False):
    m, k = x.shape
    n = n_cols or w.shape[-1]
    tm = _pick(m, (1024, 512, 256, 128))
    tn = _pick(n, (1024, 512, 256, 128))
    assert not cast_w or m == tm
    norm, residual = gain is not None, res is not None
    args, in_specs = [x], [pl.BlockSpec((tm, k), lambda i, j: (i, 0))]
    if norm:
        args.append(gain.reshape(1, k))
        in_specs.append(pl.BlockSpec((1, k), lambda i, j: (0, 0)))
    args.append(w)
    in_specs.append(_weight_spec(w, layer, k, tn, lambda i, j: (0, j)))
    if residual:
        args.append(res)
        in_specs.append(pl.BlockSpec((tm, tn), lambda i, j: (i, j)))
    if heads:
        out_shape = [jax.ShapeDtypeStruct((n // HEAD_DIM, m, HEAD_DIM), F32)]
        out_specs = [pl.BlockSpec((tn // HEAD_DIM, tm, HEAD_DIM), lambda i, j: (j, i, 0))]
    else:
        out_shape = [jax.ShapeDtypeStruct((m, n), F32)]
        out_specs = [pl.BlockSpec((tm, tn), lambda i, j: (i, j))]
    if cast_w:
        out_shape.append(jax.ShapeDtypeStruct((k, n), BF16))
        out_specs.append(pl.BlockSpec((k, tn), lambda i, j: (0, j)))
    outs = pl.pallas_call(
        functools.partial(_project_kernel, norm=norm, cast_w=cast_w, residual=residual, sigmoid=sigmoid,
                          heads=heads),
        out_shape=out_shape,
        grid=(m // tm, n // tn),
        in_specs=in_specs,
        out_specs=out_specs,
        scratch_shapes=[pltpu.VMEM((tm, k), BF16)] if norm else [],
        compiler_params=_params(("parallel", "arbitrary")),
        name="project",
    )(*args)
    return tuple(outs) if cast_w else outs[0]


def _conv_gelu_val(g, g1, g2, val, cw_ref, cb_ref):
    c = cb_ref[...] + g2 * cw_ref[0:1, :]
    c = c + g1 * cw_ref[1:2, :]
    c = c + g * cw_ref[2:3, :]
    return (jax.nn.gelu(c) * val).astype(BF16)


def _ffn_out_prompt_kernel(g_ref, halo_ref, v_ref, cw_ref, cb_ref, w_ref, r_ref, o_ref, acc_ref, *,
                           tiles_per_seq):
    i, k = pl.program_id(0), pl.program_id(1)

    @pl.when(k == 0)
    def _():
        acc_ref[...] = jnp.zeros_like(acc_ref)

    g = g_ref[...]
    row = lax.broadcasted_iota(jnp.int32, g.shape, 0)
    halo = jnp.where(i % tiles_per_seq == 0, 0.0, halo_ref[...])
    g1 = jnp.where(row == 0, halo[7:8, :], pltpu.roll(g, 1, 0))
    g2 = jnp.where(row == 0, halo[6:7, :], jnp.where(row == 1, halo[7:8, :], pltpu.roll(g, 2, 0)))
    u = _conv_gelu_val(g, g1, g2, v_ref[...], cw_ref, cb_ref)
    acc_ref[...] += jnp.dot(u, w_ref[...], preferred_element_type=F32)

    @pl.when(k == pl.num_programs(1) - 1)
    def _():
        o_ref[...] = r_ref[...] + acc_ref[...]


def _ffn_out_prompt(hu, conv_w, conv_b, w_out, res, seq):
    m = hu.shape[0]
    tm, tk = 512, D_FF // 4
    nk = D_FF // tk
    return pl.pallas_call(
        functools.partial(_ffn_out_prompt_kernel, tiles_per_seq=seq // tm),
        out_shape=jax.ShapeDtypeStruct((m, D_MODEL), F32),
        grid=(m // tm, nk),
        in_specs=[
            pl.BlockSpec((tm, tk), lambda i, k: (i, k)),
            pl.BlockSpec((8, tk), lambda i, k: (jnp.maximum(i * (tm // 8) - 1, 0), k)),
            pl.BlockSpec((tm, tk), lambda i, k: (i, k + nk)),
            pl.BlockSpec((CONV_W, tk), lambda i, k: (0, k)),
            pl.BlockSpec((1, tk), lambda i, k: (0, k)),
            pl.BlockSpec((tk, D_MODEL), lambda i, k: (k, 0)),
            pl.BlockSpec((tm, D_MODEL), lambda i, k: (i, 0)),
        ],
        out_specs=pl.BlockSpec((tm, D_MODEL), lambda i, k: (i, 0)),
        scratch_shapes=[pltpu.VMEM((tm, D_MODEL), F32)],
        compiler_params=_params(("parallel", "arbitrary")),
        name="ffn_out_prompt",
    )(hu, hu, hu, conv_w, conv_b.reshape(1, D_FF), w_out, res)


def _ffn_out_sample_kernel(g_ref, e1_ref, e2_ref, v_ref, cw_ref, cb_ref, w_ref, r_ref, o_ref, wb_ref, acc_ref, *,
                           t_len):
    k = pl.program_id(0)

    @pl.when(k == 0)
    def _():
        acc_ref[...] = jnp.zeros_like(acc_ref)

    g = g_ref[...]
    t = lax.broadcasted_iota(jnp.int32, g.shape, 0) % t_len
    g1 = jnp.where(t == 0, e1_ref[...], pltpu.roll(g, 1, 0))
    g2 = jnp.where(t < 2, e2_ref[...], pltpu.roll(g, 2, 0))
    u = _conv_gelu_val(g, g1, g2, v_ref[...], cw_ref, cb_ref)
    w = w_ref[...].astype(BF16)
    wb_ref[...] = w
    acc_ref[...] += jnp.dot(u, w, preferred_element_type=F32)

    @pl.when(k == pl.num_programs(0) - 1)
    def _():
        o_ref[...] = r_ref[...] + acc_ref[...]


def _ffn_out_sample(hu, conv_prev, conv_w, conv_b, w_out, layer, res, t_len):
    m = hu.shape[0]
    n = m // t_len
    tk = 512
    nk = D_FF // tk
    zeros = jnp.zeros((n, t_len - 1, D_FF), F32)
    e1 = jnp.concatenate([conv_prev[:, 1:2], zeros], axis=1).reshape(m, D_FF)
    e2 = jnp.concatenate([conv_prev, zeros[:, 1:]], axis=1).reshape(m, D_FF)
    return pl.pallas_call(
        functools.partial(_ffn_out_sample_kernel, t_len=t_len),
        out_shape=(jax.ShapeDtypeStruct((m, D_MODEL), F32), jax.ShapeDtypeStruct((D_FF, D_MODEL), BF16)),
        grid=(nk,),
        in_specs=[
            pl.BlockSpec((m, tk), lambda k: (0, k)),
            pl.BlockSpec((m, tk), lambda k: (0, k)),
            pl.BlockSpec((m, tk), lambda k: (0, k)),
            pl.BlockSpec((m, tk), lambda k: (0, k + nk)),
            pl.BlockSpec((CONV_W, tk), lambda k: (0, k)),
            pl.BlockSpec((1, tk), lambda k: (0, k)),
            pl.BlockSpec((None, tk, D_MODEL), lambda k: (layer, k, 0)),
            pl.BlockSpec((m, D_MODEL), lambda k: (0, 0)),
        ],
        out_specs=(pl.BlockSpec((m, D_MODEL), lambda k: (0, 0)), pl.BlockSpec((tk, D_MODEL), lambda k: (k, 0))),
        scratch_shapes=[pltpu.VMEM((m, D_MODEL), F32)],
        compiler_params=_params(("arbitrary",)),
        name="ffn_out_sample",
    )(hu, e1, e2, hu, conv_w, conv_b.reshape(1, D_FF), w_out, res)


def _ple_kernel(h_ref, g_ref, wg_ref, p_ref, wp_ref, o_ref, *rest, cast_w):
    a_scr = rest[-1]
    j = pl.program_id(1)
    tn = o_ref.shape[1]

    @pl.when(j == 0)
    def _():
        a_scr[...] = _normed(h_ref[...], g_ref[...])

    wg, wp = wg_ref[...], wp_ref[...]
    if cast_w:
        wg, wp = wg.astype(BF16), wp.astype(BF16)
        rest[0][...] = wg
        rest[1][...] = wp
    gate = jax.nn.sigmoid(jnp.dot(a_scr[...], wg, preferred_element_type=F32))
    proj = jnp.dot(p_ref[...], wp, preferred_element_type=F32)
    o_ref[...] = h_ref[:, pl.ds(pl.multiple_of(j * tn, tn), tn)] + gate * proj


def _ple_add(h, gain, w_gate, p, w_proj, layer=None, cast_w=False):
    m, d = h.shape
    kp = p.shape[1]
    tm = _pick(m, (1024, 512, 256, 128))
    tn = 1024
    assert not cast_w or m == tm
    out_shape = [jax.ShapeDtypeStruct((m, d), F32)]
    out_specs = [pl.BlockSpec((tm, tn), lambda i, j: (i, j))]
    if cast_w:
        out_shape += [jax.ShapeDtypeStruct((d, d), BF16), jax.ShapeDtypeStruct((kp, d), BF16)]
        out_specs += [pl.BlockSpec((d, tn), lambda i, j: (0, j)), pl.BlockSpec((kp, tn), lambda i, j: (0, j))]
    outs = pl.pallas_call(
        functools.partial(_ple_kernel, cast_w=cast_w),
        out_shape=out_shape,
        grid=(m // tm, d // tn),
        in_specs=[
            pl.BlockSpec((tm, d), lambda i, j: (i, 0)),
            pl.BlockSpec((1, d), lambda i, j: (0, 0)),
            _weight_spec(w_gate, layer, d, tn, lambda i, j: (0, j)),
            pl.BlockSpec((tm, kp), lambda i, j: (i, 0)),
            _weight_spec(w_proj, layer, kp, tn, lambda i, j: (0, j)),
        ],
        out_specs=out_specs,
        scratch_shapes=[pltpu.VMEM((tm, d), BF16)],
        compiler_params=_params(("parallel", "arbitrary")),
        name="ple_add",
    )(h, gain.reshape(1, d), w_gate, p, w_proj)
    return tuple(outs) if cast_w else outs[0]


def _bias_table_kernel(rbt_ref, o_ref, *, a0, ag, ar, ac, lo, hi, mod, scale):
    g = pl.program_id(0)
    n_r, n_c = o_ref.shape[2:]
    shape = (n_r, BLK)
    rows = ar * lax.broadcasted_iota(jnp.int32, shape, 0)
    cols = ac * lax.broadcasted_iota(jnp.int32, shape, 1)
    for c0 in range(0, n_c, BLK):
        dist = (a0 + ac * c0) + ag * g + rows + cols
        d = jnp.maximum(dist, 0)
        bucket = jnp.zeros(shape, jnp.int32)
        for start in BUCKET_START:
            bucket = bucket + (d >= start).astype(jnp.int32)
        ok = (dist >= lo) & (dist <= hi)
        if mod > 1:
            ok = ok & ((d & (mod - 1)) == 0)
        for h in range(N_HEADS):
            row = jnp.broadcast_to(rbt_ref[h:h + 1, :], shape)
            if scale != 1.0:
                row = row * scale
            o_ref[h, 0, :, c0:c0 + BLK] = jnp.where(ok, jnp.take_along_axis(row, bucket, axis=1), NEG)


def _bias_table(rel_bias, n_g, n_r, n_c, a0, ag, ar, ac, lo, hi, mod=1, scale=1.0):
    assert mod & (mod - 1) == 0 and n_c % BLK == 0
    rbt = jnp.pad(rel_bias.T, ((0, 0), (0, BLK - REL_BUCKETS)))
    return pl.pallas_call(
        functools.partial(_bias_table_kernel, a0=a0, ag=ag, ar=ar, ac=ac, lo=lo, hi=hi, mod=mod, scale=scale),
        out_shape=jax.ShapeDtypeStruct((N_HEADS, n_g, n_r, n_c), F32),
        grid=(n_g,),
        in_specs=[pl.BlockSpec((N_HEADS, BLK), lambda g: (0, 0))],
        out_specs=pl.BlockSpec((N_HEADS, 1, n_r, n_c), lambda g: (0, g, 0, 0)),
        compiler_params=_params(("parallel",)),
        name="bias_table",
    )(rbt)


def _dil_prompt_kernel(q_ref, kc_ref, kp_ref, vc_ref, vp_ref, bias_ref, o_ref, lse_ref, *, dil, hps):
    first = pl.program_id(1) == 0
    hb = pl.program_id(2)
    col = lax.broadcasted_iota(jnp.int32, (BLK, 2 * BLK), 1)
    edge = jnp.where(first & (col < BLK), NEG, 0.0)
    lane = lax.broadcasted_iota(jnp.int32, (BLK, BLK), 1)

    @pl.when(hb == 0)
    def _():
        lse_ref[...] = jnp.zeros_like(lse_ref)

    for r in range(dil):
        rows = pl.ds(r, BLK, stride=dil)
        lse_rows = lse_ref[rows, :]
        for hh in range(hps):
            q = q_ref[hh, rows, :].astype(BF16)
            k = jnp.concatenate([kp_ref[hh, rows, :], kc_ref[hh, rows, :]], axis=0).astype(BF16)
            v = jnp.concatenate([vp_ref[hh, rows, :], vc_ref[hh, rows, :]], axis=0).astype(BF16)
            h = hb * hps + hh
            s = lax.dot_general(q, k, (((1,), (1,)), ((), ())), preferred_element_type=F32)
            s = s * SCALE + bias_ref[h] + edge
            m = jnp.max(s, axis=-1, keepdims=True)
            p = jnp.exp(s - m)
            l = jnp.sum(p, axis=-1, keepdims=True)
            o_ref[hh, rows, :] = jnp.dot(p.astype(BF16), v, preferred_element_type=F32) / l
            lse_rows = jnp.where(lane == h, m + jnp.log(l), lse_rows)
        lse_ref[rows, :] = lse_rows


def _dil_prompt_group(qkv_hm, bias, grp, dil, b, s):
    span = BLK * dil
    nsp = s // span
    hps = N_HEADS // dil
    nhb = N_HEADS // hps

    def slab(part, prev):
        base = (grp * 3 + part) * N_HEADS // hps
        if prev:
            return lambda bi, sp, hb: (base + hb, bi * nsp + jnp.maximum(sp - 1, 0), 0)
        return lambda bi, sp, hb: (base + hb, bi * nsp + sp, 0)

    blk = (hps, span, HEAD_DIM)
    return pl.pallas_call(
        functools.partial(_dil_prompt_kernel, dil=dil, hps=hps),
        out_shape=(jax.ShapeDtypeStruct((N_HEADS, b * s, HEAD_DIM), F32),
                   jax.ShapeDtypeStruct((b * s, BLK), F32)),
        grid=(b, nsp, nhb),
        in_specs=[pl.BlockSpec(blk, slab(0, False)), pl.BlockSpec(blk, slab(1, False)),
                  pl.BlockSpec(blk, slab(1, True)), pl.BlockSpec(blk, slab(2, False)),
                  pl.BlockSpec(blk, slab(2, True)),
                  pl.BlockSpec((N_HEADS, BLK, 2 * BLK), lambda bi, sp, hb: (0, 0, 0))],
        out_specs=(pl.BlockSpec(blk, lambda bi, sp, hb: (hb, bi * nsp + sp, 0)),
                   pl.BlockSpec((span, BLK), lambda bi, sp, hb: (bi * nsp + sp, 0))),
        compiler_params=_params(("parallel", "parallel", "arbitrary")),
        name=f"dil_attn_prompt_g{grp}",
    )(qkv_hm, qkv_hm, qkv_hm, qkv_hm, qkv_hm, bias)


def _dil_combine_kernel(o0_ref, o1_ref, o2_ref, l0_ref, l1_ref, l2_ref, o_ref, *, head_major):
    l0, l1, l2 = l0_ref[...], l1_ref[...], l2_ref[...]
    mx = jnp.maximum(jnp.maximum(l0, l1), l2)
    e0, e1, e2 = jnp.exp(l0 - mx), jnp.exp(l1 - mx), jnp.exp(l2 - mx)
    den = e0 + e1 + e2
    w0, w1, w2 = e0 / den, e1 / den, e2 / den
    for h in range(N_HEADS):
        sl = slice(h * HEAD_DIM, (h + 1) * HEAD_DIM)
        g0, g1, g2 = ((r[h] for r in (o0_ref, o1_ref, o2_ref)) if head_major
                      else (r[:, sl] for r in (o0_ref, o1_ref, o2_ref)))
        o = (w0[:, h:h + 1] * g0 + w1[:, h:h + 1] * g1) + w2[:, h:h + 1] * g2
        o_ref[:, sl] = o.astype(o_ref.dtype)


def _dil_combine(outs, lses):
    m = lses[0].shape[0]
    wide = N_HEADS * HEAD_DIM
    tm = _pick(m, (256, 128, 32))
    ob = pl.BlockSpec((tm, wide), lambda i: (i, 0))
    lb = pl.BlockSpec((tm, BLK), lambda i: (i, 0))
    return pl.pallas_call(
        functools.partial(_dil_combine_kernel, head_major=False),
        out_shape=jax.ShapeDtypeStruct((m, wide), BF16),
        grid=(m // tm,),
        in_specs=[ob, ob, ob, lb, lb, lb],
        out_specs=ob,
        compiler_params=_params(("parallel",)),
        name="dil_combine",
    )(*outs, *lses)


def _dil_combine_project_kernel(o0_ref, o1_ref, o2_ref, l0_ref, l1_ref, l2_ref, w_ref, r_ref, o_ref, a_scr):
    @pl.when(pl.program_id(1) == 0)
    def _():
        _dil_combine_kernel(o0_ref, o1_ref, o2_ref, l0_ref, l1_ref, l2_ref, a_scr, head_major=True)

    o_ref[...] = r_ref[...] + jnp.dot(a_scr[...], w_ref[...], preferred_element_type=F32)


def _dil_combine_project(outs, lses, w, res):
    m = lses[0].shape[0]
    k, n = w.shape
    tm, tn = 512, 1024
    ib = pl.BlockSpec((N_HEADS, tm, HEAD_DIM), lambda i, j: (0, i, 0))
    lb = pl.BlockSpec((tm, BLK), lambda i, j: (i, 0))
    tile = pl.BlockSpec((tm, tn), lambda i, j: (i, j))
    return pl.pallas_call(
        _dil_combine_project_kernel,
        out_shape=jax.ShapeDtypeStruct((m, n), F32),
        grid=(m // tm, n // tn),
        in_specs=[ib, ib, ib, lb, lb, lb, pl.BlockSpec((k, tn), lambda i, j: (0, j)), tile],
        out_specs=tile,
        scratch_shapes=[pltpu.VMEM((tm, k), BF16)],
        compiler_params=_params(("parallel", "arbitrary")),
        name="dil_combine_project",
    )(*outs, *lses, w, res)


HEAD_TILE = 8


def _heads_first(x):
    return pltpu.einshape("mhd->hmd", x)


def _dil_sample_kernel(q_ref, k_ref, v_ref, kn_ref, vn_ref, bias_ref, biasn_ref, o_ref, lse_ref,
                       m_scr, l_scr, acc_scr):
    ht, c = pl.program_id(1), pl.program_id(2)
    k_all = _heads_first(k_ref[...])
    v_all = _heads_first(v_ref[...])

    @pl.when(c == 0)
    def _():
        m_scr[...] = jnp.full_like(m_scr, NEG)
        l_scr[...] = jnp.zeros_like(l_scr)
        acc_scr[...] = jnp.zeros_like(acc_scr)

    @pl.when((c == 0) & (ht == 0))
    def _():
        lse_ref[...] = jnp.zeros_like(lse_ref)

    def update(ks, vs, bias):
        s = jnp.concatenate(
            [lax.dot_general(q_ref[0, :, hh * HEAD_DIM:(hh + 1) * HEAD_DIM].astype(BF16), ks[hh].astype(BF16),
                             (((1,), (1,)), ((), ())), preferred_element_type=F32) for hh in range(HEAD_TILE)], axis=0)
        s = s * SCALE + bias
        m_old = m_scr[...]
        m_new = jnp.maximum(m_old, jnp.max(s, axis=-1, keepdims=True))
        alpha = jnp.exp(m_old - m_new)
        p = jnp.where(bias > 0.5 * NEG, jnp.exp(s - m_new), 0.0)
        l_scr[...] = alpha * l_scr[...] + jnp.sum(p, axis=-1, keepdims=True)
        pv = jnp.concatenate(
            [jnp.dot(p[hh * 8:(hh + 1) * 8].astype(BF16), vs[hh].astype(BF16), preferred_element_type=F32)
             for hh in range(HEAD_TILE)], axis=0)
        acc_scr[...] = alpha * acc_scr[...] + pv
        m_scr[...] = m_new

    n_keys = k_ref.shape[0]
    update(k_all, v_all, bias_ref[:, 0].reshape(HEAD_TILE * 8, n_keys))

    @pl.when(c == pl.num_programs(2) - 1)
    def _():
        cols = [slice(hh * HEAD_DIM, (hh + 1) * HEAD_DIM) for hh in range(HEAD_TILE)]
        update([kn_ref[0, :, sl] for sl in cols], [vn_ref[0, :, sl] for sl in cols],
               biasn_ref[...].reshape(HEAD_TILE * 8, BLK))
        l = jnp.maximum(l_scr[...], 1e-30)
        o = acc_scr[...] / l
        lse_rows = m_scr[...] + jnp.log(l)
        lane = lax.broadcasted_iota(jnp.int32, (8, BLK), 1)
        lse = lse_ref[0]
        for hh in range(HEAD_TILE):
            o_ref[0, :, cols[hh]] = o[hh * 8:(hh + 1) * 8]
            lse = jnp.where(lane == ht * HEAD_TILE + hh, lse_rows[hh * 8:(hh + 1) * 8], lse)
        lse_ref[0] = lse


def _dil_sample_group(q, kn, vn, buf, bias, bias_new, dil):
    n, lb = buf.shape[:2]
    n_cls = bias.shape[1]
    wide = N_HEADS * HEAD_DIM
    half = wide // 2
    rows = lb // dil
    tiles = 2 * N_HEADS // HEAD_TILE
    view = buf.reshape(n, rows, dil * tiles, HEAD_TILE, HEAD_DIM)
    nht = N_HEADS // HEAD_TILE
    return pl.pallas_call(
        _dil_sample_kernel,
        out_shape=(jax.ShapeDtypeStruct((n, 8, wide), F32), jax.ShapeDtypeStruct((n, 8, BLK), F32)),
        grid=(n, nht, n_cls),
        in_specs=[
            pl.BlockSpec((1, 8, half), lambda i, ht, c: (i, 0, ht)),
            pl.BlockSpec((None, rows, None, HEAD_TILE, HEAD_DIM), lambda i, ht, c: (i, 0, c * tiles + ht, 0, 0)),
            pl.BlockSpec((None, rows, None, HEAD_TILE, HEAD_DIM),
                         lambda i, ht, c: (i, 0, c * tiles + nht + ht, 0, 0)),
            pl.BlockSpec((1, BLK, half), lambda i, ht, c: (i, 0, ht)),
            pl.BlockSpec((1, BLK, half), lambda i, ht, c: (i, 0, ht)),
            pl.BlockSpec((HEAD_TILE, 1, 8, rows), lambda i, ht, c: (ht, c, 0, 0)),
            pl.BlockSpec((HEAD_TILE, 8, BLK), lambda i, ht, c: (ht, 0, 0)),
        ],
        out_specs=(pl.BlockSpec((1, 8, half), lambda i, ht, c: (i, 0, ht)),
                   pl.BlockSpec((1, 8, BLK), lambda i, ht, c: (i, 0, 0))),
        scratch_shapes=[pltpu.VMEM((HEAD_TILE * 8, 1), F32), pltpu.VMEM((HEAD_TILE * 8, 1), F32),
                        pltpu.VMEM((HEAD_TILE * 8, HEAD_DIM), F32)],
        compiler_params=_params(("parallel", "arbitrary", "arbitrary")),
        name="dil_attn_sample",
    )(q, view, view, kn, vn, bias, bias_new)


def _vector_rows_kernel(x_ref, o_ref):
    x = x_ref[...]
    if x.ndim == 2:
        x = jnp.stack([x[:, j * HEAD_DIM:(j + 1) * HEAD_DIM] for j in range(HEAD_TILE)])
    o_ref[...] = pltpu.einshape("hmd->mhd", x)


def _vector_rows(x, tile0, n_tiles, b, s, keep):
    tr = min(keep, 512)
    r0 = (s - keep) // tr
    per_seq = s // tr
    if x.ndim == 3:
        in_spec = pl.BlockSpec((HEAD_TILE, tr, HEAD_DIM), lambda bi, r, c: (tile0 + c, bi * per_seq + r0 + r, 0))
    else:
        in_spec = pl.BlockSpec((tr, HEAD_TILE * HEAD_DIM), lambda bi, r, c: (bi * per_seq + r0 + r, tile0 + c))
    return pl.pallas_call(
        _vector_rows_kernel,
        out_shape=jax.ShapeDtypeStruct((b, keep, n_tiles * HEAD_TILE, HEAD_DIM), F32),
        grid=(b, keep // tr, n_tiles),
        in_specs=[in_spec],
        out_specs=pl.BlockSpec((None, tr, HEAD_TILE, HEAD_DIM), lambda bi, r, c: (bi, r, c, 0)),
        compiler_params=_params(("parallel", "parallel", "parallel")),
        name="vector_rows",
    )(x)


def _pad_rows(x, rows):
    return jnp.pad(x, ((0, 0), (0, rows - x.shape[1]), (0, 0)))


def _layer_a(hp, hs, b, s, n, t, norm_g, w_in, w_out, rel_bias, bufs):
    wide = N_HEADS * HEAD_DIM
    qkv_s, w_in = _project(hs, w_in, gain=norm_g, cast_w=True)
    qkv_s = qkv_s.reshape(n, t, QKV_A)
    qkv_p = _project(hp, w_in, gain=norm_g, heads=True)
    outs_p, lses_p, outs_s, lses_s, new_p, new_s = [], [], [], [], [], []
    for grp, (win, dil) in enumerate(DIL_PAIRS):
        base = grp * 3 * wide
        bias = _bias_table(rel_bias, 1, BLK, 2 * BLK, BLK * dil, 0, dil, -dil, 0, win).reshape(N_HEADS, BLK, 2 * BLK)
        o, lse = _dil_prompt_group(qkv_p, bias, grp, dil, b, s)
        outs_p.append(o)
        lses_p.append(lse)
        keep = min(win, s)
        kv = _vector_rows(qkv_p, (grp * 3 + 1) * N_HEADS // HEAD_TILE, 2 * N_HEADS // HEAD_TILE, b, s, keep)
        new_p.append(kv.reshape(b, keep, 2, N_HEADS, HEAD_DIM))
        buf = bufs[grp]
        lb = buf.shape[1]
        n_cls = min(dil, t)
        bias_buf = _bias_table(rel_bias, n_cls, 8, lb // dil, lb, -1, 1, -dil, 0, win, dil)
        bias_new = _bias_table(rel_bias, 1, 8, BLK, 0, 0, 1, -1, 0, win, dil).reshape(N_HEADS, 8, BLK)
        q = _pad_rows(qkv_s[:, :, base:base + wide], 8)
        kn = _pad_rows(qkv_s[:, :, base + wide:base + 2 * wide], BLK)
        vn = _pad_rows(qkv_s[:, :, base + 2 * wide:base + 3 * wide], BLK)
        o, lse = _dil_sample_group(q, kn, vn, buf, bias_buf, bias_new, dil)
        outs_s.append(o[:, :t].reshape(n * t, wide))
        lses_s.append(lse[:, :t].reshape(n * t, BLK))
        kv_new = qkv_s[:, :, base + wide:base + 3 * wide].reshape(n, t, 2, N_HEADS, HEAD_DIM)
        new_s.append(jnp.concatenate([buf, kv_new], axis=1)[:, t:])
    hs, w_out = _project(_dil_combine(outs_s, lses_s), w_out, res=hs, cast_w=True)
    hp = _dil_combine_project(outs_p, lses_p, w_out, hp)
    return hp, hs, new_p, new_s


PAGES_PER_STEP = 16
CHUNKS_PER_PAGE = BLK // CMP_STRIDE
NT_DIMS = (((1,), (1,)), ((), ()))
TN_DIMS = (((0,), (0,)), ((), ()))


def _page_spec(p, half):
    return pl.BlockSpec((BLK, HEAD_TILE, HEAD_DIM),
                        lambda i, j, *rest: (rest[-1][i, j * PAGES_PER_STEP + p], half, 0))


def _cmp_proj_kernel(pt_ref, *refs):
    pages = refs[:PAGES_PER_STEP]
    w_ref, o_ref = refs[PAGES_PER_STEP:PAGES_PER_STEP + 2]
    ys = [pltpu.einshape("ctgd->tgcd", pg[...].reshape(CHUNKS_PER_PAGE, CMP_STRIDE, HEAD_TILE, HEAD_DIM))
          for pg in pages]
    for c in range(2):
        lhs = jnp.concatenate(
            [jnp.concatenate([y[t, c * KV_HEADS + kh] for kh in range(KV_HEADS) for y in ys], axis=0)
             for t in range(CMP_STRIDE)], axis=1).astype(BF16)
        acc = jnp.dot(lhs, w_ref[c], preferred_element_type=F32)
        for kh in range(KV_HEADS):
            o_ref[0, c, kh] = acc[kh * BLK:(kh + 1) * BLK]


def _cmp_proj(pages, page_table, w1r):
    n, n_pages = page_table.shape
    chunks = n_pages * CHUNKS_PER_PAGE
    grid_spec = pltpu.PrefetchScalarGridSpec(
        num_scalar_prefetch=1,
        grid=(n, n_pages // PAGES_PER_STEP),
        in_specs=[_page_spec(p, 0) for p in range(PAGES_PER_STEP)]
        + [pl.BlockSpec((2, CMP_STRIDE * HEAD_DIM, 2 * HEAD_DIM), lambda i, j, pt: (0, 0, 0))],
        out_specs=pl.BlockSpec((1, 2, KV_HEADS, BLK, 2 * HEAD_DIM), lambda i, j, pt: (i, 0, 0, j, 0)),
    )
    return pl.pallas_call(
        _cmp_proj_kernel,
        out_shape=jax.ShapeDtypeStruct((n, 2, KV_HEADS, chunks, 2 * HEAD_DIM), F32),
        grid_spec=grid_spec,
        compiler_params=_params(("parallel", "arbitrary")),
        name="nsa_cmp_proj",
    )(page_table, *([pages] * PAGES_PER_STEP), w1r.reshape(2, CMP_STRIDE * HEAD_DIM, 2 * HEAD_DIM))


def _finish_compress(a, pe_row, w1f, b1, w2, b2, n_blocks):
    rows = a.shape[0]
    cst = jnp.dot(pe_row, w1f, preferred_element_type=F32)[0:1]
    h = (b1 + cst) + a[:, :HEAD_DIM] + pltpu.roll(a[:, HEAD_DIM:], rows - 1, 0)
    x = jnp.dot(jax.nn.gelu(h).astype(BF16), w2, preferred_element_type=F32) + b2
    return jnp.where(lax.broadcasted_iota(jnp.int32, x.shape, 0) < n_blocks, x, 0.0)


def _split3(x):
    hi = x.astype(BF16)
    r = x - hi.astype(F32)
    mid = r.astype(BF16)
    return hi, mid, (r - mid.astype(F32)).astype(BF16)


def _top_n(score, n, axis):
    idx = lax.broadcasted_iota(jnp.int32, score.shape, axis).astype(F32)
    big = float(score.shape[axis])

    def body(_, carry):
        sc, sel = carry
        mx = jnp.max(sc, axis=axis, keepdims=True)
        first = jnp.min(jnp.where(sc == mx, idx, big), axis=axis, keepdims=True)
        hit = idx == first
        return jnp.where(hit, -jnp.inf, sc), jnp.where(hit, 1.0, sel)

    return lax.fori_loop(0, n, body, (score, jnp.zeros(score.shape, F32)))[1]


def _sel_scores(p_slc, blk, cur, n_blocks):
    forced = (blk == 0) | (blk == cur) | (blk == cur - 1)
    score = jnp.where(forced, FORCED_SCORE, jnp.where(blk <= cur, p_slc, -1.0))
    return jnp.where(blk < n_blocks, score, -2.0)


def _nsa_prompt_kernel(q_ref, a_ref, pe_ref, w1f_ref, b1_ref, w2_ref, b2_ref, ksel_ref, vsel_ref, kwin_ref,
                       vwin_ref, tbl_ref, tblw_ref, biasc_ref, gates_ref, msel_ref, o_ref, kc_scr, vc_scr, sel_scr,
                       vselt_ref, vwint_ref, *, n_cmp, n_sel):
    i = pl.program_id(2)

    @pl.when(i == 0)
    def _():
        for c, scr in ((0, kc_scr), (1, vc_scr)):
            scr[...] = _finish_compress(a_ref[0, c, 0], pe_ref[c], w1f_ref[c], b1_ref[c], w2_ref[c], b2_ref[c],
                                        n_cmp).astype(BF16)
        for src, dst in ((vsel_ref, vselt_ref), (vwin_ref, vwint_ref)):
            for c in range(src.shape[0] // BLK):
                dst[:, c * BLK:(c + 1) * BLK] = src[c * BLK:(c + 1) * BLK, :].T.astype(BF16)

    q = q_ref[...]
    qs = jnp.concatenate([q[:, g * HEAD_DIM:(g + 1) * HEAD_DIM] for g in range(GROUP)], axis=0).astype(BF16)
    key_i = lax.broadcasted_iota(jnp.int32, (BLK, BLK), 0)
    tok_i = lax.broadcasted_iota(jnp.int32, (BLK, BLK), 1)

    def lanes4(x):
        return jnp.concatenate([x] * GROUP, axis=1)

    qs2 = jnp.concatenate([q[:, g * HEAD_DIM:(g + 1) * HEAD_DIM] for g in range(GROUP)],
                          axis=0) * (SCALE * LOG2E)
    qs2 = qs2.astype(BF16)

    mask_c = lanes4(i * BLK + tok_i - (key_i * CMP_STRIDE + (CMP_BLOCK - 1)) >= 0)
    s = lax.dot_general(kc_scr[...], qs, NT_DIMS, preferred_element_type=F32) * SCALE
    s = jnp.where(mask_c, s + jnp.concatenate([biasc_ref[g, 0] for g in range(GROUP)], axis=1), NEG)
    m = jnp.max(s, axis=0, keepdims=True)
    p = jnp.where(mask_c, jnp.exp(s - m), 0.0)
    pn = p / jnp.maximum(jnp.sum(p, axis=0, keepdims=True), 1e-30)
    o_cmp = lax.dot_general(vc_scr[...], pn.astype(BF16), TN_DIMS, preferred_element_type=F32)
    pc = ((pn[:, 0:BLK] + pn[:, BLK:2 * BLK]) + pn[:, 2 * BLK:3 * BLK]) + pn[:, 3 * BLK:4 * BLK]

    msel = msel_ref[...]
    p_slc = sum(jnp.dot(msel, part, preferred_element_type=F32) for part in _split3(pc))
    blk = lax.broadcasted_iota(jnp.int32, p_slc.shape, 0)
    cur = (i * BLK + lax.broadcasted_iota(jnp.int32, p_slc.shape, 1)) // SEL_BLOCK
    sel_scr[...] = (1.0 - _top_n(_sel_scores(p_slc, blk, cur, n_sel), SEL_TOPN, 0)) * NEG

    def attend(k_ref, vt_ref, bias_ref, first_blk, n_blk, extra, carry):
        m_run, l_run, acc = carry
        off = pl.multiple_of(first_blk * BLK, BLK)
        k = k_ref[pl.ds(off, n_blk * BLK), :].astype(BF16)
        bias = []
        for j in range(n_blk):
            idx = jnp.maximum(i - first_blk - j, -1) + 1
            bias.append(jnp.concatenate([bias_ref[g, idx] for g in range(GROUP)], axis=1))
        s = lax.dot_general(k, qs2, NT_DIMS, preferred_element_type=F32) + jnp.concatenate(bias, axis=0)
        if extra is not None:
            s = s + extra
        m_new = jnp.maximum(m_run, jnp.max(s, axis=0, keepdims=True))
        alpha = jnp.exp2(m_run - m_new)
        p = jnp.exp2(s - m_new)
        l_new = alpha * l_run + jnp.sum(p, axis=0, keepdims=True)
        vt = vt_ref[:, pl.ds(off, n_blk * BLK)]
        return m_new, l_new, alpha * acc + jnp.dot(vt, p.astype(BF16), preferred_element_type=F32)

    init = (jnp.full((1, GROUP * BLK), NEG, F32), jnp.zeros((1, GROUP * BLK), F32),
            jnp.zeros((HEAD_DIM, GROUP * BLK), F32))

    sel_span = 4
    per_blk = BLK // SEL_BLOCK

    def sel_step(c, carry):
        rows = sel_scr[pl.ds(pl.multiple_of(c * sel_span * per_blk, 8), sel_span * per_blk), :]
        unpicked = jnp.concatenate([jnp.broadcast_to(rows[u:u + 1], (SEL_BLOCK, BLK))
                                    for u in range(sel_span * per_blk)], axis=0)
        return attend(ksel_ref, vselt_ref, tbl_ref, c * sel_span, sel_span, lanes4(unpicked), carry)

    _, l_sel, acc_sel = lax.fori_loop(0, i // sel_span + 1, sel_step, init)
    o_sel = acc_sel / jnp.maximum(l_sel, 1e-30)

    n_win = (NSA_WINDOW - 1 + BLK - 1) // BLK + 1
    _, l_win, acc_win = attend(kwin_ref, vwint_ref, tblw_ref, jnp.maximum(i - (n_win - 1), 0), n_win, None, init)
    o_win = acc_win / jnp.maximum(l_win, 1e-30)

    gt = gates_ref[0]

    def gate(branch):
        return jnp.concatenate([gt[branch * GROUP + g:branch * GROUP + g + 1, :] for g in range(GROUP)], axis=1)

    o = (gate(0) * o_cmp + gate(1) * o_sel) + gate(2) * o_win
    for g in range(GROUP):
        o_ref[0, g] = o[:, g * BLK:(g + 1) * BLK]


def _nsa_prompt(proj, a_cmp, gates_t, cmp_w, tbl, tbl_win, bias_c, msel, b, s):
    pe, w1f, b1, w2, b2 = cmp_w
    nq = s // BLK
    kcol = NSA_Q // HEAD_DIM
    const = lambda shape: pl.BlockSpec(shape, lambda bi, kh, i: (0,) * len(shape))
    return pl.pallas_call(
        functools.partial(_nsa_prompt_kernel, n_cmp=s // CMP_STRIDE - 1, n_sel=s // SEL_BLOCK),
        out_shape=jax.ShapeDtypeStruct((b, N_HEADS, HEAD_DIM, s), F32),
        grid=(b, KV_HEADS, nq),
        in_specs=[
            pl.BlockSpec((BLK, GROUP * HEAD_DIM), lambda bi, kh, i: (bi * nq + i, kh)),
            pl.BlockSpec((1, 2, 1, s // CMP_STRIDE, 2 * HEAD_DIM), lambda bi, kh, i: (bi, 0, kh, 0, 0)),
            const(pe.shape), const(w1f.shape), const(b1.shape), const(w2.shape), const(b2.shape),
            pl.BlockSpec((s, HEAD_DIM), lambda bi, kh, i: (bi, kcol + 2 * KV_HEADS + kh)),
            pl.BlockSpec((s, HEAD_DIM), lambda bi, kh, i: (bi, kcol + 3 * KV_HEADS + kh)),
            pl.BlockSpec((s, HEAD_DIM), lambda bi, kh, i: (bi, kcol + 4 * KV_HEADS + kh)),
            pl.BlockSpec((s, HEAD_DIM), lambda bi, kh, i: (bi, kcol + 5 * KV_HEADS + kh)),
            pl.BlockSpec((GROUP,) + tbl.shape[1:], lambda bi, kh, i: (kh, 0, 0, 0)),
            pl.BlockSpec((GROUP,) + tbl_win.shape[1:], lambda bi, kh, i: (kh, 0, 0, 0)),
            pl.BlockSpec((GROUP, 1, s // CMP_STRIDE, BLK), lambda bi, kh, i: (kh, 0, 0, i)),
            pl.BlockSpec((1, 16, BLK), lambda bi, kh, i: (kh, 0, bi * nq + i)),
            const(msel.shape),
        ],
        out_specs=pl.BlockSpec((1, GROUP, HEAD_DIM, BLK), lambda bi, kh, i: (bi, kh, 0, i)),
        scratch_shapes=[pltpu.VMEM((s // CMP_STRIDE, HEAD_DIM), BF16), pltpu.VMEM((s // CMP_STRIDE, HEAD_DIM), BF16),
                        pltpu.VMEM((s // SEL_BLOCK, BLK), F32),
                        pltpu.VMEM((HEAD_DIM, s), BF16), pltpu.VMEM((HEAD_DIM, s), BF16)],
        compiler_params=_params(("parallel", "parallel", "arbitrary")),
        name="nsa_prompt",
    )(proj, a_cmp, pe, w1f, b1, w2, b2, proj, proj, proj, proj, tbl, tbl_win, bias_c, gates_t, msel)


def _masked_softmax_rows(s, mask):
    s = jnp.where(mask, s, NEG)
    m = jnp.max(s, axis=-1, keepdims=True)
    p = jnp.where(mask, jnp.exp(s - m), 0.0)
    return p / jnp.maximum(jnp.sum(p, axis=-1, keepdims=True), 1e-30)


def _nsa_sample_select_kernel(q_ref, a_ref, pe_ref, w1f_ref, b1_ref, w2_ref, b2_ref, biasc_ref, msel_ref,
                              regroup_ref, ocmp_ref, sel_ref, *, n_cmp, n_sel, past):
    pcs = []
    for kh in range(KV_HEADS):
        kc, vc = (_finish_compress(a_ref[0, c, kh], pe_ref[c], w1f_ref[c], b1_ref[c], w2_ref[c], b2_ref[c],
                                   n_cmp).astype(BF16) for c in range(2))
        bias = biasc_ref[kh]
        s = lax.dot_general(q_ref[0, kh].astype(BF16), kc, NT_DIMS, preferred_element_type=F32) * SCALE + bias
        pn = _masked_softmax_rows(s, bias > 0.5 * NEG)
        ocmp_ref[0, kh] = jnp.dot(pn.astype(BF16), vc, preferred_element_type=F32)
        pcs.append(((pn[0:8] + pn[8:16]) + pn[16:24]) + pn[24:32])
    pc = jnp.concatenate(pcs, axis=0)
    msel = msel_ref[...]
    p_slc = sum(jnp.dot(part, msel, preferred_element_type=F32) for part in _split3(pc))
    blk = lax.broadcasted_iota(jnp.int32, p_slc.shape, 1)
    cur = (past + lax.broadcasted_iota(jnp.int32, p_slc.shape, 0) % 8) // SEL_BLOCK
    sel = _top_n(_sel_scores(p_slc, blk, cur, n_sel), SEL_TOPN, 1).astype(BF16)
    for j in range(regroup_ref.shape[0]):
        part = jnp.dot(sel, regroup_ref[j], preferred_element_type=F32)
        for kh in range(KV_HEADS):
            sel_ref[0, kh, j] = part[kh * 8:(kh + 1) * 8]


def _nsa_sample_select(q, a_cmp, cmp_w, bias_c, msel, regroup, past, t_len):
    pe, w1f, b1, w2, b2 = cmp_w
    n = q.shape[0]
    chunks = a_cmp.shape[3]
    n_steps = regroup.shape[0]
    const = lambda shape: pl.BlockSpec(shape, lambda i: (0,) * len(shape))
    return pl.pallas_call(
        functools.partial(_nsa_sample_select_kernel, n_cmp=chunks - 1, n_sel=(past + t_len + SEL_BLOCK - 1) // SEL_BLOCK,
                          past=past),
        out_shape=(jax.ShapeDtypeStruct((n, KV_HEADS, GROUP * 8, HEAD_DIM), F32),
                   jax.ShapeDtypeStruct((n, KV_HEADS, n_steps, 8, BLK), F32)),
        grid=(n,),
        in_specs=[
            pl.BlockSpec((1, KV_HEADS, GROUP * 8, HEAD_DIM), lambda i: (i, 0, 0, 0)),
            pl.BlockSpec((1, 2, KV_HEADS, chunks, 2 * HEAD_DIM), lambda i: (i, 0, 0, 0, 0)),
            const(pe.shape), const(w1f.shape), const(b1.shape), const(w2.shape), const(b2.shape),
            const(bias_c.shape), const(msel.shape), const(regroup.shape),
        ],
        out_specs=(pl.BlockSpec((1, KV_HEADS, GROUP * 8, HEAD_DIM), lambda i: (i, 0, 0, 0)),
                   pl.BlockSpec((1, KV_HEADS, n_steps, 8, BLK), lambda i: (i, 0, 0, 0, 0))),
        compiler_params=_params(("parallel",)),
        name="nsa_sample_select",
    )(q, a_cmp, pe, w1f, b1, w2, b2, bias_c, msel, regroup)


def _nsa_sample_attend_kernel(pt_ref, *refs):
    pages = refs[:PAGES_PER_STEP]
    (q_ref, sel_ref, expand_ref, bias_ref, kn_ref, vn_ref, biasn_ref, win_ref, kwn_ref, vwn_ref, biasw_ref,
     biaswn_ref, gates_ref, ocmp_ref, o_ref, m_scr, l_scr, acc_scr) = refs[PAGES_PER_STEP:]
    j = pl.program_id(1)
    n_steps = pl.num_programs(1)

    @pl.when(j == 0)
    def _():
        m_scr[...] = jnp.full_like(m_scr, NEG)
        l_scr[...] = jnp.zeros_like(l_scr)
        acc_scr[...] = jnp.zeros_like(acc_scr)

    def picked(kh, step):
        sel = sel_ref[0, kh, step].astype(BF16)
        return jnp.concatenate([sel] * GROUP, axis=0)

    def update(kh, k, v, bias, mask):
        qs = q_ref[0, kh].astype(BF16)
        s = lax.dot_general(qs, k, NT_DIMS, preferred_element_type=F32) * SCALE + bias
        s = jnp.where(mask, s, NEG)
        m_old = m_scr[kh]
        m_new = jnp.maximum(m_old, jnp.max(s, axis=-1, keepdims=True))
        alpha = jnp.exp(m_old - m_new)
        p = jnp.where(mask, jnp.exp(s - m_new), 0.0)
        l_scr[kh] = alpha * l_scr[kh] + jnp.sum(p, axis=-1, keepdims=True)
        acc_scr[kh] = alpha * acc_scr[kh] + jnp.dot(p.astype(BF16), v, preferred_element_type=F32)
        m_scr[kh] = m_new

    tiles = [_heads_first(pg[...]) for pg in pages]
    expand = expand_ref[...]
    for kh in range(KV_HEADS):
        k = jnp.concatenate([tl[kh] for tl in tiles], axis=0).astype(BF16)
        v = jnp.concatenate([tl[KV_HEADS + kh] for tl in tiles], axis=0).astype(BF16)
        in_sel = jnp.dot(picked(kh, j), expand, preferred_element_type=F32) > 0.5
        update(kh, k, v, bias_ref[kh], in_sel)

    @pl.when(j == n_steps - 1)
    def _():
        win = _heads_first(win_ref[...])
        for kh in range(KV_HEADS):
            biasn = biasn_ref[kh]
            new_sel = picked(kh, n_steps)[:, 0:1] > 0.5
            update(kh, kn_ref[0, kh].astype(BF16), vn_ref[0, kh].astype(BF16), biasn, (biasn > 0.5 * NEG) & new_sel)
            o_sel = acc_scr[kh] / jnp.maximum(l_scr[kh], 1e-30)
            biasw = jnp.concatenate([biasw_ref[kh], biaswn_ref[kh]], axis=1)
            kw = jnp.concatenate([win[kh], kwn_ref[0, kh]], axis=0).astype(BF16)
            vw = jnp.concatenate([win[KV_HEADS + kh], vwn_ref[0, kh]], axis=0).astype(BF16)
            sw = lax.dot_general(q_ref[0, kh].astype(BF16), kw, NT_DIMS, preferred_element_type=F32) * SCALE + biasw
            pw = _masked_softmax_rows(sw, biasw > 0.5 * NEG)
            o_win = jnp.dot(pw.astype(BF16), vw, preferred_element_type=F32)
            o_ref[0, kh] = ((gates_ref[0, 0, kh] * ocmp_ref[0, kh] + gates_ref[1, 0, kh] * o_sel)
                            + gates_ref[2, 0, kh] * o_win)


def _nsa_sample_attend(pool, page_table, q, sel, expand, bias_sel, k_new, v_new, bias_new, win_buf, kw_new, vw_new,
                       bias_win, bias_win_new, gates, o_cmp):
    n, n_pages = page_table.shape
    n_steps = n_pages // PAGES_PER_STEP
    keys = PAGES_PER_STEP * BLK
    rows = GROUP * 8
    lw = win_buf.shape[1]
    per = lambda shape: pl.BlockSpec((1, KV_HEADS) + shape, lambda i, j, pt: (i, 0) + (0,) * len(shape))
    whole = lambda shape: pl.BlockSpec(shape, lambda i, j, pt: (0,) * len(shape))
    grid_spec = pltpu.PrefetchScalarGridSpec(
        num_scalar_prefetch=1,
        grid=(n, n_steps),
        in_specs=[_page_spec(p, 1) for p in range(PAGES_PER_STEP)] + [
            per((rows, HEAD_DIM)),
            per((n_steps + 1, 8, BLK)),
            whole(expand.shape),
            pl.BlockSpec((KV_HEADS, rows, keys), lambda i, j, pt: (0, 0, j)),
            per((BLK, HEAD_DIM)), per((BLK, HEAD_DIM)), whole((KV_HEADS, rows, BLK)),
            pl.BlockSpec((None, lw, HEAD_TILE, HEAD_DIM), lambda i, j, pt: (i, 0, 0, 0)),
            per((BLK, HEAD_DIM)), per((BLK, HEAD_DIM)), whole((KV_HEADS, rows, lw)), whole((KV_HEADS, rows, BLK)),
            pl.BlockSpec((3, 1, KV_HEADS, rows, HEAD_DIM), lambda i, j, pt: (0, i, 0, 0, 0)),
            per((rows, HEAD_DIM)),
        ],
        out_specs=per((rows, HEAD_DIM)),
        scratch_shapes=[pltpu.VMEM((KV_HEADS, rows, 1), F32), pltpu.VMEM((KV_HEADS, rows, 1), F32),
                        pltpu.VMEM((KV_HEADS, rows, HEAD_DIM), F32)],
    )
    return pl.pallas_call(
        _nsa_sample_attend_kernel,
        out_shape=jax.ShapeDtypeStruct((n, KV_HEADS, rows, HEAD_DIM), F32),
        grid_spec=grid_spec,
        compiler_params=_params(("parallel", "arbitrary")),
        name="nsa_sample_attend",
    )(page_table, *([pool] * PAGES_PER_STEP), q, sel, expand, bias_sel, k_new, v_new, bias_new, win_buf, kw_new,
      vw_new, bias_win, bias_win_new, gates, o_cmp)


def _sel_weights(n_cmp_rows, n_sel_cols):
    ratio = SEL_BLOCK // CMP_STRIDE
    span = CMP_BLOCK // CMP_STRIDE
    c = jnp.arange(n_cmp_rows)[:, None]
    j = jnp.arange(n_sel_cols)[None, :]
    o = c - ratio * j + (span - 1)
    cnt = jnp.minimum(o, span - 1) - jnp.maximum(o - (ratio - 1), 0) + 1
    return jnp.where((o >= 0) & (o <= ratio + span - 2), cnt, 0).astype(BF16)


def _layer_b(hp, hs, b, s, n, t, norm_g, w_in, w_out, rel_bias, cmp, pool, page_table, win_buf):
    cmp_pe, cmp_w1, cmp_b1, cmp_w2, cmp_b2 = cmp
    n_kvcol = 6 * NSA_KV
    w_main = w_in
    w_gate = jnp.pad(w_in[:, NSA_Q + n_kvcol:], ((0, 0), (0, BLK - 3 * N_HEADS))).astype(BF16)
    w1 = cmp_w1.reshape(2, 2, CMP_STRIDE, HEAD_DIM, HEAD_DIM)
    w1r = jnp.concatenate([w1[:, 0], w1[:, 1]], axis=-1).astype(BF16)
    pe_row = jnp.pad(cmp_pe.reshape(2, 1, CMP_BLOCK * HEAD_DIM), ((0, 0), (0, 7), (0, 0))).astype(BF16)
    cmp_w = (pe_row, cmp_w1.reshape(2, CMP_BLOCK * HEAD_DIM, HEAD_DIM).astype(BF16), cmp_b1.reshape(2, 1, HEAD_DIM),
             cmp_w2.astype(BF16), cmp_b2.reshape(2, 1, HEAD_DIM))
    weights = (w_main, w_gate, w_out, w1r, cmp_w)
    hs, new_win_s, new_kv_s, w_main, w_out = _nsa_sample_path(hs, n, t, norm_g, weights, rel_bias, pool, page_table,
                                                              win_buf)
    weights = (w_main, w_gate, w_out, w1r, cmp_w)
    hp, new_win_p, new_kv_p = _nsa_prompt_path(hp, b, s, norm_g, weights, rel_bias)
    return hp, hs, new_win_p, new_win_s, new_kv_p, new_kv_s


def _nsa_prompt_path(hp, b, s, norm_g, weights, rel_bias):
    w_main, w_gate, w_out, w1r, cmp_w = weights
    proj_p = _project(hp, w_main, gain=norm_g)
    gates_p = _project(hp, w_gate, gain=norm_g, sigmoid=True)
    col_tile = HEAD_TILE * HEAD_DIM
    new_kv = _vector_rows(proj_p, NSA_Q // col_tile, 4 * NSA_KV // col_tile, b, s, s)
    keep = min(NSA_WINDOW, s)
    new_win = _vector_rows(proj_p, (NSA_Q + 4 * NSA_KV) // col_tile, 2 * NSA_KV // col_tile, b, s, keep)
    table_p = jnp.arange(b * s // BLK, dtype=jnp.int32).reshape(b, s // BLK)
    a_cmp_p = _cmp_proj(new_kv.reshape(b * s, 2 * HEAD_TILE, HEAD_DIM), table_p, w1r)
    nq = s // BLK
    n_win = (NSA_WINDOW - 1 + BLK - 1) // BLK + 1
    tbl = _bias_table(rel_bias, nq + 1, BLK, BLK, -BLK, BLK, -1, 1, 0, 1 << 30, scale=LOG2E)
    tbl_win = _bias_table(rel_bias, n_win + 1, BLK, BLK, -BLK, BLK, -1, 1, 0, NSA_WINDOW - 1, scale=LOG2E)
    bias_c = _bias_table(rel_bias, 1, s // CMP_STRIDE, s, -(CMP_BLOCK - 1), 0, -CMP_STRIDE, 1, 0, 1 << 30)
    gates_t = gates_p[:, :3 * N_HEADS].reshape(b * s, 3, KV_HEADS, GROUP).transpose(2, 1, 3, 0)
    gates_t = jnp.pad(gates_t.reshape(KV_HEADS, 3 * GROUP, b * s), ((0, 0), (0, 16 - 3 * GROUP), (0, 0)))
    msel_p = _sel_weights(s // CMP_STRIDE, s // SEL_BLOCK).T
    o_t = _nsa_prompt(proj_p, a_cmp_p, gates_t, cmp_w, tbl, tbl_win, bias_c, msel_p, b, s)
    o_p = o_t.transpose(0, 3, 1, 2).reshape(b * s, NSA_Q).astype(BF16)
    hp = _project(o_p, w_out, res=hp)
    return (hp, new_win.reshape(b, keep, 2, KV_HEADS, HEAD_DIM), new_kv.reshape(b, s, 4, KV_HEADS, HEAD_DIM))


def _nsa_sample_path(hs, n, t, norm_g, weights, rel_bias, pool, page_table, win_buf):
    w_main, w_gate, w_out, w1r, cmp_w = weights
    past = page_table.shape[1] * BLK
    proj_s, w_main = _project(hs, w_main, n_cols=NSA_Q + 6 * NSA_KV, gain=norm_g, cast_w=True)
    gates_s = _project(hs, w_gate, gain=norm_g, sigmoid=True)
    kv_s = proj_s[:, NSA_Q:].reshape(n, t, 6, KV_HEADS, HEAD_DIM)
    rows = GROUP * 8

    def head_rows(x):
        x = jnp.pad(x.transpose(0, 2, 3, 1, 4), ((0, 0), (0, 0), (0, 0), (0, 8 - t), (0, 0)))
        return x.reshape(n, KV_HEADS, rows, x.shape[-1])

    def new_rows(c):
        return jnp.pad(kv_s[:, :, c].transpose(0, 2, 1, 3), ((0, 0), (0, 0), (0, BLK - t), (0, 0)))

    def head_table(x, cols):
        return x.reshape(KV_HEADS, rows, cols)

    q_s = head_rows(proj_s[:, :NSA_Q].reshape(n, t, KV_HEADS, GROUP, HEAD_DIM))
    pool2 = pool.reshape(pool.shape[0] * BLK, 2 * HEAD_TILE, HEAD_DIM)
    a_cmp_s = _cmp_proj(pool2, page_table, w1r)
    chunks = past // CMP_STRIDE
    big = 1 << 30
    bias_cs = head_table(_bias_table(rel_bias, 1, 8, chunks, past - (CMP_BLOCK - 1), 0, 1, -CMP_STRIDE, 0, big), chunks)
    n_steps = page_table.shape[1] // PAGES_PER_STEP
    n_sel_pad = (n_steps + 1) * BLK
    msel_s = _sel_weights(chunks, n_sel_pad)
    per_step = PAGES_PER_STEP * BLK // SEL_BLOCK
    jj = jnp.arange(n_sel_pad)[None, :, None]
    ll = jnp.arange(BLK)[None, None, :]
    st = jnp.arange(n_steps + 1)[:, None, None]
    regroup = ((jj == st * per_step + ll) & (ll < per_step)).astype(BF16)
    o_cmp_s, sel_s = _nsa_sample_select(q_s, a_cmp_s, cmp_w, bias_cs, msel_s, regroup, past, t)
    expand = (jnp.arange(BLK)[:, None] == jnp.arange(PAGES_PER_STEP * BLK)[None, :] // SEL_BLOCK).astype(BF16)
    bias_sel = head_table(_bias_table(rel_bias, 1, 8, past, past, 0, 1, -1, 0, big), past)
    bias_new = head_table(_bias_table(rel_bias, 1, 8, BLK, 0, 0, 1, -1, 0, big), BLK)
    lw = win_buf.shape[1]
    bias_win = head_table(_bias_table(rel_bias, 1, 8, lw, lw, 0, 1, -1, 0, NSA_WINDOW - 1), lw)
    bias_win_new = head_table(_bias_table(rel_bias, 1, 8, BLK, 0, 0, 1, -1, 0, NSA_WINDOW - 1), BLK)
    g_s = gates_s[:, :3 * N_HEADS].reshape(n, t, 3, KV_HEADS, GROUP, 1)
    g_s = jnp.stack([head_rows(g_s[:, :, c]) for c in range(3)])
    g_s = jnp.broadcast_to(g_s, (3, n, KV_HEADS, rows, HEAD_DIM))
    o_s = _nsa_sample_attend(pool2, page_table, q_s, sel_s, expand, bias_sel, new_rows(2), new_rows(3), bias_new,
                             win_buf.reshape(n, lw, HEAD_TILE, HEAD_DIM), new_rows(4), new_rows(5), bias_win,
                             bias_win_new,
                             g_s, o_cmp_s)
    o_s = o_s.reshape(n, KV_HEADS, GROUP, 8, HEAD_DIM)[:, :, :, :t].transpose(0, 3, 1, 2, 4)
    hs, w_out = _project(o_s.reshape(n * t, NSA_Q).astype(BF16), w_out, res=hs, cast_w=True)
    new_win_s = jnp.concatenate([win_buf, kv_s[:, :, 4:]], axis=1)[:, t:]
    return hs, new_win_s, kv_s[:, :, :4], w_main, w_out


def _ffn_and_ple(hp, hs, b, s, n, t, i, norm_ffn, norm_ple, w_ffn_in, conv_w, conv_b, w_ffn_out, state_conv,
                 p_prompt, p_sample, w_ple_gate, w_ple_proj):
    hu_s, w_ffn_in = _project(hs, w_ffn_in, layer=i, gain=norm_ffn, cast_w=True)
    hu_p = _project(hp, w_ffn_in, gain=norm_ffn)
    conv_p = hu_p.reshape(b, s, 2 * D_FF)[:, s - (CONV_W - 1):, :D_FF]
    conv_s = jnp.concatenate([state_conv, hu_s.reshape(n, t, 2 * D_FF)[:, :, :D_FF]], axis=1)[:, t:]
    hs, w_ffn_out = _ffn_out_sample(hu_s, state_conv, conv_w, conv_b, w_ffn_out, i, hs, t)
    hp = _ffn_out_prompt(hu_p, conv_w, conv_b, w_ffn_out, hp, s)
    hs, w_ple_gate, w_ple_proj = _ple_add(hs, norm_ple, w_ple_gate, p_sample.astype(BF16), w_ple_proj, layer=i,
                                          cast_w=True)
    hp = _ple_add(hp, norm_ple, w_ple_gate, p_prompt.astype(BF16), w_ple_proj)
    return hp, hs, conv_p, conv_s


def kernel(x_prompt, x_sample, state_dil_w128, state_dil_w512, state_dil_w2048, state_nsa_win, state_conv,
           cache_nsa_kv, page_table, p_prompt, p_sample, rel_bias, norm_mix, norm_ffn, norm_ple, norm_final,
           w_in_a, w_out_a, w_in_b, w_out_b, cmp_pe, cmp_w1, cmp_b1, cmp_w2, cmp_b2, w_ffn_in, conv_w, conv_b,
           w_ffn_out, w_ple_gate, w_ple_proj):
    b, s, d = x_prompt.shape
    n, t, _ = x_sample.shape
    depth = norm_mix.shape[0]
    hp, hs = x_prompt.reshape(b * s, d), x_sample.reshape(n * t, d)
    dil_p, dil_s = [[] for _ in range(N_DIL)], [[] for _ in range(N_DIL)]
    win_p, win_s, kv_p, kv_s, conv_p, conv_s = [], [], [], [], [], []
    for i in range(depth):
        li = i // 2
        if i % 2 == 0:
            hp, hs, new_p, new_s = _layer_a(
                hp, hs, b, s, n, t, norm_mix[i], w_in_a[li], w_out_a[li], rel_bias,
                (state_dil_w128[li], state_dil_w512[li], state_dil_w2048[li]))
            for g in range(N_DIL):
                dil_p[g].append(new_p[g])
                dil_s[g].append(new_s[g])
        else:
            hp, hs, wp, ws, rp, rs = _layer_b(
                hp, hs, b, s, n, t, norm_mix[i], w_in_b[li], w_out_b[li], rel_bias,
                (cmp_pe[li], cmp_w1[li], cmp_b1[li], cmp_w2[li], cmp_b2[li]), cache_nsa_kv[li], page_table,
                state_nsa_win[li])
            win_p.append(wp)
            win_s.append(ws)
            kv_p.append(rp)
            kv_s.append(rs)
        hp, hs, cp, cs = _ffn_and_ple(
            hp, hs, b, s, n, t, i, norm_ffn[i], norm_ple[i], w_ffn_in, conv_w[i], conv_b[i],
            w_ffn_out, state_conv[i], p_prompt[i].reshape(b * s, -1), p_sample[i].reshape(n * t, -1),
            w_ple_gate, w_ple_proj)
        conv_p.append(cp)
        conv_s.append(cs)
    y_prompt = _rmsnorm(hp, norm_final, F32).reshape(b, s, d)
    y_sample = _rmsnorm(hs, norm_final, F32).reshape(n, t, d)
    return (y_prompt, y_sample,
            jnp.stack(dil_p[0]), jnp.stack(dil_s[0]), jnp.stack(dil_p[1]), jnp.stack(dil_s[1]),
            jnp.stack(dil_p[2]), jnp.stack(dil_s[2]),
            jnp.stack(win_p), jnp.stack(win_s), jnp.stack(conv_p), jnp.stack(conv_s),
            jnp.stack(kv_p), jnp.stack(kv_s))
```

```python
import functools

import jax
import jax.numpy as jnp
from jax import lax
from jax.experimental import pallas as pl
from jax.experimental.pallas import tpu as pltpu

F32 = jnp.float32
BF16 = jnp.bfloat16

D_MODEL = 2048
HEAD_DIM = 128
N_HEADS = 16
DIL_PAIRS = ((128, 1), (512, 4), (2048, 16))
N_DIL = 3
BLK = 128
KV_HEADS = 4
GROUP = 4
CMP_BLOCK = 32
CMP_STRIDE = 16
SEL_BLOCK = 64
SEL_TOPN = 16
NSA_WINDOW = 512
D_FF = 5632
CONV_W = 3
REL_BUCKETS = 32
EPS = 1e-6
NEG = -1e30
FORCED_SCORE = 1e4
SCALE = HEAD_DIM ** -0.5
LOG2E = 1.4426950408889634
QKV_A = N_DIL * 3 * N_HEADS * HEAD_DIM
NSA_Q = N_HEADS * HEAD_DIM
NSA_KV = KV_HEADS * HEAD_DIM

BUCKET_START = (1, 2, 3, 4, 5, 6, 7, 8, 9, 10, 11, 12, 13, 14, 15, 16, 22, 30, 40, 54, 73, 99,
                134, 182, 246, 332, 450, 609, 825, 1117, 1513)

VMEM_LIMIT_V7X = 56 * 1024 * 1024


def _params(sem, vmem=VMEM_LIMIT_V7X):
    return pltpu.CompilerParams(dimension_semantics=sem, vmem_limit_bytes=vmem)


def _pick(n, cands):
    for c in cands:
        if n % c == 0:
            return c
    return n


def _rmsnorm_kernel(x_ref, g_ref, o_ref):
    x = x_ref[...]
    ms = jnp.mean(x * x, axis=-1, keepdims=True)
    o_ref[...] = ((x * lax.rsqrt(ms + EPS)) * g_ref[...]).astype(o_ref.dtype)


def _rmsnorm(x, g, out_dtype):
    m, d = x.shape
    tm = _pick(m, (512, 256, 128, 32))
    return pl.pallas_call(
        _rmsnorm_kernel,
        out_shape=jax.ShapeDtypeStruct((m, d), out_dtype),
        grid=(m // tm,),
        in_specs=[pl.BlockSpec((tm, d), lambda i: (i, 0)), pl.BlockSpec((1, d), lambda i: (0, 0))],
        out_specs=pl.BlockSpec((tm, d), lambda i: (i, 0)),
        compiler_params=_params(("parallel",)),
        name="rmsnorm",
    )(x, g.reshape(1, d))


def _normed(x, g):
    ms = jnp.mean(x * x, axis=-1, keepdims=True)
    return ((x * lax.rsqrt(ms + EPS)) * g).astype(BF16)


def _project_kernel(*refs, norm, cast_w, residual, sigmoid, heads):
    refs = list(refs)
    x_ref = refs.pop(0)
    g_ref = refs.pop(0) if norm else None
    w_ref = refs.pop(0)
    r_ref = refs.pop(0) if residual else None
    o_ref = refs.pop(0)
    wb_ref = refs.pop(0) if cast_w else None
    if norm:
        a_scr = refs.pop(0)

        @pl.when(pl.program_id(1) == 0)
        def _():
            a_scr[...] = _normed(x_ref[...], g_ref[...])

        a = a_scr[...]
    else:
        a = x_ref[...]
    w = w_ref[...]
    if cast_w:
        w = w.astype(BF16)
        wb_ref[...] = w
    acc = jnp.dot(a, w, preferred_element_type=F32)
    if sigmoid:
        acc = jax.nn.sigmoid(acc)
    if residual:
        acc = r_ref[...] + acc
    if heads:
        for j in range(o_ref.shape[0]):
            o_ref[j] = acc[:, j * HEAD_DIM:(j + 1) * HEAD_DIM]
    else:
        o_ref[...] = acc


def _weight_spec(w, layer, k, tn, index):
    if w.ndim == 2:
        return pl.BlockSpec((k, tn), index)
    return pl.BlockSpec((None, k, tn), lambda *g: (layer,) + index(*g))


def _project(x, w, *, layer=None, n_cols=None, gain=None, res=None, cast_w=False, sigmoid=False, heads=False):
    m, k = x.shape
    n = n_cols or w.shape[-1]
    tm = _pick(m, (1024, 512, 256, 128))
    tn = _pick(n, (1024, 512, 256, 128))
    assert not cast_w or m == tm
    norm, residual = gain is not None, res is not None
    args, in_specs = [x], [pl.BlockSpec((tm, k), lambda i, j: (i, 0))]
    if norm:
        args.append(gain.reshape(1, k))
        in_specs.append(pl.BlockSpec((1, k), lambda i, j: (0, 0)))
    args.append(w)
    in_specs.append(_weight_spec(w, layer, k, tn, lambda i, j: (0, j)))
    if residual:
        args.append(res)
        in_specs.append(pl.BlockSpec((tm, tn), lambda i, j: (i, j)))
    if heads:
        out_shape = [jax.ShapeDtypeStruct((n // HEAD_DIM, m, HEAD_DIM), F32)]
        out_specs = [pl.BlockSpec((tn // HEAD_DIM, tm, HEAD_DIM), lambda i, j: (j, i, 0))]
    else:
        out_shape = [jax.ShapeDtypeStruct((m, n), F32)]
        out_specs = [pl.BlockSpec((tm, tn), lambda i, j: (i, j))]
    if cast_w:
        out_shape.append(jax.ShapeDtypeStruct((k, n), BF16))
        out_specs.append(pl.BlockSpec((k, tn), lambda i, j: (0, j)))
    outs = pl.pallas_call(
        functools.partial(_project_kernel, norm=norm, cast_w=cast_w, residual=residual, sigmoid=sigmoid,
                          heads=heads),
        out_shape=out_shape,
        grid=(m // tm, n // tn),
        in_specs=in_specs,
        out_specs=out_specs,
        scratch_shapes=[pltpu.VMEM((tm, k), BF16)] if norm else [],
        compiler_params=_params(("parallel", "arbitrary")),
        name="project",
    )(*args)
    return tuple(outs) if cast_w else outs[0]


def _conv_gelu_val(g, g1, g2, val, cw_ref, cb_ref):
    c = cb_ref[...] + g2 * cw_ref[0:1, :]
    c = c + g1 * cw_ref[1:2, :]
    c = c + g * cw_ref[2:3, :]
    return (jax.nn.gelu(c) * val).astype(BF16)


def _ffn_out_prompt_kernel(g_ref, halo_ref, v_ref, cw_ref, cb_ref, w_ref, r_ref, o_ref, acc_ref, *,
                           tiles_per_seq):
    i, k = pl.program_id(0), pl.program_id(1)

    @pl.when(k == 0)
    def _():
        acc_ref[...] = jnp.zeros_like(acc_ref)

    g = g_ref[...]
    row = lax.broadcasted_iota(jnp.int32, g.shape, 0)
    halo = jnp.where(i % tiles_per_seq == 0, 0.0, halo_ref[...])
    g1 = jnp.where(row == 0, halo[7:8, :], pltpu.roll(g, 1, 0))
    g2 = jnp.where(row == 0, halo[6:7, :], jnp.where(row == 1, halo[7:8, :], pltpu.roll(g, 2, 0)))
    u = _conv_gelu_val(g, g1, g2, v_ref[...], cw_ref, cb_ref)
    acc_ref[...] += jnp.dot(u, w_ref[...], preferred_element_type=F32)

    @pl.when(k == pl.num_programs(1) - 1)
    def _():
        o_ref[...] = r_ref[...] + acc_ref[...]


def _ffn_out_prompt(hu, conv_w, conv_b, w_out, res, seq):
    m = hu.shape[0]
    tm, tk = 512, D_FF // 4
    nk = D_FF // tk
    return pl.pallas_call(
        functools.partial(_ffn_out_prompt_kernel, tiles_per_seq=seq // tm),
        out_shape=jax.ShapeDtypeStruct((m, D_MODEL), F32),
        grid=(m // tm, nk),
        in_specs=[
            pl.BlockSpec((tm, tk), lambda i, k: (i, k)),
            pl.BlockSpec((8, tk), lambda i, k: (jnp.maximum(i * (tm // 8) - 1, 0), k)),
            pl.BlockSpec((tm, tk), lambda i, k: (i, k + nk)),
            pl.BlockSpec((CONV_W, tk), lambda i, k: (0, k)),
            pl.BlockSpec((1, tk), lambda i, k: (0, k)),
            pl.BlockSpec((tk, D_MODEL), lambda i, k: (k, 0)),
            pl.BlockSpec((tm, D_MODEL), lambda i, k: (i, 0)),
        ],
        out_specs=pl.BlockSpec((tm, D_MODEL), lambda i, k: (i, 0)),
        scratch_shapes=[pltpu.VMEM((tm, D_MODEL), F32)],
        compiler_params=_params(("parallel", "arbitrary")),
        name="ffn_out_prompt",
    )(hu, hu, hu, conv_w, conv_b.reshape(1, D_FF), w_out, res)


def _ffn_out_sample_kernel(g_ref, e1_ref, e2_ref, v_ref, cw_ref, cb_ref, w_ref, r_ref, o_ref, wb_ref, acc_ref, *,
                           t_len):
    k = pl.program_id(0)

    @pl.when(k == 0)
    def _():
        acc_ref[...] = jnp.zeros_like(acc_ref)

    g = g_ref[...]
    t = lax.broadcasted_iota(jnp.int32, g.shape, 0) % t_len
    g1 = jnp.where(t == 0, e1_ref[...], pltpu.roll(g, 1, 0))
    g2 = jnp.where(t < 2, e2_ref[...], pltpu.roll(g, 2, 0))
    u = _conv_gelu_val(g, g1, g2, v_ref[...], cw_ref, cb_ref)
    w = w_ref[...].astype(BF16)
    wb_ref[...] = w
    acc_ref[...] += jnp.dot(u, w, preferred_element_type=F32)

    @pl.when(k == pl.num_programs(0) - 1)
    def _():
        o_ref[...] = r_ref[...] + acc_ref[...]


def _ffn_out_sample(hu, conv_prev, conv_w, conv_b, w_out, layer, res, t_len):
    m = hu.shape[0]
    n = m // t_len
    tk = 512
    nk = D_FF // tk
    zeros = jnp.zeros((n, t_len - 1, D_FF), F32)
    e1 = jnp.concatenate([conv_prev[:, 1:2], zeros], axis=1).reshape(m, D_FF)
    e2 = jnp.concatenate([conv_prev, zeros[:, 1:]], axis=1).reshape(m, D_FF)
    return pl.pallas_call(
        functools.partial(_ffn_out_sample_kernel, t_len=t_len),
        out_shape=(jax.ShapeDtypeStruct((m, D_MODEL), F32), jax.ShapeDtypeStruct((D_FF, D_MODEL), BF16)),
        grid=(nk,),
        in_specs=[
            pl.BlockSpec((m, tk), lambda k: (0, k)),
            pl.BlockSpec((m, tk), lambda k: (0, k)),
            pl.BlockSpec((m, tk), lambda k: (0, k)),
            pl.BlockSpec((m, tk), lambda k: (0, k + nk)),
            pl.BlockSpec((CONV_W, tk), lambda k: (0, k)),
            pl.BlockSpec((1, tk), lambda k: (0, k)),
            pl.BlockSpec((None, tk, D_MODEL), lambda k: (layer, k, 0)),
            pl.BlockSpec((m, D_MODEL), lambda k: (0, 0)),
        ],
        out_specs=(pl.BlockSpec((m, D_MODEL), lambda k: (0, 0)), pl.BlockSpec((tk, D_MODEL), lambda k: (k, 0))),
        scratch_shapes=[pltpu.VMEM((m, D_MODEL), F32)],
        compiler_params=_params(("arbitrary",)),
        name="ffn_out_sample",
    )(hu, e1, e2, hu, conv_w, conv_b.reshape(1, D_FF), w_out, res)


def _ple_kernel(h_ref, g_ref, wg_ref, p_ref, wp_ref, o_ref, *rest, cast_w):
    a_scr = rest[-1]
    j = pl.program_id(1)
    tn = o_ref.shape[1]

    @pl.when(j == 0)
    def _():
        a_scr[...] = _normed(h_ref[...], g_ref[...])

    wg, wp = wg_ref[...], wp_ref[...]
    if cast_w:
        wg, wp = wg.astype(BF16), wp.astype(BF16)
        rest[0][...] = wg
        rest[1][...] = wp
    gate = jax.nn.sigmoid(jnp.dot(a_scr[...], wg, preferred_element_type=F32))
    proj = jnp.dot(p_ref[...], wp, preferred_element_type=F32)
    o_ref[...] = h_ref[:, pl.ds(pl.multiple_of(j * tn, tn), tn)] + gate * proj


def _ple_add(h, gain, w_gate, p, w_proj, layer=None, cast_w=False):
    m, d = h.shape
    kp = p.shape[1]
    tm = _pick(m, (1024, 512, 256, 128))
    tn = 1024
    assert not cast_w or m == tm
    out_shape = [jax.ShapeDtypeStruct((m, d), F32)]
    out_specs = [pl.BlockSpec((tm, tn), lambda i, j: (i, j))]
    if cast_w:
        out_shape += [jax.ShapeDtypeStruct((d, d), BF16), jax.ShapeDtypeStruct((kp, d), BF16)]
        out_specs += [pl.BlockSpec((d, tn), lambda i, j: (0, j)), pl.BlockSpec((kp, tn), lambda i, j: (0, j))]
    outs = pl.pallas_call(
        functools.partial(_ple_kernel, cast_w=cast_w),
        out_shape=out_shape,
        grid=(m // tm, d // tn),
        in_specs=[
            pl.BlockSpec((tm, d), lambda i, j: (i, 0)),
            pl.BlockSpec((1, d), lambda i, j: (0, 0)),
            _weight_spec(w_gate, layer, d, tn, lambda i, j: (0, j)),
            pl.BlockSpec((tm, kp), lambda i, j: (i, 0)),
            _weight_spec(w_proj, layer, kp, tn, lambda i, j: (0, j)),
        ],
        out_specs=out_specs,
        scratch_shapes=[pltpu.VMEM((tm, d), BF16)],
        compiler_params=_params(("parallel", "arbitrary")),
        name="ple_add",
    )(h, gain.reshape(1, d), w_gate, p, w_proj)
    return tuple(outs) if cast_w else outs[0]


def _bias_table_kernel(rbt_ref, o_ref, *, a0, ag, ar, ac, lo, hi, mod, scale):
    g = pl.program_id(0)
    n_r, n_c = o_ref.shape[2:]
    shape = (n_r, BLK)
    rows = ar * lax.broadcasted_iota(jnp.int32, shape, 0)
    cols = ac * lax.broadcasted_iota(jnp.int32, shape, 1)
    for c0 in range(0, n_c, BLK):
        dist = (a0 + ac * c0) + ag * g + rows + cols
        d = jnp.maximum(dist, 0)
        bucket = jnp.zeros(shape, jnp.int32)
        for start in BUCKET_START:
            bucket = bucket + (d >= start).astype(jnp.int32)
        ok = (dist >= lo) & (dist <= hi)
        if mod > 1:
            ok = ok & ((d & (mod - 1)) == 0)
        for h in range(N_HEADS):
            row = jnp.broadcast_to(rbt_ref[h:h + 1, :], shape)
            if scale != 1.0:
                row = row * scale
            o_ref[h, 0, :, c0:c0 + BLK] = jnp.where(ok, jnp.take_along_axis(row, bucket, axis=1), NEG)


def _bias_table(rel_bias, n_g, n_r, n_c, a0, ag, ar, ac, lo, hi, mod=1, scale=1.0):
    assert mod & (mod - 1) == 0 and n_c % BLK == 0
    rbt = jnp.pad(rel_bias.T, ((0, 0), (0, BLK - REL_BUCKETS)))
    return pl.pallas_call(
        functools.partial(_bias_table_kernel, a0=a0, ag=ag, ar=ar, ac=ac, lo=lo, hi=hi, mod=mod, scale=scale),
        out_shape=jax.ShapeDtypeStruct((N_HEADS, n_g, n_r, n_c), F32),
        grid=(n_g,),
        in_specs=[pl.BlockSpec((N_HEADS, BLK), lambda g: (0, 0))],
        out_specs=pl.BlockSpec((N_HEADS, 1, n_r, n_c), lambda g: (0, g, 0, 0)),
        compiler_params=_params(("parallel",)),
        name="bias_table",
    )(rbt)


def _dil_prompt_kernel(q_ref, kc_ref, vc_ref, bias_ref, o_ref, lse_ref, *prev_scr, dil, hps):
    first = pl.program_id(1) == 0
    hb = pl.program_id(2)
    lane = lax.broadcasted_iota(jnp.int32, (BLK, BLK), 1)

    @pl.when(hb == 0)
    def _():
        lse_ref[...] = jnp.zeros_like(lse_ref)

    if prev_scr:
        @pl.when(first & (hb == 0))
        def _():
            for scr in prev_scr:
                scr[...] = jnp.zeros_like(scr)

    for r in range(dil):
        rows = pl.ds(r, BLK, stride=dil)
        lse_rows = lse_ref[rows, :]
        for hh in range(hps):
            h = hb * hps + hh
            q = q_ref[hh, rows, :].astype(BF16)
            k, v, bias = kc_ref[hh, rows, :], vc_ref[hh, rows, :], bias_ref[h]
            if prev_scr:
                kp, vp = (scr[h, rows, :] for scr in prev_scr)
                k, v = jnp.concatenate([kp, k], axis=0), jnp.concatenate([vp, v], axis=0)
                col = lax.broadcasted_iota(jnp.int32, (BLK, 2 * BLK), 1)
                bias = bias + jnp.where(first & (col < BLK), NEG, 0.0)
            else:
                bias = bias[:, BLK:]
            s = lax.dot_general(q, k.astype(BF16), (((1,), (1,)), ((), ())), preferred_element_type=F32)
            s = s * SCALE + bias
            m = jnp.max(s, axis=-1, keepdims=True)
            p = jnp.exp(s - m)
            l = jnp.sum(p, axis=-1, keepdims=True)
            o_ref[hh, rows, :] = jnp.dot(p.astype(BF16), v.astype(BF16), preferred_element_type=F32) / l
            lse_rows = jnp.where(lane == h, m + jnp.log(l), lse_rows)
        lse_ref[rows, :] = lse_rows

    if prev_scr:
        heads = pl.ds(hb * hps, hps)
        prev_scr[0][heads] = kc_ref[...]
        prev_scr[1][heads] = vc_ref[...]


def _dil_prompt_group(qkv_hm, bias, grp, dil, b, s):
    span = BLK * dil
    nsp = s // span
    hps = N_HEADS // dil
    nhb = N_HEADS // hps

    def slab(part):
        base = (grp * 3 + part) * N_HEADS // hps
        return lambda bi, sp, hb: (base + hb, bi * nsp + sp, 0)

    blk = (hps, span, HEAD_DIM)
    carry = [pltpu.VMEM((N_HEADS, span, HEAD_DIM), F32)] * 2 if nsp > 1 else []
    return pl.pallas_call(
        functools.partial(_dil_prompt_kernel, dil=dil, hps=hps),
        out_shape=(jax.ShapeDtypeStruct((N_HEADS, b * s, HEAD_DIM), F32),
                   jax.ShapeDtypeStruct((b * s, BLK), F32)),
        grid=(b, nsp, nhb),
        in_specs=[pl.BlockSpec(blk, slab(0)), pl.BlockSpec(blk, slab(1)), pl.BlockSpec(blk, slab(2)),
                  pl.BlockSpec((N_HEADS, BLK, 2 * BLK), lambda bi, sp, hb: (0, 0, 0))],
        out_specs=(pl.BlockSpec(blk, lambda bi, sp, hb: (hb, bi * nsp + sp, 0)),
                   pl.BlockSpec((span, BLK), lambda bi, sp, hb: (bi * nsp + sp, 0))),
        scratch_shapes=carry,
        compiler_params=_params(("parallel", "arbitrary", "arbitrary")),
        name=f"dil_attn_prompt_g{grp}",
    )(qkv_hm, qkv_hm, qkv_hm, bias)


def _dil_combine_kernel(o0_ref, o1_ref, o2_ref, l0_ref, l1_ref, l2_ref, o_ref, *, head_major):
    l0, l1, l2 = l0_ref[...], l1_ref[...], l2_ref[...]
    mx = jnp.maximum(jnp.maximum(l0, l1), l2)
    e0, e1, e2 = jnp.exp(l0 - mx), jnp.exp(l1 - mx), jnp.exp(l2 - mx)
    den = e0 + e1 + e2
    w0, w1, w2 = e0 / den, e1 / den, e2 / den
    for h in range(N_HEADS):
        sl = slice(h * HEAD_DIM, (h + 1) * HEAD_DIM)
        g0, g1, g2 = ((r[h] for r in (o0_ref, o1_ref, o2_ref)) if head_major
                      else (r[:, sl] for r in (o0_ref, o1_ref, o2_ref)))
        o = (w0[:, h:h + 1] * g0 + w1[:, h:h + 1] * g1) + w2[:, h:h + 1] * g2
        o_ref[:, sl] = o.astype(o_ref.dtype)


def _dil_combine(outs, lses, head_major):
    m = lses[0].shape[0]
    wide = N_HEADS * HEAD_DIM
    tm = _pick(m, (256, 128, 32))
    ob = pl.BlockSpec((tm, wide), lambda i: (i, 0))
    ib = pl.BlockSpec((N_HEADS, tm, HEAD_DIM), lambda i: (0, i, 0)) if head_major else ob
    lb = pl.BlockSpec((tm, BLK), lambda i: (i, 0))
    return pl.pallas_call(
        functools.partial(_dil_combine_kernel, head_major=head_major),
        out_shape=jax.ShapeDtypeStruct((m, wide), BF16),
        grid=(m // tm,),
        in_specs=[ib, ib, ib, lb, lb, lb],
        out_specs=ob,
        compiler_params=_params(("parallel",)),
        name="dil_combine",
    )(*outs, *lses)


HEAD_TILE = 8


def _heads_first(x):
    return pltpu.einshape("mhd->hmd", x)


def _dil_sample_kernel(q_ref, k_ref, v_ref, kn_ref, vn_ref, bias_ref, biasn_ref, o_ref, lse_ref,
                       m_scr, l_scr, acc_scr):
    ht, c = pl.program_id(1), pl.program_id(2)
    k_all = _heads_first(k_ref[...])
    v_all = _heads_first(v_ref[...])

    @pl.when(c == 0)
    def _():
        m_scr[...] = jnp.full_like(m_scr, NEG)
        l_scr[...] = jnp.zeros_like(l_scr)
        acc_scr[...] = jnp.zeros_like(acc_scr)

    @pl.when((c == 0) & (ht == 0))
    def _():
        lse_ref[...] = jnp.zeros_like(lse_ref)

    def update(ks, vs, bias):
        s = jnp.concatenate(
            [lax.dot_general(q_ref[0, :, hh * HEAD_DIM:(hh + 1) * HEAD_DIM].astype(BF16), ks[hh].astype(BF16),
                             (((1,), (1,)), ((), ())), preferred_element_type=F32) for hh in range(HEAD_TILE)], axis=0)
        s = s * SCALE + bias
        m_old = m_scr[...]
        m_new = jnp.maximum(m_old, jnp.max(s, axis=-1, keepdims=True))
        alpha = jnp.exp(m_old - m_new)
        p = jnp.where(bias > 0.5 * NEG, jnp.exp(s - m_new), 0.0)
        l_scr[...] = alpha * l_scr[...] + jnp.sum(p, axis=-1, keepdims=True)
        pv = jnp.concatenate(
            [jnp.dot(p[hh * 8:(hh + 1) * 8].astype(BF16), vs[hh].astype(BF16), preferred_element_type=F32)
             for hh in range(HEAD_TILE)], axis=0)
        acc_scr[...] = alpha * acc_scr[...] + pv
        m_scr[...] = m_new

    n_keys = k_ref.shape[0]
    update(k_all, v_all, bias_ref[:, 0].reshape(HEAD_TILE * 8, n_keys))

    @pl.when(c == pl.num_programs(2) - 1)
    def _():
        cols = [slice(hh * HEAD_DIM, (hh + 1) * HEAD_DIM) for hh in range(HEAD_TILE)]
        update([kn_ref[0, :, sl] for sl in cols], [vn_ref[0, :, sl] for sl in cols],
               biasn_ref[...].reshape(HEAD_TILE * 8, BLK))
        l = jnp.maximum(l_scr[...], 1e-30)
        o = acc_scr[...] / l
        lse_rows = m_scr[...] + jnp.log(l)
        lane = lax.broadcasted_iota(jnp.int32, (8, BLK), 1)
        lse = lse_ref[0]
        for hh in range(HEAD_TILE):
            o_ref[0, :, cols[hh]] = o[hh * 8:(hh + 1) * 8]
            lse = jnp.where(lane == ht * HEAD_TILE + hh, lse_rows[hh * 8:(hh + 1) * 8], lse)
        lse_ref[0] = lse


def _dil_sample_group(q, kn, vn, buf, bias, bias_new, dil):
    n, lb = buf.shape[:2]
    n_cls = bias.shape[1]
    wide = N_HEADS * HEAD_DIM
    half = wide // 2
    rows = lb // dil
    tiles = 2 * N_HEADS // HEAD_TILE
    view = buf.reshape(n, rows, dil * tiles, HEAD_TILE, HEAD_DIM)
    nht = N_HEADS // HEAD_TILE
    return pl.pallas_call(
        _dil_sample_kernel,
        out_shape=(jax.ShapeDtypeStruct((n, 8, wide), F32), jax.ShapeDtypeStruct((n, 8, BLK), F32)),
        grid=(n, nht, n_cls),
        in_specs=[
            pl.BlockSpec((1, 8, half), lambda i, ht, c: (i, 0, ht)),
            pl.BlockSpec((None, rows, None, HEAD_TILE, HEAD_DIM), lambda i, ht, c: (i, 0, c * tiles + ht, 0, 0)),
            pl.BlockSpec((None, rows, None, HEAD_TILE, HEAD_DIM),
                         lambda i, ht, c: (i, 0, c * tiles + nht + ht, 0, 0)),
            pl.BlockSpec((1, BLK, half), lambda i, ht, c: (i, 0, ht)),
            pl.BlockSpec((1, BLK, half), lambda i, ht, c: (i, 0, ht)),
            pl.BlockSpec((HEAD_TILE, 1, 8, rows), lambda i, ht, c: (ht, c, 0, 0)),
            pl.BlockSpec((HEAD_TILE, 8, BLK), lambda i, ht, c: (ht, 0, 0)),
        ],
        out_specs=(pl.BlockSpec((1, 8, half), lambda i, ht, c: (i, 0, ht)),
                   pl.BlockSpec((1, 8, BLK), lambda i, ht, c: (i, 0, 0))),
        scratch_shapes=[pltpu.VMEM((HEAD_TILE * 8, 1), F32), pltpu.VMEM((HEAD_TILE * 8, 1), F32),
                        pltpu.VMEM((HEAD_TILE * 8, HEAD_DIM), F32)],
        compiler_params=_params(("parallel", "arbitrary", "arbitrary")),
        name="dil_attn_sample",
    )(q, view, view, kn, vn, bias, bias_new)


def _vector_rows_kernel(x_ref, o_ref):
    x = x_ref[...]
    if x.ndim == 2:
        x = jnp.stack([x[:, j * HEAD_DIM:(j + 1) * HEAD_DIM] for j in range(HEAD_TILE)])
    o_ref[...] = pltpu.einshape("hmd->mhd", x)


def _vector_rows(x, tile0, n_tiles, b, s, keep):
    tr = min(keep, 512)
    r0 = (s - keep) // tr
    per_seq = s // tr
    if x.ndim == 3:
        in_spec = pl.BlockSpec((HEAD_TILE, tr, HEAD_DIM), lambda bi, r, c: (tile0 + c, bi * per_seq + r0 + r, 0))
    else:
        in_spec = pl.BlockSpec((tr, HEAD_TILE * HEAD_DIM), lambda bi, r, c: (bi * per_seq + r0 + r, tile0 + c))
    return pl.pallas_call(
        _vector_rows_kernel,
        out_shape=jax.ShapeDtypeStruct((b, keep, n_tiles * HEAD_TILE, HEAD_DIM), F32),
        grid=(b, keep // tr, n_tiles),
        in_specs=[in_spec],
        out_specs=pl.BlockSpec((None, tr, HEAD_TILE, HEAD_DIM), lambda bi, r, c: (bi, r, c, 0)),
        compiler_params=_params(("parallel", "parallel", "parallel")),
        name="vector_rows",
    )(x)


def _pad_rows(x, rows):
    return jnp.pad(x, ((0, 0), (0, rows - x.shape[1]), (0, 0)))


def _layer_a(hp, hs, b, s, n, t, norm_g, w_in, w_out, rel_bias, bufs):
    wide = N_HEADS * HEAD_DIM
    qkv_s, w_in = _project(hs, w_in, gain=norm_g, cast_w=True)
    qkv_s = qkv_s.reshape(n, t, QKV_A)
    qkv_p = _project(hp, w_in, gain=norm_g, heads=True)
    outs_p, lses_p, outs_s, lses_s, new_p, new_s = [], [], [], [], [], []
    for grp, (win, dil) in enumerate(DIL_PAIRS):
        base = grp * 3 * wide
        bias = _bias_table(rel_bias, 1, BLK, 2 * BLK, BLK * dil, 0, dil, -dil, 0, win).reshape(N_HEADS, BLK, 2 * BLK)
        o, lse = _dil_prompt_group(qkv_p, bias, grp, dil, b, s)
        outs_p.append(o)
        lses_p.append(lse)
        keep = min(win, s)
        kv = _vector_rows(qkv_p, (grp * 3 + 1) * N_HEADS // HEAD_TILE, 2 * N_HEADS // HEAD_TILE, b, s, keep)
        new_p.append(kv.reshape(b, keep, 2, N_HEADS, HEAD_DIM))
        buf = bufs[grp]
        lb = buf.shape[1]
        n_cls = min(dil, t)
        bias_buf = _bias_table(rel_bias, n_cls, 8, lb // dil, lb, -1, 1, -dil, 0, win, dil)
        bias_new = _bias_table(rel_bias, 1, 8, BLK, 0, 0, 1, -1, 0, win, dil).reshape(N_HEADS, 8, BLK)
        q = _pad_rows(qkv_s[:, :, base:base + wide], 8)
        kn = _pad_rows(qkv_s[:, :, base + wide:base + 2 * wide], BLK)
        vn = _pad_rows(qkv_s[:, :, base + 2 * wide:base + 3 * wide], BLK)
        o, lse = _dil_sample_group(q, kn, vn, buf, bias_buf, bias_new, dil)
        outs_s.append(o[:, :t].reshape(n * t, wide))
        lses_s.append(lse[:, :t].reshape(n * t, BLK))
        kv_new = qkv_s[:, :, base + wide:base + 3 * wide].reshape(n, t, 2, N_HEADS, HEAD_DIM)
        new_s.append(jnp.concatenate([buf, kv_new], axis=1)[:, t:])
    hs, w_out = _project(_dil_combine(outs_s, lses_s, False), w_out, res=hs, cast_w=True)
    hp = _project(_dil_combine(outs_p, lses_p, True), w_out, res=hp)
    return hp, hs, new_p, new_s


PAGES_PER_STEP = 16
CHUNKS_PER_PAGE = BLK // CMP_STRIDE
NT_DIMS = (((1,), (1,)), ((), ()))
TN_DIMS = (((0,), (0,)), ((), ()))


def _page_spec(p, half):
    return pl.BlockSpec((BLK, HEAD_TILE, HEAD_DIM),
                        lambda i, j, *rest: (rest[-1][i, j * PAGES_PER_STEP + p], half, 0))


def _cmp_proj_kernel(pt_ref, *refs):
    pages = refs[:PAGES_PER_STEP]
    w_ref, o_ref = refs[PAGES_PER_STEP:PAGES_PER_STEP + 2]
    ys = [pltpu.einshape("ctgd->tgcd", pg[...].reshape(CHUNKS_PER_PAGE, CMP_STRIDE, HEAD_TILE, HEAD_DIM))
          for pg in pages]
    for c in range(2):
        lhs = jnp.concatenate(
            [jnp.concatenate([y[t, c * KV_HEADS + kh] for kh in range(KV_HEADS) for y in ys], axis=0)
             for t in range(CMP_STRIDE)], axis=1).astype(BF16)
        acc = jnp.dot(lhs, w_ref[c], preferred_element_type=F32)
        for kh in range(KV_HEADS):
            o_ref[0, c, kh] = acc[kh * BLK:(kh + 1) * BLK]


def _cmp_proj(pages, page_table, w1r):
    n, n_pages = page_table.shape
    chunks = n_pages * CHUNKS_PER_PAGE
    grid_spec = pltpu.PrefetchScalarGridSpec(
        num_scalar_prefetch=1,
        grid=(n, n_pages // PAGES_PER_STEP),
        in_specs=[_page_spec(p, 0) for p in range(PAGES_PER_STEP)]
        + [pl.BlockSpec((2, CMP_STRIDE * HEAD_DIM, 2 * HEAD_DIM), lambda i, j, pt: (0, 0, 0))],
        out_specs=pl.BlockSpec((1, 2, KV_HEADS, BLK, 2 * HEAD_DIM), lambda i, j, pt: (i, 0, 0, j, 0)),
    )
    return pl.pallas_call(
        _cmp_proj_kernel,
        out_shape=jax.ShapeDtypeStruct((n, 2, KV_HEADS, chunks, 2 * HEAD_DIM), F32),
        grid_spec=grid_spec,
        compiler_params=_params(("parallel", "arbitrary")),
        name="nsa_cmp_proj",
    )(page_table, *([pages] * PAGES_PER_STEP), w1r.reshape(2, CMP_STRIDE * HEAD_DIM, 2 * HEAD_DIM))


def _finish_compress(a, pe_row, w1f, b1, w2, b2, n_blocks):
    rows = a.shape[0]
    cst = jnp.dot(pe_row, w1f, preferred_element_type=F32)[0:1]
    h = (b1 + cst) + a[:, :HEAD_DIM] + pltpu.roll(a[:, HEAD_DIM:], rows - 1, 0)
    x = jnp.dot(jax.nn.gelu(h).astype(BF16), w2, preferred_element_type=F32) + b2
    return jnp.where(lax.broadcasted_iota(jnp.int32, x.shape, 0) < n_blocks, x, 0.0)


def _split3(x):
    hi = x.astype(BF16)
    r = x - hi.astype(F32)
    mid = r.astype(BF16)
    return hi, mid, (r - mid.astype(F32)).astype(BF16)


def _top_n(score, n, axis):
    idx = lax.broadcasted_iota(jnp.int32, score.shape, axis).astype(F32)
    big = float(score.shape[axis])

    def body(_, carry):
        sc, sel = carry
        mx = jnp.max(sc, axis=axis, keepdims=True)
        first = jnp.min(jnp.where(sc == mx, idx, big), axis=axis, keepdims=True)
        hit = idx == first
        return jnp.where(hit, -jnp.inf, sc), jnp.where(hit, 1.0, sel)

    return lax.fori_loop(0, n, body, (score, jnp.zeros(score.shape, F32)))[1]


def _sel_scores(p_slc, blk, cur, n_blocks):
    forced = (blk == 0) | (blk == cur) | (blk == cur - 1)
    score = jnp.where(forced, FORCED_SCORE, jnp.where(blk <= cur, p_slc, -1.0))
    return jnp.where(blk < n_blocks, score, -2.0)


def _nsa_prompt_kernel(q_ref, a_ref, pe_ref, w1f_ref, b1_ref, w2_ref, b2_ref, ksel_ref, vsel_ref, kwin_ref,
                       vwin_ref, tbl_ref, tblw_ref, biasc_ref, gates_ref, msel_ref, o_ref, kc_scr, vc_scr, sel_scr,
                       vselt_ref, vwint_ref, *, n_cmp, n_sel):
    i = pl.program_id(2)

    @pl.when(i == 0)
    def _():
        for c, scr in ((0, kc_scr), (1, vc_scr)):
            scr[...] = _finish_compress(a_ref[0, c, 0], pe_ref[c], w1f_ref[c], b1_ref[c], w2_ref[c], b2_ref[c],
                                        n_cmp).astype(BF16)
        for src, dst in ((vsel_ref, vselt_ref), (vwin_ref, vwint_ref)):
            for c in range(src.shape[0] // BLK):
                dst[:, c * BLK:(c + 1) * BLK] = src[c * BLK:(c + 1) * BLK, :].T.astype(BF16)

    q = q_ref[...]
    qs = jnp.concatenate([q[:, g * HEAD_DIM:(g + 1) * HEAD_DIM] for g in range(GROUP)], axis=0).astype(BF16)
    key_i = lax.broadcasted_iota(jnp.int32, (BLK, BLK), 0)
    tok_i = lax.broadcasted_iota(jnp.int32, (BLK, BLK), 1)

    def lanes4(x):
        return jnp.concatenate([x] * GROUP, axis=1)

    qs2 = jnp.concatenate([q[:, g * HEAD_DIM:(g + 1) * HEAD_DIM] for g in range(GROUP)],
                          axis=0) * (SCALE * LOG2E)
    qs2 = qs2.astype(BF16)

    mask_c = lanes4(i * BLK + tok_i - (key_i * CMP_STRIDE + (CMP_BLOCK - 1)) >= 0)
    s = lax.dot_general(kc_scr[...], qs, NT_DIMS, preferred_element_type=F32) * SCALE
    s = jnp.where(mask_c, s + jnp.concatenate([biasc_ref[g, 0] for g in range(GROUP)], axis=1), NEG)
    m = jnp.max(s, axis=0, keepdims=True)
    p = jnp.where(mask_c, jnp.exp(s - m), 0.0)
    pn = p / jnp.maximum(jnp.sum(p, axis=0, keepdims=True), 1e-30)
    o_cmp = lax.dot_general(vc_scr[...], pn.astype(BF16), TN_DIMS, preferred_element_type=F32)
    pc = ((pn[:, 0:BLK] + pn[:, BLK:2 * BLK]) + pn[:, 2 * BLK:3 * BLK]) + pn[:, 3 * BLK:4 * BLK]

    msel = msel_ref[...]
    p_slc = sum(jnp.dot(msel, part, preferred_element_type=F32) for part in _split3(pc))
    blk = lax.broadcasted_iota(jnp.int32, p_slc.shape, 0)
    cur = (i * BLK + lax.broadcasted_iota(jnp.int32, p_slc.shape, 1)) // SEL_BLOCK
    sel_scr[...] = (1.0 - _top_n(_sel_scores(p_slc, blk, cur, n_sel), SEL_TOPN, 0)) * NEG

    def attend(k_ref, vt_ref, bias_ref, first_blk, n_blk, extra, carry):
        m_run, l_run, acc = carry
        off = pl.multiple_of(first_blk * BLK, BLK)
        k = k_ref[pl.ds(off, n_blk * BLK), :].astype(BF16)
        bias = []
        for j in range(n_blk):
            idx = jnp.maximum(i - first_blk - j, -1) + 1
            bias.append(jnp.concatenate([bias_ref[g, idx] for g in range(GROUP)], axis=1))
        s = lax.dot_general(k, qs2, NT_DIMS, preferred_element_type=F32) + jnp.concatenate(bias, axis=0)
        if extra is not None:
            s = s + extra
        m_new = jnp.maximum(m_run, jnp.max(s, axis=0, keepdims=True))
        alpha = jnp.exp2(m_run - m_new)
        p = jnp.exp2(s - m_new)
        l_new = alpha * l_run + jnp.sum(p, axis=0, keepdims=True)
        vt = vt_ref[:, pl.ds(off, n_blk * BLK)]
        return m_new, l_new, alpha * acc + jnp.dot(vt, p.astype(BF16), preferred_element_type=F32)

    init = (jnp.full((1, GROUP * BLK), NEG, F32), jnp.zeros((1, GROUP * BLK), F32),
            jnp.zeros((HEAD_DIM, GROUP * BLK), F32))

    sel_span = 4
    per_blk = BLK // SEL_BLOCK

    def sel_step(c, carry):
        rows = sel_scr[pl.ds(pl.multiple_of(c * sel_span * per_blk, 8), sel_span * per_blk), :]
        unpicked = jnp.concatenate([jnp.broadcast_to(rows[u:u + 1], (SEL_BLOCK, BLK))
                                    for u in range(sel_span * per_blk)], axis=0)
        return attend(ksel_ref, vselt_ref, tbl_ref, c * sel_span, sel_span, lanes4(unpicked), carry)

    _, l_sel, acc_sel = lax.fori_loop(0, i // sel_span + 1, sel_step, init)
    o_sel = acc_sel / jnp.maximum(l_sel, 1e-30)

    n_win = (NSA_WINDOW - 1 + BLK - 1) // BLK + 1
    _, l_win, acc_win = attend(kwin_ref, vwint_ref, tblw_ref, jnp.maximum(i - (n_win - 1), 0), n_win, None, init)
    o_win = acc_win / jnp.maximum(l_win, 1e-30)

    gt = gates_ref[0]

    def gate(branch):
        return jnp.concatenate([gt[branch * GROUP + g:branch * GROUP + g + 1, :] for g in range(GROUP)], axis=1)

    o = (gate(0) * o_cmp + gate(1) * o_sel) + gate(2) * o_win
    for g in range(GROUP):
        o_ref[0, g] = o[:, g * BLK:(g + 1) * BLK]


def _nsa_prompt(proj, a_cmp, gates_t, cmp_w, tbl, tbl_win, bias_c, msel, b, s):
    pe, w1f, b1, w2, b2 = cmp_w
    nq = s // BLK
    kcol = NSA_Q // HEAD_DIM
    const = lambda shape: pl.BlockSpec(shape, lambda bi, kh, i: (0,) * len(shape))
    return pl.pallas_call(
        functools.partial(_nsa_prompt_kernel, n_cmp=s // CMP_STRIDE - 1, n_sel=s // SEL_BLOCK),
        out_shape=jax.ShapeDtypeStruct((b, N_HEADS, HEAD_DIM, s), F32),
        grid=(b, KV_HEADS, nq),
        in_specs=[
            pl.BlockSpec((BLK, GROUP * HEAD_DIM), lambda bi, kh, i: (bi * nq + i, kh)),
            pl.BlockSpec((1, 2, 1, s // CMP_STRIDE, 2 * HEAD_DIM), lambda bi, kh, i: (bi, 0, kh, 0, 0)),
            const(pe.shape), const(w1f.shape), const(b1.shape), const(w2.shape), const(b2.shape),
            pl.BlockSpec((s, HEAD_DIM), lambda bi, kh, i: (bi, kcol + 2 * KV_HEADS + kh)),
            pl.BlockSpec((s, HEAD_DIM), lambda bi, kh, i: (bi, kcol + 3 * KV_HEADS + kh)),
            pl.BlockSpec((s, HEAD_DIM), lambda bi, kh, i: (bi, kcol + 4 * KV_HEADS + kh)),
            pl.BlockSpec((s, HEAD_DIM), lambda bi, kh, i: (bi, kcol + 5 * KV_HEADS + kh)),
            pl.BlockSpec((GROUP,) + tbl.shape[1:], lambda bi, kh, i: (kh, 0, 0, 0)),
            pl.BlockSpec((GROUP,) + tbl_win.shape[1:], lambda bi, kh, i: (kh, 0, 0, 0)),
            pl.BlockSpec((GROUP, 1, s // CMP_STRIDE, BLK), lambda bi, kh, i: (kh, 0, 0, i)),
            pl.BlockSpec((1, 16, BLK), lambda bi, kh, i: (kh, 0, bi * nq + i)),
            const(msel.shape),
        ],
        out_specs=pl.BlockSpec((1, GROUP, HEAD_DIM, BLK), lambda bi, kh, i: (bi, kh, 0, i)),
        scratch_shapes=[pltpu.VMEM((s // CMP_STRIDE, HEAD_DIM), BF16), pltpu.VMEM((s // CMP_STRIDE, HEAD_DIM), BF16),
                        pltpu.VMEM((s // SEL_BLOCK, BLK), F32),
                        pltpu.VMEM((HEAD_DIM, s), BF16), pltpu.VMEM((HEAD_DIM, s), BF16)],
        compiler_params=_params(("parallel", "parallel", "arbitrary")),
        name="nsa_prompt",
    )(proj, a_cmp, pe, w1f, b1, w2, b2, proj, proj, proj, proj, tbl, tbl_win, bias_c, gates_t, msel)


def _masked_softmax_rows(s, mask):
    s = jnp.where(mask, s, NEG)
    m = jnp.max(s, axis=-1, keepdims=True)
    p = jnp.where(mask, jnp.exp(s - m), 0.0)
    return p / jnp.maximum(jnp.sum(p, axis=-1, keepdims=True), 1e-30)


def _nsa_sample_select_kernel(q_ref, a_ref, pe_ref, w1f_ref, b1_ref, w2_ref, b2_ref, biasc_ref, msel_ref,
                              regroup_ref, ocmp_ref, sel_ref, *, n_cmp, n_sel, past):
    pcs = []
    for kh in range(KV_HEADS):
        kc, vc = (_finish_compress(a_ref[0, c, kh], pe_ref[c], w1f_ref[c], b1_ref[c], w2_ref[c], b2_ref[c],
                                   n_cmp).astype(BF16) for c in range(2))
        bias = biasc_ref[kh]
        s = lax.dot_general(q_ref[0, kh].astype(BF16), kc, NT_DIMS, preferred_element_type=F32) * SCALE + bias
        pn = _masked_softmax_rows(s, bias > 0.5 * NEG)
        ocmp_ref[0, kh] = jnp.dot(pn.astype(BF16), vc, preferred_element_type=F32)
        pcs.append(((pn[0:8] + pn[8:16]) + pn[16:24]) + pn[24:32])
    pc = jnp.concatenate(pcs, axis=0)
    msel = msel_ref[...]
    p_slc = sum(jnp.dot(part, msel, preferred_element_type=F32) for part in _split3(pc))
    blk = lax.broadcasted_iota(jnp.int32, p_slc.shape, 1)
    cur = (past + lax.broadcasted_iota(jnp.int32, p_slc.shape, 0) % 8) // SEL_BLOCK
    sel = _top_n(_sel_scores(p_slc, blk, cur, n_sel), SEL_TOPN, 1).astype(BF16)
    for j in range(regroup_ref.shape[0]):
        part = jnp.dot(sel, regroup_ref[j], preferred_element_type=F32)
        for kh in range(KV_HEADS):
            sel_ref[0, kh, j] = part[kh * 8:(kh + 1) * 8]


def _nsa_sample_select(q, a_cmp, cmp_w, bias_c, msel, regroup, past, t_len):
    pe, w1f, b1, w2, b2 = cmp_w
    n = q.shape[0]
    chunks = a_cmp.shape[3]
    n_steps = regroup.shape[0]
    const = lambda shape: pl.BlockSpec(shape, lambda i: (0,) * len(shape))
    return pl.pallas_call(
        functools.partial(_nsa_sample_select_kernel, n_cmp=chunks - 1, n_sel=(past + t_len + SEL_BLOCK - 1) // SEL_BLOCK,
                          past=past),
        out_shape=(jax.ShapeDtypeStruct((n, KV_HEADS, GROUP * 8, HEAD_DIM), F32),
                   jax.ShapeDtypeStruct((n, KV_HEADS, n_steps, 8, BLK), F32)),
        grid=(n,),
        in_specs=[
            pl.BlockSpec((1, KV_HEADS, GROUP * 8, HEAD_DIM), lambda i: (i, 0, 0, 0)),
            pl.BlockSpec((1, 2, KV_HEADS, chunks, 2 * HEAD_DIM), lambda i: (i, 0, 0, 0, 0)),
            const(pe.shape), const(w1f.shape), const(b1.shape), const(w2.shape), const(b2.shape),
            const(bias_c.shape), const(msel.shape), const(regroup.shape),
        ],
        out_specs=(pl.BlockSpec((1, KV_HEADS, GROUP * 8, HEAD_DIM), lambda i: (i, 0, 0, 0)),
                   pl.BlockSpec((1, KV_HEADS, n_steps, 8, BLK), lambda i: (i, 0, 0, 0, 0))),
        compiler_params=_params(("parallel",)),
        name="nsa_sample_select",
    )(q, a_cmp, pe, w1f, b1, w2, b2, bias_c, msel, regroup)


def _nsa_sample_attend_kernel(pt_ref, *refs):
    pages = refs[:PAGES_PER_STEP]
    (q_ref, sel_ref, expand_ref, bias_ref, kn_ref, vn_ref, biasn_ref, win_ref, kwn_ref, vwn_ref, biasw_ref,
     biaswn_ref, gates_ref, ocmp_ref, o_ref, m_scr, l_scr, acc_scr) = refs[PAGES_PER_STEP:]
    j = pl.program_id(1)
    n_steps = pl.num_programs(1)

    @pl.when(j == 0)
    def _():
        m_scr[...] = jnp.full_like(m_scr, NEG)
        l_scr[...] = jnp.zeros_like(l_scr)
        acc_scr[...] = jnp.zeros_like(acc_scr)

    def picked(kh, step):
        sel = sel_ref[0, kh, step].astype(BF16)
        return jnp.concatenate([sel] * GROUP, axis=0)

    def update(kh, k, v, bias, mask):
        qs = q_ref[0, kh].astype(BF16)
        s = lax.dot_general(qs, k, NT_DIMS, preferred_element_type=F32) * SCALE + bias
        s = jnp.where(mask, s, NEG)
        m_old = m_scr[kh]
        m_new = jnp.maximum(m_old, jnp.max(s, axis=-1, keepdims=True))
        alpha = jnp.exp(m_old - m_new)
        p = jnp.where(mask, jnp.exp(s - m_new), 0.0)
        l_scr[kh] = alpha * l_scr[kh] + jnp.sum(p, axis=-1, keepdims=True)
        acc_scr[kh] = alpha * acc_scr[kh] + jnp.dot(p.astype(BF16), v, preferred_element_type=F32)
        m_scr[kh] = m_new

    tiles = [_heads_first(pg[...]) for pg in pages]
    expand = expand_ref[...]
    for kh in range(KV_HEADS):
        k = jnp.concatenate([tl[kh] for tl in tiles], axis=0).astype(BF16)
        v = jnp.concatenate([tl[KV_HEADS + kh] for tl in tiles], axis=0).astype(BF16)
        in_sel = jnp.dot(picked(kh, j), expand, preferred_element_type=F32) > 0.5
        update(kh, k, v, bias_ref[kh], in_sel)

    @pl.when(j == n_steps - 1)
    def _():
        win = _heads_first(win_ref[...])
        for kh in range(KV_HEADS):
            biasn = biasn_ref[kh]
            new_sel = picked(kh, n_steps)[:, 0:1] > 0.5
            update(kh, kn_ref[0, kh].astype(BF16), vn_ref[0, kh].astype(BF16), biasn, (biasn > 0.5 * NEG) & new_sel)
            o_sel = acc_scr[kh] / jnp.maximum(l_scr[kh], 1e-30)
            biasw = jnp.concatenate([biasw_ref[kh], biaswn_ref[kh]], axis=1)
            kw = jnp.concatenate([win[kh], kwn_ref[0, kh]], axis=0).astype(BF16)
            vw = jnp.concatenate([win[KV_HEADS + kh], vwn_ref[0, kh]], axis=0).astype(BF16)
            sw = lax.dot_general(q_ref[0, kh].astype(BF16), kw, NT_DIMS, preferred_element_type=F32) * SCALE + biasw
            pw = _masked_softmax_rows(sw, biasw > 0.5 * NEG)
            o_win = jnp.dot(pw.astype(BF16), vw, preferred_element_type=F32)
            o_ref[0, kh] = ((gates_ref[0, 0, kh] * ocmp_ref[0, kh] + gates_ref[1, 0, kh] * o_sel)
                            + gates_ref[2, 0, kh] * o_win)


def _nsa_sample_attend(pool, page_table, q, sel, expand, bias_sel, k_new, v_new, bias_new, win_buf, kw_new, vw_new,
                       bias_win, bias_win_new, gates, o_cmp):
    n, n_pages = page_table.shape
    n_steps = n_pages // PAGES_PER_STEP
    keys = PAGES_PER_STEP * BLK
    rows = GROUP * 8
    lw = win_buf.shape[1]
    per = lambda shape: pl.BlockSpec((1, KV_HEADS) + shape, lambda i, j, pt: (i, 0) + (0,) * len(shape))
    whole = lambda shape: pl.BlockSpec(shape, lambda i, j, pt: (0,) * len(shape))
    grid_spec = pltpu.PrefetchScalarGridSpec(
        num_scalar_prefetch=1,
        grid=(n, n_steps),
        in_specs=[_page_spec(p, 1) for p in range(PAGES_PER_STEP)] + [
            per((rows, HEAD_DIM)),
            per((n_steps + 1, 8, BLK)),
            whole(expand.shape),
            pl.BlockSpec((KV_HEADS, rows, keys), lambda i, j, pt: (0, 0, j)),
            per((BLK, HEAD_DIM)), per((BLK, HEAD_DIM)), whole((KV_HEADS, rows, BLK)),
            pl.BlockSpec((None, lw, HEAD_TILE, HEAD_DIM), lambda i, j, pt: (i, 0, 0, 0)),
            per((BLK, HEAD_DIM)), per((BLK, HEAD_DIM)), whole((KV_HEADS, rows, lw)), whole((KV_HEADS, rows, BLK)),
            pl.BlockSpec((3, 1, KV_HEADS, rows, HEAD_DIM), lambda i, j, pt: (0, i, 0, 0, 0)),
            per((rows, HEAD_DIM)),
        ],
        out_specs=per((rows, HEAD_DIM)),
        scratch_shapes=[pltpu.VMEM((KV_HEADS, rows, 1), F32), pltpu.VMEM((KV_HEADS, rows, 1), F32),
                        pltpu.VMEM((KV_HEADS, rows, HEAD_DIM), F32)],
    )
    return pl.pallas_call(
        _nsa_sample_attend_kernel,
        out_shape=jax.ShapeDtypeStruct((n, KV_HEADS, rows, HEAD_DIM), F32),
        grid_spec=grid_spec,
        compiler_params=_params(("parallel", "arbitrary")),
        name="nsa_sample_attend",
    )(page_table, *([pool] * PAGES_PER_STEP), q, sel, expand, bias_sel, k_new, v_new, bias_new, win_buf, kw_new,
      vw_new, bias_win, bias_win_new, gates, o_cmp)


def _sel_weights(n_cmp_rows, n_sel_cols):
    ratio = SEL_BLOCK // CMP_STRIDE
    span = CMP_BLOCK // CMP_STRIDE
    c = jnp.arange(n_cmp_rows)[:, None]
    j = jnp.arange(n_sel_cols)[None, :]
    o = c - ratio * j + (span - 1)
    cnt = jnp.minimum(o, span - 1) - jnp.maximum(o - (ratio - 1), 0) + 1
    return jnp.where((o >= 0) & (o <= ratio + span - 2), cnt, 0).astype(BF16)


def _layer_b(hp, hs, b, s, n, t, norm_g, w_in, w_out, rel_bias, cmp, pool, page_table, win_buf):
    cmp_pe, cmp_w1, cmp_b1, cmp_w2, cmp_b2 = cmp
    n_kvcol = 6 * NSA_KV
    w_main = w_in
    w_gate = jnp.pad(w_in[:, NSA_Q + n_kvcol:], ((0, 0), (0, BLK - 3 * N_HEADS))).astype(BF16)
    w1 = cmp_w1.reshape(2, 2, CMP_STRIDE, HEAD_DIM, HEAD_DIM)
    w1r = jnp.concatenate([w1[:, 0], w1[:, 1]], axis=-1).astype(BF16)
    pe_row = jnp.pad(cmp_pe.reshape(2, 1, CMP_BLOCK * HEAD_DIM), ((0, 0), (0, 7), (0, 0))).astype(BF16)
    cmp_w = (pe_row, cmp_w1.reshape(2, CMP_BLOCK * HEAD_DIM, HEAD_DIM).astype(BF16), cmp_b1.reshape(2, 1, HEAD_DIM),
             cmp_w2.astype(BF16), cmp_b2.reshape(2, 1, HEAD_DIM))
    weights = (w_main, w_gate, w_out, w1r, cmp_w)
    hs, new_win_s, new_kv_s, w_main, w_out = _nsa_sample_path(hs, n, t, norm_g, weights, rel_bias, pool, page_table,
                                                              win_buf)
    weights = (w_main, w_gate, w_out, w1r, cmp_w)
    hp, new_win_p, new_kv_p = _nsa_prompt_path(hp, b, s, norm_g, weights, rel_bias)
    return hp, hs, new_win_p, new_win_s, new_kv_p, new_kv_s


def _nsa_prompt_path(hp, b, s, norm_g, weights, rel_bias):
    w_main, w_gate, w_out, w1r, cmp_w = weights
    proj_p = _project(hp, w_main, gain=norm_g)
    gates_p = _project(hp, w_gate, gain=norm_g, sigmoid=True)
    col_tile = HEAD_TILE * HEAD_DIM
    new_kv = _vector_rows(proj_p, NSA_Q // col_tile, 4 * NSA_KV // col_tile, b, s, s)
    keep = min(NSA_WINDOW, s)
    new_win = _vector_rows(proj_p, (NSA_Q + 4 * NSA_KV) // col_tile, 2 * NSA_KV // col_tile, b, s, keep)
    table_p = jnp.arange(b * s // BLK, dtype=jnp.int32).reshape(b, s // BLK)
    a_cmp_p = _cmp_proj(new_kv.reshape(b * s, 2 * HEAD_TILE, HEAD_DIM), table_p, w1r)
    nq = s // BLK
    n_win = (NSA_WINDOW - 1 + BLK - 1) // BLK + 1
    tbl = _bias_table(rel_bias, nq + 1, BLK, BLK, -BLK, BLK, -1, 1, 0, 1 << 30, scale=LOG2E)
    tbl_win = _bias_table(rel_bias, n_win + 1, BLK, BLK, -BLK, BLK, -1, 1, 0, NSA_WINDOW - 1, scale=LOG2E)
    bias_c = _bias_table(rel_bias, 1, s // CMP_STRIDE, s, -(CMP_BLOCK - 1), 0, -CMP_STRIDE, 1, 0, 1 << 30)
    gates_t = gates_p[:, :3 * N_HEADS].reshape(b * s, 3, KV_HEADS, GROUP).transpose(2, 1, 3, 0)
    gates_t = jnp.pad(gates_t.reshape(KV_HEADS, 3 * GROUP, b * s), ((0, 0), (0, 16 - 3 * GROUP), (0, 0)))
    msel_p = _sel_weights(s // CMP_STRIDE, s // SEL_BLOCK).T
    o_t = _nsa_prompt(proj_p, a_cmp_p, gates_t, cmp_w, tbl, tbl_win, bias_c, msel_p, b, s)
    o_p = o_t.transpose(0, 3, 1, 2).reshape(b * s, NSA_Q).astype(BF16)
    hp = _project(o_p, w_out, res=hp)
    return (hp, new_win.reshape(b, keep, 2, KV_HEADS, HEAD_DIM), new_kv.reshape(b, s, 4, KV_HEADS, HEAD_DIM))


def _nsa_sample_path(hs, n, t, norm_g, weights, rel_bias, pool, page_table, win_buf):
    w_main, w_gate, w_out, w1r, cmp_w = weights
    past = page_table.shape[1] * BLK
    proj_s, w_main = _project(hs, w_main, n_cols=NSA_Q + 6 * NSA_KV, gain=norm_g, cast_w=True)
    gates_s = _project(hs, w_gate, gain=norm_g, sigmoid=True)
    kv_s = proj_s[:, NSA_Q:].reshape(n, t, 6, KV_HEADS, HEAD_DIM)
    rows = GROUP * 8

    def head_rows(x):
        x = jnp.pad(x.transpose(0, 2, 3, 1, 4), ((0, 0), (0, 0), (0, 0), (0, 8 - t), (0, 0)))
        return x.reshape(n, KV_HEADS, rows, x.shape[-1])

    def new_rows(c):
        return jnp.pad(kv_s[:, :, c].transpose(0, 2, 1, 3), ((0, 0), (0, 0), (0, BLK - t), (0, 0)))

    def head_table(x, cols):
        return x.reshape(KV_HEADS, rows, cols)

    q_s = head_rows(proj_s[:, :NSA_Q].reshape(n, t, KV_HEADS, GROUP, HEAD_DIM))
    pool2 = pool.reshape(pool.shape[0] * BLK, 2 * HEAD_TILE, HEAD_DIM)
    a_cmp_s = _cmp_proj(pool2, page_table, w1r)
    chunks = past // CMP_STRIDE
    big = 1 << 30
    bias_cs = head_table(_bias_table(rel_bias, 1, 8, chunks, past - (CMP_BLOCK - 1), 0, 1, -CMP_STRIDE, 0, big), chunks)
    n_steps = page_table.shape[1] // PAGES_PER_STEP
    n_sel_pad = (n_steps + 1) * BLK
    msel_s = _sel_weights(chunks, n_sel_pad)
    per_step = PAGES_PER_STEP * BLK // SEL_BLOCK
    jj = jnp.arange(n_sel_pad)[None, :, None]
    ll = jnp.arange(BLK)[None, None, :]
    st = jnp.arange(n_steps + 1)[:, None, None]
    regroup = ((jj == st * per_step + ll) & (ll < per_step)).astype(BF16)
    o_cmp_s, sel_s = _nsa_sample_select(q_s, a_cmp_s, cmp_w, bias_cs, msel_s, regroup, past, t)
    expand = (jnp.arange(BLK)[:, None] == jnp.arange(PAGES_PER_STEP * BLK)[None, :] // SEL_BLOCK).astype(BF16)
    bias_sel = head_table(_bias_table(rel_bias, 1, 8, past, past, 0, 1, -1, 0, big), past)
    bias_new = head_table(_bias_table(rel_bias, 1, 8, BLK, 0, 0, 1, -1, 0, big), BLK)
    lw = win_buf.shape[1]
    bias_win = head_table(_bias_table(rel_bias, 1, 8, lw, lw, 0, 1, -1, 0, NSA_WINDOW - 1), lw)
    bias_win_new = head_table(_bias_table(rel_bias, 1, 8, BLK, 0, 0, 1, -1, 0, NSA_WINDOW - 1), BLK)
    g_s = gates_s[:, :3 * N_HEADS].reshape(n, t, 3, KV_HEADS, GROUP, 1)
    g_s = jnp.stack([head_rows(g_s[:, :, c]) for c in range(3)])
    g_s = jnp.broadcast_to(g_s, (3, n, KV_HEADS, rows, HEAD_DIM))
    o_s = _nsa_sample_attend(pool2, page_table, q_s, sel_s, expand, bias_sel, new_rows(2), new_rows(3), bias_new,
                             win_buf.reshape(n, lw, HEAD_TILE, HEAD_DIM), new_rows(4), new_rows(5), bias_win,
                             bias_win_new,
                             g_s, o_cmp_s)
    o_s = o_s.reshape(n, KV_HEADS, GROUP, 8, HEAD_DIM)[:, :, :, :t].transpose(0, 3, 1, 2, 4)
    hs, w_out = _project(o_s.reshape(n * t, NSA_Q).astype(BF16), w_out, res=hs, cast_w=True)
    new_win_s = jnp.concatenate([win_buf, kv_s[:, :, 4:]], axis=1)[:, t:]
    return hs, new_win_s, kv_s[:, :, :4], w_main, w_out


def _ffn_and_ple(hp, hs, b, s, n, t, i, norm_ffn, norm_ple, w_ffn_in, conv_w, conv_b, w_ffn_out, state_conv,
                 p_prompt, p_sample, w_ple_gate, w_ple_proj):
    hu_s, w_ffn_in = _project(hs, w_ffn_in, layer=i, gain=norm_ffn, cast_w=True)
    hu_p = _project(hp, w_ffn_in, gain=norm_ffn)
    conv_p = hu_p.reshape(b, s, 2 * D_FF)[:, s - (CONV_W - 1):, :D_FF]
    conv_s = jnp.concatenate([state_conv, hu_s.reshape(n, t, 2 * D_FF)[:, :, :D_FF]], axis=1)[:, t:]
    hs, w_ffn_out = _ffn_out_sample(hu_s, state_conv, conv_w, conv_b, w_ffn_out, i, hs, t)
    hp = _ffn_out_prompt(hu_p, conv_w, conv_b, w_ffn_out, hp, s)
    hs, w_ple_gate, w_ple_proj = _ple_add(hs, norm_ple, w_ple_gate, p_sample.astype(BF16), w_ple_proj, layer=i,
                                          cast_w=True)
    hp = _ple_add(hp, norm_ple, w_ple_gate, p_prompt.astype(BF16), w_ple_proj)
    return hp, hs, conv_p, conv_s


def kernel(x_prompt, x_sample, state_dil_w128, state_dil_w512, state_dil_w2048, state_nsa_win, state_conv,
           cache_nsa_kv, page_table, p_prompt, p_sample, rel_bias, norm_mix, norm_ffn, norm_ple, norm_final,
           w_in_a, w_out_a, w_in_b, w_out_b, cmp_pe, cmp_w1, cmp_b1, cmp_w2, cmp_b2, w_ffn_in, conv_w, conv_b,
           w_ffn_out, w_ple_gate, w_ple_proj):
    b, s, d = x_prompt.shape
    n, t, _ = x_sample.shape
    depth = norm_mix.shape[0]
    hp, hs = x_prompt.reshape(b * s, d), x_sample.reshape(n * t, d)
    dil_p, dil_s = [[] for _ in range(N_DIL)], [[] for _ in range(N_DIL)]
    win_p, win_s, kv_p, kv_s, conv_p, conv_s = [], [], [], [], [], []
    for i in range(depth):
        li = i // 2
        if i % 2 == 0:
            hp, hs, new_p, new_s = _layer_a(
                hp, hs, b, s, n, t, norm_mix[i], w_in_a[li], w_out_a[li], rel_bias,
                (state_dil_w128[li], state_dil_w512[li], state_dil_w2048[li]))
            for g in range(N_DIL):
                dil_p[g].append(new_p[g])
                dil_s[g].append(new_s[g])
        else:
            hp, hs, wp, ws, rp, rs = _layer_b(
                hp, hs, b, s, n, t, norm_mix[i], w_in_b[li], w_out_b[li], rel_bias,
                (cmp_pe[li], cmp_w1[li], cmp_b1[li], cmp_w2[li], cmp_b2[li]), cache_nsa_kv[li], page_table,
                state_nsa_win[li])
            win_p.append(wp)
            win_s.append(ws)
            kv_p.append(rp)
            kv_s.append(rs)
        hp, hs, cp, cs = _ffn_and_ple(
            hp, hs, b, s, n, t, i, norm_ffn[i], norm_ple[i], w_ffn_in, conv_w[i], conv_b[i],
            w_ffn_out, state_conv[i], p_prompt[i].reshape(b * s, -1), p_sample[i].reshape(n * t, -1),
            w_ple_gate, w_ple_proj)
        conv_p.append(cp)
        conv_s.append(cs)
    y_prompt = _rmsnorm(hp, norm_final, F32).reshape(b, s, d)
    y_sample = _rmsnorm(hs, norm_final, F32).reshape(n, t, d)
    return (y_prompt, y_sample,
            jnp.stack(dil_p[0]), jnp.stack(dil_s[0]), jnp.stack(dil_p[1]), jnp.stack(dil_s[1]),
            jnp.stack(dil_p[2]), jnp.stack(dil_s[2]),
            jnp.stack(win_p), jnp.stack(win_s), jnp.stack(conv_p), jnp.stack(conv_s),
            jnp.stack(kv_p), jnp.stack(kv_s))
```

```python
import functools

import jax
import jax.numpy as jnp
from jax import lax
from jax.experimental import pallas as pl
from jax.experimental.pallas import tpu as pltpu

F32 = jnp.float32
BF16 = jnp.bfloat16

D_MODEL = 2048
HEAD_DIM = 128
N_HEADS = 16
DIL_PAIRS = ((128, 1), (512, 4), (2048, 16))
N_DIL = 3
BLK = 128
KV_HEADS = 4
GROUP = 4
CMP_BLOCK = 32
CMP_STRIDE = 16
SEL_BLOCK = 64
SEL_TOPN = 16
NSA_WINDOW = 512
D_FF = 5632
CONV_W = 3
REL_BUCKETS = 32
EPS = 1e-6
NEG = -1e30
FORCED_SCORE = 1e4
SCALE = HEAD_DIM ** -0.5
LOG2E = 1.4426950408889634
QKV_A = N_DIL * 3 * N_HEADS * HEAD_DIM
NSA_Q = N_HEADS * HEAD_DIM
NSA_KV = KV_HEADS * HEAD_DIM

BUCKET_START = (1, 2, 3, 4, 5, 6, 7, 8, 9, 10, 11, 12, 13, 14, 15, 16, 22, 30, 40, 54, 73, 99,
                134, 182, 246, 332, 450, 609, 825, 1117, 1513)

VMEM_LIMIT_V7X = 56 * 1024 * 1024


def _params(sem, vmem=VMEM_LIMIT_V7X):
    return pltpu.CompilerParams(dimension_semantics=sem, vmem_limit_bytes=vmem)


def _pick(n, cands):
    for c in cands:
        if n % c == 0:
            return c
    return n


def _rmsnorm_kernel(x_ref, g_ref, o_ref):
    x = x_ref[...]
    ms = jnp.mean(x * x, axis=-1, keepdims=True)
    o_ref[...] = ((x * lax.rsqrt(ms + EPS)) * g_ref[...]).astype(o_ref.dtype)


def _rmsnorm(x, g, out_dtype):
    m, d = x.shape
    tm = _pick(m, (512, 256, 128, 32))
    return pl.pallas_call(
        _rmsnorm_kernel,
        out_shape=jax.ShapeDtypeStruct((m, d), out_dtype),
        grid=(m // tm,),
        in_specs=[pl.BlockSpec((tm, d), lambda i: (i, 0)), pl.BlockSpec((1, d), lambda i: (0, 0))],
        out_specs=pl.BlockSpec((tm, d), lambda i: (i, 0)),
        compiler_params=_params(("parallel",)),
        name="rmsnorm",
    )(x, g.reshape(1, d))


def _normed(x, g):
    ms = jnp.mean(x * x, axis=-1, keepdims=True)
    return ((x * lax.rsqrt(ms + EPS)) * g).astype(BF16)


def _project_kernel(*refs, norm, cast_w, residual, sigmoid, heads):
    refs = list(refs)
    x_ref = refs.pop(0)
    g_ref = refs.pop(0) if norm else None
    w_ref = refs.pop(0)
    r_ref = refs.pop(0) if residual else None
    o_ref = refs.pop(0)
    wb_ref = refs.pop(0) if cast_w else None
    if norm:
        a_scr = refs.pop(0)

        @pl.when(pl.program_id(1) == 0)
        def _():
            a_scr[...] = _normed(x_ref[...], g_ref[...])

        a = a_scr[...]
    else:
        a = x_ref[...]
    w = w_ref[...]
    if cast_w:
        w = w.astype(BF16)
        wb_ref[...] = w
    acc = jnp.dot(a, w, preferred_element_type=F32)
    if sigmoid:
        acc = jax.nn.sigmoid(acc)
    if residual:
        acc = r_ref[...] + acc
    if heads:
        for j in range(o_ref.shape[0]):
            o_ref[j] = acc[:, j * HEAD_DIM:(j + 1) * HEAD_DIM]
    else:
        o_ref[...] = acc


def _weight_spec(w, layer, k, tn, index):
    if w.ndim == 2:
        return pl.BlockSpec((k, tn), index)
    return pl.BlockSpec((None, k, tn), lambda *g: (layer,) + index(*g))


def _project(x, w, *, layer=None, n_cols=None, gain=None, res=None, cast_w=False, sigmoid=False, heads=False):
    m, k = x.shape
    n = n_cols or w.shape[-1]
    tm = _pick(m, (1024, 512, 256, 128))
    tn = _pick(n, (1024, 512, 256, 128))
    assert not cast_w or m == tm
    norm, residual = gain is not None, res is not None
    args, in_specs = [x], [pl.BlockSpec((tm, k), lambda i, j: (i, 0))]
    if norm:
        args.append(gain.reshape(1, k))
        in_specs.append(pl.BlockSpec((1, k), lambda i, j: (0, 0)))
    args.append(w)
    in_specs.append(_weight_spec(w, layer, k, tn, lambda i, j: (0, j)))
    if residual:
        args.append(res)
        in_specs.append(pl.BlockSpec((tm, tn), lambda i, j: (i, j)))
    if heads:
        out_shape = [jax.ShapeDtypeStruct((n // HEAD_DIM, m, HEAD_DIM), F32)]
        out_specs = [pl.BlockSpec((tn // HEAD_DIM, tm, HEAD_DIM), lambda i, j: (j, i, 0))]
    else:
        out_shape = [jax.ShapeDtypeStruct((m, n), F32)]
        out_specs = [pl.BlockSpec((tm, tn), lambda i, j: (i, j))]
    if cast_w:
        out_shape.append(jax.ShapeDtypeStruct((k, n), BF16))
        out_specs.append(pl.BlockSpec((k, tn), lambda i, j: (0, j)))
    outs = pl.pallas_call(
        functools.partial(_project_kernel, norm=norm, cast_w=cast_w, residual=residual, sigmoid=sigmoid,
                          heads=heads),
        out_shape=out_shape,
        grid=(m // tm, n // tn),
        in_specs=in_specs,
        out_specs=out_specs,
        scratch_shapes=[pltpu.VMEM((tm, k), BF16)] if norm else [],
        compiler_params=_params(("parallel", "arbitrary")),
        name="project",
    )(*args)
    return tuple(outs) if cast_w else outs[0]


def _conv_gelu_val(g, g1, g2, val, cw_ref, cb_ref):
    c = cb_ref[...] + g2 * cw_ref[0:1, :]
    c = c + g1 * cw_ref[1:2, :]
    c = c + g * cw_ref[2:3, :]
    return (jax.nn.gelu(c) * val).astype(BF16)


def _ffn_out_prompt_kernel(g_ref, halo_ref, v_ref, cw_ref, cb_ref, w_ref, r_ref, o_ref, acc_ref, *,
                           tiles_per_seq):
    i, k = pl.program_id(0), pl.program_id(1)

    @pl.when(k == 0)
    def _():
        acc_ref[...] = jnp.zeros_like(acc_ref)

    g = g_ref[...]
    row = lax.broadcasted_iota(jnp.int32, g.shape, 0)
    halo = jnp.where(i % tiles_per_seq == 0, 0.0, halo_ref[...])
    g1 = jnp.where(row == 0, halo[7:8, :], pltpu.roll(g, 1, 0))
    g2 = jnp.where(row == 0, halo[6:7, :], jnp.where(row == 1, halo[7:8, :], pltpu.roll(g, 2, 0)))
    u = _conv_gelu_val(g, g1, g2, v_ref[...], cw_ref, cb_ref)
    acc_ref[...] += jnp.dot(u, w_ref[...], preferred_element_type=F32)

    @pl.when(k == pl.num_programs(1) - 1)
    def _():
        o_ref[...] = r_ref[...] + acc_ref[...]


def _ffn_out_prompt(hu, conv_w, conv_b, w_out, res, seq):
    m = hu.shape[0]
    tm, tk = 512, D_FF // 4
    nk = D_FF // tk
    return pl.pallas_call(
        functools.partial(_ffn_out_prompt_kernel, tiles_per_seq=seq // tm),
        out_shape=jax.ShapeDtypeStruct((m, D_MODEL), F32),
        grid=(m // tm, nk),
        in_specs=[
            pl.BlockSpec((tm, tk), lambda i, k: (i, k)),
            pl.BlockSpec((8, tk), lambda i, k: (jnp.maximum(i * (tm // 8) - 1, 0), k)),
            pl.BlockSpec((tm, tk), lambda i, k: (i, k + nk)),
            pl.BlockSpec((CONV_W, tk), lambda i, k: (0, k)),
            pl.BlockSpec((1, tk), lambda i, k: (0, k)),
            pl.BlockSpec((tk, D_MODEL), lambda i, k: (k, 0)),
            pl.BlockSpec((tm, D_MODEL), lambda i, k: (i, 0)),
        ],
        out_specs=pl.BlockSpec((tm, D_MODEL), lambda i, k: (i, 0)),
        scratch_shapes=[pltpu.VMEM((tm, D_MODEL), F32)],
        compiler_params=_params(("parallel", "arbitrary")),
        name="ffn_out_prompt",
    )(hu, hu, hu, conv_w, conv_b.reshape(1, D_FF), w_out, res)


def _ffn_out_sample_kernel(g_ref, e1_ref, e2_ref, v_ref, cw_ref, cb_ref, w_ref, r_ref, o_ref, wb_ref, acc_ref, *,
                           t_len):
    k = pl.program_id(0)

    @pl.when(k == 0)
    def _():
        acc_ref[...] = jnp.zeros_like(acc_ref)

    g = g_ref[...]
    t = lax.broadcasted_iota(jnp.int32, g.shape, 0) % t_len
    g1 = jnp.where(t == 0, e1_ref[...], pltpu.roll(g, 1, 0))
    g2 = jnp.where(t < 2, e2_ref[...], pltpu.roll(g, 2, 0))
    u = _conv_gelu_val(g, g1, g2, v_ref[...], cw_ref, cb_ref)
    w = w_ref[...].astype(BF16)
    wb_ref[...] = w
    acc_ref[...] += jnp.dot(u, w, preferred_element_type=F32)

    @pl.when(k == pl.num_programs(0) - 1)
    def _():
        o_ref[...] = r_ref[...] + acc_ref[...]


def _ffn_out_sample(hu, conv_prev, conv_w, conv_b, w_out, layer, res, t_len):
    m = hu.shape[0]
    n = m // t_len
    tk = 512
    nk = D_FF // tk
    zeros = jnp.zeros((n, t_len - 1, D_FF), F32)
    e1 = jnp.concatenate([conv_prev[:, 1:2], zeros], axis=1).reshape(m, D_FF)
    e2 = jnp.concatenate([conv_prev, zeros[:, 1:]], axis=1).reshape(m, D_FF)
    return pl.pallas_call(
        functools.partial(_ffn_out_sample_kernel, t_len=t_len),
        out_shape=(jax.ShapeDtypeStruct((m, D_MODEL), F32), jax.ShapeDtypeStruct((D_FF, D_MODEL), BF16)),
        grid=(nk,),
        in_specs=[
            pl.BlockSpec((m, tk), lambda k: (0, k)),
            pl.BlockSpec((m, tk), lambda k: (0, k)),
            pl.BlockSpec((m, tk), lambda k: (0, k)),
            pl.BlockSpec((m, tk), lambda k: (0, k + nk)),
            pl.BlockSpec((CONV_W, tk), lambda k: (0, k)),
            pl.BlockSpec((1, tk), lambda k: (0, k)),
            pl.BlockSpec((None, tk, D_MODEL), lambda k: (layer, k, 0)),
            pl.BlockSpec((m, D_MODEL), lambda k: (0, 0)),
        ],
        out_specs=(pl.BlockSpec((m, D_MODEL), lambda k: (0, 0)), pl.BlockSpec((tk, D_MODEL), lambda k: (k, 0))),
        scratch_shapes=[pltpu.VMEM((m, D_MODEL), F32)],
        compiler_params=_params(("arbitrary",)),
        name="ffn_out_sample",
    )(hu, e1, e2, hu, conv_w, conv_b.reshape(1, D_FF), w_out, res)


def _ple_kernel(h_ref, g_ref, wg_ref, p_ref, wp_ref, o_ref, *rest, cast_w):
    a_scr = rest[-1]
    j = pl.program_id(1)
    tn = o_ref.shape[1]

    @pl.when(j == 0)
    def _():
        a_scr[...] = _normed(h_ref[...], g_ref[...])

    wg, wp = wg_ref[...], wp_ref[...]
    if cast_w:
        wg, wp = wg.astype(BF16), wp.astype(BF16)
        rest[0][...] = wg
        rest[1][...] = wp
    gate = jax.nn.sigmoid(jnp.dot(a_scr[...], wg, preferred_element_type=F32))
    proj = jnp.dot(p_ref[...], wp, preferred_element_type=F32)
    o_ref[...] = h_ref[:, pl.ds(pl.multiple_of(j * tn, tn), tn)] + gate * proj


def _ple_add(h, gain, w_gate, p, w_proj, layer=None, cast_w=False):
    m, d = h.shape
    kp = p.shape[1]
    tm = _pick(m, (1024, 512, 256, 128))
    tn = 1024
    assert not cast_w or m == tm
    out_shape = [jax.ShapeDtypeStruct((m, d), F32)]
    out_specs = [pl.BlockSpec((tm, tn), lambda i, j: (i, j))]
    if cast_w:
        out_shape += [jax.ShapeDtypeStruct((d, d), BF16), jax.ShapeDtypeStruct((kp, d), BF16)]
        out_specs += [pl.BlockSpec((d, tn), lambda i, j: (0, j)), pl.BlockSpec((kp, tn), lambda i, j: (0, j))]
    outs = pl.pallas_call(
        functools.partial(_ple_kernel, cast_w=cast_w),
        out_shape=out_shape,
        grid=(m // tm, d // tn),
        in_specs=[
            pl.BlockSpec((tm, d), lambda i, j: (i, 0)),
            pl.BlockSpec((1, d), lambda i, j: (0, 0)),
            _weight_spec(w_gate, layer, d, tn, lambda i, j: (0, j)),
            pl.BlockSpec((tm, kp), lambda i, j: (i, 0)),
            _weight_spec(w_proj, layer, kp, tn, lambda i, j: (0, j)),
        ],
        out_specs=out_specs,
        scratch_shapes=[pltpu.VMEM((tm, d), BF16)],
        compiler_params=_params(("parallel", "arbitrary")),
        name="ple_add",
    )(h, gain.reshape(1, d), w_gate, p, w_proj)
    return tuple(outs) if cast_w else outs[0]


def _bias_table_kernel(rbt_ref, o_ref, *, a0, ag, ar, ac, lo, hi, mod, scale):
    g = pl.program_id(0)
    n_r, n_c = o_ref.shape[2:]
    shape = (n_r, BLK)
    rows = ar * lax.broadcasted_iota(jnp.int32, shape, 0)
    cols = ac * lax.broadcasted_iota(jnp.int32, shape, 1)
    for c0 in range(0, n_c, BLK):
        dist = (a0 + ac * c0) + ag * g + rows + cols
        d = jnp.maximum(dist, 0)
        bucket = jnp.zeros(shape, jnp.int32)
        for start in BUCKET_START:
            bucket = bucket + (d >= start).astype(jnp.int32)
        ok = (dist >= lo) & (dist <= hi)
        if mod > 1:
            ok = ok & ((d & (mod - 1)) == 0)
        for h in range(N_HEADS):
            row = jnp.broadcast_to(rbt_ref[h:h + 1, :], shape)
            if scale != 1.0:
                row = row * scale
            o_ref[h, 0, :, c0:c0 + BLK] = jnp.where(ok, jnp.take_along_axis(row, bucket, axis=1), NEG)


def _bias_table(rel_bias, n_g, n_r, n_c, a0, ag, ar, ac, lo, hi, mod=1, scale=1.0):
    assert mod & (mod - 1) == 0 and n_c % BLK == 0
    rbt = jnp.pad(rel_bias.T, ((0, 0), (0, BLK - REL_BUCKETS)))
    return pl.pallas_call(
        functools.partial(_bias_table_kernel, a0=a0, ag=ag, ar=ar, ac=ac, lo=lo, hi=hi, mod=mod, scale=scale),
        out_shape=jax.ShapeDtypeStruct((N_HEADS, n_g, n_r, n_c), F32),
        grid=(n_g,),
        in_specs=[pl.BlockSpec((N_HEADS, BLK), lambda g: (0, 0))],
        out_specs=pl.BlockSpec((N_HEADS, 1, n_r, n_c), lambda g: (0, g, 0, 0)),
        compiler_params=_params(("parallel",)),
        name="bias_table",
    )(rbt)


def _dil_prompt_kernel(q_ref, kc_ref, vc_ref, bias_ref, o_ref, lse_ref, *prev_scr, dil, hps):
    first = pl.program_id(1) == 0
    hb = pl.program_id(2)
    lane = lax.broadcasted_iota(jnp.int32, (BLK, BLK), 1)

    @pl.when(hb == 0)
    def _():
        lse_ref[...] = jnp.zeros_like(lse_ref)

    if prev_scr:
        @pl.when(first & (hb == 0))
        def _():
            for scr in prev_scr:
                scr[...] = jnp.zeros_like(scr)

    for r in range(dil):
        rows = pl.ds(r, BLK, stride=dil)
        lse_rows = lse_ref[rows, :]
        for hh in range(hps):
            h = hb * hps + hh
            q = q_ref[hh, rows, :].astype(BF16)
            k, v, bias = kc_ref[hh, rows, :], vc_ref[hh, rows, :], bias_ref[h]
            kp, vp = (scr[h, rows, :] for scr in prev_scr) if prev_scr else (jnp.zeros_like(k),) * 2
            k, v = jnp.concatenate([kp, k], axis=0), jnp.concatenate([vp, v], axis=0)
            col = lax.broadcasted_iota(jnp.int32, (BLK, 2 * BLK), 1)
            bias = bias + jnp.where(first & (col < BLK), NEG, 0.0)
            s = lax.dot_general(q, k.astype(BF16), (((1,), (1,)), ((), ())), preferred_element_type=F32)
            s = s * SCALE + bias
            m = jnp.max(s, axis=-1, keepdims=True)
            p = jnp.exp(s - m)
            l = jnp.sum(p, axis=-1, keepdims=True)
            o_ref[hh, rows, :] = jnp.dot(p.astype(BF16), v.astype(BF16), preferred_element_type=F32) / l
            lse_rows = jnp.where(lane == h, m + jnp.log(l), lse_rows)
        lse_ref[rows, :] = lse_rows

    if prev_scr:
        heads = pl.ds(hb * hps, hps)
        prev_scr[0][heads] = kc_ref[...]
        prev_scr[1][heads] = vc_ref[...]


def _dil_prompt_group(qkv_hm, bias, grp, dil, b, s):
    span = BLK * dil
    nsp = s // span
    hps = N_HEADS // dil
    nhb = N_HEADS // hps

    def slab(part):
        base = (grp * 3 + part) * N_HEADS // hps
        return lambda bi, sp, hb: (base + hb, bi * nsp + sp, 0)

    blk = (hps, span, HEAD_DIM)
    carry = [pltpu.VMEM((N_HEADS, span, HEAD_DIM), F32)] * 2 if nsp > 1 else []
    return pl.pallas_call(
        functools.partial(_dil_prompt_kernel, dil=dil, hps=hps),
        out_shape=(jax.ShapeDtypeStruct((N_HEADS, b * s, HEAD_DIM), F32),
                   jax.ShapeDtypeStruct((b * s, BLK), F32)),
        grid=(b, nsp, nhb),
        in_specs=[pl.BlockSpec(blk, slab(0)), pl.BlockSpec(blk, slab(1)), pl.BlockSpec(blk, slab(2)),
                  pl.BlockSpec((N_HEADS, BLK, 2 * BLK), lambda bi, sp, hb: (0, 0, 0))],
        out_specs=(pl.BlockSpec(blk, lambda bi, sp, hb: (hb, bi * nsp + sp, 0)),
                   pl.BlockSpec((span, BLK), lambda bi, sp, hb: (bi * nsp + sp, 0))),
        scratch_shapes=carry,
        compiler_params=_params(("parallel", "arbitrary", "arbitrary")),
        name=f"dil_attn_prompt_g{grp}",
    )(qkv_hm, qkv_hm, qkv_hm, bias)


def _dil_combine_kernel(o0_ref, o1_ref, o2_ref, l0_ref, l1_ref, l2_ref, o_ref, *, head_major):
    l0, l1, l2 = l0_ref[...], l1_ref[...], l2_ref[...]
    mx = jnp.maximum(jnp.maximum(l0, l1), l2)
    e0, e1, e2 = jnp.exp(l0 - mx), jnp.exp(l1 - mx), jnp.exp(l2 - mx)
    den = e0 + e1 + e2
    w0, w1, w2 = e0 / den, e1 / den, e2 / den
    for h in range(N_HEADS):
        sl = slice(h * HEAD_DIM, (h + 1) * HEAD_DIM)
        g0, g1, g2 = ((r[h] for r in (o0_ref, o1_ref, o2_ref)) if head_major
                      else (r[:, sl] for r in (o0_ref, o1_ref, o2_ref)))
        o = (w0[:, h:h + 1] * g0 + w1[:, h:h + 1] * g1) + w2[:, h:h + 1] * g2
        o_ref[:, sl] = o.astype(o_ref.dtype)


def _dil_combine(outs, lses, head_major):
    m = lses[0].shape[0]
    wide = N_HEADS * HEAD_DIM
    tm = _pick(m, (256, 128, 32))
    ob = pl.BlockSpec((tm, wide), lambda i: (i, 0))
    ib = pl.BlockSpec((N_HEADS, tm, HEAD_DIM), lambda i: (0, i, 0)) if head_major else ob
    lb = pl.BlockSpec((tm, BLK), lambda i: (i, 0))
    return pl.pallas_call(
        functools.partial(_dil_combine_kernel, head_major=head_major),
        out_shape=jax.ShapeDtypeStruct((m, wide), BF16),
        grid=(m // tm,),
        in_specs=[ib, ib, ib, lb, lb, lb],
        out_specs=ob,
        compiler_params=_params(("parallel",)),
        name="dil_combine",
    )(*outs, *lses)


HEAD_TILE = 8


def _heads_first(x):
    return pltpu.einshape("mhd->hmd", x)


def _dil_sample_kernel(q_ref, k_ref, v_ref, kn_ref, vn_ref, bias_ref, biasn_ref, o_ref, lse_ref,
                       m_scr, l_scr, acc_scr):
    ht, c = pl.program_id(1), pl.program_id(2)
    k_all = _heads_first(k_ref[...])
    v_all = _heads_first(v_ref[...])

    @pl.when(c == 0)
    def _():
        m_scr[...] = jnp.full_like(m_scr, NEG)
        l_scr[...] = jnp.zeros_like(l_scr)
        acc_scr[...] = jnp.zeros_like(acc_scr)

    @pl.when((c == 0) & (ht == 0))
    def _():
        lse_ref[...] = jnp.zeros_like(lse_ref)

    def update(ks, vs, bias):
        s = jnp.concatenate(
            [lax.dot_general(q_ref[0, :, hh * HEAD_DIM:(hh + 1) * HEAD_DIM].astype(BF16), ks[hh].astype(BF16),
                             (((1,), (1,)), ((), ())), preferred_element_type=F32) for hh in range(HEAD_TILE)], axis=0)
        s = s * SCALE + bias
        m_old = m_scr[...]
        m_new = jnp.maximum(m_old, jnp.max(s, axis=-1, keepdims=True))
        alpha = jnp.exp(m_old - m_new)
        p = jnp.where(bias > 0.5 * NEG, jnp.exp(s - m_new), 0.0)
        l_scr[...] = alpha * l_scr[...] + jnp.sum(p, axis=-1, keepdims=True)
        pv = jnp.concatenate(
            [jnp.dot(p[hh * 8:(hh + 1) * 8].astype(BF16), vs[hh].astype(BF16), preferred_element_type=F32)
             for hh in range(HEAD_TILE)], axis=0)
        acc_scr[...] = alpha * acc_scr[...] + pv
        m_scr[...] = m_new

    n_keys = k_ref.shape[0]
    update(k_all, v_all, bias_ref[:, 0].reshape(HEAD_TILE * 8, n_keys))

    @pl.when(c == pl.num_programs(2) - 1)
    def _():
        cols = [slice(hh * HEAD_DIM, (hh + 1) * HEAD_DIM) for hh in range(HEAD_TILE)]
        update([kn_ref[0, :, sl] for sl in cols], [vn_ref[0, :, sl] for sl in cols],
               biasn_ref[...].reshape(HEAD_TILE * 8, BLK))
        l = jnp.maximum(l_scr[...], 1e-30)
        o = acc_scr[...] / l
        lse_rows = m_scr[...] + jnp.log(l)
        lane = lax.broadcasted_iota(jnp.int32, (8, BLK), 1)
        lse = lse_ref[0]
        for hh in range(HEAD_TILE):
            o_ref[0, :, cols[hh]] = o[hh * 8:(hh + 1) * 8]
            lse = jnp.where(lane == ht * HEAD_TILE + hh, lse_rows[hh * 8:(hh + 1) * 8], lse)
        lse_ref[0] = lse


def _dil_sample_group(q, kn, vn, buf, bias, bias_new, dil):
    n, lb = buf.shape[:2]
    n_cls = bias.shape[1]
    wide = N_HEADS * HEAD_DIM
    half = wide // 2
    rows = lb // dil
    tiles = 2 * N_HEADS // HEAD_TILE
    view = buf.reshape(n, rows, dil * tiles, HEAD_TILE, HEAD_DIM)
    nht = N_HEADS // HEAD_TILE
    return pl.pallas_call(
        _dil_sample_kernel,
        out_shape=(jax.ShapeDtypeStruct((n, 8, wide), F32), jax.ShapeDtypeStruct((n, 8, BLK), F32)),
        grid=(n, nht, n_cls),
        in_specs=[
            pl.BlockSpec((1, 8, half), lambda i, ht, c: (i, 0, ht)),
            pl.BlockSpec((None, rows, None, HEAD_TILE, HEAD_DIM), lambda i, ht, c: (i, 0, c * tiles + ht, 0, 0)),
            pl.BlockSpec((None, rows, None, HEAD_TILE, HEAD_DIM),
                         lambda i, ht, c: (i, 0, c * tiles + nht + ht, 0, 0)),
            pl.BlockSpec((1, BLK, half), lambda i, ht, c: (i, 0, ht)),
            pl.BlockSpec((1, BLK, half), lambda i, ht, c: (i, 0, ht)),
            pl.BlockSpec((HEAD_TILE, 1, 8, rows), lambda i, ht, c: (ht, c, 0, 0)),
            pl.BlockSpec((HEAD_TILE, 8, BLK), lambda i, ht, c: (ht, 0, 0)),
        ],
        out_specs=(pl.BlockSpec((1, 8, half), lambda i, ht, c: (i, 0, ht)),
                   pl.BlockSpec((1, 8, BLK), lambda i, ht, c: (i, 0, 0))),
        scratch_shapes=[pltpu.VMEM((HEAD_TILE * 8, 1), F32), pltpu.VMEM((HEAD_TILE * 8, 1), F32),
                        pltpu.VMEM((HEAD_TILE * 8, HEAD_DIM), F32)],
        compiler_params=_params(("parallel", "arbitrary", "arbitrary")),
        name="dil_attn_sample",
    )(q, view, view, kn, vn, bias, bias_new)


def _vector_rows_kernel(x_ref, o_ref):
    x = x_ref[...]
    if x.ndim == 2:
        x = jnp.stack([x[:, j * HEAD_DIM:(j + 1) * HEAD_DIM] for j in range(HEAD_TILE)])
    o_ref[...] = pltpu.einshape("hmd->mhd", x)


def _vector_rows(x, tile0, n_tiles, b, s, keep):
    tr = min(keep, 512)
    r0 = (s - keep) // tr
    per_seq = s // tr
    if x.ndim == 3:
        in_spec = pl.BlockSpec((HEAD_TILE, tr, HEAD_DIM), lambda bi, r, c: (tile0 + c, bi * per_seq + r0 + r, 0))
    else:
        in_spec = pl.BlockSpec((tr, HEAD_TILE * HEAD_DIM), lambda bi, r, c: (bi * per_seq + r0 + r, tile0 + c))
    return pl.pallas_call(
        _vector_rows_kernel,
        out_shape=jax.ShapeDtypeStruct((b, keep, n_tiles * HEAD_TILE, HEAD_DIM), F32),
        grid=(b, keep // tr, n_tiles),
        in_specs=[in_spec],
        out_specs=pl.BlockSpec((None, tr, HEAD_TILE, HEAD_DIM), lambda bi, r, c: (bi, r, c, 0)),
        compiler_params=_params(("parallel", "parallel", "parallel")),
        name="vector_rows",
    )(x)


def _pad_rows(x, rows):
    return jnp.pad(x, ((0, 0), (0, rows - x.shape[1]), (0, 0)))


def _layer_a(hp, hs, b, s, n, t, norm_g, w_in, w_out, rel_bias, bufs):
    wide = N_HEADS * HEAD_DIM
    qkv_s, w_in = _project(hs, w_in, gain=norm_g, cast_w=True)
    qkv_s = qkv_s.reshape(n, t, QKV_A)
    qkv_p = _project(hp, w_in, gain=norm_g, heads=True)
    outs_p, lses_p, outs_s, lses_s, new_p, new_s = [], [], [], [], [], []
    for grp, (win, dil) in enumerate(DIL_PAIRS):
        base = grp * 3 * wide
        bias = _bias_table(rel_bias, 1, BLK, 2 * BLK, BLK * dil, 0, dil, -dil, 0, win).reshape(N_HEADS, BLK, 2 * BLK)
        o, lse = _dil_prompt_group(qkv_p, bias, grp, dil, b, s)
        outs_p.append(o)
        lses_p.append(lse)
        keep = min(win, s)
        kv = _vector_rows(qkv_p, (grp * 3 + 1) * N_HEADS // HEAD_TILE, 2 * N_HEADS // HEAD_TILE, b, s, keep)
        new_p.append(kv.reshape(b, keep, 2, N_HEADS, HEAD_DIM))
        buf = bufs[grp]
        lb = buf.shape[1]
        n_cls = min(dil, t)
        bias_buf = _bias_table(rel_bias, n_cls, 8, lb // dil, lb, -1, 1, -dil, 0, win, dil)
        bias_new = _bias_table(rel_bias, 1, 8, BLK, 0, 0, 1, -1, 0, win, dil).reshape(N_HEADS, 8, BLK)
        q = _pad_rows(qkv_s[:, :, base:base + wide], 8)
        kn = _pad_rows(qkv_s[:, :, base + wide:base + 2 * wide], BLK)
        vn = _pad_rows(qkv_s[:, :, base + 2 * wide:base + 3 * wide], BLK)
        o, lse = _dil_sample_group(q, kn, vn, buf, bias_buf, bias_new, dil)
        outs_s.append(o[:, :t].reshape(n * t, wide))
        lses_s.append(lse[:, :t].reshape(n * t, BLK))
        kv_new = qkv_s[:, :, base + wide:base + 3 * wide].reshape(n, t, 2, N_HEADS, HEAD_DIM)
        new_s.append(jnp.concatenate([buf, kv_new], axis=1)[:, t:])
    hs, w_out = _project(_dil_combine(outs_s, lses_s, False), w_out, res=hs, cast_w=True)
    hp = _project(_dil_combine(outs_p, lses_p, True), w_out, res=hp)
    return hp, hs, new_p, new_s


PAGES_PER_STEP = 16
CHUNKS_PER_PAGE = BLK // CMP_STRIDE
NT_DIMS = (((1,), (1,)), ((), ()))
TN_DIMS = (((0,), (0,)), ((), ()))


def _page_spec(p, half):
    return pl.BlockSpec((BLK, HEAD_TILE, HEAD_DIM),
                        lambda i, j, *rest: (rest[-1][i, j * PAGES_PER_STEP + p], half, 0))


def _cmp_proj_kernel(pt_ref, *refs):
    pages = refs[:PAGES_PER_STEP]
    w_ref, o_ref = refs[PAGES_PER_STEP:PAGES_PER_STEP + 2]
    ys = [pltpu.einshape("ctgd->tgcd", pg[...].reshape(CHUNKS_PER_PAGE, CMP_STRIDE, HEAD_TILE, HEAD_DIM))
          for pg in pages]
    for c in range(2):
        lhs = jnp.concatenate(
            [jnp.concatenate([y[t, c * KV_HEADS + kh] for kh in range(KV_HEADS) for y in ys], axis=0)
             for t in range(CMP_STRIDE)], axis=1).astype(BF16)
        acc = jnp.dot(lhs, w_ref[c], preferred_element_type=F32)
        for kh in range(KV_HEADS):
            o_ref[0, c, kh] = acc[kh * BLK:(kh + 1) * BLK]


def _cmp_proj(pages, page_table, w1r):
    n, n_pages = page_table.shape
    chunks = n_pages * CHUNKS_PER_PAGE
    grid_spec = pltpu.PrefetchScalarGridSpec(
        num_scalar_prefetch=1,
        grid=(n, n_pages // PAGES_PER_STEP),
        in_specs=[_page_spec(p, 0) for p in range(PAGES_PER_STEP)]
        + [pl.BlockSpec((2, CMP_STRIDE * HEAD_DIM, 2 * HEAD_DIM), lambda i, j, pt: (0, 0, 0))],
        out_specs=pl.BlockSpec((1, 2, KV_HEADS, BLK, 2 * HEAD_DIM), lambda i, j, pt: (i, 0, 0, j, 0)),
    )
    return pl.pallas_call(
        _cmp_proj_kernel,
        out_shape=jax.ShapeDtypeStruct((n, 2, KV_HEADS, chunks, 2 * HEAD_DIM), F32),
        grid_spec=grid_spec,
        compiler_params=_params(("parallel", "arbitrary")),
        name="nsa_cmp_proj",
    )(page_table, *([pages] * PAGES_PER_STEP), w1r.reshape(2, CMP_STRIDE * HEAD_DIM, 2 * HEAD_DIM))


def _finish_compress(a, pe_row, w1f, b1, w2, b2, n_blocks):
    rows = a.shape[0]
    cst = jnp.dot(pe_row, w1f, preferred_element_type=F32)[0:1]
    h = (b1 + cst) + a[:, :HEAD_DIM] + pltpu.roll(a[:, HEAD_DIM:], rows - 1, 0)
    x = jnp.dot(jax.nn.gelu(h).astype(BF16), w2, preferred_element_type=F32) + b2
    return jnp.where(lax.broadcasted_iota(jnp.int32, x.shape, 0) < n_blocks, x, 0.0)


def _split3(x):
    hi = x.astype(BF16)
    r = x - hi.astype(F32)
    mid = r.astype(BF16)
    return hi, mid, (r - mid.astype(F32)).astype(BF16)


def _top_n(score, n, axis):
    idx = lax.broadcasted_iota(jnp.int32, score.shape, axis).astype(F32)
    big = float(score.shape[axis])

    def body(_, carry):
        sc, sel = carry
        mx = jnp.max(sc, axis=axis, keepdims=True)
        first = jnp.min(jnp.where(sc == mx, idx, big), axis=axis, keepdims=True)
        hit = idx == first
        return jnp.where(hit, -jnp.inf, sc), jnp.where(hit, 1.0, sel)

    return lax.fori_loop(0, n, body, (score, jnp.zeros(score.shape, F32)))[1]


def _sel_scores(p_slc, blk, cur, n_blocks):
    forced = (blk == 0) | (blk == cur) | (blk == cur - 1)
    score = jnp.where(forced, FORCED_SCORE, jnp.where(blk <= cur, p_slc, -1.0))
    return jnp.where(blk < n_blocks, score, -2.0)


def _nsa_prompt_kernel(q_ref, a_ref, pe_ref, w1f_ref, b1_ref, w2_ref, b2_ref, ksel_ref, vsel_ref, kwin_ref,
                       vwin_ref, tbl_ref, tblw_ref, biasc_ref, gates_ref, msel_ref, o_ref, kc_scr, vc_scr, sel_scr,
                       vselt_ref, vwint_ref, *, n_cmp, n_sel):
    i = pl.program_id(2)

    @pl.when(i == 0)
    def _():
        for c, scr in ((0, kc_scr), (1, vc_scr)):
            scr[...] = _finish_compress(a_ref[0, c, 0], pe_ref[c], w1f_ref[c], b1_ref[c], w2_ref[c], b2_ref[c],
                                        n_cmp).astype(BF16)
        for src, dst in ((vsel_ref, vselt_ref), (vwin_ref, vwint_ref)):
            for c in range(src.shape[0] // BLK):
                dst[:, c * BLK:(c + 1) * BLK] = src[c * BLK:(c + 1) * BLK, :].T.astype(BF16)

    q = q_ref[...]
    qs = jnp.concatenate([q[:, g * HEAD_DIM:(g + 1) * HEAD_DIM] for g in range(GROUP)], axis=0).astype(BF16)
    key_i = lax.broadcasted_iota(jnp.int32, (BLK, BLK), 0)
    tok_i = lax.broadcasted_iota(jnp.int32, (BLK, BLK), 1)

    def lanes4(x):
        return jnp.concatenate([x] * GROUP, axis=1)

    qs2 = jnp.concatenate([q[:, g * HEAD_DIM:(g + 1) * HEAD_DIM] for g in range(GROUP)],
                          axis=0) * (SCALE * LOG2E)
    qs2 = qs2.astype(BF16)

    mask_c = lanes4(i * BLK + tok_i - (key_i * CMP_STRIDE + (CMP_BLOCK - 1)) >= 0)
    s = lax.dot_general(kc_scr[...], qs, NT_DIMS, preferred_element_type=F32) * SCALE
    s = jnp.where(mask_c, s + jnp.concatenate([biasc_ref[g, 0] for g in range(GROUP)], axis=1), NEG)
    m = jnp.max(s, axis=0, keepdims=True)
    p = jnp.where(mask_c, jnp.exp(s - m), 0.0)
    pn = p / jnp.maximum(jnp.sum(p, axis=0, keepdims=True), 1e-30)
    o_cmp = lax.dot_general(vc_scr[...], pn.astype(BF16), TN_DIMS, preferred_element_type=F32)
    pc = ((pn[:, 0:BLK] + pn[:, BLK:2 * BLK]) + pn[:, 2 * BLK:3 * BLK]) + pn[:, 3 * BLK:4 * BLK]

    msel = msel_ref[...]
    p_slc = sum(jnp.dot(msel, part, preferred_element_type=F32) for part in _split3(pc))
    blk = lax.broadcasted_iota(jnp.int32, p_slc.shape, 0)
    cur = (i * BLK + lax.broadcasted_iota(jnp.int32, p_slc.shape, 1)) // SEL_BLOCK
    sel_scr[...] = (1.0 - _top_n(_sel_scores(p_slc, blk, cur, n_sel), SEL_TOPN, 0)) * NEG

    def attend(k_ref, vt_ref, bias_ref, first_blk, n_blk, extra, carry):
        m_run, l_run, acc = carry
        off = pl.multiple_of(first_blk * BLK, BLK)
        k = k_ref[pl.ds(off, n_blk * BLK), :].astype(BF16)
        bias = []
        for j in range(n_blk):
            idx = jnp.maximum(i - first_blk - j, -1) + 1
            bias.append(jnp.concatenate([bias_ref[g, idx] for g in range(GROUP)], axis=1))
        s = lax.dot_general(k, qs2, NT_DIMS, preferred_element_type=F32) + jnp.concatenate(bias, axis=0)
        if extra is not None:
            s = s + extra
        m_new = jnp.maximum(m_run, jnp.max(s, axis=0, keepdims=True))
        alpha = jnp.exp2(m_run - m_new)
        p = jnp.exp2(s - m_new)
        l_new = alpha * l_run + jnp.sum(p, axis=0, keepdims=True)
        vt = vt_ref[:, pl.ds(off, n_blk * BLK)]
        return m_new, l_new, alpha * acc + jnp.dot(vt, p.astype(BF16), preferred_element_type=F32)

    init = (jnp.full((1, GROUP * BLK), NEG, F32), jnp.zeros((1, GROUP * BLK), F32),
            jnp.zeros((HEAD_DIM, GROUP * BLK), F32))

    sel_span = 4
    per_blk = BLK // SEL_BLOCK

    def sel_step(c, carry):
        rows = sel_scr[pl.ds(pl.multiple_of(c * sel_span * per_blk, 8), sel_span * per_blk), :]
        unpicked = jnp.concatenate([jnp.broadcast_to(rows[u:u + 1], (SEL_BLOCK, BLK))
                                    for u in range(sel_span * per_blk)], axis=0)
        return attend(ksel_ref, vselt_ref, tbl_ref, c * sel_span, sel_span, lanes4(unpicked), carry)

    _, l_sel, acc_sel = lax.fori_loop(0, i // sel_span + 1, sel_step, init)
    o_sel = acc_sel / jnp.maximum(l_sel, 1e-30)

    n_win = (NSA_WINDOW - 1 + BLK - 1) // BLK + 1
    _, l_win, acc_win = attend(kwin_ref, vwint_ref, tblw_ref, jnp.maximum(i - (n_win - 1), 0), n_win, None, init)
    o_win = acc_win / jnp.maximum(l_win, 1e-30)

    gt = gates_ref[0]

    def gate(branch):
        return jnp.concatenate([gt[branch * GROUP + g:branch * GROUP + g + 1, :] for g in range(GROUP)], axis=1)

    o = (gate(0) * o_cmp + gate(1) * o_sel) + gate(2) * o_win
    for g in range(GROUP):
        o_ref[0, g] = o[:, g * BLK:(g + 1) * BLK]


def _nsa_prompt(proj, a_cmp, gates_t, cmp_w, tbl, tbl_win, bias_c, msel, b, s):
    pe, w1f, b1, w2, b2 = cmp_w
    nq = s // BLK
    kcol = NSA_Q // HEAD_DIM
    const = lambda shape: pl.BlockSpec(shape, lambda bi, kh, i: (0,) * len(shape))
    return pl.pallas_call(
        functools.partial(_nsa_prompt_kernel, n_cmp=s // CMP_STRIDE - 1, n_sel=s // SEL_BLOCK),
        out_shape=jax.ShapeDtypeStruct((b, N_HEADS, HEAD_DIM, s), F32),
        grid=(b, KV_HEADS, nq),
        in_specs=[
            pl.BlockSpec((BLK, GROUP * HEAD_DIM), lambda bi, kh, i: (bi * nq + i, kh)),
            pl.BlockSpec((1, 2, 1, s // CMP_STRIDE, 2 * HEAD_DIM), lambda bi, kh, i: (bi, 0, kh, 0, 0)),
            const(pe.shape), const(w1f.shape), const(b1.shape), const(w2.shape), const(b2.shape),
            pl.BlockSpec((s, HEAD_DIM), lambda bi, kh, i: (bi, kcol + 2 * KV_HEADS + kh)),
            pl.BlockSpec((s, HEAD_DIM), lambda bi, kh, i: (bi, kcol + 3 * KV_HEADS + kh)),
            pl.BlockSpec((s, HEAD_DIM), lambda bi, kh, i: (bi, kcol + 4 * KV_HEADS + kh)),
            pl.BlockSpec((s, HEAD_DIM), lambda bi, kh, i: (bi, kcol + 5 * KV_HEADS + kh)),
            pl.BlockSpec((GROUP,) + tbl.shape[1:], lambda bi, kh, i: (kh, 0, 0, 0)),
            pl.BlockSpec((GROUP,) + tbl_win.shape[1:], lambda bi, kh, i: (kh, 0, 0, 0)),
            pl.BlockSpec((GROUP, 1, s // CMP_STRIDE, BLK), lambda bi, kh, i: (kh, 0, 0, i)),
            pl.BlockSpec((1, 16, BLK), lambda bi, kh, i: (kh, 0, bi * nq + i)),
            const(msel.shape),
        ],
        out_specs=pl.BlockSpec((1, GROUP, HEAD_DIM, BLK), lambda bi, kh, i: (bi, kh, 0, i)),
        scratch_shapes=[pltpu.VMEM((s // CMP_STRIDE, HEAD_DIM), BF16), pltpu.VMEM((s // CMP_STRIDE, HEAD_DIM), BF16),
                        pltpu.VMEM((s // SEL_BLOCK, BLK), F32),
                        pltpu.VMEM((HEAD_DIM, s), BF16), pltpu.VMEM((HEAD_DIM, s), BF16)],
        compiler_params=_params(("parallel", "parallel", "arbitrary")),
        name="nsa_prompt",
    )(proj, a_cmp, pe, w1f, b1, w2, b2, proj, proj, proj, proj, tbl, tbl_win, bias_c, gates_t, msel)


def _masked_softmax_rows(s, mask):
    s = jnp.where(mask, s, NEG)
    m = jnp.max(s, axis=-1, keepdims=True)
    p = jnp.where(mask, jnp.exp(s - m), 0.0)
    return p / jnp.maximum(jnp.sum(p, axis=-1, keepdims=True), 1e-30)


def _nsa_sample_select_kernel(q_ref, a_ref, pe_ref, w1f_ref, b1_ref, w2_ref, b2_ref, biasc_ref, msel_ref,
                              regroup_ref, ocmp_ref, sel_ref, *, n_cmp, n_sel, past):
    pcs = []
    for kh in range(KV_HEADS):
        kc, vc = (_finish_compress(a_ref[0, c, kh], pe_ref[c], w1f_ref[c], b1_ref[c], w2_ref[c], b2_ref[c],
                                   n_cmp).astype(BF16) for c in range(2))
        bias = biasc_ref[kh]
        s = lax.dot_general(q_ref[0, kh].astype(BF16), kc, NT_DIMS, preferred_element_type=F32) * SCALE + bias
        pn = _masked_softmax_rows(s, bias > 0.5 * NEG)
        ocmp_ref[0, kh] = jnp.dot(pn.astype(BF16), vc, preferred_element_type=F32)
        pcs.append(((pn[0:8] + pn[8:16]) + pn[16:24]) + pn[24:32])
    pc = jnp.concatenate(pcs, axis=0)
    msel = msel_ref[...]
    p_slc = sum(jnp.dot(part, msel, preferred_element_type=F32) for part in _split3(pc))
    blk = lax.broadcasted_iota(jnp.int32, p_slc.shape, 1)
    cur = (past + lax.broadcasted_iota(jnp.int32, p_slc.shape, 0) % 8) // SEL_BLOCK
    sel = _top_n(_sel_scores(p_slc, blk, cur, n_sel), SEL_TOPN, 1).astype(BF16)
    for j in range(regroup_ref.shape[0]):
        part = jnp.dot(sel, regroup_ref[j], preferred_element_type=F32)
        for kh in range(KV_HEADS):
            sel_ref[0, kh, j] = part[kh * 8:(kh + 1) * 8]


def _nsa_sample_select(q, a_cmp, cmp_w, bias_c, msel, regroup, past, t_len):
    pe, w1f, b1, w2, b2 = cmp_w
    n = q.shape[0]
    chunks = a_cmp.shape[3]
    n_steps = regroup.shape[0]
    const = lambda shape: pl.BlockSpec(shape, lambda i: (0,) * len(shape))
    return pl.pallas_call(
        functools.partial(_nsa_sample_select_kernel, n_cmp=chunks - 1, n_sel=(past + t_len + SEL_BLOCK - 1) // SEL_BLOCK,
                          past=past),
        out_shape=(jax.ShapeDtypeStruct((n, KV_HEADS, GROUP * 8, HEAD_DIM), F32),
                   jax.ShapeDtypeStruct((n, KV_HEADS, n_steps, 8, BLK), F32)),
        grid=(n,),
        in_specs=[
            pl.BlockSpec((1, KV_HEADS, GROUP * 8, HEAD_DIM), lambda i: (i, 0, 0, 0)),
            pl.BlockSpec((1, 2, KV_HEADS, chunks, 2 * HEAD_DIM), lambda i: (i, 0, 0, 0, 0)),
            const(pe.shape), const(w1f.shape), const(b1.shape), const(w2.shape), const(b2.shape),
            const(bias_c.shape), const(msel.shape), const(regroup.shape),
        ],
        out_specs=(pl.BlockSpec((1, KV_HEADS, GROUP * 8, HEAD_DIM), lambda i: (i, 0, 0, 0)),
                   pl.BlockSpec((1, KV_HEADS, n_steps, 8, BLK), lambda i: (i, 0, 0, 0, 0))),
        compiler_params=_params(("parallel",)),
        name="nsa_sample_select",
    )(q, a_cmp, pe, w1f, b1, w2, b2, bias_c, msel, regroup)


def _nsa_sample_attend_kernel(pt_ref, *refs):
    pages = refs[:PAGES_PER_STEP]
    (q_ref, sel_ref, expand_ref, bias_ref, kn_ref, vn_ref, biasn_ref, win_ref, kwn_ref, vwn_ref, biasw_ref,
     biaswn_ref, gates_ref, ocmp_ref, o_ref, m_scr, l_scr, acc_scr) = refs[PAGES_PER_STEP:]
    j = pl.program_id(1)
    n_steps = pl.num_programs(1)

    @pl.when(j == 0)
    def _():
        m_scr[...] = jnp.full_like(m_scr, NEG)
        l_scr[...] = jnp.zeros_like(l_scr)
        acc_scr[...] = jnp.zeros_like(acc_scr)

    def picked(kh, step):
        sel = sel_ref[0, kh, step].astype(BF16)
        return jnp.concatenate([sel] * GROUP, axis=0)

    def update(kh, k, v, bias, mask):
        qs = q_ref[0, kh].astype(BF16)
        s = lax.dot_general(qs, k, NT_DIMS, preferred_element_type=F32) * SCALE + bias
        s = jnp.where(mask, s, NEG)
        m_old = m_scr[kh]
        m_new = jnp.maximum(m_old, jnp.max(s, axis=-1, keepdims=True))
        alpha = jnp.exp(m_old - m_new)
        p = jnp.where(mask, jnp.exp(s - m_new), 0.0)
        l_scr[kh] = alpha * l_scr[kh] + jnp.sum(p, axis=-1, keepdims=True)
        acc_scr[kh] = alpha * acc_scr[kh] + jnp.dot(p.astype(BF16), v, preferred_element_type=F32)
        m_scr[kh] = m_new

    tiles = [_heads_first(pg[...]) for pg in pages]
    expand = expand_ref[...]
    for kh in range(KV_HEADS):
        k = jnp.concatenate([tl[kh] for tl in tiles], axis=0).astype(BF16)
        v = jnp.concatenate([tl[KV_HEADS + kh] for tl in tiles], axis=0).astype(BF16)
        in_sel = jnp.dot(picked(kh, j), expand, preferred_element_type=F32) > 0.5
        update(kh, k, v, bias_ref[kh], in_sel)

    @pl.when(j == n_steps - 1)
    def _():
        win = _heads_first(win_ref[...])
        for kh in range(KV_HEADS):
            biasn = biasn_ref[kh]
            new_sel = picked(kh, n_steps)[:, 0:1] > 0.5
            update(kh, kn_ref[0, kh].astype(BF16), vn_ref[0, kh].astype(BF16), biasn, (biasn > 0.5 * NEG) & new_sel)
            o_sel = acc_scr[kh] / jnp.maximum(l_scr[kh], 1e-30)
            biasw = jnp.concatenate([biasw_ref[kh], biaswn_ref[kh]], axis=1)
            kw = jnp.concatenate([win[kh], kwn_ref[0, kh]], axis=0).astype(BF16)
            vw = jnp.concatenate([win[KV_HEADS + kh], vwn_ref[0, kh]], axis=0).astype(BF16)
            sw = lax.dot_general(q_ref[0, kh].astype(BF16), kw, NT_DIMS, preferred_element_type=F32) * SCALE + biasw
            pw = _masked_softmax_rows(sw, biasw > 0.5 * NEG)
            o_win = jnp.dot(pw.astype(BF16), vw, preferred_element_type=F32)
            o_ref[0, kh] = ((gates_ref[0, 0, kh] * ocmp_ref[0, kh] + gates_ref[1, 0, kh] * o_sel)
                            + gates_ref[2, 0, kh] * o_win)


def _nsa_sample_attend(pool, page_table, q, sel, expand, bias_sel, k_new, v_new, bias_new, win_buf, kw_new, vw_new,
                       bias_win, bias_win_new, gates, o_cmp):
    n, n_pages = page_table.shape
    n_steps = n_pages // PAGES_PER_STEP
    keys = PAGES_PER_STEP * BLK
    rows = GROUP * 8
    lw = win_buf.shape[1]
    per = lambda shape: pl.BlockSpec((1, KV_HEADS) + shape, lambda i, j, pt: (i, 0) + (0,) * len(shape))
    whole = lambda shape: pl.BlockSpec(shape, lambda i, j, pt: (0,) * len(shape))
    grid_spec = pltpu.PrefetchScalarGridSpec(
        num_scalar_prefetch=1,
        grid=(n, n_steps),
        in_specs=[_page_spec(p, 1) for p in range(PAGES_PER_STEP)] + [
            per((rows, HEAD_DIM)),
            per((n_steps + 1, 8, BLK)),
            whole(expand.shape),
            pl.BlockSpec((KV_HEADS, rows, keys), lambda i, j, pt: (0, 0, j)),
            per((BLK, HEAD_DIM)), per((BLK, HEAD_DIM)), whole((KV_HEADS, rows, BLK)),
            pl.BlockSpec((None, lw, HEAD_TILE, HEAD_DIM), lambda i, j, pt: (i, 0, 0, 0)),
            per((BLK, HEAD_DIM)), per((BLK, HEAD_DIM)), whole((KV_HEADS, rows, lw)), whole((KV_HEADS, rows, BLK)),
            pl.BlockSpec((3, 1, KV_HEADS, rows, HEAD_DIM), lambda i, j, pt: (0, i, 0, 0, 0)),
            per((rows, HEAD_DIM)),
        ],
        out_specs=per((rows, HEAD_DIM)),
        scratch_shapes=[pltpu.VMEM((KV_HEADS, rows, 1), F32), pltpu.VMEM((KV_HEADS, rows, 1), F32),
                        pltpu.VMEM((KV_HEADS, rows, HEAD_DIM), F32)],
    )
    return pl.pallas_call(
        _nsa_sample_attend_kernel,
        out_shape=jax.ShapeDtypeStruct((n, KV_HEADS, rows, HEAD_DIM), F32),
        grid_spec=grid_spec,
        compiler_params=_params(("parallel", "arbitrary")),
        name="nsa_sample_attend",
    )(page_table, *([pool] * PAGES_PER_STEP), q, sel, expand, bias_sel, k_new, v_new, bias_new, win_buf, kw_new,
      vw_new, bias_win, bias_win_new, gates, o_cmp)


def _sel_weights(n_cmp_rows, n_sel_cols):
    ratio = SEL_BLOCK // CMP_STRIDE
    span = CMP_BLOCK // CMP_STRIDE
    c = jnp.arange(n_cmp_rows)[:, None]
    j = jnp.arange(n_sel_cols)[None, :]
    o = c - ratio * j + (span - 1)
    cnt = jnp.minimum(o, span - 1) - jnp.maximum(o - (ratio - 1), 0) + 1
    return jnp.where((o >= 0) & (o <= ratio + span - 2), cnt, 0).astype(BF16)


def _layer_b(hp, hs, b, s, n, t, norm_g, w_in, w_out, rel_bias, cmp, pool, page_table, win_buf):
    cmp_pe, cmp_w1, cmp_b1, cmp_w2, cmp_b2 = cmp
    n_kvcol = 6 * NSA_KV
    w_main = w_in
    w_gate = jnp.pad(w_in[:, NSA_Q + n_kvcol:], ((0, 0), (0, BLK - 3 * N_HEADS))).astype(BF16)
    w1 = cmp_w1.reshape(2, 2, CMP_STRIDE, HEAD_DIM, HEAD_DIM)
    w1r = jnp.concatenate([w1[:, 0], w1[:, 1]], axis=-1).astype(BF16)
    pe_row = jnp.pad(cmp_pe.reshape(2, 1, CMP_BLOCK * HEAD_DIM), ((0, 0), (0, 7), (0, 0))).astype(BF16)
    cmp_w = (pe_row, cmp_w1.reshape(2, CMP_BLOCK * HEAD_DIM, HEAD_DIM).astype(BF16), cmp_b1.reshape(2, 1, HEAD_DIM),
             cmp_w2.astype(BF16), cmp_b2.reshape(2, 1, HEAD_DIM))
    weights = (w_main, w_gate, w_out, w1r, cmp_w)
    hs, new_win_s, new_kv_s, w_main, w_out = _nsa_sample_path(hs, n, t, norm_g, weights, rel_bias, pool, page_table,
                                                              win_buf)
    weights = (w_main, w_gate, w_out, w1r, cmp_w)
    hp, new_win_p, new_kv_p = _nsa_prompt_path(hp, b, s, norm_g, weights, rel_bias)
    return hp, hs, new_win_p, new_win_s, new_kv_p, new_kv_s


def _nsa_prompt_path(hp, b, s, norm_g, weights, rel_bias):
    w_main, w_gate, w_out, w1r, cmp_w = weights
    proj_p = _project(hp, w_main, gain=norm_g)
    gates_p = _project(hp, w_gate, gain=norm_g, sigmoid=True)
    col_tile = HEAD_TILE * HEAD_DIM
    new_kv = _vector_rows(proj_p, NSA_Q // col_tile, 4 * NSA_KV // col_tile, b, s, s)
    keep = min(NSA_WINDOW, s)
    new_win = _vector_rows(proj_p, (NSA_Q + 4 * NSA_KV) // col_tile, 2 * NSA_KV // col_tile, b, s, keep)
    table_p = jnp.arange(b * s // BLK, dtype=jnp.int32).reshape(b, s // BLK)
    a_cmp_p = _cmp_proj(new_kv.reshape(b * s, 2 * HEAD_TILE, HEAD_DIM), table_p, w1r)
    nq = s // BLK
    n_win = (NSA_WINDOW - 1 + BLK - 1) // BLK + 1
    tbl = _bias_table(rel_bias, nq + 1, BLK, BLK, -BLK, BLK, -1, 1, 0, 1 << 30, scale=LOG2E)
    tbl_win = _bias_table(rel_bias, n_win + 1, BLK, BLK, -BLK, BLK, -1, 1, 0, NSA_WINDOW - 1, scale=LOG2E)
    bias_c = _bias_table(rel_bias, 1, s // CMP_STRIDE, s, -(CMP_BLOCK - 1), 0, -CMP_STRIDE, 1, 0, 1 << 30)
    gates_t = gates_p[:, :3 * N_HEADS].reshape(b * s, 3, KV_HEADS, GROUP).transpose(2, 1, 3, 0)
    gates_t = jnp.pad(gates_t.reshape(KV_HEADS, 3 * GROUP, b * s), ((0, 0), (0, 16 - 3 * GROUP), (0, 0)))
    msel_p = _sel_weights(s // CMP_STRIDE, s // SEL_BLOCK).T
    o_t = _nsa_prompt(proj_p, a_cmp_p, gates_t, cmp_w, tbl, tbl_win, bias_c, msel_p, b, s)
    o_p = o_t.transpose(0, 3, 1, 2).reshape(b * s, NSA_Q).astype(BF16)
    hp = _project(o_p, w_out, res=hp)
    return (hp, new_win.reshape(b, keep, 2, KV_HEADS, HEAD_DIM), new_kv.reshape(b, s, 4, KV_HEADS, HEAD_DIM))


def _nsa_sample_path(hs, n, t, norm_g, weights, rel_bias, pool, page_table, win_buf):
    w_main, w_gate, w_out, w1r, cmp_w = weights
    past = page_table.shape[1] * BLK
    proj_s, w_main = _project(hs, w_main, n_cols=NSA_Q + 6 * NSA_KV, gain=norm_g, cast_w=True)
    gates_s = _project(hs, w_gate, gain=norm_g, sigmoid=True)
    kv_s = proj_s[:, NSA_Q:].reshape(n, t, 6, KV_HEADS, HEAD_DIM)
    rows = GROUP * 8

    def head_rows(x):
        x = jnp.pad(x.transpose(0, 2, 3, 1, 4), ((0, 0), (0, 0), (0, 0), (0, 8 - t), (0, 0)))
        return x.reshape(n, KV_HEADS, rows, x.shape[-1])

    def new_rows(c):
        return jnp.pad(kv_s[:, :, c].transpose(0, 2, 1, 3), ((0, 0), (0, 0), (0, BLK - t), (0, 0)))

    def head_table(x, cols):
        return x.reshape(KV_HEADS, rows, cols)

    q_s = head_rows(proj_s[:, :NSA_Q].reshape(n, t, KV_HEADS, GROUP, HEAD_DIM))
    pool2 = pool.reshape(pool.shape[0] * BLK, 2 * HEAD_TILE, HEAD_DIM)
    a_cmp_s = _cmp_proj(pool2, page_table, w1r)
    chunks = past // CMP_STRIDE
    big = 1 << 30
    bias_cs = head_table(_bias_table(rel_bias, 1, 8, chunks, past - (CMP_BLOCK - 1), 0, 1, -CMP_STRIDE, 0, big), chunks)
    n_steps = page_table.shape[1] // PAGES_PER_STEP
    n_sel_pad = (n_steps + 1) * BLK
    msel_s = _sel_weights(chunks, n_sel_pad)
    per_step = PAGES_PER_STEP * BLK // SEL_BLOCK
    jj = jnp.arange(n_sel_pad)[None, :, None]
    ll = jnp.arange(BLK)[None, None, :]
    st = jnp.arange(n_steps + 1)[:, None, None]
    regroup = ((jj == st * per_step + ll) & (ll < per_step)).astype(BF16)
    o_cmp_s, sel_s = _nsa_sample_select(q_s, a_cmp_s, cmp_w, bias_cs, msel_s, regroup, past, t)
    expand = (jnp.arange(BLK)[:, None] == jnp.arange(PAGES_PER_STEP * BLK)[None, :] // SEL_BLOCK).astype(BF16)
    bias_sel = head_table(_bias_table(rel_bias, 1, 8, past, past, 0, 1, -1, 0, big), past)
    bias_new = head_table(_bias_table(rel_bias, 1, 8, BLK, 0, 0, 1, -1, 0, big), BLK)
    lw = win_buf.shape[1]
    bias_win = head_table(_bias_table(rel_bias, 1, 8, lw, lw, 0, 1, -1, 0, NSA_WINDOW - 1), lw)
    bias_win_new = head_table(_bias_table(rel_bias, 1, 8, BLK, 0, 0, 1, -1, 0, NSA_WINDOW - 1), BLK)
    g_s = gates_s[:, :3 * N_HEADS].reshape(n, t, 3, KV_HEADS, GROUP, 1)
    g_s = jnp.stack([head_rows(g_s[:, :, c]) for c in range(3)])
    g_s = jnp.broadcast_to(g_s, (3, n, KV_HEADS, rows, HEAD_DIM))
    o_s = _nsa_sample_attend(pool2, page_table, q_s, sel_s, expand, bias_sel, new_rows(2), new_rows(3), bias_new,
                             win_buf.reshape(n, lw, HEAD_TILE, HEAD_DIM), new_rows(4), new_rows(5), bias_win,
                             bias_win_new,
                             g_s, o_cmp_s)
    o_s = o_s.reshape(n, KV_HEADS, GROUP, 8, HEAD_DIM)[:, :, :, :t].transpose(0, 3, 1, 2, 4)
    hs, w_out = _project(o_s.reshape(n * t, NSA_Q).astype(BF16), w_out, res=hs, cast_w=True)
    new_win_s = jnp.concatenate([win_buf, kv_s[:, :, 4:]], axis=1)[:, t:]
    return hs, new_win_s, kv_s[:, :, :4], w_main, w_out


def _ffn_and_ple(hp, hs, b, s, n, t, i, norm_ffn, norm_ple, w_ffn_in, conv_w, conv_b, w_ffn_out, state_conv,
                 p_prompt, p_sample, w_ple_gate, w_ple_proj):
    hu_s, w_ffn_in = _project(hs, w_ffn_in, layer=i, gain=norm_ffn, cast_w=True)
    hu_p = _project(hp, w_ffn_in, gain=norm_ffn)
    conv_p = hu_p.reshape(b, s, 2 * D_FF)[:, s - (CONV_W - 1):, :D_FF]
    conv_s = jnp.concatenate([state_conv, hu_s.reshape(n, t, 2 * D_FF)[:, :, :D_FF]], axis=1)[:, t:]
    hs, w_ffn_out = _ffn_out_sample(hu_s, state_conv, conv_w, conv_b, w_ffn_out, i, hs, t)
    hp = _ffn_out_prompt(hu_p, conv_w, conv_b, w_ffn_out, hp, s)
    hs, w_ple_gate, w_ple_proj = _ple_add(hs, norm_ple, w_ple_gate, p_sample.astype(BF16), w_ple_proj, layer=i,
                                          cast_w=True)
    hp = _ple_add(hp, norm_ple, w_ple_gate, p_prompt.astype(BF16), w_ple_proj)
    return hp, hs, conv_p, conv_s


def kernel(x_prompt, x_sample, state_dil_w128, state_dil_w512, state_dil_w2048, state_nsa_win, state_conv,
           cache_nsa_kv, page_table, p_prompt, p_sample, rel_bias, norm_mix, norm_ffn, norm_ple, norm_final,
           w_in_a, w_out_a, w_in_b, w_out_b, cmp_pe, cmp_w1, cmp_b1, cmp_w2, cmp_b2, w_ffn_in, conv_w, conv_b,
           w_ffn_out, w_ple_gate, w_ple_proj):
    b, s, d = x_prompt.shape
    n, t, _ = x_sample.shape
    depth = norm_mix.shape[0]
    hp, hs = x_prompt.reshape(b * s, d), x_sample.reshape(n * t, d)
    dil_p, dil_s = [[] for _ in range(N_DIL)], [[] for _ in range(N_DIL)]
    win_p, win_s, kv_p, kv_s, conv_p, conv_s = [], [], [], [], [], []
    for i in range(depth):
        li = i // 2
        if i % 2 == 0:
            hp, hs, new_p, new_s = _layer_a(
                hp, hs, b, s, n, t, norm_mix[i], w_in_a[li], w_out_a[li], rel_bias,
                (state_dil_w128[li], state_dil_w512[li], state_dil_w2048[li]))
            for g in range(N_DIL):
                dil_p[g].append(new_p[g])
                dil_s[g].append(new_s[g])
        else:
            hp, hs, wp, ws, rp, rs = _layer_b(
                hp, hs, b, s, n, t, norm_mix[i], w_in_b[li], w_out_b[li], rel_bias,
                (cmp_pe[li], cmp_w1[li], cmp_b1[li], cmp_w2[li], cmp_b2[li]), cache_nsa_kv[li], page_table,
                state_nsa_win[li])
            win_p.append(wp)
            win_s.append(ws)
            kv_p.append(rp)
            kv_s.append(rs)
        hp, hs, cp, cs = _ffn_and_ple(
            hp, hs, b, s, n, t, i, norm_ffn[i], norm_ple[i], w_ffn_in, conv_w[i], conv_b[i],
            w_ffn_out, state_conv[i], p_prompt[i].reshape(b * s, -1), p_sample[i].reshape(n * t, -1),
            w_ple_gate, w_ple_proj)
        conv_p.append(cp)
        conv_s.append(cs)
    y_prompt = _rmsnorm(hp, norm_final, F32).reshape(b, s, d)
    y_sample = _rmsnorm(hs, norm_final, F32).reshape(n, t, d)
    return (y_prompt, y_sample,
            jnp.stack(dil_p[0]), jnp.stack(dil_s[0]), jnp.stack(dil_p[1]), jnp.stack(dil_s[1]),
            jnp.stack(dil_p[2]), jnp.stack(dil_s[2]),
            jnp.stack(win_p), jnp.stack(win_s), jnp.stack(conv_p), jnp.stack(conv_s),
            jnp.stack(kv_p), jnp.stack(kv_s))
```
